```python
import jax
import jax.numpy as jnp
from jax import lax

D_MODEL = 1024
BATCH = 4
SEQ = 8192
DEPTH = 2

HEAD_DIM = 64
Q_BLOCK = 128
ROPE_THETA = 10000.0
NORM_EPS = 1e-6

A_HEADS = 8
A_KV_HEADS = 2
IDX_HEADS = 4
IDX_DIM = 64
DSA_TOPK = 256
B_HEADS = 8
B_KV_HEADS = 2
SWA_WINDOW = 128
C_HEADS = 8
C_KV_HEADS = 2
CMP_STRIDE = 16
CMP_LEN = 2 * CMP_STRIDE
CMP_HIDDEN = 128
SLC_BLOCK = 64
SLC_TOPN = 16
NSA_WINDOW = 512
D_HEADS = 8
Q_LORA = 256
KV_LORA = 128
NOPE_DIM = 64
ROPE_DIM = 32
V_DIM = 64

N_EVEN = (DEPTH + 1) // 2
N_ODD = DEPTH // 2

AB_SPLITS = (A_HEADS * HEAD_DIM, A_KV_HEADS * HEAD_DIM, A_KV_HEADS * HEAD_DIM, IDX_HEADS * IDX_DIM, IDX_DIM, IDX_HEADS, A_HEADS * HEAD_DIM, B_HEADS * HEAD_DIM, B_KV_HEADS * HEAD_DIM, B_KV_HEADS * HEAD_DIM, B_HEADS * HEAD_DIM)
AB_COLS = sum(AB_SPLITS)
AB_WIDTH = (A_HEADS + B_HEADS) * HEAD_DIM
CD_SPLITS = (C_HEADS * HEAD_DIM,) + (C_KV_HEADS * HEAD_DIM,) * 6 + (C_HEADS * 3, C_HEADS * HEAD_DIM, Q_LORA, KV_LORA, ROPE_DIM, D_HEADS * V_DIM)
CD_COLS = sum(CD_SPLITS)
CD_WIDTH = C_HEADS * HEAD_DIM + D_HEADS * V_DIM

kernel_name = 'hybrid_dsa_swa_nsa_mla_trunk'


def rms_norm(z, g):
    zf = z.astype(jnp.float32)
    y = zf * lax.rsqrt(jnp.mean(zf * zf, axis=-1, keepdims=True) + NORM_EPS)
    return (y * g.astype(jnp.float32)).astype(z.dtype)


def rope_tables(pos, dim):
    inv = jnp.power(jnp.float32(ROPE_THETA), -jnp.arange(0, dim, 2, dtype=jnp.float32) / dim)
    ang = pos.astype(jnp.float32)[:, None] * inv[None, :]
    return jnp.cos(ang), jnp.sin(ang)


def apply_rope(z, cos, sin):
    half = z.shape[-1] // 2
    z1 = z[..., :half].astype(jnp.float32)
    z2 = z[..., half:].astype(jnp.float32)
    c, s = cos[:, None, :], sin[:, None, :]
    return jnp.concatenate([z1 * c - z2 * s, z2 * c + z1 * s], axis=-1).astype(z.dtype)


def masked_softmax(s, mask):
    s = jnp.where(mask, s.astype(jnp.float32), -jnp.inf)
    m = jnp.max(s, axis=-1, keepdims=True)
    m = jnp.where(jnp.isfinite(m), m, 0.0)
    p = jnp.exp(s - m)
    return p / jnp.maximum(jnp.sum(p, axis=-1, keepdims=True), 1e-30)


def split_cols(y, sizes):
    offs, acc = [], 0
    for s in sizes[:-1]:
        acc += s
        offs.append(acc)
    return jnp.split(y, offs, axis=-1)


def dsa_attention(q, k, v, qi, ki, wi, k_sel):
    bsz, seq = q.shape[:2]
    grp = A_HEADS // A_KV_HEADS
    scale = HEAD_DIM ** -0.5
    idx_scale = (IDX_DIM * IDX_HEADS) ** -0.5
    key_pos = jnp.arange(seq)
    gather = jax.vmap(lambda zb, ib: zb[ib])

    def block(i):
        t0 = i * Q_BLOCK
        t = t0 + jnp.arange(Q_BLOCK)
        qb = lax.dynamic_slice_in_dim(q, t0, Q_BLOCK, 1).reshape(bsz, Q_BLOCK, A_KV_HEADS, grp, HEAD_DIM)
        qib = lax.dynamic_slice_in_dim(qi, t0, Q_BLOCK, 1)
        wib = lax.dynamic_slice_in_dim(wi, t0, Q_BLOCK, 1).astype(jnp.float32)
        logits = jnp.einsum('bqhd,bsd->bqhs', qib, ki).astype(jnp.float32)
        score = jnp.einsum('bqh,bqhs->bqs', wib, jax.nn.relu(logits)) * idx_scale
        causal = key_pos[None, :] <= t[:, None]
        score = jnp.where(causal[None], score, -jnp.inf)
        _, idx = lax.top_k(score, k_sel)
        valid = idx <= t[None, :, None]
        ks = gather(k, idx)
        vs = gather(v, idx)
        s = jnp.einsum('bqgrd,bqkgd->bqgrk', qb, ks) * scale
        p = masked_softmax(s, valid[:, :, None, None, :])
        o = jnp.einsum('bqgrk,bqkgd->bqgrd', p.astype(v.dtype), vs)
        return o.reshape(bsz, Q_BLOCK, A_HEADS * HEAD_DIM)

    out = lax.map(block, jnp.arange(seq // Q_BLOCK))
    return out.transpose(1, 0, 2, 3).reshape(bsz, seq, A_HEADS * HEAD_DIM)


def swa_sink_attention(q, k, v, sinks):
    bsz, seq = q.shape[:2]
    nb = seq // Q_BLOCK
    grp = B_HEADS // B_KV_HEADS
    scale = HEAD_DIM ** -0.5
    qb = q.reshape(bsz, nb, Q_BLOCK, B_KV_HEADS, grp, HEAD_DIM)

    def band(z):
        zp = jnp.pad(z, ((0, 0), (Q_BLOCK, 0), (0, 0), (0, 0))).reshape(bsz, nb + 1, Q_BLOCK, B_KV_HEADS, HEAD_DIM)
        return jnp.concatenate([zp[:, :-1], zp[:, 1:]], axis=2)

    kb, vb = band(k), band(v)
    s = jnp.einsum('bnqgrd,bnkgd->bngrqk', qb, kb).astype(jnp.float32) * scale
    qi = jnp.arange(Q_BLOCK)[None, :, None]
    kj = jnp.arange(2 * Q_BLOCK)[None, None, :]
    blk = jnp.arange(nb)[:, None, None]
    diff = qi + Q_BLOCK - kj
    key_abs = (blk - 1) * Q_BLOCK + kj
    mask = (diff >= 0) & (diff < SWA_WINDOW) & (key_abs >= 0)
    s = jnp.where(mask[None, :, None, None], s, -jnp.inf)
    sink = sinks.astype(jnp.float32).reshape(1, 1, B_KV_HEADS, grp, 1, 1)
    m = jnp.maximum(jnp.max(s, axis=-1, keepdims=True), sink)
    p = jnp.exp(s - m)
    p = p / (jnp.sum(p, axis=-1, keepdims=True) + jnp.exp(sink - m))
    o = jnp.einsum('bngrqk,bnkgd->bnqgrd', p.astype(v.dtype), vb)
    return o.reshape(bsz, seq, B_HEADS * HEAD_DIM)


def nsa_compress(z, pe, w1, w2):
    bsz, seq = z.shape[:2]
    ch = z.reshape(bsz, seq // CMP_STRIDE, CMP_STRIDE, C_KV_HEADS, HEAD_DIM)
    blocks = jnp.concatenate([ch[:, :-1], ch[:, 1:]], axis=2) + pe[None, None, :, None, :]
    flat = blocks.transpose(0, 1, 3, 2, 4).reshape(bsz, seq // CMP_STRIDE - 1, C_KV_HEADS, CMP_LEN * HEAD_DIM)
    return jax.nn.silu(flat @ w1) @ w2


def nsa_attention(q, kc, vc, ks, vs, kw, vw, gates, n_sel):
    bsz, seq = q.shape[:2]
    n_cmp = kc.shape[1]
    n_slc = seq // SLC_BLOCK
    grp = C_HEADS // C_KV_HEADS
    scale = HEAD_DIM ** -0.5
    ratio = SLC_BLOCK // CMP_STRIDE
    pad_back = ratio * n_slc + ratio - n_cmp - 1
    cmp_end = jnp.arange(n_cmp) * CMP_STRIDE + (CMP_LEN - 1)
    blk_ids = jnp.arange(n_slc)
    w_overlap = jnp.where(jnp.arange(ratio) == 0, 1.0, 2.0).astype(jnp.float32)
    ks_blk = ks.reshape(bsz, n_slc, SLC_BLOCK, C_KV_HEADS, HEAD_DIM).transpose(0, 3, 1, 2, 4)
    vs_blk = vs.reshape(bsz, n_slc, SLC_BLOCK, C_KV_HEADS, HEAD_DIM).transpose(0, 3, 1, 2, 4)
    kw_pad = jnp.pad(kw, ((0, 0), (NSA_WINDOW, 0), (0, 0), (0, 0)))
    vw_pad = jnp.pad(vw, ((0, 0), (NSA_WINDOW, 0), (0, 0), (0, 0)))
    gather = jax.vmap(jax.vmap(lambda zb, ib: zb[ib]))
    j_w = jnp.arange(Q_BLOCK + NSA_WINDOW)
    diff_w = jnp.arange(Q_BLOCK)[:, None] + NSA_WINDOW - j_w[None, :]

    def block(i):
        t0 = i * Q_BLOCK
        t = t0 + jnp.arange(Q_BLOCK)
        qb = lax.dynamic_slice_in_dim(q, t0, Q_BLOCK, 1).reshape(bsz, Q_BLOCK, C_KV_HEADS, grp, HEAD_DIM)
        gb = lax.dynamic_slice_in_dim(gates, t0, Q_BLOCK, 1).reshape(bsz, Q_BLOCK, C_KV_HEADS, grp, 3)
        s_c = jnp.einsum('bqgrd,bngd->bqgrn', qb, kc) * scale
        mask_c = cmp_end[None, :] <= t[:, None]
        p_c = masked_softmax(s_c, mask_c[None, :, None, None, :])
        o_c = jnp.einsum('bqgrn,bngd->bqgrd', p_c.astype(vc.dtype), vc)
        imp = jnp.pad(jnp.sum(p_c, axis=3), ((0, 0), (0, 0), (0, 0), (1, pad_back)))
        imp_s = jnp.sum(imp[..., :ratio * n_slc].reshape(bsz, Q_BLOCK, C_KV_HEADS, n_slc, ratio) * w_overlap, axis=-1) + imp[..., ratio::ratio]
        cur = t // SLC_BLOCK
        admiss = blk_ids[None, :] <= cur[:, None]
        forced = (blk_ids[None, :] == 0) | (blk_ids[None, :] >= cur[:, None] - 1)
        imp_s = jnp.where((admiss & forced)[None, :, None, :], jnp.inf, imp_s)
        imp_s = jnp.where(admiss[None, :, None, :], imp_s, -jnp.inf)
        _, sel = lax.top_k(imp_s, n_sel)
        sel_t = sel.transpose(0, 2, 1, 3)
        k_sel = gather(ks_blk, sel_t)
        v_sel = gather(vs_blk, sel_t).reshape(bsz, C_KV_HEADS, Q_BLOCK, n_sel * SLC_BLOCK, HEAD_DIM)
        s_s = jnp.einsum('bqgrd,bgqnld->bqgrnl', qb, k_sel) * scale
        s_s = s_s.reshape(bsz, Q_BLOCK, C_KV_HEADS, grp, n_sel * SLC_BLOCK)
        tok = sel[..., None] * SLC_BLOCK + jnp.arange(SLC_BLOCK)
        mask_s = (tok <= t[None, :, None, None, None]).reshape(bsz, Q_BLOCK, C_KV_HEADS, 1, n_sel * SLC_BLOCK)
        p_s = masked_softmax(s_s, mask_s)
        o_s = jnp.einsum('bqgrm,bgqmd->bqgrd', p_s.astype(vs.dtype), v_sel)
        kwb = lax.dynamic_slice_in_dim(kw_pad, t0, Q_BLOCK + NSA_WINDOW, 1)
        vwb = lax.dynamic_slice_in_dim(vw_pad, t0, Q_BLOCK + NSA_WINDOW, 1)
        s_w = jnp.einsum('bqgrd,bkgd->bqgrk', qb, kwb) * scale
        mask_w = (diff_w >= 0) & (diff_w < NSA_WINDOW) & ((t0 - NSA_WINDOW + j_w)[None, :] >= 0)
        p_w = masked_softmax(s_w, mask_w[None, :, None, None, :])
        o_w = jnp.einsum('bqgrk,bkgd->bqgrd', p_w.astype(vw.dtype), vwb)
        o = gb[..., 0:1] * o_c + gb[..., 1:2] * o_s + gb[..., 2:3] * o_w
        return o.reshape(bsz, Q_BLOCK, C_HEADS * HEAD_DIM)

    out = lax.map(block, jnp.arange(seq // Q_BLOCK))
    return out.transpose(1, 0, 2, 3).reshape(bsz, seq, C_HEADS * HEAD_DIM)


def mla_attention(q_nope, q_rope, k_nope, k_rope, v):
    bsz, seq = q_nope.shape[:2]
    scale = (NOPE_DIM + ROPE_DIM) ** -0.5
    key_pos = jnp.arange(seq)

    def block(i):
        t0 = i * Q_BLOCK
        t = t0 + jnp.arange(Q_BLOCK)
        qn = lax.dynamic_slice_in_dim(q_nope, t0, Q_BLOCK, 1)
        qr = lax.dynamic_slice_in_dim(q_rope, t0, Q_BLOCK, 1)
        s = (jnp.einsum('bqhd,bshd->bhqs', qn, k_nope) + jnp.einsum('bqhd,bsd->bhqs', qr, k_rope)) * scale
        mask = key_pos[None, :] <= t[:, None]
        p = masked_softmax(s, mask[None, None])
        o = jnp.einsum('bhqs,bshd->bqhd', p.astype(v.dtype), v)
        return o.reshape(bsz, Q_BLOCK, D_HEADS * V_DIM)

    out = lax.map(block, jnp.arange(seq // Q_BLOCK))
    return out.transpose(1, 0, 2, 3).reshape(bsz, seq, D_HEADS * V_DIM)


def layer_ab(x, norm_g, w_in, a_qk_norm, a_kidx_norm, b_qk_norm, b_sinks, w_out):
    bsz, seq, _ = x.shape
    y = rms_norm(x, norm_g) @ w_in
    qa, ka, va, qi, ki, wi, za, qb, kb, vb, zb = split_cols(y, AB_SPLITS)
    pos = jnp.arange(seq)
    cos, sin = rope_tables(pos, HEAD_DIM)
    icos, isin = rope_tables(pos, IDX_DIM)

    def heads(z, n):
        return z.reshape(bsz, seq, n, HEAD_DIM)

    qa = apply_rope(rms_norm(heads(qa, A_HEADS), a_qk_norm[0]), cos, sin)
    ka = apply_rope(rms_norm(heads(ka, A_KV_HEADS), a_qk_norm[1]), cos, sin)
    qi = apply_rope(qi.reshape(bsz, seq, IDX_HEADS, IDX_DIM), icos, isin)
    ki = apply_rope(rms_norm(ki, a_kidx_norm)[:, :, None, :], icos, isin)[:, :, 0]
    k_sel = min(DSA_TOPK, seq // 4)
    oa = dsa_attention(qa, ka, heads(va, A_KV_HEADS), qi, ki, wi, k_sel)
    qb = apply_rope(rms_norm(heads(qb, B_HEADS), b_qk_norm[0]), cos, sin)
    kb = apply_rope(rms_norm(heads(kb, B_KV_HEADS), b_qk_norm[1]), cos, sin)
    ob = swa_sink_attention(qb, kb, heads(vb, B_KV_HEADS), b_sinks)
    mix = jnp.concatenate([oa * jax.nn.silu(za), ob * jax.nn.silu(zb)], axis=-1)
    return x + mix @ w_out


def layer_cd(x, norm_g, w_in, c_q_norm, c_k_norm, c_cmp_pe, c_cmp_w1, c_cmp_w2, d_q_lat_norm, d_kv_lat_norm, d_w_uq, d_w_ukv, d_nope_norm, d_rope_norm, w_out):
    bsz, seq, _ = x.shape
    y = rms_norm(x, norm_g) @ w_in
    qc, kc, vc, ks, vs, kw, vw, gc, zc, cq, ckv, kr, zd = split_cols(y, CD_SPLITS)
    pos = jnp.arange(seq)
    cos, sin = rope_tables(pos, HEAD_DIM)

    def heads(z, n):
        return z.reshape(bsz, seq, n, HEAD_DIM)

    qc = apply_rope(rms_norm(heads(qc, C_HEADS), c_q_norm), cos, sin)
    kc = nsa_compress(heads(kc, C_KV_HEADS), c_cmp_pe[0], c_cmp_w1[0], c_cmp_w2[0])
    vc = nsa_compress(heads(vc, C_KV_HEADS), c_cmp_pe[1], c_cmp_w1[1], c_cmp_w2[1])
    n_cmp = kc.shape[1]
    ccos, csin = rope_tables(jnp.arange(n_cmp) * CMP_STRIDE + (CMP_LEN - 1), HEAD_DIM)
    kc = apply_rope(rms_norm(kc, c_k_norm[0]), ccos, csin)
    ks = apply_rope(rms_norm(heads(ks, C_KV_HEADS), c_k_norm[1]), cos, sin)
    kw = apply_rope(rms_norm(heads(kw, C_KV_HEADS), c_k_norm[2]), cos, sin)
    gates = jax.nn.sigmoid(gc.reshape(bsz, seq, C_HEADS, 3))
    n_sel = min(SLC_TOPN, seq // SLC_BLOCK)
    oc = nsa_attention(qc, kc, vc, ks, heads(vs, C_KV_HEADS), kw, heads(vw, C_KV_HEADS), gates, n_sel)
    q = (rms_norm(cq, d_q_lat_norm) @ d_w_uq).reshape(bsz, seq, D_HEADS, NOPE_DIM + ROPE_DIM)
    kv = (rms_norm(ckv, d_kv_lat_norm) @ d_w_ukv).reshape(bsz, seq, D_HEADS, NOPE_DIM + V_DIM)
    rcos, rsin = rope_tables(pos, ROPE_DIM)
    q_nope = rms_norm(q[..., :NOPE_DIM], d_nope_norm[0])
    q_rope = apply_rope(rms_norm(q[..., NOPE_DIM:], d_rope_norm[0]), rcos, rsin)
    k_nope = rms_norm(kv[..., :NOPE_DIM], d_nope_norm[1])
    k_rope = apply_rope(rms_norm(kr[:, :, None, :], d_rope_norm[1]), rcos, rsin)[:, :, 0]
    od = mla_attention(q_nope, q_rope, k_nope, k_rope, kv[..., NOPE_DIM:])
    mix = jnp.concatenate([oc * jax.nn.silu(zc), od * jax.nn.silu(zd)], axis=-1)
    return x + mix @ w_out


def setup_inputs(seed: int = 0) -> dict:
    key = jax.random.key(seed)
    k = jax.random.split(key, 22)
    f32 = jnp.float32

    def w(kk, shape, fan_in):
        return jax.random.normal(kk, shape, f32) * (fan_in ** -0.5)

    def gain(kk, shape):
        return 1.0 + 0.05 * jax.random.normal(kk, shape, f32)

    return {
        'x': jax.random.normal(k[0], (BATCH, SEQ, D_MODEL), f32),
        'ab_norm': gain(k[1], (N_EVEN, D_MODEL)),
        'ab_w_in': w(k[2], (N_EVEN, D_MODEL, AB_COLS), D_MODEL),
        'a_qk_norm': gain(k[3], (N_EVEN, 2, HEAD_DIM)),
        'a_kidx_norm': gain(k[4], (N_EVEN, IDX_DIM)),
        'b_qk_norm': gain(k[5], (N_EVEN, 2, HEAD_DIM)),
        'b_sinks': 0.5 * jax.random.normal(k[6], (N_EVEN, B_HEADS), f32),
        'ab_w_out': w(k[7], (N_EVEN, AB_WIDTH, D_MODEL), AB_WIDTH),
        'cd_norm': gain(k[8], (N_ODD, D_MODEL)),
        'cd_w_in': w(k[9], (N_ODD, D_MODEL, CD_COLS), D_MODEL),
        'c_q_norm': gain(k[10], (N_ODD, HEAD_DIM)),
        'c_k_norm': gain(k[11], (N_ODD, 3, HEAD_DIM)),
        'c_cmp_pe': 0.1 * jax.random.normal(k[12], (N_ODD, 2, CMP_LEN, HEAD_DIM), f32),
        'c_cmp_w1': w(k[13], (N_ODD, 2, CMP_LEN * HEAD_DIM, CMP_HIDDEN), CMP_LEN * HEAD_DIM),
        'c_cmp_w2': w(k[14], (N_ODD, 2, CMP_HIDDEN, HEAD_DIM), CMP_HIDDEN),
        'd_q_lat_norm': gain(k[15], (N_ODD, Q_LORA)),
        'd_kv_lat_norm': gain(k[16], (N_ODD, KV_LORA)),
        'd_w_uq': w(k[17], (N_ODD, Q_LORA, D_HEADS * (NOPE_DIM + ROPE_DIM)), Q_LORA),
        'd_w_ukv': w(k[18], (N_ODD, KV_LORA, D_HEADS * (NOPE_DIM + V_DIM)), KV_LORA),
        'd_nope_norm': gain(k[19], (N_ODD, 2, NOPE_DIM)),
        'd_rope_norm': gain(k[20], (N_ODD, 2, ROPE_DIM)),
        'cd_w_out': w(k[21], (N_ODD, CD_WIDTH, D_MODEL), CD_WIDTH),
    }


def reference(x, ab_norm, ab_w_in, a_qk_norm, a_kidx_norm, b_qk_norm, b_sinks, ab_w_out, cd_norm, cd_w_in, c_q_norm, c_k_norm, c_cmp_pe, c_cmp_w1, c_cmp_w2, d_q_lat_norm, d_kv_lat_norm, d_w_uq, d_w_ukv, d_nope_norm, d_rope_norm, cd_w_out):
    for layer in range(DEPTH):
        j = layer // 2
        if layer % 2 == 0:
            x = layer_ab(x, ab_norm[j], ab_w_in[j], a_qk_norm[j], a_kidx_norm[j], b_qk_norm[j], b_sinks[j], ab_w_out[j])
        else:
            x = layer_cd(x, cd_norm[j], cd_w_in[j], c_q_norm[j], c_k_norm[j], c_cmp_pe[j], c_cmp_w1[j], c_cmp_w2[j], d_q_lat_norm[j], d_kv_lat_norm[j], d_w_uq[j], d_w_ukv[j], d_nope_norm[j], d_rope_norm[j], cd_w_out[j])
    return x
```

```python
import functools

import jax
import jax.numpy as jnp
from jax import lax
from jax.experimental import pallas as pl
from jax.experimental.pallas import tpu as pltpu

F32, BF16, I32 = jnp.float32, jnp.bfloat16, jnp.int32

D_MODEL = 1024
HEAD_DIM = 64
N_HEADS = 8
N_KV = 2
GRP = N_HEADS // N_KV
IDX_HEADS = 4
DSA_TOPK = 256
SWA_WINDOW = 128
CMP_STRIDE = 16
SLC_BLOCK = 64
SLC_TOPN = 16
NSA_WINDOW = 512
Q_LORA = 256
KV_LORA = 128
NOPE_DIM = 64
ROPE_DIM = 32
ROPE_THETA = 10000.0
EPS = 1e-6

LANES = 128
TM = 512
TQ = 128
CH = 512
TQ_MLA = 512
VMEM_LIMIT = 56 * 1024 * 1024

NEG_INF = float("-inf")
M_INIT = -1e30
INT_MIN = -(2 ** 31)

AB_W = 2944


def _nn(a, b):
    return jnp.dot(a, b, preferred_element_type=F32)


def _nt(a, b):
    return lax.dot_general(a, b, (((1,), (1,)), ((), ())), preferred_element_type=F32)


def _sigmoid(z):
    return 1.0 / (1.0 + jnp.exp(-z))


def _silu(z):
    return z * _sigmoid(z)


def _row_rms(x, g):
    return x * lax.rsqrt(jnp.mean(x * x, axis=-1, keepdims=True) + EPS) * g


def _seg_norm(s, bd, gain, group):
    s2 = s * s
    hi = s2.astype(BF16)
    lo = (s2 - hi.astype(F32)).astype(BF16)
    ss = _nn(hi, bd) + _nn(lo, bd)
    return s * lax.rsqrt(ss * (1.0 / group) + EPS) * gain


def _rope(s, cos, sa, sb, half):
    return s * cos + pltpu.roll(s, half, 1) * sa + pltpu.roll(s, LANES - half, 1) * sb


def _order_key(v):
    bits = lax.bitcast_convert_type(v, I32)
    return bits ^ ((bits >> 31) & 0x7FFFFFFF)


def _stack_rows(x, n):
    return jnp.concatenate([x] * n, axis=0)


def _flash_update(s, v, m_ref, l_ref, acc_ref, rows):
    m_prev = m_ref[rows]
    m_new = jnp.maximum(m_prev, jnp.max(s, axis=1, keepdims=True))
    alpha = jnp.exp(m_prev - m_new)
    p = jnp.exp(s - m_new)
    l_ref[rows] = alpha * l_ref[rows] + jnp.sum(p, axis=1, keepdims=True)
    acc_ref[rows] = alpha * acc_ref[rows] + _nn(p.astype(BF16), v)
    m_ref[rows] = m_new


def _place_q_heads(y, bd, gain, cos, sa, sb, out_ref, scale):
    lo_half = lax.broadcasted_iota(I32, (y.shape[0], LANES), 1) < HEAD_DIM
    for j in range(4):
        s = y[:, LANES * j:LANES * (j + 1)]
        s = _rope(_seg_norm(s, bd, gain, HEAD_DIM), cos, sa, sb, HEAD_DIM // 2) * scale
        r = pltpu.roll(s, HEAD_DIM, 1)
        if j < 2:
            out_ref[2 * j] = jnp.where(lo_half, s, 0.0).astype(BF16)
            out_ref[2 * j + 1] = jnp.where(lo_half, r, 0.0).astype(BF16)
        else:
            out_ref[2 * j] = jnp.where(lo_half, 0.0, r).astype(BF16)
            out_ref[2 * j + 1] = jnp.where(lo_half, 0.0, s).astype(BF16)


def _gated_store(o, z_ref, o_ref, tq):
    lo_half = lax.broadcasted_iota(I32, (tq, LANES), 1) < HEAD_DIM
    for j in range(4):
        a = o[(2 * j) * tq:(2 * j + 1) * tq]
        b = o[(2 * j + 1) * tq:(2 * j + 2) * tq]
        if j < 2:
            slab = jnp.where(lo_half, a, pltpu.roll(b, HEAD_DIM, 1))
        else:
            slab = jnp.where(lo_half, pltpu.roll(a, HEAD_DIM, 1), b)
        z = z_ref[:, LANES * j:LANES * (j + 1)]
        o_ref[:, LANES * j:LANES * (j + 1)] = (slab * _silu(z)).astype(BF16)


def _proj_ab_kernel(x_ref, g_ref, w_ref, cos_ref, sa_ref, sb_ref, bd_ref, gains_ref,
                    qa_ref, ka_ref, va_ref, qi_ref, ki_ref, wi_ref, za_ref,
                    qb_ref, kb_ref, vb_ref, zb_ref):
    xn = _row_rms(x_ref[...], g_ref[...]).astype(BF16)
    cos, sa, sb, bd = cos_ref[...], sa_ref[...], sb_ref[...], bd_ref[...]
    lo_half = lax.broadcasted_iota(I32, (TM, LANES), 1) < HEAD_DIM

    def proj(c0, n):
        return _nn(xn, w_ref[:, c0:c0 + n])

    def k_slab(c0, gain):
        return _rope(_seg_norm(proj(c0, LANES), bd, gain, HEAD_DIM), cos, sa, sb, HEAD_DIM // 2)

    _place_q_heads(proj(0, 512), bd, gains_ref[0:1, :], cos, sa, sb, qa_ref, HEAD_DIM ** -0.5)
    ka_ref[...] = k_slab(512, gains_ref[1:2, :]).astype(BF16)
    va_ref[...] = proj(640, LANES).astype(BF16)
    qi = proj(768, 256)
    for j in range(2):
        s = _rope(qi[:, LANES * j:LANES * (j + 1)], cos, sa, sb, HEAD_DIM // 2)
        qi_ref[2 * j] = jnp.where(lo_half, s, 0.0).astype(BF16)
        qi_ref[2 * j + 1] = jnp.where(lo_half, pltpu.roll(s, HEAD_DIM, 1), 0.0).astype(BF16)
    kiw = proj(1024, LANES)
    wi_ref[...] = kiw
    ki_ref[...] = _rope(_seg_norm(kiw, bd, gains_ref[2:3, :], HEAD_DIM), cos, sa, sb, HEAD_DIM // 2).astype(BF16)
    za_ref[...] = proj(1152, 512)
    _place_q_heads(proj(1664, 512), bd, gains_ref[3:4, :], cos, sa, sb, qb_ref, HEAD_DIM ** -0.5)
    kb_ref[...] = k_slab(2176, gains_ref[4:5, :]).astype(BF16)
    vb_ref[...] = proj(2304, LANES).astype(BF16)
    zb_ref[...] = proj(2432, 512)


def _count_chunks(keys_ref, nch, rows, pred):
    lane = lax.broadcasted_iota(I32, (rows, LANES), 1)

    def body(c, acc):
        kk = keys_ref[c]
        for j in range(CH // LANES):
            col = lane + (c * CH + j * LANES)
            acc = acc + pred(kk[:, j * LANES:(j + 1) * LANES], col)
        return acc

    acc = lax.fori_loop(0, nch, body, jnp.zeros((rows, LANES), F32))
    return jnp.sum(acc, axis=1, keepdims=True)


def _kth_select(count, rows, k, idx_bits, idx_all):
    def bit_body(it, tb):
        cand_b = tb | jnp.left_shift(jnp.int32(1), 31 - it)
        cand_s = cand_b ^ INT_MIN
        cnt = count(lambda kk, col: jnp.where(kk >= cand_s, 1.0, 0.0))
        return jnp.where(cnt >= k, cand_b, tb)

    tb = lax.fori_loop(0, 32, bit_body, jnp.zeros((rows, 1), I32))
    thr = tb ^ INT_MIN
    need = k - count(lambda kk, col: jnp.where(kk > thr, 1.0, 0.0))
    n_eq = count(lambda kk, col: jnp.where(kk == thr, 1.0, 0.0))
    excess = jnp.max(n_eq - need) > 0.0

    def tie_search():
        def idx_body(it, xb):
            cand = xb | jnp.left_shift(jnp.int32(1), idx_bits - 1 - it)
            before = count(lambda kk, col: jnp.where(kk == thr, jnp.where(col < cand, 1.0, 0.0), 0.0))
            return jnp.where(before < need, cand, xb)

        return lax.fori_loop(0, idx_bits, idx_body, jnp.zeros((rows, 1), I32))

    xcut = lax.cond(excess, tie_search, lambda: jnp.full((rows, 1), idx_all, I32))
    return thr, xcut


def _dsa_kernel(qi_ref, wi_ref, ki_ref, qa_ref, ka_ref, va_ref, za_ref, o_ref,
                keys_ref, m_ref, l_ref, acc_ref, *, seq, topk):
    i = pl.program_id(1)
    t0 = i * TQ
    nch = (t0 + TQ + CH - 1) // CH
    row_c = t0 + lax.broadcasted_iota(I32, (TQ, CH), 0)
    col_c = lax.broadcasted_iota(I32, (TQ, CH), 1)
    row_l = t0 + lax.broadcasted_iota(I32, (TQ, LANES), 0)
    lane = lax.broadcasted_iota(I32, (TQ, LANES), 1)

    qi = qi_ref[...].reshape(IDX_HEADS * TQ, LANES)
    w = wi_ref[...]
    idx_scale = (HEAD_DIM * IDX_HEADS) ** -0.5
    ws = [w[:, HEAD_DIM + h:HEAD_DIM + h + 1] * idx_scale for h in range(IDX_HEADS)]

    def score_chunk(c, carry):
        off = pl.multiple_of(c * CH, CH)
        lg = _nt(qi, ki_ref[pl.ds(off, CH), :])
        sc = ws[0] * jnp.maximum(lg[0:TQ], 0.0)
        for h in range(1, IDX_HEADS):
            sc = sc + ws[h] * jnp.maximum(lg[h * TQ:(h + 1) * TQ], 0.0)
        sc = jnp.where(sc == 0.0, 0.0, sc)
        keys_ref[c] = jnp.where(col_c + off <= row_c, _order_key(sc), INT_MIN)
        return carry

    lax.fori_loop(0, nch, score_chunk, 0)

    count = functools.partial(_count_chunks, keys_ref, nch, TQ)
    thr, xcut = _kth_select(count, TQ, float(topk), seq.bit_length() - 1, seq)

    m_ref[...] = jnp.full(m_ref.shape, M_INIT, F32)
    l_ref[...] = jnp.zeros(l_ref.shape, F32)
    acc_ref[...] = jnp.zeros(acc_ref.shape, F32)
    q_all = qa_ref[...].reshape(N_HEADS * TQ, LANES)

    def att_chunk(c, carry):
        off = pl.multiple_of(c * CH, CH)
        kk = keys_ref[c]
        slabs = []
        for j in range(CH // LANES):
            ks = kk[:, j * LANES:(j + 1) * LANES]
            col = lane + (off + j * LANES)
            tie = jnp.where(ks == thr, jnp.where(col <= xcut, 0.0, NEG_INF), NEG_INF)
            slabs.append(jnp.where(col <= row_l, jnp.where(ks > thr, 0.0, tie), NEG_INF))
        bias = _stack_rows(jnp.concatenate(slabs, axis=1), N_HEADS)
        s = _nt(q_all, ka_ref[pl.ds(off, CH), :]) + bias
        _flash_update(s, va_ref[pl.ds(off, CH), :], m_ref, l_ref, acc_ref, slice(None))
        return carry

    lax.fori_loop(0, nch, att_chunk, 0)
    o = acc_ref[...] / jnp.maximum(l_ref[...], 1e-30)
    _gated_store(o, za_ref, o_ref, TQ)


def _swa_kernel(q_ref, kp_ref, kc_ref, vp_ref, vc_ref, sink_ref, z_ref, o_ref):
    i = pl.program_id(1)
    q = q_ref[...].reshape(N_HEADS * TQ, LANES)
    qi = _stack_rows(lax.broadcasted_iota(I32, (TQ, TQ), 0), N_HEADS)
    kj = lax.broadcasted_iota(I32, (N_HEADS * TQ, TQ), 1)
    no_prev = jnp.where(i > 0, 0, 1 << 20)
    sp = jnp.where(kj > qi + no_prev, _nt(q, kp_ref[...]), NEG_INF)
    sc = jnp.where(kj <= qi, _nt(q, kc_ref[...]), NEG_INF)
    sink = sink_ref[...]
    m = jnp.maximum(jnp.maximum(jnp.max(sp, axis=1, keepdims=True), jnp.max(sc, axis=1, keepdims=True)), sink)
    pp = jnp.exp(sp - m)
    pc = jnp.exp(sc - m)
    den = jnp.sum(pp, axis=1, keepdims=True) + jnp.sum(pc, axis=1, keepdims=True) + jnp.exp(sink - m)
    o = (_nn(pp.astype(BF16), vp_ref[...]) + _nn(pc.astype(BF16), vc_ref[...])) / den
    _gated_store(o, z_ref, o_ref, TQ)


def _out_proj_kernel(x_ref, ma_ref, mb_ref, w_ref, o_ref):
    half = ma_ref.shape[1]
    o_ref[...] = x_ref[...] + _nn(ma_ref[...], w_ref[0:half, :]) + _nn(mb_ref[...], w_ref[half:2 * half, :])


def _proj_cd_kernel(x_ref, g_ref, w_ref, cos_ref, sa_ref, sb_ref, bd_ref, gains_ref,
                    qc_ref, kc_ref, vc_ref, ks_ref, vs_ref, kw_ref, vw_ref, gt_ref, zc_ref,
                    cq_ref, ckv_ref, kr_ref, zd_ref):
    xn = _row_rms(x_ref[...], g_ref[...]).astype(BF16)
    cos, sa, sb, bd = cos_ref[...], sa_ref[...], sb_ref[...], bd_ref[...]

    def proj(c0, n):
        return _nn(xn, w_ref[:, c0:c0 + n])

    def k_slab(c0, gain):
        return _rope(_seg_norm(proj(c0, LANES), bd, gain, HEAD_DIM), cos, sa, sb, HEAD_DIM // 2)

    _place_q_heads(proj(0, 512), bd, gains_ref[0:1, :], cos, sa, sb, qc_ref, HEAD_DIM ** -0.5)
    kc_ref[...] = proj(512, LANES)
    vc_ref[...] = proj(640, LANES)
    ks_ref[...] = k_slab(768, gains_ref[1:2, :]).astype(BF16)
    vs_ref[...] = proj(896, LANES).astype(BF16)
    kw_ref[...] = k_slab(1024, gains_ref[2:3, :]).astype(BF16)
    vw_ref[...] = proj(1152, LANES).astype(BF16)
    gt_ref[...] = _sigmoid(proj(1280, LANES))
    zc_ref[...] = proj(1408, 512)
    cq_ref[...] = proj(1920, 256)
    ckv_ref[...] = proj(2176, LANES)
    kr_ref[...] = proj(2304, LANES)
    zd_ref[...] = proj(2432, 512)


def _compress_kernel(c_ref, pe_ref, w1a_ref, w1b_ref, w2_ref, gain_ref, cos_ref, sa_ref, sb_ref, bd_ref,
                     o_ref, *, is_key):
    c = c_ref[...]
    n = c.shape[0]
    a = _nn((c + pe_ref[0:1, :]).astype(BF16), w1a_ref[...])
    b = _nn((c + pe_ref[1:2, :]).astype(BF16), w1b_ref[...])
    h = _silu(a + pltpu.roll(b, n - 1, 0))
    y = _nn(h.astype(BF16), w2_ref[...])
    if is_key:
        y = _rope(_seg_norm(y, bd_ref[...], gain_ref[...], HEAD_DIM), cos_ref[...], sa_ref[...], sb_ref[...],
                  HEAD_DIM // 2)
    o_ref[...] = y.astype(BF16)


def _nsa_kernel(q_ref, kc_ref, vc_ref, ks_ref, vs_ref, kw_ref, vw_ref, gt_ref, z_ref, wsel_ref, o_ref,
                m_ref, l_ref, acc_ref, *, seq, n_sel):
    i = pl.program_id(1)
    t0 = i * TQ
    nch = (t0 + TQ + CH - 1) // CH
    ncmp = kc_ref.shape[0]
    rows_all = N_HEADS * TQ
    q = q_ref[...].reshape(rows_all, LANES)

    t_cmp = _stack_rows(t0 + lax.broadcasted_iota(I32, (TQ, ncmp), 0), N_HEADS)
    cend = lax.broadcasted_iota(I32, (rows_all, ncmp), 1) * CMP_STRIDE + (2 * CMP_STRIDE - 1)
    s = jnp.where(cend <= t_cmp, _nt(q, kc_ref[...]), NEG_INF)
    m = jnp.max(s, axis=1, keepdims=True)
    m = jnp.where(m > NEG_INF, m, 0.0)
    p = jnp.exp(s - m)
    p = p / jnp.maximum(jnp.sum(p, axis=1, keepdims=True), 1e-30)
    o_cmp = _nn(p.astype(BF16), vc_ref[...])

    lane = lax.broadcasted_iota(I32, (TQ, LANES), 1)
    row_l = t0 + lax.broadcasted_iota(I32, (TQ, LANES), 0)
    cur = row_l >> 6
    admiss = lane <= cur
    wsel = wsel_ref[...]
    keys = []
    for g in range(N_KV):
        imp = p[(GRP * g) * TQ:(GRP * g + 1) * TQ]
        for r in range(1, GRP):
            imp = imp + p[(GRP * g + r) * TQ:(GRP * g + r + 1) * TQ]
        hi = imp.astype(BF16)
        lo = (imp - hi.astype(F32)).astype(BF16)
        imp_s = _nn(hi, wsel) + _nn(lo, wsel)
        forced = jnp.where(lane == 0, jnp.inf, jnp.where(lane >= cur - 1, jnp.inf, imp_s))
        keys.append(_order_key(jnp.where(admiss, forced, NEG_INF)))
    kk = jnp.concatenate(keys, axis=0)
    lane2 = _stack_rows(lane, N_KV)

    def count(pred):
        return jnp.sum(pred(kk, lane2), axis=1, keepdims=True)

    thr, xcut = _kth_select(count, N_KV * TQ, float(n_sel), 7, LANES)
    tie = jnp.where(kk == thr, jnp.where(lane2 <= xcut, 1.0, 0.0), 0.0)
    sel = jnp.where(kk > thr, 1.0, tie).astype(BF16)

    m_ref[...] = jnp.full(m_ref.shape, M_INIT, F32)
    l_ref[...] = jnp.zeros(l_ref.shape, F32)
    acc_ref[...] = jnp.zeros(acc_ref.shape, F32)
    blk_row = lax.broadcasted_iota(I32, (LANES, CH), 0)
    blk_col = lax.broadcasted_iota(I32, (LANES, CH), 1) >> 6
    row_c = t0 + lax.broadcasted_iota(I32, (TQ, CH), 0)
    col_c = lax.broadcasted_iota(I32, (TQ, CH), 1)

    def slc_chunk(c, carry):
        off = pl.multiple_of(c * CH, CH)
        expand = jnp.where(blk_row == blk_col + c * (CH // SLC_BLOCK), 1.0, 0.0).astype(BF16)
        tok = _nn(sel, expand)
        causal = col_c + off <= row_c
        parts = []
        for g in range(N_KV):
            bias_g = jnp.where(causal, jnp.where(tok[g * TQ:(g + 1) * TQ] > 0.5, 0.0, NEG_INF), NEG_INF)
            parts.extend([bias_g] * GRP)
        sc = _nt(q, ks_ref[pl.ds(off, CH), :]) + jnp.concatenate(parts, axis=0)
        _flash_update(sc, vs_ref[pl.ds(off, CH), :], m_ref, l_ref, acc_ref, slice(None))
        return carry

    lax.fori_loop(0, nch, slc_chunk, 0)
    o_slc = acc_ref[...] / jnp.maximum(l_ref[...], 1e-30)

    span = NSA_WINDOW + TQ
    start = pl.multiple_of(jnp.maximum(t0 - NSA_WINDOW, 0), TQ)
    t_w = _stack_rows(t0 + lax.broadcasted_iota(I32, (TQ, span), 0), N_HEADS)
    diff = t_w - (start + lax.broadcasted_iota(I32, (rows_all, span), 1))
    sw = _nt(q, kw_ref[pl.ds(start, span), :])
    sw = jnp.where(diff >= 0, jnp.where(diff < NSA_WINDOW, sw, NEG_INF), NEG_INF)
    pw = jnp.exp(sw - jnp.max(sw, axis=1, keepdims=True))
    o_win = _nn(pw.astype(BF16), vw_ref[pl.ds(start, span), :]) / jnp.sum(pw, axis=1, keepdims=True)

    gates = gt_ref[...]
    outs = []
    for h in range(N_HEADS):
        rs = slice(h * TQ, (h + 1) * TQ)
        outs.append(gates[:, 3 * h:3 * h + 1] * o_cmp[rs] + gates[:, 3 * h + 1:3 * h + 2] * o_slc[rs]
                    + gates[:, 3 * h + 2:3 * h + 3] * o_win[rs])
    _gated_store(jnp.concatenate(outs, axis=0), z_ref, o_ref, TQ)


def _mla_prep_kernel(cq_ref, ckv_ref, kr_ref, wq_ref, wkv_ref, lat_ref, gains_ref, cos_ref, sa_ref, sb_ref,
                     bd64_ref, bd32_ref, q_ref, k_ref, v_ref):
    lane = lax.broadcasted_iota(I32, (TM, LANES), 1)
    lo_half = lane < NOPE_DIM
    cos, sa, sb = cos_ref[...], sa_ref[...], sb_ref[...]
    bd64, bd32 = bd64_ref[...], bd32_ref[...]
    scale = (NOPE_DIM + ROPE_DIM) ** -0.5

    def rope_slab(s, gain):
        return _rope(_seg_norm(s, bd32, gain, ROPE_DIM), cos, sa, sb, ROPE_DIM // 2)

    q = _nn(_row_rms(cq_ref[...], lat_ref[0:1, :]).astype(BF16), wq_ref[...])
    kv = _nn(_row_rms(ckv_ref[...], lat_ref[1:2, 0:KV_LORA]).astype(BF16), wkv_ref[...])
    k_rope = pltpu.roll(rope_slab(kr_ref[...], gains_ref[3:4, :]), NOPE_DIM, 1)
    q_rope = [rope_slab(q[:, 512 + LANES * j:512 + LANES * (j + 1)], gains_ref[1:2, :]) * scale for j in range(2)]
    for j in range(4):
        qn = _seg_norm(q[:, LANES * j:LANES * (j + 1)], bd64, gains_ref[0:1, :], NOPE_DIM) * scale
        kn = _seg_norm(kv[:, LANES * j:LANES * (j + 1)], bd64, gains_ref[2:3, :], NOPE_DIM)
        for e in range(2):
            h = 2 * j + e
            qn_h = qn if e == 0 else pltpu.roll(qn, NOPE_DIM, 1)
            kn_h = kn if e == 0 else pltpu.roll(kn, NOPE_DIM, 1)
            shift = (NOPE_DIM - ROPE_DIM * (h % 4)) % LANES
            qr = q_rope[h // 4]
            qr_h = qr if shift == 0 else pltpu.roll(qr, shift, 1)
            q_ref[h] = jnp.where(lo_half, qn_h, jnp.where(lane < NOPE_DIM + ROPE_DIM, qr_h, 0.0)).astype(BF16)
            k_ref[h] = jnp.where(lo_half, kn_h, k_rope).astype(BF16)
    v_ref[...] = kv[:, 512:1024].astype(BF16)


def _mla_kernel(q_ref, k_ref, v_ref, z_ref, o_ref, m_ref, l_ref, acc_ref):
    i = pl.program_id(2)
    tq = TQ_MLA
    m_ref[...] = jnp.full(m_ref.shape, M_INIT, F32)
    l_ref[...] = jnp.zeros(l_ref.shape, F32)
    acc_ref[...] = jnp.zeros(acc_ref.shape, F32)

    def step(c, masked):
        off = pl.multiple_of(c * CH, CH)
        v = v_ref[pl.ds(off, CH), :]
        for e in range(2):
            s = _nt(q_ref[e], k_ref[e, pl.ds(off, CH), :])
            if masked:
                s = jnp.where(lax.broadcasted_iota(I32, (tq, CH), 1) <= lax.broadcasted_iota(I32, (tq, CH), 0),
                              s, NEG_INF)
            _flash_update(s, v, m_ref, l_ref, acc_ref, slice(e * tq, (e + 1) * tq))

    def body(c, carry):
        step(c, False)
        return carry

    lax.fori_loop(0, i, body, 0)
    step(i, True)
    o = acc_ref[...] / jnp.maximum(l_ref[...], 1e-30)
    lo_half = lax.broadcasted_iota(I32, (tq, LANES), 1) < NOPE_DIM
    slab = jnp.where(lo_half, o[0:tq], o[tq:2 * tq])
    o_ref[...] = (slab * _silu(z_ref[...])).astype(BF16)


def _rope_tables(pos, dim):
    inv = jnp.power(jnp.float32(ROPE_THETA), -jnp.arange(0, dim, 2, dtype=F32) / dim)
    ang = pos.astype(F32)[:, None] * inv[None, :]
    cos, sin = jnp.cos(ang), jnp.sin(ang)
    reps = LANES // dim
    zero = jnp.zeros_like(sin)
    cos_t = jnp.tile(jnp.concatenate([cos, cos], axis=1), (1, reps))
    sa_t = jnp.tile(jnp.concatenate([zero, sin], axis=1), (1, reps))
    sb_t = jnp.tile(jnp.concatenate([-sin, zero], axis=1), (1, reps))
    return cos_t, sa_t, sb_t


def _block_diag(group):
    r = jnp.arange(LANES) // group
    return (r[:, None] == r[None, :]).astype(BF16)


def _tile_gain(g, width=LANES):
    return jnp.tile(g.astype(F32), width // g.shape[0])


def _pad_lanes(g, width=LANES):
    return jnp.concatenate([g.astype(F32), jnp.zeros((width - g.shape[0],), F32)])


def _pad_cols(w, width):
    return jnp.concatenate([w, jnp.zeros((w.shape[0], width - w.shape[1]), w.dtype)], axis=1)


def _params(*sem):
    return pltpu.CompilerParams(dimension_semantics=sem, vmem_limit_bytes=VMEM_LIMIT)


def kernel(x, ab_norm, ab_w_in, a_qk_norm, a_kidx_norm, b_qk_norm, b_sinks, ab_w_out, cd_norm, cd_w_in, c_q_norm, c_k_norm, c_cmp_pe, c_cmp_w1, c_cmp_w2, d_q_lat_norm, d_kv_lat_norm, d_w_uq, d_w_ukv, d_nope_norm, d_rope_norm, cd_w_out):
    bsz, seq, d = x.shape
    n = bsz * seq
    assert d == D_MODEL and seq % CH == 0 and seq >= NSA_WINDOW + TQ and seq // SLC_BLOCK <= LANES
    nq = seq // TQ
    nrow = n // TM
    srow = seq // TM
    xf = x.reshape(n, d)
    pos = jnp.arange(seq)
    cos64, sa64, sb64 = _rope_tables(pos, HEAD_DIM)
    cos32, sa32, sb32 = _rope_tables(pos, ROPE_DIM)
    bd64, bd32 = _block_diag(HEAD_DIM), _block_diag(ROPE_DIM)

    def row_spec(width, rows=TM):
        return pl.BlockSpec((rows, width), lambda i: (i, 0))

    def const_spec(shape):
        return pl.BlockSpec(shape, lambda i: tuple(0 for _ in shape))

    tab_spec = pl.BlockSpec((TM, LANES), lambda i: (i % srow, 0))
    head_spec = lambda nh: pl.BlockSpec((nh, TM, LANES), lambda i: (0, i, 0))

    wa = ab_w_in[0]
    w_ab = jnp.concatenate([wa[:, 0:1088], _pad_cols(wa[:, 1088:1092], 64), wa[:, 1092:]], axis=1).astype(BF16)
    gains_ab = jnp.stack([_tile_gain(a_qk_norm[0, 0]), _tile_gain(a_qk_norm[0, 1]), _pad_lanes(a_kidx_norm[0]),
                          _tile_gain(b_qk_norm[0, 0]), _tile_gain(b_qk_norm[0, 1])]
                         + [jnp.zeros((LANES,), F32)] * 3)
    sds = jax.ShapeDtypeStruct
    qa, ka, va, qi, ki, wi, za, qb, kb, vb, zb = pl.pallas_call(
        _proj_ab_kernel,
        grid=(nrow,),
        in_specs=[row_spec(d), const_spec((1, d)), const_spec((d, AB_W)), tab_spec, tab_spec, tab_spec,
                  const_spec((LANES, LANES)), const_spec((8, LANES))],
        out_specs=[head_spec(8), row_spec(LANES), row_spec(LANES), head_spec(4), row_spec(LANES), row_spec(LANES),
                   row_spec(512), head_spec(8), row_spec(LANES), row_spec(LANES), row_spec(512)],
        out_shape=[sds((8, n, LANES), BF16), sds((n, LANES), BF16), sds((n, LANES), BF16), sds((4, n, LANES), BF16),
                   sds((n, LANES), BF16), sds((n, LANES), F32), sds((n, 512), F32), sds((8, n, LANES), BF16),
                   sds((n, LANES), BF16), sds((n, LANES), BF16), sds((n, 512), F32)],
        compiler_params=_params("parallel"),
    )(xf, ab_norm[0][None, :], w_ab, cos64, sa64, sb64, bd64, gains_ab)

    qt_heads = lambda nh: pl.BlockSpec((nh, TQ, LANES), lambda b, i: (0, b * nq + i, 0))
    qt_rows = lambda width: pl.BlockSpec((TQ, width), lambda b, i: (b * nq + i, 0))
    seq_rows = pl.BlockSpec((seq, LANES), lambda b, i: (b, 0))
    att_scratch = [pltpu.VMEM((N_HEADS * TQ, 1), F32), pltpu.VMEM((N_HEADS * TQ, 1), F32),
                   pltpu.VMEM((N_HEADS * TQ, LANES), F32)]

    mix_a = pl.pallas_call(
        functools.partial(_dsa_kernel, seq=seq, topk=min(DSA_TOPK, seq // 4)),
        grid=(bsz, nq),
        in_specs=[qt_heads(4), qt_rows(LANES), seq_rows, qt_heads(8), seq_rows, seq_rows, qt_rows(512)],
        out_specs=qt_rows(512),
        out_shape=sds((n, 512), BF16),
        scratch_shapes=[pltpu.VMEM((seq // CH, TQ, CH), I32)] + att_scratch,
        compiler_params=_params("parallel", "arbitrary"),
    )(qi, wi, ki, qa, ka, va, za)

    prev_rows = pl.BlockSpec((TQ, LANES), lambda b, i: (b * nq + jnp.maximum(i - 1, 0), 0))
    sink_rows = jnp.repeat(b_sinks[0].astype(F32), TQ)[:, None]
    mix_b = pl.pallas_call(
        _swa_kernel,
        grid=(bsz, nq),
        in_specs=[qt_heads(8), prev_rows, qt_rows(LANES), prev_rows, qt_rows(LANES),
                  pl.BlockSpec((N_HEADS * TQ, 1), lambda b, i: (0, 0)), qt_rows(512)],
        out_specs=qt_rows(512),
        out_shape=sds((n, 512), BF16),
        compiler_params=_params("parallel", "arbitrary"),
    )(qb, kb, kb, vb, vb, sink_rows, zb)

    def out_proj(xin, m0, m1, w):
        return pl.pallas_call(
            _out_proj_kernel,
            grid=(nrow,),
            in_specs=[row_spec(d), row_spec(512), row_spec(512), const_spec((2 * 512, d))],
            out_specs=row_spec(d),
            out_shape=sds((n, d), F32),
            compiler_params=_params("parallel"),
        )(xin, m0, m1, w.astype(BF16))

    x1 = out_proj(xf, mix_a, mix_b, ab_w_out[0])

    wc = cd_w_in[0]
    w_cd = jnp.concatenate([wc[:, 0:1280], _pad_cols(wc[:, 1280:1304], LANES), wc[:, 1304:2200],
                            _pad_cols(wc[:, 2200:2232], LANES), wc[:, 2232:]], axis=1).astype(BF16)
    gains_cd = jnp.stack([_tile_gain(c_q_norm[0]), _tile_gain(c_k_norm[0, 1]), _tile_gain(c_k_norm[0, 2])]
                         + [jnp.zeros((LANES,), F32)] * 5)
    qc, kc_raw, vc_raw, ks, vs, kw, vw, gates, zc, cq, ckv, kr, zd = pl.pallas_call(
        _proj_cd_kernel,
        grid=(nrow,),
        in_specs=[row_spec(d), const_spec((1, d)), const_spec((d, AB_W)), tab_spec, tab_spec, tab_spec,
                  const_spec((LANES, LANES)), const_spec((8, LANES))],
        out_specs=[head_spec(8), row_spec(LANES), row_spec(LANES), row_spec(LANES), row_spec(LANES),
                   row_spec(LANES), row_spec(LANES), row_spec(LANES), row_spec(512), row_spec(256),
                   row_spec(LANES), row_spec(LANES), row_spec(512)],
        out_shape=[sds((8, n, LANES), BF16), sds((n, LANES), F32), sds((n, LANES), F32), sds((n, LANES), BF16),
                   sds((n, LANES), BF16), sds((n, LANES), BF16), sds((n, LANES), BF16), sds((n, LANES), F32),
                   sds((n, 512), F32), sds((n, 256), F32), sds((n, LANES), F32), sds((n, LANES), F32),
                   sds((n, 512), F32)],
        compiler_params=_params("parallel"),
    )(x1, cd_norm[0][None, :], w_cd, cos64, sa64, sb64, bd64, gains_cd)

    ncmp = seq // CMP_STRIDE
    cw = CMP_STRIDE * N_KV * HEAD_DIM
    eye = jnp.eye(N_KV, dtype=F32)
    ccos, csa, csb = _rope_tables(jnp.arange(ncmp) * CMP_STRIDE + (2 * CMP_STRIDE - 1), HEAD_DIM)

    def compress(raw, which, is_key):
        w1 = c_cmp_w1[0, which].reshape(2, CMP_STRIDE, HEAD_DIM, -1)
        hid = w1.shape[-1]
        w1e = jnp.einsum("tjdu,gh->tjgdhu", w1, eye).reshape(2, cw, N_KV * hid).astype(BF16)
        w2e = jnp.einsum("ud,gh->guhd", c_cmp_w2[0, which], eye).reshape(N_KV * hid, N_KV * HEAD_DIM).astype(BF16)
        pe = c_cmp_pe[0, which].reshape(2, CMP_STRIDE, 1, HEAD_DIM)
        pe = jnp.broadcast_to(pe, (2, CMP_STRIDE, N_KV, HEAD_DIM)).reshape(2, cw)
        return pl.pallas_call(
            functools.partial(_compress_kernel, is_key=is_key),
            grid=(bsz,),
            in_specs=[pl.BlockSpec((ncmp, cw), lambda b: (b, 0)), const_spec((2, cw)),
                      const_spec((cw, N_KV * hid)), const_spec((cw, N_KV * hid)),
                      const_spec((N_KV * hid, LANES)), const_spec((1, LANES)),
                      const_spec((ncmp, LANES)), const_spec((ncmp, LANES)), const_spec((ncmp, LANES)),
                      const_spec((LANES, LANES))],
            out_specs=pl.BlockSpec((ncmp, LANES), lambda b: (b, 0)),
            out_shape=sds((bsz * ncmp, LANES), BF16),
            compiler_params=_params("parallel"),
        )(raw.reshape(bsz * ncmp, cw), pe, w1e[0], w1e[1], w2e, _tile_gain(c_k_norm[0, 0])[None, :],
          ccos, csa, csb, bd64)

    kc = compress(kc_raw, 0, True)
    vc = compress(vc_raw, 1, False)

    ratio = SLC_BLOCK // CMP_STRIDE
    mm = jnp.arange(ncmp)[:, None]
    jj = jnp.arange(LANES)[None, :]
    wsel = (jnp.where((mm == ratio * jj - 1) | (mm == ratio * jj + ratio - 1), 1.0, 0.0)
            + jnp.where((mm >= ratio * jj) & (mm < ratio * jj + ratio - 1), 2.0, 0.0))
    wsel = jnp.where((mm < ncmp - 1) & (jj < seq // SLC_BLOCK), wsel, 0.0).astype(BF16)

    cmp_rows = pl.BlockSpec((ncmp, LANES), lambda b, i: (b, 0))
    mix_c = pl.pallas_call(
        functools.partial(_nsa_kernel, seq=seq, n_sel=min(SLC_TOPN, seq // SLC_BLOCK)),
        grid=(bsz, nq),
        in_specs=[qt_heads(8), cmp_rows, cmp_rows, seq_rows, seq_rows, seq_rows, seq_rows, qt_rows(LANES),
                  qt_rows(512), pl.BlockSpec((ncmp, LANES), lambda b, i: (0, 0))],
        out_specs=qt_rows(512),
        out_shape=sds((n, 512), BF16),
        scratch_shapes=att_scratch,
        compiler_params=_params("parallel", "arbitrary"),
    )(qc, kc, vc, ks, vs, kw, vw, gates, zc, wsel)

    wq = d_w_uq[0].reshape(Q_LORA, N_HEADS, NOPE_DIM + ROPE_DIM)
    wq = jnp.concatenate([wq[:, :, :NOPE_DIM].reshape(Q_LORA, -1), wq[:, :, NOPE_DIM:].reshape(Q_LORA, -1)],
                         axis=1).astype(BF16)
    wkv = d_w_ukv[0].reshape(KV_LORA, N_HEADS, NOPE_DIM + HEAD_DIM)
    wkv = jnp.concatenate([wkv[:, :, :NOPE_DIM].reshape(KV_LORA, -1), wkv[:, :, NOPE_DIM:].reshape(KV_LORA, -1)],
                          axis=1).astype(BF16)
    lat_gains = jnp.stack([d_q_lat_norm[0].astype(F32), _pad_lanes(d_kv_lat_norm[0], Q_LORA)]
                          + [jnp.zeros((Q_LORA,), F32)] * 6)
    gains_d = jnp.stack([_tile_gain(d_nope_norm[0, 0]), _tile_gain(d_rope_norm[0, 0]), _tile_gain(d_nope_norm[0, 1]),
                         _pad_lanes(d_rope_norm[0, 1])] + [jnp.zeros((LANES,), F32)] * 4)
    q_cat, k_cat, v_d = pl.pallas_call(
        _mla_prep_kernel,
        grid=(nrow,),
        in_specs=[row_spec(Q_LORA), row_spec(LANES), row_spec(LANES), const_spec((Q_LORA, 768)),
                  const_spec((KV_LORA, 1024)), const_spec((8, Q_LORA)), const_spec((8, LANES)),
                  tab_spec, tab_spec, tab_spec, const_spec((LANES, LANES)), const_spec((LANES, LANES))],
        out_specs=[head_spec(8), head_spec(8), row_spec(512)],
        out_shape=[sds((8, n, LANES), BF16), sds((8, n, LANES), BF16), sds((n, 512), BF16)],
        compiler_params=_params("parallel"),
    )(cq, ckv, kr, wq, wkv, lat_gains, gains_d, cos32, sa32, sb32, bd64, bd32)

    nqm = seq // TQ_MLA
    mix_d = pl.pallas_call(
        _mla_kernel,
        grid=(bsz, N_HEADS // 2, nqm),
        in_specs=[pl.BlockSpec((2, TQ_MLA, LANES), lambda b, hp, i: (hp, b * nqm + i, 0)),
                  pl.BlockSpec((2, seq, LANES), lambda b, hp, i: (hp, b, 0)),
                  pl.BlockSpec((seq, LANES), lambda b, hp, i: (b, hp)),
                  pl.BlockSpec((TQ_MLA, LANES), lambda b, hp, i: (b * nqm + i, hp))],
        out_specs=pl.BlockSpec((TQ_MLA, LANES), lambda b, hp, i: (b * nqm + i, hp)),
        out_shape=sds((n, 512), BF16),
        scratch_shapes=[pltpu.VMEM((2 * TQ_MLA, 1), F32), pltpu.VMEM((2 * TQ_MLA, 1), F32),
                        pltpu.VMEM((2 * TQ_MLA, LANES), F32)],
        compiler_params=_params("parallel", "parallel", "arbitrary"),
    )(q_cat, k_cat, v_d, zd)

    x2 = out_proj(x1, mix_c, mix_d, cd_w_out[0])
    return x2.reshape(bsz, seq, d)
```

```python
import functools

import jax
import jax.numpy as jnp
from jax import lax
from jax.experimental import pallas as pl
from jax.experimental.pallas import tpu as pltpu

F32, BF16, I32 = jnp.float32, jnp.bfloat16, jnp.int32

D_MODEL = 1024
HEAD_DIM = 64
N_HEADS = 8
N_KV = 2
GRP = N_HEADS // N_KV
IDX_HEADS = 4
DSA_TOPK = 256
SWA_WINDOW = 128
CMP_STRIDE = 16
SLC_BLOCK = 64
SLC_TOPN = 16
NSA_WINDOW = 512
Q_LORA = 256
KV_LORA = 128
NOPE_DIM = 64
ROPE_DIM = 32
ROPE_THETA = 10000.0
EPS = 1e-6

LANES = 128
TM = 512
TQ = 128
CH = 512
TQ_MLA = 512
VMEM_LIMIT = 56 * 1024 * 1024

NEG_INF = float("-inf")
M_INIT = -1e30
INT_MIN = -(2 ** 31)

AB_W = 2944


def _nn(a, b):
    return jnp.dot(a, b, preferred_element_type=F32)


def _nt(a, b):
    return lax.dot_general(a, b, (((1,), (1,)), ((), ())), preferred_element_type=F32)


def _sigmoid(z):
    return 1.0 / (1.0 + jnp.exp(-z))


def _silu(z):
    return z * _sigmoid(z)


def _row_rms(x, g):
    return x * lax.rsqrt(jnp.mean(x * x, axis=-1, keepdims=True) + EPS) * g


def _seg_norm(s, bd, gain, group):
    s2 = s * s
    hi = s2.astype(BF16)
    lo = (s2 - hi.astype(F32)).astype(BF16)
    ss = _nn(hi, bd) + _nn(lo, bd)
    return s * lax.rsqrt(ss * (1.0 / group) + EPS) * gain


def _rope(s, cos, sa, sb, half):
    return s * cos + pltpu.roll(s, half, 1) * sa + pltpu.roll(s, LANES - half, 1) * sb


def _order_key(v):
    bits = lax.bitcast_convert_type(v, I32)
    return bits ^ ((bits >> 31) & 0x7FFFFFFF)


def _stack_rows(x, n):
    return jnp.concatenate([x] * n, axis=0)


def _with_ones(v):
    return jnp.concatenate([v.astype(BF16), jnp.ones(v.shape, BF16)], axis=1)


def _flash_update(s, v_ext, m_ref, acc_ref, rows):
    m_prev = m_ref[rows]
    m_new = jnp.maximum(m_prev, jnp.max(s, axis=1, keepdims=True))
    alpha = jnp.exp(m_prev - m_new)
    p = jnp.exp(s - jnp.concatenate([m_new] * (s.shape[1] // LANES), axis=1))
    acc_ref[rows] = jnp.concatenate([alpha, alpha], axis=1) * acc_ref[rows] + _nn(p.astype(BF16), v_ext)
    m_ref[rows] = m_new


def _flash_init(m_ref, acc_ref):
    m_ref[...] = jnp.full(m_ref.shape, M_INIT, F32)
    acc_ref[...] = jnp.zeros(acc_ref.shape, F32)


def _flash_result(acc_ref):
    acc = acc_ref[...]
    return acc[:, 0:LANES] / jnp.maximum(acc[:, LANES:2 * LANES], 1e-30)


def _place_q_heads(y, bd, gain, cos, sa, sb, out_ref, scale):
    lo_half = lax.broadcasted_iota(I32, (y.shape[0], LANES), 1) < HEAD_DIM
    for j in range(4):
        s = y[:, LANES * j:LANES * (j + 1)]
        s = _rope(_seg_norm(s, bd, gain, HEAD_DIM), cos, sa, sb, HEAD_DIM // 2) * scale
        r = pltpu.roll(s, HEAD_DIM, 1)
        if j < 2:
            out_ref[2 * j] = jnp.where(lo_half, s, 0.0).astype(BF16)
            out_ref[2 * j + 1] = jnp.where(lo_half, r, 0.0).astype(BF16)
        else:
            out_ref[2 * j] = jnp.where(lo_half, 0.0, r).astype(BF16)
            out_ref[2 * j + 1] = jnp.where(lo_half, 0.0, s).astype(BF16)


def _gated_store(o, z_ref, o_ref, tq):
    lo_half = lax.broadcasted_iota(I32, (tq, LANES), 1) < HEAD_DIM
    for j in range(4):
        a = o[(2 * j) * tq:(2 * j + 1) * tq]
        b = o[(2 * j + 1) * tq:(2 * j + 2) * tq]
        if j < 2:
            slab = jnp.where(lo_half, a, pltpu.roll(b, HEAD_DIM, 1))
        else:
            slab = jnp.where(lo_half, pltpu.roll(a, HEAD_DIM, 1), b)
        z = z_ref[:, LANES * j:LANES * (j + 1)]
        o_ref[:, LANES * j:LANES * (j + 1)] = (slab * _silu(z)).astype(BF16)


def _proj_ab_kernel(x_ref, g_ref, w_ref, cos_ref, sa_ref, sb_ref, bd_ref, gains_ref,
                    qa_ref, ka_ref, va_ref, qi_ref, ki_ref, wi_ref, za_ref,
                    qb_ref, kb_ref, vb_ref, zb_ref):
    xn = _row_rms(x_ref[...], g_ref[...]).astype(BF16)
    cos, sa, sb, bd = cos_ref[...], sa_ref[...], sb_ref[...], bd_ref[...]
    lo_half = lax.broadcasted_iota(I32, (TM, LANES), 1) < HEAD_DIM

    def proj(c0, n):
        return _nn(xn, w_ref[:, c0:c0 + n])

    def k_slab(c0, gain):
        return _rope(_seg_norm(proj(c0, LANES), bd, gain, HEAD_DIM), cos, sa, sb, HEAD_DIM // 2)

    _place_q_heads(proj(0, 512), bd, gains_ref[0:1, :], cos, sa, sb, qa_ref, HEAD_DIM ** -0.5)
    ka_ref[...] = k_slab(512, gains_ref[1:2, :]).astype(BF16)
    va_ref[...] = _with_ones(proj(640, LANES))
    qi = proj(768, 256)
    for j in range(2):
        s = _rope(qi[:, LANES * j:LANES * (j + 1)], cos, sa, sb, HEAD_DIM // 2)
        qi_ref[2 * j] = jnp.where(lo_half, s, 0.0).astype(BF16)
        qi_ref[2 * j + 1] = jnp.where(lo_half, pltpu.roll(s, HEAD_DIM, 1), 0.0).astype(BF16)
    kiw = proj(1024, LANES)
    wi_ref[...] = kiw
    ki_ref[...] = _rope(_seg_norm(kiw, bd, gains_ref[2:3, :], HEAD_DIM), cos, sa, sb, HEAD_DIM // 2).astype(BF16)
    za_ref[...] = proj(1152, 512)
    _place_q_heads(proj(1664, 512), bd, gains_ref[3:4, :], cos, sa, sb, qb_ref, HEAD_DIM ** -0.5)
    kb_ref[...] = k_slab(2176, gains_ref[4:5, :]).astype(BF16)
    vb_ref[...] = _with_ones(proj(2304, LANES))
    zb_ref[...] = proj(2432, 512)


def _count_chunks(keys_ref, nch, rows, pred):
    lane = lax.broadcasted_iota(I32, (rows, LANES), 1)

    def body(c, acc):
        kk = keys_ref[c]
        for j in range(CH // LANES):
            col = lane + (c * CH + j * LANES)
            acc = acc + pred(kk[:, j * LANES:(j + 1) * LANES], col)
        return acc

    acc = lax.fori_loop(0, nch, body, jnp.zeros((rows, LANES), F32))
    return jnp.sum(acc, axis=1, keepdims=True)


def _kth_select(count, rows, k, idx_bits, idx_all):
    def bit_body(it, tb):
        cand_b = tb | jnp.left_shift(jnp.int32(1), 31 - it)
        cand_s = cand_b ^ INT_MIN
        cnt = count(lambda kk, col: jnp.where(kk >= cand_s, 1.0, 0.0))
        return jnp.where(cnt >= k, cand_b, tb)

    tb = lax.fori_loop(0, 32, bit_body, jnp.zeros((rows, 1), I32))
    thr = tb ^ INT_MIN
    need = k - count(lambda kk, col: jnp.where(kk > thr, 1.0, 0.0))
    n_eq = count(lambda kk, col: jnp.where(kk == thr, 1.0, 0.0))
    excess = jnp.max(n_eq - need) > 0.0

    def tie_search():
        def idx_body(it, xb):
            cand = xb | jnp.left_shift(jnp.int32(1), idx_bits - 1 - it)
            before = count(lambda kk, col: jnp.where(kk == thr, jnp.where(col < cand, 1.0, 0.0), 0.0))
            return jnp.where(before < need, cand, xb)

        return lax.fori_loop(0, idx_bits, idx_body, jnp.zeros((rows, 1), I32))

    xcut = lax.cond(excess, tie_search, lambda: jnp.full((rows, 1), idx_all, I32))
    return thr, xcut


def _dsa_kernel(qi_ref, wi_ref, ki_ref, qa_ref, ka_ref, va_ref, za_ref, o_ref,
                keys_ref, m_ref, acc_ref, *, seq, topk):
    i = pl.program_id(1)
    t0 = i * TQ
    nch = (t0 + TQ + CH - 1) // CH
    row_c = t0 + lax.broadcasted_iota(I32, (TQ, CH), 0)
    col_c = lax.broadcasted_iota(I32, (TQ, CH), 1)
    row_l = t0 + lax.broadcasted_iota(I32, (TQ, LANES), 0)
    lane = lax.broadcasted_iota(I32, (TQ, LANES), 1)

    qi = qi_ref[...].reshape(IDX_HEADS * TQ, LANES)
    w = wi_ref[...]
    idx_scale = (HEAD_DIM * IDX_HEADS) ** -0.5
    ws = [w[:, HEAD_DIM + h:HEAD_DIM + h + 1] * idx_scale for h in range(IDX_HEADS)]

    def score_chunk(c, carry):
        off = pl.multiple_of(c * CH, CH)
        lg = _nt(qi, ki_ref[pl.ds(off, CH), :])
        sc = ws[0] * jnp.maximum(lg[0:TQ], 0.0)
        for h in range(1, IDX_HEADS):
            sc = sc + ws[h] * jnp.maximum(lg[h * TQ:(h + 1) * TQ], 0.0)
        sc = jnp.where(sc == 0.0, 0.0, sc)
        keys_ref[c] = jnp.where(col_c + off <= row_c, _order_key(sc), INT_MIN)
        return carry

    lax.fori_loop(0, nch, score_chunk, 0)

    count = functools.partial(_count_chunks, keys_ref, nch, TQ)
    thr, xcut = _kth_select(count, TQ, float(topk), seq.bit_length() - 1, seq)

    _flash_init(m_ref, acc_ref)
    q_all = qa_ref[...].reshape(N_HEADS * TQ, LANES)

    def att_chunk(c, carry):
        off = pl.multiple_of(c * CH, CH)
        kk = keys_ref[c]
        slabs = []
        for j in range(CH // LANES):
            ks = kk[:, j * LANES:(j + 1) * LANES]
            col = lane + (off + j * LANES)
            tie = jnp.where(ks == thr, jnp.where(col <= xcut, 0.0, NEG_INF), NEG_INF)
            slabs.append(jnp.where(col <= row_l, jnp.where(ks > thr, 0.0, tie), NEG_INF))
        bias = _stack_rows(jnp.concatenate(slabs, axis=1), N_HEADS)
        s = _nt(q_all, ka_ref[pl.ds(off, CH), :]) + bias
        _flash_update(s, va_ref[pl.ds(off, CH), :], m_ref, acc_ref, slice(None))
        return carry

    lax.fori_loop(0, nch, att_chunk, 0)
    _gated_store(_flash_result(acc_ref), za_ref, o_ref, TQ)


def _swa_kernel(q_ref, kp_ref, kc_ref, vp_ref, vc_ref, sink_ref, z_ref, o_ref):
    i = pl.program_id(1)
    q = q_ref[...].reshape(N_HEADS * TQ, LANES)
    qi = _stack_rows(lax.broadcasted_iota(I32, (TQ, TQ), 0), N_HEADS)
    kj = lax.broadcasted_iota(I32, (N_HEADS * TQ, TQ), 1)
    no_prev = jnp.where(i > 0, 0, 1 << 20)
    sp = jnp.where(kj > qi + no_prev, _nt(q, kp_ref[...]), NEG_INF)
    sc = jnp.where(kj <= qi, _nt(q, kc_ref[...]), NEG_INF)
    sink = sink_ref[...]
    m = jnp.maximum(jnp.maximum(jnp.max(sp, axis=1, keepdims=True), jnp.max(sc, axis=1, keepdims=True)), sink)
    pp = jnp.exp(sp - m)
    pc = jnp.exp(sc - m)
    o = _nn(pp.astype(BF16), vp_ref[...]) + _nn(pc.astype(BF16), vc_ref[...])
    den = o[:, LANES:2 * LANES] + jnp.exp(sink - m)
    _gated_store(o[:, 0:LANES] / den, z_ref, o_ref, TQ)


def _out_proj_kernel(x_ref, ma_ref, mb_ref, w_ref, o_ref):
    half = ma_ref.shape[1]
    o_ref[...] = x_ref[...] + _nn(ma_ref[...], w_ref[0:half, :]) + _nn(mb_ref[...], w_ref[half:2 * half, :])


def _proj_cd_kernel(x_ref, g_ref, w_ref, cos_ref, sa_ref, sb_ref, bd_ref, gains_ref,
                    qc_ref, kc_ref, vc_ref, ks_ref, vs_ref, kw_ref, vw_ref, gt_ref, zc_ref,
                    cq_ref, ckv_ref, kr_ref, zd_ref):
    xn = _row_rms(x_ref[...], g_ref[...]).astype(BF16)
    cos, sa, sb, bd = cos_ref[...], sa_ref[...], sb_ref[...], bd_ref[...]

    def proj(c0, n):
        return _nn(xn, w_ref[:, c0:c0 + n])

    def k_slab(c0, gain):
        return _rope(_seg_norm(proj(c0, LANES), bd, gain, HEAD_DIM), cos, sa, sb, HEAD_DIM // 2)

    _place_q_heads(proj(0, 512), bd, gains_ref[0:1, :], cos, sa, sb, qc_ref, HEAD_DIM ** -0.5)
    kc_ref[...] = proj(512, LANES)
    vc_ref[...] = proj(640, LANES)
    ks_ref[...] = k_slab(768, gains_ref[1:2, :]).astype(BF16)
    vs_ref[...] = _with_ones(proj(896, LANES))
    kw_ref[...] = k_slab(1024, gains_ref[2:3, :]).astype(BF16)
    vw_ref[...] = _with_ones(proj(1152, LANES))
    gt_ref[...] = _sigmoid(proj(1280, LANES))
    zc_ref[...] = proj(1408, 512)
    cq_ref[...] = proj(1920, 256)
    ckv_ref[...] = proj(2176, LANES)
    kr_ref[...] = proj(2304, LANES)
    zd_ref[...] = proj(2432, 512)


def _compress_kernel(c_ref, pe_ref, w1a_ref, w1b_ref, w2_ref, gain_ref, cos_ref, sa_ref, sb_ref, bd_ref,
                     o_ref, *, is_key):
    c = c_ref[...]
    n = c.shape[0]
    a = _nn((c + pe_ref[0:1, :]).astype(BF16), w1a_ref[...])
    b = _nn((c + pe_ref[1:2, :]).astype(BF16), w1b_ref[...])
    h = _silu(a + pltpu.roll(b, n - 1, 0))
    y = _nn(h.astype(BF16), w2_ref[...])
    if is_key:
        y = _rope(_seg_norm(y, bd_ref[...], gain_ref[...], HEAD_DIM), cos_ref[...], sa_ref[...], sb_ref[...],
                  HEAD_DIM // 2)
    o_ref[...] = y.astype(BF16)


def _nsa_kernel(q_ref, kc_ref, vc_ref, ks_ref, vs_ref, kw_ref, vw_ref, gt_ref, z_ref, wsel_ref, o_ref,
                m_ref, acc_ref, *, seq, n_sel):
    i = pl.program_id(1)
    t0 = i * TQ
    nch = (t0 + TQ + CH - 1) // CH
    ncmp = kc_ref.shape[0]
    rows_all = N_HEADS * TQ
    q = q_ref[...].reshape(rows_all, LANES)

    t_cmp = _stack_rows(t0 + lax.broadcasted_iota(I32, (TQ, ncmp), 0), N_HEADS)
    cend = lax.broadcasted_iota(I32, (rows_all, ncmp), 1) * CMP_STRIDE + (2 * CMP_STRIDE - 1)
    s = jnp.where(cend <= t_cmp, _nt(q, kc_ref[...]), NEG_INF)
    m = jnp.max(s, axis=1, keepdims=True)
    m = jnp.where(m > NEG_INF, m, 0.0)
    p = jnp.exp(s - m)
    p = p / jnp.maximum(jnp.sum(p, axis=1, keepdims=True), 1e-30)
    o_cmp = _nn(p.astype(BF16), vc_ref[...])

    lane = lax.broadcasted_iota(I32, (TQ, LANES), 1)
    row_l = t0 + lax.broadcasted_iota(I32, (TQ, LANES), 0)
    cur = row_l >> 6
    admiss = lane <= cur
    wsel = wsel_ref[...]
    keys = []
    for g in range(N_KV):
        imp = p[(GRP * g) * TQ:(GRP * g + 1) * TQ]
        for r in range(1, GRP):
            imp = imp + p[(GRP * g + r) * TQ:(GRP * g + r + 1) * TQ]
        hi = imp.astype(BF16)
        lo = (imp - hi.astype(F32)).astype(BF16)
        imp_s = _nn(hi, wsel) + _nn(lo, wsel)
        forced = jnp.where(lane == 0, jnp.inf, jnp.where(lane >= cur - 1, jnp.inf, imp_s))
        keys.append(_order_key(jnp.where(admiss, forced, NEG_INF)))
    kk = jnp.concatenate(keys, axis=0)
    lane2 = _stack_rows(lane, N_KV)

    def count(pred):
        return jnp.sum(pred(kk, lane2), axis=1, keepdims=True)

    thr, xcut = _kth_select(count, N_KV * TQ, float(n_sel), 7, LANES)
    tie = jnp.where(kk == thr, jnp.where(lane2 <= xcut, 1.0, 0.0), 0.0)
    sel = jnp.where(kk > thr, 1.0, tie).astype(BF16)

    _flash_init(m_ref, acc_ref)
    blk_row =lax.broadcasted_iota(I32, (LANES, CH), 0)
    blk_col = lax.broadcasted_iota(I32, (LANES, CH), 1) >> 6
    row_c = t0 + lax.broadcasted_iota(I32, (TQ, CH), 0)
    col_c = lax.broadcasted_iota(I32, (TQ, CH), 1)

    def slc_chunk(c, carry):
        off = pl.multiple_of(c * CH, CH)
        expand = jnp.where(blk_row == blk_col + c * (CH // SLC_BLOCK), 1.0, 0.0).astype(BF16)
        tok = _nn(sel, expand)
        causal = col_c + off <= row_c
        parts = []
        for g in range(N_KV):
            bias_g = jnp.where(causal, jnp.where(tok[g * TQ:(g + 1) * TQ] > 0.5, 0.0, NEG_INF), NEG_INF)
            parts.extend([bias_g] * GRP)
        sc = _nt(q, ks_ref[pl.ds(off, CH), :]) + jnp.concatenate(parts, axis=0)
        _flash_update(sc, vs_ref[pl.ds(off, CH), :], m_ref, acc_ref, slice(None))
        return carry

    lax.fori_loop(0, nch, slc_chunk, 0)
    o_slc = _flash_result(acc_ref)

    span = NSA_WINDOW + TQ
    start = pl.multiple_of(jnp.maximum(t0 - NSA_WINDOW, 0), TQ)
    t_w = _stack_rows(t0 + lax.broadcasted_iota(I32, (TQ, span), 0), N_HEADS)
    diff = t_w - (start + lax.broadcasted_iota(I32, (rows_all, span), 1))
    sw = _nt(q, kw_ref[pl.ds(start, span), :])
    sw = jnp.where(diff >= 0, jnp.where(diff < NSA_WINDOW, sw, NEG_INF), NEG_INF)
    pw = jnp.exp(sw - jnp.max(sw, axis=1, keepdims=True))
    o_win = _nn(pw.astype(BF16), vw_ref[pl.ds(start, span), :])
    o_win = o_win[:, 0:LANES] / o_win[:, LANES:2 * LANES]

    gates = gt_ref[...]
    outs = []
    for h in range(N_HEADS):
        rs = slice(h * TQ, (h + 1) * TQ)
        outs.append(gates[:, 3 * h:3 * h + 1] * o_cmp[rs] + gates[:, 3 * h + 1:3 * h + 2] * o_slc[rs]
                    + gates[:, 3 * h + 2:3 * h + 3] * o_win[rs])
    _gated_store(jnp.concatenate(outs, axis=0), z_ref, o_ref, TQ)


def _mla_prep_kernel(cq_ref, ckv_ref, kr_ref, wq_ref, wkv_ref, lat_ref, gains_ref, cos_ref, sa_ref, sb_ref,
                     bd64_ref, bd32_ref, q_ref, k_ref, v_ref):
    lane = lax.broadcasted_iota(I32, (TM, LANES), 1)
    lo_half = lane < NOPE_DIM
    cos, sa, sb = cos_ref[...], sa_ref[...], sb_ref[...]
    bd64, bd32 = bd64_ref[...], bd32_ref[...]
    scale = (NOPE_DIM + ROPE_DIM) ** -0.5

    def rope_slab(s, gain):
        return _rope(_seg_norm(s, bd32, gain, ROPE_DIM), cos, sa, sb, ROPE_DIM // 2)

    q = _nn(_row_rms(cq_ref[...], lat_ref[0:1, :]).astype(BF16), wq_ref[...])
    kv = _nn(_row_rms(ckv_ref[...], lat_ref[1:2, 0:KV_LORA]).astype(BF16), wkv_ref[...])
    k_rope = pltpu.roll(rope_slab(kr_ref[...], gains_ref[3:4, :]), NOPE_DIM, 1)
    q_rope = [rope_slab(q[:, 512 + LANES * j:512 + LANES * (j + 1)], gains_ref[1:2, :]) * scale for j in range(2)]
    for j in range(4):
        qn = _seg_norm(q[:, LANES * j:LANES * (j + 1)], bd64, gains_ref[0:1, :], NOPE_DIM) * scale
        kn = _seg_norm(kv[:, LANES * j:LANES * (j + 1)], bd64, gains_ref[2:3, :], NOPE_DIM)
        for e in range(2):
            h = 2 * j + e
            qn_h = qn if e == 0 else pltpu.roll(qn, NOPE_DIM, 1)
            kn_h = kn if e == 0 else pltpu.roll(kn, NOPE_DIM, 1)
            shift = (NOPE_DIM - ROPE_DIM * (h % 4)) % LANES
            qr = q_rope[h // 4]
            qr_h = qr if shift == 0 else pltpu.roll(qr, shift, 1)
            q_ref[h] = jnp.where(lo_half, qn_h, jnp.where(lane < NOPE_DIM + ROPE_DIM, qr_h, 0.0)).astype(BF16)
            k_ref[h] = jnp.where(lo_half, kn_h, k_rope).astype(BF16)
    for j in range(4):
        v_ref[j] = _with_ones(kv[:, 512 + LANES * j:512 + LANES * (j + 1)])


def _mla_kernel(q_ref, k_ref, v_ref, z_ref, o_ref, m_ref, acc_ref):
    i = pl.program_id(2)
    tq = TQ_MLA
    _flash_init(m_ref, acc_ref)

    def step(c, masked):
        off = pl.multiple_of(c * CH, CH)
        v = v_ref[0, pl.ds(off, CH), :]
        for e in range(2):
            s = _nt(q_ref[e], k_ref[e, pl.ds(off, CH), :])
            if masked:
                s = jnp.where(lax.broadcasted_iota(I32, (tq, CH), 1) <= lax.broadcasted_iota(I32, (tq, CH), 0),
                              s, NEG_INF)
            _flash_update(s, v, m_ref, acc_ref, slice(e * tq, (e + 1) * tq))

    def body(c, carry):
        step(c, False)
        return carry

    lax.fori_loop(0, i, body, 0)
    step(i, True)
    o = _flash_result(acc_ref)
    lo_half = lax.broadcasted_iota(I32, (tq, LANES), 1) < NOPE_DIM
    slab = jnp.where(lo_half, o[0:tq], o[tq:2 * tq])
    o_ref[...] = (slab * _silu(z_ref[...])).astype(BF16)


def _rope_tables(pos, dim):
    inv = jnp.power(jnp.float32(ROPE_THETA), -jnp.arange(0, dim, 2, dtype=F32) / dim)
    ang = pos.astype(F32)[:, None] * inv[None, :]
    cos, sin = jnp.cos(ang), jnp.sin(ang)
    reps = LANES // dim
    zero = jnp.zeros_like(sin)
    cos_t = jnp.tile(jnp.concatenate([cos, cos], axis=1), (1, reps))
    sa_t = jnp.tile(jnp.concatenate([zero, sin], axis=1), (1, reps))
    sb_t = jnp.tile(jnp.concatenate([-sin, zero], axis=1), (1, reps))
    return cos_t, sa_t, sb_t


def _block_diag(group):
    r = jnp.arange(LANES) // group
    return (r[:, None] == r[None, :]).astype(BF16)


def _tile_gain(g, width=LANES):
    return jnp.tile(g.astype(F32), width // g.shape[0])


def _pad_lanes(g, width=LANES):
    return jnp.concatenate([g.astype(F32), jnp.zeros((width - g.shape[0],), F32)])


def _pad_cols(w, width):
    return jnp.concatenate([w, jnp.zeros((w.shape[0], width - w.shape[1]), w.dtype)], axis=1)


def _params(*sem):
    return pltpu.CompilerParams(dimension_semantics=sem, vmem_limit_bytes=VMEM_LIMIT)


def kernel(x, ab_norm, ab_w_in, a_qk_norm, a_kidx_norm, b_qk_norm, b_sinks, ab_w_out, cd_norm, cd_w_in, c_q_norm, c_k_norm, c_cmp_pe, c_cmp_w1, c_cmp_w2, d_q_lat_norm, d_kv_lat_norm, d_w_uq, d_w_ukv, d_nope_norm, d_rope_norm, cd_w_out):
    bsz, seq, d = x.shape
    n = bsz * seq
    assert d == D_MODEL and seq % CH == 0 and seq >= NSA_WINDOW + TQ and seq // SLC_BLOCK <= LANES
    nq = seq // TQ
    nrow = n // TM
    srow = seq // TM
    xf = x.reshape(n, d)
    pos = jnp.arange(seq)
    cos64, sa64, sb64 = _rope_tables(pos, HEAD_DIM)
    cos32, sa32, sb32 = _rope_tables(pos, ROPE_DIM)
    bd64, bd32 = _block_diag(HEAD_DIM), _block_diag(ROPE_DIM)

    def row_spec(width, rows=TM):
        return pl.BlockSpec((rows, width), lambda i: (i, 0))

    def const_spec(shape):
        return pl.BlockSpec(shape, lambda i: tuple(0 for _ in shape))

    tab_spec = pl.BlockSpec((TM, LANES), lambda i: (i % srow, 0))
    head_spec = lambda nh: pl.BlockSpec((nh, TM, LANES), lambda i: (0, i, 0))

    wa = ab_w_in[0]
    w_ab = jnp.concatenate([wa[:, 0:1088], _pad_cols(wa[:, 1088:1092], 64), wa[:, 1092:]], axis=1).astype(BF16)
    gains_ab = jnp.stack([_tile_gain(a_qk_norm[0, 0]), _tile_gain(a_qk_norm[0, 1]), _pad_lanes(a_kidx_norm[0]),
                          _tile_gain(b_qk_norm[0, 0]), _tile_gain(b_qk_norm[0, 1])]
                         + [jnp.zeros((LANES,), F32)] * 3)
    sds = jax.ShapeDtypeStruct
    qa, ka, va, qi, ki, wi, za, qb, kb, vb, zb = pl.pallas_call(
        _proj_ab_kernel,
        grid=(nrow,),
        in_specs=[row_spec(d), const_spec((1, d)), const_spec((d, AB_W)), tab_spec, tab_spec, tab_spec,
                  const_spec((LANES, LANES)), const_spec((8, LANES))],
        out_specs=[head_spec(8), row_spec(LANES), row_spec(2 * LANES), head_spec(4), row_spec(LANES), row_spec(LANES),
                   row_spec(512), head_spec(8), row_spec(LANES), row_spec(2 * LANES), row_spec(512)],
        out_shape=[sds((8, n, LANES), BF16), sds((n, LANES), BF16), sds((n, 2 * LANES), BF16),
                   sds((4, n, LANES), BF16), sds((n, LANES), BF16), sds((n, LANES), F32), sds((n, 512), F32),
                   sds((8, n, LANES), BF16), sds((n, LANES), BF16), sds((n, 2 * LANES), BF16), sds((n, 512), F32)],
        compiler_params=_params("parallel"),
    )(xf, ab_norm[0][None, :], w_ab, cos64, sa64, sb64, bd64, gains_ab)

    qt_heads = lambda nh: pl.BlockSpec((nh, TQ, LANES), lambda b, i: (0, b * nq + i, 0))
    qt_rows = lambda width: pl.BlockSpec((TQ, width), lambda b, i: (b * nq + i, 0))
    seq_rows = pl.BlockSpec((seq, LANES), lambda b, i: (b, 0))
    seq_rows_v = pl.BlockSpec((seq, 2 * LANES), lambda b, i: (b, 0))
    att_scratch = [pltpu.VMEM((N_HEADS * TQ, LANES), F32), pltpu.VMEM((N_HEADS * TQ, 2 * LANES), F32)]

    mix_a = pl.pallas_call(
        functools.partial(_dsa_kernel, seq=seq, topk=min(DSA_TOPK, seq // 4)),
        grid=(bsz, nq),
        in_specs=[qt_heads(4), qt_rows(LANES), seq_rows, qt_heads(8), seq_rows, seq_rows_v, qt_rows(512)],
        out_specs=qt_rows(512),
        out_shape=sds((n, 512), BF16),
        scratch_shapes=[pltpu.VMEM((seq // CH, TQ, CH), I32)] + att_scratch,
        compiler_params=_params("parallel", "arbitrary"),
    )(qi, wi, ki, qa, ka, va, za)

    prev_rows = lambda width: pl.BlockSpec((TQ, width), lambda b, i: (b * nq + jnp.maximum(i - 1, 0), 0))
    sink_rows = jnp.broadcast_to(jnp.repeat(b_sinks[0].astype(F32), TQ)[:, None], (N_HEADS * TQ, LANES))
    mix_b = pl.pallas_call(
        _swa_kernel,
        grid=(bsz, nq),
        in_specs=[qt_heads(8), prev_rows(LANES), qt_rows(LANES), prev_rows(2 * LANES), qt_rows(2 * LANES),
                  pl.BlockSpec((N_HEADS * TQ, LANES), lambda b, i: (0, 0)), qt_rows(512)],
        out_specs=qt_rows(512),
        out_shape=sds((n, 512), BF16),
        compiler_params=_params("parallel", "arbitrary"),
    )(qb, kb, kb, vb, vb, sink_rows, zb)

    def out_proj(xin, m0, m1, w):
        return pl.pallas_call(
            _out_proj_kernel,
            grid=(nrow,),
            in_specs=[row_spec(d), row_spec(512), row_spec(512), const_spec((2 * 512, d))],
            out_specs=row_spec(d),
            out_shape=sds((n, d), F32),
            compiler_params=_params("parallel"),
        )(xin, m0, m1, w.astype(BF16))

    x1 = out_proj(xf, mix_a, mix_b, ab_w_out[0])

    wc = cd_w_in[0]
    w_cd = jnp.concatenate([wc[:, 0:1280], _pad_cols(wc[:, 1280:1304], LANES), wc[:, 1304:2200],
                            _pad_cols(wc[:, 2200:2232], LANES), wc[:, 2232:]], axis=1).astype(BF16)
    gains_cd = jnp.stack([_tile_gain(c_q_norm[0]), _tile_gain(c_k_norm[0, 1]), _tile_gain(c_k_norm[0, 2])]
                         + [jnp.zeros((LANES,), F32)] * 5)
    qc, kc_raw, vc_raw, ks, vs, kw, vw, gates, zc, cq, ckv, kr, zd = pl.pallas_call(
        _proj_cd_kernel,
        grid=(nrow,),
        in_specs=[row_spec(d), const_spec((1, d)), const_spec((d, AB_W)), tab_spec, tab_spec, tab_spec,
                  const_spec((LANES, LANES)), const_spec((8, LANES))],
        out_specs=[head_spec(8), row_spec(LANES), row_spec(LANES), row_spec(LANES), row_spec(2 * LANES),
                   row_spec(LANES), row_spec(2 * LANES), row_spec(LANES), row_spec(512), row_spec(256),
                   row_spec(LANES), row_spec(LANES), row_spec(512)],
        out_shape=[sds((8, n, LANES), BF16), sds((n, LANES), F32), sds((n, LANES), F32), sds((n, LANES), BF16),
                   sds((n, 2 * LANES), BF16), sds((n, LANES), BF16), sds((n, 2 * LANES), BF16), sds((n, LANES), F32),
                   sds((n, 512), F32), sds((n, 256), F32), sds((n, LANES), F32), sds((n, LANES), F32),
                   sds((n, 512), F32)],
        compiler_params=_params("parallel"),
    )(x1, cd_norm[0][None, :], w_cd, cos64, sa64, sb64, bd64, gains_cd)

    ncmp = seq // CMP_STRIDE
    cw = CMP_STRIDE * N_KV * HEAD_DIM
    eye = jnp.eye(N_KV, dtype=F32)
    ccos, csa, csb = _rope_tables(jnp.arange(ncmp) * CMP_STRIDE + (2 * CMP_STRIDE - 1), HEAD_DIM)

    def compress(raw, which, is_key):
        w1 = c_cmp_w1[0, which].reshape(2, CMP_STRIDE, HEAD_DIM, -1)
        hid = w1.shape[-1]
        w1e = jnp.einsum("tjdu,gh->tjgdhu", w1, eye).reshape(2, cw, N_KV * hid).astype(BF16)
        w2e = jnp.einsum("ud,gh->guhd", c_cmp_w2[0, which], eye).reshape(N_KV * hid, N_KV * HEAD_DIM).astype(BF16)
        pe = c_cmp_pe[0, which].reshape(2, CMP_STRIDE, 1, HEAD_DIM)
        pe = jnp.broadcast_to(pe, (2, CMP_STRIDE, N_KV, HEAD_DIM)).reshape(2, cw)
        return pl.pallas_call(
            functools.partial(_compress_kernel, is_key=is_key),
            grid=(bsz,),
            in_specs=[pl.BlockSpec((ncmp, cw), lambda b: (b, 0)), const_spec((2, cw)),
                      const_spec((cw, N_KV * hid)), const_spec((cw, N_KV * hid)),
                      const_spec((N_KV * hid, LANES)), const_spec((1, LANES)),
                      const_spec((ncmp, LANES)), const_spec((ncmp, LANES)), const_spec((ncmp, LANES)),
                      const_spec((LANES, LANES))],
            out_specs=pl.BlockSpec((ncmp, LANES), lambda b: (b, 0)),
            out_shape=sds((bsz * ncmp, LANES), BF16),
            compiler_params=_params("parallel"),
        )(raw.reshape(bsz * ncmp, cw), pe, w1e[0], w1e[1], w2e, _tile_gain(c_k_norm[0, 0])[None, :],
          ccos, csa, csb, bd64)

    kc = compress(kc_raw, 0, True)
    vc = compress(vc_raw, 1, False)

    ratio = SLC_BLOCK // CMP_STRIDE
    mm = jnp.arange(ncmp)[:, None]
    jj = jnp.arange(LANES)[None, :]
    wsel = (jnp.where((mm == ratio * jj - 1) | (mm == ratio * jj + ratio - 1), 1.0, 0.0)
            + jnp.where((mm >= ratio * jj) & (mm < ratio * jj + ratio - 1), 2.0, 0.0))
    wsel = jnp.where((mm < ncmp - 1) & (jj < seq // SLC_BLOCK), wsel, 0.0).astype(BF16)

    cmp_rows = pl.BlockSpec((ncmp, LANES), lambda b, i: (b, 0))
    mix_c = pl.pallas_call(
        functools.partial(_nsa_kernel, seq=seq, n_sel=min(SLC_TOPN, seq // SLC_BLOCK)),
        grid=(bsz, nq),
        in_specs=[qt_heads(8), cmp_rows, cmp_rows, seq_rows, seq_rows_v, seq_rows, seq_rows_v, qt_rows(LANES),
                  qt_rows(512), pl.BlockSpec((ncmp, LANES), lambda b, i: (0, 0))],
        out_specs=qt_rows(512),
        out_shape=sds((n, 512), BF16),
        scratch_shapes=att_scratch,
        compiler_params=_params("parallel", "arbitrary"),
    )(qc, kc, vc, ks, vs, kw, vw, gates, zc, wsel)

    wq = d_w_uq[0].reshape(Q_LORA, N_HEADS, NOPE_DIM + ROPE_DIM)
    wq = jnp.concatenate([wq[:, :, :NOPE_DIM].reshape(Q_LORA, -1), wq[:, :, NOPE_DIM:].reshape(Q_LORA, -1)],
                         axis=1).astype(BF16)
    wkv = d_w_ukv[0].reshape(KV_LORA, N_HEADS, NOPE_DIM + HEAD_DIM)
    wkv = jnp.concatenate([wkv[:, :, :NOPE_DIM].reshape(KV_LORA, -1), wkv[:, :, NOPE_DIM:].reshape(KV_LORA, -1)],
                          axis=1).astype(BF16)
    lat_gains = jnp.stack([d_q_lat_norm[0].astype(F32), _pad_lanes(d_kv_lat_norm[0], Q_LORA)]
                          + [jnp.zeros((Q_LORA,), F32)] * 6)
    gains_d = jnp.stack([_tile_gain(d_nope_norm[0, 0]), _tile_gain(d_rope_norm[0, 0]), _tile_gain(d_nope_norm[0, 1]),
                         _pad_lanes(d_rope_norm[0, 1])] + [jnp.zeros((LANES,), F32)] * 4)
    q_cat, k_cat, v_d = pl.pallas_call(
        _mla_prep_kernel,
        grid=(nrow,),
        in_specs=[row_spec(Q_LORA), row_spec(LANES), row_spec(LANES), const_spec((Q_LORA, 768)),
                  const_spec((KV_LORA, 1024)), const_spec((8, Q_LORA)), const_spec((8, LANES)),
                  tab_spec, tab_spec, tab_spec, const_spec((LANES, LANES)), const_spec((LANES, LANES))],
        out_specs=[head_spec(8), head_spec(8), pl.BlockSpec((4, TM, 2 * LANES), lambda i: (0, i, 0))],
        out_shape=[sds((8, n, LANES), BF16), sds((8, n, LANES), BF16), sds((4, n, 2 * LANES), BF16)],
        compiler_params=_params("parallel"),
    )(cq, ckv, kr, wq, wkv, lat_gains, gains_d, cos32, sa32, sb32, bd64, bd32)

    nqm = seq // TQ_MLA
    mix_d = pl.pallas_call(
        _mla_kernel,
        grid=(bsz, N_HEADS // 2, nqm),
        in_specs=[pl.BlockSpec((2, TQ_MLA, LANES), lambda b, hp, i: (hp, b * nqm + i, 0)),
                  pl.BlockSpec((2, seq, LANES), lambda b, hp, i: (hp, b, 0)),
                  pl.BlockSpec((1, seq, 2 * LANES), lambda b, hp, i: (hp, b, 0)),
                  pl.BlockSpec((TQ_MLA, LANES), lambda b, hp, i: (b * nqm + i, hp))],
        out_specs=pl.BlockSpec((TQ_MLA, LANES), lambda b, hp, i: (b * nqm + i, hp)),
        out_shape=sds((n, 512), BF16),
        scratch_shapes=[pltpu.VMEM((2 * TQ_MLA, LANES), F32), pltpu.VMEM((2 * TQ_MLA, 2 * LANES), F32)],
        compiler_params=_params("parallel", "parallel", "arbitrary"),
    )(q_cat, k_cat, v_d, zd)

    x2 = out_proj(x1, mix_c, mix_d, cd_w_out[0])
    return x2.reshape(bsz, seq, d)
```

```python
import functools

import jax
import jax.numpy as jnp
from jax import lax
from jax.experimental import pallas as pl
from jax.experimental.pallas import tpu as pltpu

F32, BF16, I32 = jnp.float32, jnp.bfloat16, jnp.int32

D_MODEL = 1024
HEAD_DIM = 64
N_HEADS = 8
N_KV = 2
GRP = N_HEADS // N_KV
IDX_HEADS = 4
DSA_TOPK = 256
SWA_WINDOW = 128
CMP_STRIDE = 16
SLC_BLOCK = 64
SLC_TOPN = 16
NSA_WINDOW = 512
Q_LORA = 256
KV_LORA = 128
NOPE_DIM = 64
ROPE_DIM = 32
ROPE_THETA = 10000.0
EPS = 1e-6

LANES = 128
TM = 512
TQ = 128
CH = 512
TQ_MLA = 512
UNROLL = 4
VMEM_LIMIT = 56 * 1024 * 1024

NEG_INF = float("-inf")
M_INIT = -1e30
INT_MIN = -(2 ** 31)

AB_W = 2944


def _nn(a, b):
    return jnp.dot(a, b, preferred_element_type=F32)


def _nt(a, b):
    return lax.dot_general(a, b, (((1,), (1,)), ((), ())), preferred_element_type=F32)


def _sigmoid(z):
    return 1.0 / (1.0 + jnp.exp(-z))


def _silu(z):
    return z * _sigmoid(z)


def _row_rms(x, g):
    return x * lax.rsqrt(jnp.mean(x * x, axis=-1, keepdims=True) + EPS) * g


def _seg_norm(s, bd, gain, group):
    s2 = s * s
    hi = s2.astype(BF16)
    lo = (s2 - hi.astype(F32)).astype(BF16)
    ss = _nn(hi, bd) + _nn(lo, bd)
    return s * lax.rsqrt(ss * (1.0 / group) + EPS) * gain


def _rope(s, cos, sa, sb, half):
    return s * cos + pltpu.roll(s, half, 1) * sa + pltpu.roll(s, LANES - half, 1) * sb


def _order_key(v):
    bits = lax.bitcast_convert_type(v, I32)
    return bits ^ ((bits >> 31) & 0x7FFFFFFF)


def _stack_rows(x, n):
    return jnp.concatenate([x] * n, axis=0)


def _with_ones(v):
    return jnp.concatenate([v.astype(BF16), jnp.ones(v.shape, BF16)], axis=1)


def _flash_update(s, v_ext, m_ref, acc_ref, rows):
    m_prev = m_ref[rows]
    m_new = jnp.maximum(m_prev, jnp.max(s, axis=1, keepdims=True))
    alpha = jnp.exp(m_prev - m_new)
    p = jnp.exp(s - jnp.concatenate([m_new] * (s.shape[1] // LANES), axis=1))
    acc_ref[rows] = jnp.concatenate([alpha, alpha], axis=1) * acc_ref[rows] + _nn(p.astype(BF16), v_ext)
    m_ref[rows] = m_new


def _unrolled_loop(n, step, unroll=UNROLL):
    def body(j, carry):
        for u in range(unroll):
            step(j * unroll + u)
        return carry

    lax.fori_loop(0, n // unroll, body, 0)
    base = (n // unroll) * unroll
    p = unroll // 2
    while p >= 1:
        def tail(base=base, p=p):
            for u in range(p):
                step(base + u)

        pl.when((n & p) != 0)(tail)
        base = base + (n & p)
        p //= 2


def _flash_init(m_ref, acc_ref):
    m_ref[...] = jnp.full(m_ref.shape, M_INIT, F32)
    acc_ref[...] = jnp.zeros(acc_ref.shape, F32)


def _flash_result(acc_ref):
    acc = acc_ref[...]
    return acc[:, 0:LANES] / jnp.maximum(acc[:, LANES:2 * LANES], 1e-30)


def _place_q_heads(y, bd, gain, cos, sa, sb, out_ref, scale):
    lo_half = lax.broadcasted_iota(I32, (y.shape[0], LANES), 1) < HEAD_DIM
    for j in range(4):
        s = y[:, LANES * j:LANES * (j + 1)]
        s = _rope(_seg_norm(s, bd, gain, HEAD_DIM), cos, sa, sb, HEAD_DIM // 2) * scale
        r = pltpu.roll(s, HEAD_DIM, 1)
        if j < 2:
            out_ref[2 * j] = jnp.where(lo_half, s, 0.0).astype(BF16)
            out_ref[2 * j + 1] = jnp.where(lo_half, r, 0.0).astype(BF16)
        else:
            out_ref[2 * j] = jnp.where(lo_half, 0.0, r).astype(BF16)
            out_ref[2 * j + 1] = jnp.where(lo_half, 0.0, s).astype(BF16)


def _gated_store(o, z_ref, o_ref, tq):
    lo_half = lax.broadcasted_iota(I32, (tq, LANES), 1) < HEAD_DIM
    for j in range(4):
        a = o[(2 * j) * tq:(2 * j + 1) * tq]
        b = o[(2 * j + 1) * tq:(2 * j + 2) * tq]
        if j < 2:
            slab = jnp.where(lo_half, a, pltpu.roll(b, HEAD_DIM, 1))
        else:
            slab = jnp.where(lo_half, pltpu.roll(a, HEAD_DIM, 1), b)
        z = z_ref[:, LANES * j:LANES * (j + 1)]
        o_ref[:, LANES * j:LANES * (j + 1)] = (slab * _silu(z)).astype(BF16)


def _proj_ab_kernel(x_ref, g_ref, w_ref, cos_ref, sa_ref, sb_ref, bd_ref, gains_ref,
                    qa_ref, ka_ref, va_ref, qi_ref, ki_ref, wi_ref, za_ref,
                    qb_ref, kb_ref, vb_ref, zb_ref):
    xn = _row_rms(x_ref[...], g_ref[...]).astype(BF16)
    cos, sa, sb, bd = cos_ref[...], sa_ref[...], sb_ref[...], bd_ref[...]
    lo_half = lax.broadcasted_iota(I32, (TM, LANES), 1) < HEAD_DIM

    def proj(c0, n):
        return _nn(xn, w_ref[:, c0:c0 + n])

    def k_slab(c0, gain):
        return _rope(_seg_norm(proj(c0, LANES), bd, gain, HEAD_DIM), cos, sa, sb, HEAD_DIM // 2)

    _place_q_heads(proj(0, 512), bd, gains_ref[0:1, :], cos, sa, sb, qa_ref, HEAD_DIM ** -0.5)
    ka_ref[...] = k_slab(512, gains_ref[1:2, :]).astype(BF16)
    va_ref[...] = _with_ones(proj(640, LANES))
    qi = proj(768, 256)
    for j in range(2):
        s = _rope(qi[:, LANES * j:LANES * (j + 1)], cos, sa, sb, HEAD_DIM // 2)
        qi_ref[2 * j] = jnp.where(lo_half, s, 0.0).astype(BF16)
        qi_ref[2 * j + 1] = jnp.where(lo_half, pltpu.roll(s, HEAD_DIM, 1), 0.0).astype(BF16)
    kiw = proj(1024, LANES)
    wi_ref[...] = kiw
    ki_ref[...] = _rope(_seg_norm(kiw, bd, gains_ref[2:3, :], HEAD_DIM), cos, sa, sb, HEAD_DIM // 2).astype(BF16)
    za_ref[...] = proj(1152, 512)
    _place_q_heads(proj(1664, 512), bd, gains_ref[3:4, :], cos, sa, sb, qb_ref, HEAD_DIM ** -0.5)
    kb_ref[...] = k_slab(2176, gains_ref[4:5, :]).astype(BF16)
    vb_ref[...] = _with_ones(proj(2304, LANES))
    zb_ref[...] = proj(2432, 512)


def _count_chunks(keys_ref, nch, rows, pred):
    lane = lax.broadcasted_iota(I32, (rows, LANES), 1)

    def body(c, acc):
        kk = keys_ref[c]
        for j in range(CH // LANES):
            col = lane + (c * CH + j * LANES)
            acc = acc + pred(kk[:, j * LANES:(j + 1) * LANES], col)
        return acc

    acc = lax.fori_loop(0, nch, body, jnp.zeros((rows, LANES), F32))
    return jnp.sum(acc, axis=1, keepdims=True)


def _kth_select(count, rows, k, idx_bits, idx_all):
    def bit_body(it, tb):
        cand_b = tb | jnp.left_shift(jnp.int32(1), 31 - it)
        cand_s = cand_b ^ INT_MIN
        cnt = count(lambda kk, col: jnp.where(kk >= cand_s, 1.0, 0.0))
        return jnp.where(cnt >= k, cand_b, tb)

    tb = lax.fori_loop(0, 32, bit_body, jnp.zeros((rows, 1), I32))
    thr = tb ^ INT_MIN
    need = k - count(lambda kk, col: jnp.where(kk > thr, 1.0, 0.0))
    n_eq = count(lambda kk, col: jnp.where(kk == thr, 1.0, 0.0))
    excess = jnp.max(n_eq - need) > 0.0

    def tie_search():
        def idx_body(it, xb):
            cand = xb | jnp.left_shift(jnp.int32(1), idx_bits - 1 - it)
            before = count(lambda kk, col: jnp.where(kk == thr, jnp.where(col < cand, 1.0, 0.0), 0.0))
            return jnp.where(before < need, cand, xb)

        return lax.fori_loop(0, idx_bits, idx_body, jnp.zeros((rows, 1), I32))

    xcut = lax.cond(excess, tie_search, lambda: jnp.full((rows, 1), idx_all, I32))
    return thr, xcut


def _dsa_kernel(qi_ref, wi_ref, ki_ref, qa_ref, ka_ref, va_ref, za_ref, o_ref,
                keys_ref, m_ref, acc_ref, *, seq, topk):
    i = pl.program_id(1)
    t0 = i * TQ
    nch = (t0 + TQ + CH - 1) // CH
    row_c = t0 + lax.broadcasted_iota(I32, (TQ, CH), 0)
    col_c = lax.broadcasted_iota(I32, (TQ, CH), 1)
    row_l = t0 + lax.broadcasted_iota(I32, (TQ, LANES), 0)
    lane = lax.broadcasted_iota(I32, (TQ, LANES), 1)

    qi = qi_ref[...].reshape(IDX_HEADS * TQ, LANES)
    w = wi_ref[...]
    idx_scale = (HEAD_DIM * IDX_HEADS) ** -0.5
    ws = [w[:, HEAD_DIM + h:HEAD_DIM + h + 1] * idx_scale for h in range(IDX_HEADS)]

    def score_chunk(c, carry):
        off = pl.multiple_of(c * CH, CH)
        lg = _nt(qi, ki_ref[pl.ds(off, CH), :])
        sc = ws[0] * jnp.maximum(lg[0:TQ], 0.0)
        for h in range(1, IDX_HEADS):
            sc = sc + ws[h] * jnp.maximum(lg[h * TQ:(h + 1) * TQ], 0.0)
        sc = jnp.where(sc == 0.0, 0.0, sc)
        keys_ref[c] = jnp.where(col_c + off <= row_c, _order_key(sc), INT_MIN)
        return carry

    lax.fori_loop(0, nch, score_chunk, 0)

    count = functools.partial(_count_chunks, keys_ref, nch, TQ)
    thr, xcut = _kth_select(count, TQ, float(topk), seq.bit_length() - 1, seq)

    _flash_init(m_ref, acc_ref)
    q_all = qa_ref[...].reshape(N_HEADS * TQ, LANES)

    def att_chunk(c):
        off = pl.multiple_of(c * CH, CH)
        kk = keys_ref[c]
        slabs = []
        for j in range(CH // LANES):
            ks = kk[:, j * LANES:(j + 1) * LANES]
            col = lane + (off + j * LANES)
            tie = jnp.where(ks == thr, jnp.where(col <= xcut, 0.0, NEG_INF), NEG_INF)
            slabs.append(jnp.where(col <= row_l, jnp.where(ks > thr, 0.0, tie), NEG_INF))
        bias = _stack_rows(jnp.concatenate(slabs, axis=1), N_HEADS)
        s = _nt(q_all, ka_ref[pl.ds(off, CH), :]) + bias
        _flash_update(s, va_ref[pl.ds(off, CH), :], m_ref, acc_ref, slice(None))

    _unrolled_loop(nch, att_chunk)
    _gated_store(_flash_result(acc_ref), za_ref, o_ref, TQ)


def _swa_kernel(q_ref, kp_ref, kc_ref, vp_ref, vc_ref, sink_ref, z_ref, o_ref):
    i = pl.program_id(1)
    q = q_ref[...].reshape(N_HEADS * TQ, LANES)
    qi = _stack_rows(lax.broadcasted_iota(I32, (TQ, TQ), 0), N_HEADS)
    kj = lax.broadcasted_iota(I32, (N_HEADS * TQ, TQ), 1)
    no_prev = jnp.where(i > 0, 0, 1 << 20)
    sp = jnp.where(kj > qi + no_prev, _nt(q, kp_ref[...]), NEG_INF)
    sc = jnp.where(kj <= qi, _nt(q, kc_ref[...]), NEG_INF)
    sink = sink_ref[...]
    m = jnp.maximum(jnp.maximum(jnp.max(sp, axis=1, keepdims=True), jnp.max(sc, axis=1, keepdims=True)), sink)
    pp = jnp.exp(sp - m)
    pc = jnp.exp(sc - m)
    o = _nn(pp.astype(BF16), vp_ref[...]) + _nn(pc.astype(BF16), vc_ref[...])
    den = o[:, LANES:2 * LANES] + jnp.exp(sink - m)
    _gated_store(o[:, 0:LANES] / den, z_ref, o_ref, TQ)


def _out_proj_kernel(x_ref, ma_ref, mb_ref, w_ref, o_ref):
    half = ma_ref.shape[1]
    o_ref[...] = x_ref[...] + _nn(ma_ref[...], w_ref[0:half, :]) + _nn(mb_ref[...], w_ref[half:2 * half, :])


def _proj_cd_kernel(x_ref, g_ref, w_ref, cos_ref, sa_ref, sb_ref, bd_ref, gains_ref,
                    qc_ref, kc_ref, vc_ref, ks_ref, vs_ref, kw_ref, vw_ref, gt_ref, zc_ref,
                    cq_ref, ckv_ref, kr_ref, zd_ref):
    xn = _row_rms(x_ref[...], g_ref[...]).astype(BF16)
    cos, sa, sb, bd = cos_ref[...], sa_ref[...], sb_ref[...], bd_ref[...]

    def proj(c0, n):
        return _nn(xn, w_ref[:, c0:c0 + n])

    def k_slab(c0, gain):
        return _rope(_seg_norm(proj(c0, LANES), bd, gain, HEAD_DIM), cos, sa, sb, HEAD_DIM // 2)

    _place_q_heads(proj(0, 512), bd, gains_ref[0:1, :], cos, sa, sb, qc_ref, HEAD_DIM ** -0.5)
    kc_ref[...] = proj(512, LANES)
    vc_ref[...] = proj(640, LANES)
    ks_ref[...] = k_slab(768, gains_ref[1:2, :]).astype(BF16)
    vs_ref[...] = _with_ones(proj(896, LANES))
    kw_ref[...] = k_slab(1024, gains_ref[2:3, :]).astype(BF16)
    vw_ref[...] = _with_ones(proj(1152, LANES))
    gt_ref[...] = _sigmoid(proj(1280, LANES))
    zc_ref[...] = proj(1408, 512)
    cq_ref[...] = proj(1920, 256)
    ckv_ref[...] = proj(2176, LANES)
    kr_ref[...] = proj(2304, LANES)
    zd_ref[...] = proj(2432, 512)


def _compress_kernel(c_ref, pe_ref, w1a_ref, w1b_ref, w2_ref, gain_ref, cos_ref, sa_ref, sb_ref, bd_ref,
                     o_ref, *, is_key):
    c = c_ref[...]
    n = c.shape[0]
    a = _nn((c + pe_ref[0:1, :]).astype(BF16), w1a_ref[...])
    b = _nn((c + pe_ref[1:2, :]).astype(BF16), w1b_ref[...])
    h = _silu(a + pltpu.roll(b, n - 1, 0))
    y = _nn(h.astype(BF16), w2_ref[...])
    if is_key:
        y = _rope(_seg_norm(y, bd_ref[...], gain_ref[...], HEAD_DIM), cos_ref[...], sa_ref[...], sb_ref[...],
                  HEAD_DIM // 2)
    o_ref[...] = y.astype(BF16)


def _nsa_kernel(q_ref, kc_ref, vc_ref, ks_ref, vs_ref, kw_ref, vw_ref, gt_ref, z_ref, wsel_ref, o_ref,
                m_ref, acc_ref, *, seq, n_sel):
    i = pl.program_id(1)
    t0 = i * TQ
    nch = (t0 + TQ + CH - 1) // CH
    ncmp = kc_ref.shape[0]
    rows_all = N_HEADS * TQ
    q = q_ref[...].reshape(rows_all, LANES)

    t_cmp = _stack_rows(t0 + lax.broadcasted_iota(I32, (TQ, ncmp), 0), N_HEADS)
    cend = lax.broadcasted_iota(I32, (rows_all, ncmp), 1) * CMP_STRIDE + (2 * CMP_STRIDE - 1)
    s = jnp.where(cend <= t_cmp, _nt(q, kc_ref[...]), NEG_INF)
    m = jnp.max(s, axis=1, keepdims=True)
    m = jnp.where(m > NEG_INF, m, 0.0)
    p = jnp.exp(s - m)
    p = p / jnp.maximum(jnp.sum(p, axis=1, keepdims=True), 1e-30)
    o_cmp = _nn(p.astype(BF16), vc_ref[...])

    lane = lax.broadcasted_iota(I32, (TQ, LANES), 1)
    row_l = t0 + lax.broadcasted_iota(I32, (TQ, LANES), 0)
    cur = row_l >> 6
    admiss = lane <= cur
    wsel = wsel_ref[...]
    keys = []
    for g in range(N_KV):
        imp = p[(GRP * g) * TQ:(GRP * g + 1) * TQ]
        for r in range(1, GRP):
            imp = imp + p[(GRP * g + r) * TQ:(GRP * g + r + 1) * TQ]
        hi = imp.astype(BF16)
        lo = (imp - hi.astype(F32)).astype(BF16)
        imp_s = _nn(hi, wsel) + _nn(lo, wsel)
        forced = jnp.where(lane == 0, jnp.inf, jnp.where(lane >= cur - 1, jnp.inf, imp_s))
        keys.append(_order_key(jnp.where(admiss, forced, NEG_INF)))
    kk = jnp.concatenate(keys, axis=0)
    lane2 = _stack_rows(lane, N_KV)

    def count(pred):
        return jnp.sum(pred(kk, lane2), axis=1, keepdims=True)

    thr, xcut = _kth_select(count, N_KV * TQ, float(n_sel), 7, LANES)
    tie = jnp.where(kk == thr, jnp.where(lane2 <= xcut, 1.0, 0.0), 0.0)
    sel = jnp.where(kk > thr, 1.0, tie).astype(BF16)

    _flash_init(m_ref, acc_ref)
    blk_row =lax.broadcasted_iota(I32, (LANES, CH), 0)
    blk_col = lax.broadcasted_iota(I32, (LANES, CH), 1) >> 6
    row_c = t0 + lax.broadcasted_iota(I32, (TQ, CH), 0)
    col_c = lax.broadcasted_iota(I32, (TQ, CH), 1)

    def slc_chunk(c):
        off = pl.multiple_of(c * CH, CH)
        expand = jnp.where(blk_row == blk_col + c * (CH // SLC_BLOCK), 1.0, 0.0).astype(BF16)
        tok = _nn(sel, expand)
        causal = col_c + off <= row_c
        parts = []
        for g in range(N_KV):
            bias_g = jnp.where(causal, jnp.where(tok[g * TQ:(g + 1) * TQ] > 0.5, 0.0, NEG_INF), NEG_INF)
            parts.extend([bias_g] * GRP)
        sc = _nt(q, ks_ref[pl.ds(off, CH), :]) + jnp.concatenate(parts, axis=0)
        _flash_update(sc, vs_ref[pl.ds(off, CH), :], m_ref, acc_ref, slice(None))

    _unrolled_loop(nch, slc_chunk)
    o_slc = _flash_result(acc_ref)

    span = NSA_WINDOW + TQ
    start = pl.multiple_of(jnp.maximum(t0 - NSA_WINDOW, 0), TQ)
    t_w = _stack_rows(t0 + lax.broadcasted_iota(I32, (TQ, span), 0), N_HEADS)
    diff = t_w - (start + lax.broadcasted_iota(I32, (rows_all, span), 1))
    sw = _nt(q, kw_ref[pl.ds(start, span), :])
    sw = jnp.where(diff >= 0, jnp.where(diff < NSA_WINDOW, sw, NEG_INF), NEG_INF)
    pw = jnp.exp(sw - jnp.max(sw, axis=1, keepdims=True))
    o_win = _nn(pw.astype(BF16), vw_ref[pl.ds(start, span), :])
    o_win = o_win[:, 0:LANES] / o_win[:, LANES:2 * LANES]

    gates = gt_ref[...]
    outs = []
    for h in range(N_HEADS):
        rs = slice(h * TQ, (h + 1) * TQ)
        outs.append(gates[:, 3 * h:3 * h + 1] * o_cmp[rs] + gates[:, 3 * h + 1:3 * h + 2] * o_slc[rs]
                    + gates[:, 3 * h + 2:3 * h + 3] * o_win[rs])
    _gated_store(jnp.concatenate(outs, axis=0), z_ref, o_ref, TQ)


def _mla_prep_kernel(cq_ref, ckv_ref, kr_ref, wq_ref, wkv_ref, lat_ref, gains_ref, cos_ref, sa_ref, sb_ref,
                     bd64_ref, bd32_ref, q_ref, k_ref, v_ref):
    lane = lax.broadcasted_iota(I32, (TM, LANES), 1)
    lo_half = lane < NOPE_DIM
    cos, sa, sb = cos_ref[...], sa_ref[...], sb_ref[...]
    bd64, bd32 = bd64_ref[...], bd32_ref[...]
    scale = (NOPE_DIM + ROPE_DIM) ** -0.5

    def rope_slab(s, gain):
        return _rope(_seg_norm(s, bd32, gain, ROPE_DIM), cos, sa, sb, ROPE_DIM // 2)

    q = _nn(_row_rms(cq_ref[...], lat_ref[0:1, :]).astype(BF16), wq_ref[...])
    kv = _nn(_row_rms(ckv_ref[...], lat_ref[1:2, 0:KV_LORA]).astype(BF16), wkv_ref[...])
    k_rope = pltpu.roll(rope_slab(kr_ref[...], gains_ref[3:4, :]), NOPE_DIM, 1)
    q_rope = [rope_slab(q[:, 512 + LANES * j:512 + LANES * (j + 1)], gains_ref[1:2, :]) * scale for j in range(2)]
    for j in range(4):
        qn = _seg_norm(q[:, LANES * j:LANES * (j + 1)], bd64, gains_ref[0:1, :], NOPE_DIM) * scale
        kn = _seg_norm(kv[:, LANES * j:LANES * (j + 1)], bd64, gains_ref[2:3, :], NOPE_DIM)
        for e in range(2):
            h = 2 * j + e
            qn_h = qn if e == 0 else pltpu.roll(qn, NOPE_DIM, 1)
            kn_h = kn if e == 0 else pltpu.roll(kn, NOPE_DIM, 1)
            shift = (NOPE_DIM - ROPE_DIM * (h % 4)) % LANES
            qr = q_rope[h // 4]
            qr_h = qr if shift == 0 else pltpu.roll(qr, shift, 1)
            q_ref[h] = jnp.where(lo_half, qn_h, jnp.where(lane < NOPE_DIM + ROPE_DIM, qr_h, 0.0)).astype(BF16)
            k_ref[h] = jnp.where(lo_half, kn_h, k_rope).astype(BF16)
    for j in range(4):
        v_ref[j] = _with_ones(kv[:, 512 + LANES * j:512 + LANES * (j + 1)])


def _mla_kernel(q_ref, k_ref, v_ref, z_ref, o_ref, m_ref, acc_ref):
    i = pl.program_id(2)
    tq = TQ_MLA
    _flash_init(m_ref, acc_ref)

    def step(c, masked):
        off = pl.multiple_of(c * CH, CH)
        v = v_ref[0, pl.ds(off, CH), :]
        for e in range(2):
            s = _nt(q_ref[e], k_ref[e, pl.ds(off, CH), :])
            if masked:
                s = jnp.where(lax.broadcasted_iota(I32, (tq, CH), 1) <= lax.broadcasted_iota(I32, (tq, CH), 0),
                              s, NEG_INF)
            _flash_update(s, v, m_ref, acc_ref, slice(e * tq, (e + 1) * tq))

    _unrolled_loop(i, lambda c: step(c, False))
    step(i, True)
    o = _flash_result(acc_ref)
    lo_half = lax.broadcasted_iota(I32, (tq, LANES), 1) < NOPE_DIM
    slab = jnp.where(lo_half, o[0:tq], o[tq:2 * tq])
    o_ref[...] = (slab * _silu(z_ref[...])).astype(BF16)


def _rope_tables(pos, dim):
    inv = jnp.power(jnp.float32(ROPE_THETA), -jnp.arange(0, dim, 2, dtype=F32) / dim)
    ang = pos.astype(F32)[:, None] * inv[None, :]
    cos, sin = jnp.cos(ang), jnp.sin(ang)
    reps = LANES // dim
    zero = jnp.zeros_like(sin)
    cos_t = jnp.tile(jnp.concatenate([cos, cos], axis=1), (1, reps))
    sa_t = jnp.tile(jnp.concatenate([zero, sin], axis=1), (1, reps))
    sb_t = jnp.tile(jnp.concatenate([-sin, zero], axis=1), (1, reps))
    return cos_t, sa_t, sb_t


def _block_diag(group):
    r = jnp.arange(LANES) // group
    return (r[:, None] == r[None, :]).astype(BF16)


def _tile_gain(g, width=LANES):
    return jnp.tile(g.astype(F32), width // g.shape[0])


def _pad_lanes(g, width=LANES):
    return jnp.concatenate([g.astype(F32), jnp.zeros((width - g.shape[0],), F32)])


def _pad_cols(w, width):
    return jnp.concatenate([w, jnp.zeros((w.shape[0], width - w.shape[1]), w.dtype)], axis=1)


def _params(*sem, flags=None):
    return pltpu.CompilerParams(dimension_semantics=sem, vmem_limit_bytes=VMEM_LIMIT, flags=flags)


def kernel(x, ab_norm, ab_w_in, a_qk_norm, a_kidx_norm, b_qk_norm, b_sinks, ab_w_out, cd_norm, cd_w_in, c_q_norm, c_k_norm, c_cmp_pe, c_cmp_w1, c_cmp_w2, d_q_lat_norm, d_kv_lat_norm, d_w_uq, d_w_ukv, d_nope_norm, d_rope_norm, cd_w_out):
    bsz, seq, d = x.shape
    n = bsz * seq
    assert d == D_MODEL and seq % CH == 0 and seq >= NSA_WINDOW + TQ and seq // SLC_BLOCK <= LANES
    nq = seq // TQ
    nrow = n // TM
    srow = seq // TM
    xf = x.reshape(n, d)
    pos = jnp.arange(seq)
    cos64, sa64, sb64 = _rope_tables(pos, HEAD_DIM)
    cos32, sa32, sb32 = _rope_tables(pos, ROPE_DIM)
    bd64, bd32 = _block_diag(HEAD_DIM), _block_diag(ROPE_DIM)

    def row_spec(width, rows=TM):
        return pl.BlockSpec((rows, width), lambda i: (i, 0))

    def const_spec(shape):
        return pl.BlockSpec(shape, lambda i: tuple(0 for _ in shape))

    tab_spec = pl.BlockSpec((TM, LANES), lambda i: (i % srow, 0))
    head_spec = lambda nh: pl.BlockSpec((nh, TM, LANES), lambda i: (0, i, 0))

    wa = ab_w_in[0]
    w_ab = jnp.concatenate([wa[:, 0:1088], _pad_cols(wa[:, 1088:1092], 64), wa[:, 1092:]], axis=1).astype(BF16)
    gains_ab = jnp.stack([_tile_gain(a_qk_norm[0, 0]), _tile_gain(a_qk_norm[0, 1]), _pad_lanes(a_kidx_norm[0]),
                          _tile_gain(b_qk_norm[0, 0]), _tile_gain(b_qk_norm[0, 1])]
                         + [jnp.zeros((LANES,), F32)] * 3)
    sds = jax.ShapeDtypeStruct
    qa, ka, va, qi, ki, wi, za, qb, kb, vb, zb = pl.pallas_call(
        _proj_ab_kernel,
        grid=(nrow,),
        in_specs=[row_spec(d), const_spec((1, d)), const_spec((d, AB_W)), tab_spec, tab_spec, tab_spec,
                  const_spec((LANES, LANES)), const_spec((8, LANES))],
        out_specs=[head_spec(8), row_spec(LANES), row_spec(2 * LANES), head_spec(4), row_spec(LANES), row_spec(LANES),
                   row_spec(512), head_spec(8), row_spec(LANES), row_spec(2 * LANES), row_spec(512)],
        out_shape=[sds((8, n, LANES), BF16), sds((n, LANES), BF16), sds((n, 2 * LANES), BF16),
                   sds((4, n, LANES), BF16), sds((n, LANES), BF16), sds((n, LANES), F32), sds((n, 512), F32),
                   sds((8, n, LANES), BF16), sds((n, LANES), BF16), sds((n, 2 * LANES), BF16), sds((n, 512), F32)],
        compiler_params=_params("parallel"),
    )(xf, ab_norm[0][None, :], w_ab, cos64, sa64, sb64, bd64, gains_ab)

    qt_heads = lambda nh: pl.BlockSpec((nh, TQ, LANES), lambda b, i: (0, b * nq + i, 0))
    qt_rows = lambda width: pl.BlockSpec((TQ, width), lambda b, i: (b * nq + i, 0))
    seq_rows = pl.BlockSpec((seq, LANES), lambda b, i: (b, 0))
    seq_rows_v = pl.BlockSpec((seq, 2 * LANES), lambda b, i: (b, 0))
    att_scratch = [pltpu.VMEM((N_HEADS * TQ, LANES), F32), pltpu.VMEM((N_HEADS * TQ, 2 * LANES), F32)]

    mix_a = pl.pallas_call(
        functools.partial(_dsa_kernel, seq=seq, topk=min(DSA_TOPK, seq // 4)),
        grid=(bsz, nq),
        in_specs=[qt_heads(4), qt_rows(LANES), seq_rows, qt_heads(8), seq_rows, seq_rows_v, qt_rows(512)],
        out_specs=qt_rows(512),
        out_shape=sds((n, 512), BF16),
        scratch_shapes=[pltpu.VMEM((seq // CH, TQ, CH), I32)] + att_scratch,
        compiler_params=_params("parallel", "arbitrary"),
    )(qi, wi, ki, qa, ka, va, za)

    prev_rows = lambda width: pl.BlockSpec((TQ, width), lambda b, i: (b * nq + jnp.maximum(i - 1, 0), 0))
    sink_rows = jnp.broadcast_to(jnp.repeat(b_sinks[0].astype(F32), TQ)[:, None], (N_HEADS * TQ, LANES))
    mix_b = pl.pallas_call(
        _swa_kernel,
        grid=(bsz, nq),
        in_specs=[qt_heads(8), prev_rows(LANES), qt_rows(LANES), prev_rows(2 * LANES), qt_rows(2 * LANES),
                  pl.BlockSpec((N_HEADS * TQ, LANES), lambda b, i: (0, 0)), qt_rows(512)],
        out_specs=qt_rows(512),
        out_shape=sds((n, 512), BF16),
        compiler_params=_params("parallel", "arbitrary"),
    )(qb, kb, kb, vb, vb, sink_rows, zb)

    def out_proj(xin, m0, m1, w):
        return pl.pallas_call(
            _out_proj_kernel,
            grid=(nrow,),
            in_specs=[row_spec(d), row_spec(512), row_spec(512), const_spec((2 * 512, d))],
            out_specs=row_spec(d),
            out_shape=sds((n, d), F32),
            compiler_params=_params("parallel"),
        )(xin, m0, m1, w.astype(BF16))

    x1 = out_proj(xf, mix_a, mix_b, ab_w_out[0])

    wc = cd_w_in[0]
    w_cd = jnp.concatenate([wc[:, 0:1280], _pad_cols(wc[:, 1280:1304], LANES), wc[:, 1304:2200],
                            _pad_cols(wc[:, 2200:2232], LANES), wc[:, 2232:]], axis=1).astype(BF16)
    gains_cd = jnp.stack([_tile_gain(c_q_norm[0]), _tile_gain(c_k_norm[0, 1]), _tile_gain(c_k_norm[0, 2])]
                         + [jnp.zeros((LANES,), F32)] * 5)
    qc, kc_raw, vc_raw, ks, vs, kw, vw, gates, zc, cq, ckv, kr, zd = pl.pallas_call(
        _proj_cd_kernel,
        grid=(nrow,),
        in_specs=[row_spec(d), const_spec((1, d)), const_spec((d, AB_W)), tab_spec, tab_spec, tab_spec,
                  const_spec((LANES, LANES)), const_spec((8, LANES))],
        out_specs=[head_spec(8), row_spec(LANES), row_spec(LANES), row_spec(LANES), row_spec(2 * LANES),
                   row_spec(LANES), row_spec(2 * LANES), row_spec(LANES), row_spec(512), row_spec(256),
                   row_spec(LANES), row_spec(LANES), row_spec(512)],
        out_shape=[sds((8, n, LANES), BF16), sds((n, LANES), F32), sds((n, LANES), F32), sds((n, LANES), BF16),
                   sds((n, 2 * LANES), BF16), sds((n, LANES), BF16), sds((n, 2 * LANES), BF16), sds((n, LANES), F32),
                   sds((n, 512), F32), sds((n, 256), F32), sds((n, LANES), F32), sds((n, LANES), F32),
                   sds((n, 512), F32)],
        compiler_params=_params("parallel"),
    )(x1, cd_norm[0][None, :], w_cd, cos64, sa64, sb64, bd64, gains_cd)

    ncmp = seq // CMP_STRIDE
    cw = CMP_STRIDE * N_KV * HEAD_DIM
    eye = jnp.eye(N_KV, dtype=F32)
    ccos, csa, csb = _rope_tables(jnp.arange(ncmp) * CMP_STRIDE + (2 * CMP_STRIDE - 1), HEAD_DIM)

    def compress(raw, which, is_key):
        w1 = c_cmp_w1[0, which].reshape(2, CMP_STRIDE, HEAD_DIM, -1)
        hid = w1.shape[-1]
        w1e = jnp.einsum("tjdu,gh->tjgdhu", w1, eye).reshape(2, cw, N_KV * hid).astype(BF16)
        w2e = jnp.einsum("ud,gh->guhd", c_cmp_w2[0, which], eye).reshape(N_KV * hid, N_KV * HEAD_DIM).astype(BF16)
        pe = c_cmp_pe[0, which].reshape(2, CMP_STRIDE, 1, HEAD_DIM)
        pe = jnp.broadcast_to(pe, (2, CMP_STRIDE, N_KV, HEAD_DIM)).reshape(2, cw)
        return pl.pallas_call(
            functools.partial(_compress_kernel, is_key=is_key),
            grid=(bsz,),
            in_specs=[pl.BlockSpec((ncmp, cw), lambda b: (b, 0)), const_spec((2, cw)),
                      const_spec((cw, N_KV * hid)), const_spec((cw, N_KV * hid)),
                      const_spec((N_KV * hid, LANES)), const_spec((1, LANES)),
                      const_spec((ncmp, LANES)), const_spec((ncmp, LANES)), const_spec((ncmp, LANES)),
                      const_spec((LANES, LANES))],
            out_specs=pl.BlockSpec((ncmp, LANES), lambda b: (b, 0)),
            out_shape=sds((bsz * ncmp, LANES), BF16),
            compiler_params=_params("parallel"),
        )(raw.reshape(bsz * ncmp, cw), pe, w1e[0], w1e[1], w2e, _tile_gain(c_k_norm[0, 0])[None, :],
          ccos, csa, csb, bd64)

    kc = compress(kc_raw, 0, True)
    vc = compress(vc_raw, 1, False)

    ratio = SLC_BLOCK // CMP_STRIDE
    mm = jnp.arange(ncmp)[:, None]
    jj = jnp.arange(LANES)[None, :]
    wsel = (jnp.where((mm == ratio * jj - 1) | (mm == ratio * jj + ratio - 1), 1.0, 0.0)
            + jnp.where((mm >= ratio * jj) & (mm < ratio * jj + ratio - 1), 2.0, 0.0))
    wsel = jnp.where((mm < ncmp - 1) & (jj < seq // SLC_BLOCK), wsel, 0.0).astype(BF16)

    cmp_rows = pl.BlockSpec((ncmp, LANES), lambda b, i: (b, 0))
    mix_c = pl.pallas_call(
        functools.partial(_nsa_kernel, seq=seq, n_sel=min(SLC_TOPN, seq // SLC_BLOCK)),
        grid=(bsz, nq),
        in_specs=[qt_heads(8), cmp_rows, cmp_rows, seq_rows, seq_rows_v, seq_rows, seq_rows_v, qt_rows(LANES),
                  qt_rows(512), pl.BlockSpec((ncmp, LANES), lambda b, i: (0, 0))],
        out_specs=qt_rows(512),
        out_shape=sds((n, 512), BF16),
        scratch_shapes=att_scratch,
        compiler_params=_params("parallel", "arbitrary"),
    )(qc, kc, vc, ks, vs, kw, vw, gates, zc, wsel)

    wq = d_w_uq[0].reshape(Q_LORA, N_HEADS, NOPE_DIM + ROPE_DIM)
    wq = jnp.concatenate([wq[:, :, :NOPE_DIM].reshape(Q_LORA, -1), wq[:, :, NOPE_DIM:].reshape(Q_LORA, -1)],
                         axis=1).astype(BF16)
    wkv = d_w_ukv[0].reshape(KV_LORA, N_HEADS, NOPE_DIM + HEAD_DIM)
    wkv = jnp.concatenate([wkv[:, :, :NOPE_DIM].reshape(KV_LORA, -1), wkv[:, :, NOPE_DIM:].reshape(KV_LORA, -1)],
                          axis=1).astype(BF16)
    lat_gains = jnp.stack([d_q_lat_norm[0].astype(F32), _pad_lanes(d_kv_lat_norm[0], Q_LORA)]
                          + [jnp.zeros((Q_LORA,), F32)] * 6)
    gains_d = jnp.stack([_tile_gain(d_nope_norm[0, 0]), _tile_gain(d_rope_norm[0, 0]), _tile_gain(d_nope_norm[0, 1]),
                         _pad_lanes(d_rope_norm[0, 1])] + [jnp.zeros((LANES,), F32)] * 4)
    q_cat, k_cat, v_d = pl.pallas_call(
        _mla_prep_kernel,
        grid=(nrow,),
        in_specs=[row_spec(Q_LORA), row_spec(LANES), row_spec(LANES), const_spec((Q_LORA, 768)),
                  const_spec((KV_LORA, 1024)), const_spec((8, Q_LORA)), const_spec((8, LANES)),
                  tab_spec, tab_spec, tab_spec, const_spec((LANES, LANES)), const_spec((LANES, LANES))],
        out_specs=[head_spec(8), head_spec(8), pl.BlockSpec((4, TM, 2 * LANES), lambda i: (0, i, 0))],
        out_shape=[sds((8, n, LANES), BF16), sds((8, n, LANES), BF16), sds((4, n, 2 * LANES), BF16)],
        compiler_params=_params("parallel"),
    )(cq, ckv, kr, wq, wkv, lat_gains, gains_d, cos32, sa32, sb32, bd64, bd32)

    nqm = seq // TQ_MLA
    mix_d = pl.pallas_call(
        _mla_kernel,
        grid=(bsz, N_HEADS // 2, nqm),
        in_specs=[pl.BlockSpec((2, TQ_MLA, LANES), lambda b, hp, i: (hp, b * nqm + i, 0)),
                  pl.BlockSpec((2, seq, LANES), lambda b, hp, i: (hp, b, 0)),
                  pl.BlockSpec((1, seq, 2 * LANES), lambda b, hp, i: (hp, b, 0)),
                  pl.BlockSpec((TQ_MLA, LANES), lambda b, hp, i: (b * nqm + i, hp))],
        out_specs=pl.BlockSpec((TQ_MLA, LANES), lambda b, hp, i: (b * nqm + i, hp)),
        out_shape=sds((n, 512), BF16),
        scratch_shapes=[pltpu.VMEM((2 * TQ_MLA, LANES), F32), pltpu.VMEM((2 * TQ_MLA, 2 * LANES), F32)],
        compiler_params=_params("parallel", "parallel", "arbitrary"),
    )(q_cat, k_cat, v_d, zd)

    x2 = out_proj(x1, mix_c, mix_d, cd_w_out[0])
    return x2.reshape(bsz, seq, d)
```

```python
import functools

import jax
import jax.numpy as jnp
from jax import lax
from jax.experimental import pallas as pl
from jax.experimental.pallas import tpu as pltpu

F32, BF16, I32 = jnp.float32, jnp.bfloat16, jnp.int32

D_MODEL = 1024
HEAD_DIM = 64
N_HEADS = 8
N_KV = 2
GRP = N_HEADS // N_KV
IDX_HEADS = 4
DSA_TOPK = 256
SWA_WINDOW = 128
CMP_STRIDE = 16
SLC_BLOCK = 64
SLC_TOPN = 16
NSA_WINDOW = 512
Q_LORA = 256
KV_LORA = 128
NOPE_DIM = 64
ROPE_DIM = 32
ROPE_THETA = 10000.0
EPS = 1e-6

LANES = 128
TM = 512
TQ = 128
CH = 512
TQ_MLA = 512
UNROLL = 4
VMEM_LIMIT = 56 * 1024 * 1024

NEG_INF = float("-inf")
M_INIT = -1e30
INT_MIN = -(2 ** 31)

AB_W = 2944


def _nn(a, b):
    return jnp.dot(a, b, preferred_element_type=F32)


def _nt(a, b):
    return lax.dot_general(a, b, (((1,), (1,)), ((), ())), preferred_element_type=F32)


def _sigmoid(z):
    return 1.0 / (1.0 + jnp.exp(-z))


def _silu(z):
    return z * _sigmoid(z)


def _row_rms(x, g):
    return x * lax.rsqrt(jnp.mean(x * x, axis=-1, keepdims=True) + EPS) * g


def _seg_norm(s, bd, gain, group):
    s2 = s * s
    hi = s2.astype(BF16)
    lo = (s2 - hi.astype(F32)).astype(BF16)
    ss = _nn(hi, bd) + _nn(lo, bd)
    return s * lax.rsqrt(ss * (1.0 / group) + EPS) * gain


def _rope(s, cos, sa, sb, half):
    return s * cos + pltpu.roll(s, half, 1) * sa + pltpu.roll(s, LANES - half, 1) * sb


def _order_key(v):
    bits = lax.bitcast_convert_type(v, I32)
    return bits ^ ((bits >> 31) & 0x7FFFFFFF)


def _stack_rows(x, n):
    return jnp.concatenate([x] * n, axis=0)


def _with_ones(v):
    return jnp.concatenate([v.astype(BF16), jnp.ones(v.shape, BF16)], axis=1)


def _flash_update(s, v_ext, m_ref, acc_ref, rows):
    m_prev = m_ref[rows]
    m_new = jnp.maximum(m_prev, jnp.max(s, axis=1, keepdims=True))
    alpha = jnp.exp(m_prev - m_new)
    p = jnp.exp(s - jnp.concatenate([m_new] * (s.shape[1] // LANES), axis=1))
    acc_ref[rows] = jnp.concatenate([alpha, alpha], axis=1) * acc_ref[rows] + _nn(p.astype(BF16), v_ext)
    m_ref[rows] = m_new


def _unrolled_loop(n, step, unroll=UNROLL):
    def body(j, carry):
        for u in range(unroll):
            step(j * unroll + u)
        return carry

    lax.fori_loop(0, n // unroll, body, 0)
    base = (n // unroll) * unroll
    p = unroll // 2
    while p >= 1:
        def tail(base=base, p=p):
            for u in range(p):
                step(base + u)

        pl.when((n & p) != 0)(tail)
        base = base + (n & p)
        p //= 2


def _flash_init(m_ref, acc_ref):
    m_ref[...] = jnp.full(m_ref.shape, M_INIT, F32)
    acc_ref[...] = jnp.zeros(acc_ref.shape, F32)


def _flash_result(acc_ref):
    acc = acc_ref[...]
    return acc[:, 0:LANES] / jnp.maximum(acc[:, LANES:2 * LANES], 1e-30)


def _place_q_heads(y, bd, gain, cos, sa, sb, out_ref, scale):
    lo_half = lax.broadcasted_iota(I32, (y.shape[0], LANES), 1) < HEAD_DIM
    for j in range(4):
        s = y[:, LANES * j:LANES * (j + 1)]
        s = _rope(_seg_norm(s, bd, gain, HEAD_DIM), cos, sa, sb, HEAD_DIM // 2) * scale
        r = pltpu.roll(s, HEAD_DIM, 1)
        if j < 2:
            out_ref[2 * j] = jnp.where(lo_half, s, 0.0).astype(BF16)
            out_ref[2 * j + 1] = jnp.where(lo_half, r, 0.0).astype(BF16)
        else:
            out_ref[2 * j] = jnp.where(lo_half, 0.0, r).astype(BF16)
            out_ref[2 * j + 1] = jnp.where(lo_half, 0.0, s).astype(BF16)


def _gated_store(o, z_ref, o_ref, tq):
    lo_half = lax.broadcasted_iota(I32, (tq, LANES), 1) < HEAD_DIM
    for j in range(4):
        a = o[(2 * j) * tq:(2 * j + 1) * tq]
        b = o[(2 * j + 1) * tq:(2 * j + 2) * tq]
        if j < 2:
            slab = jnp.where(lo_half, a, pltpu.roll(b, HEAD_DIM, 1))
        else:
            slab = jnp.where(lo_half, pltpu.roll(a, HEAD_DIM, 1), b)
        z = z_ref[:, LANES * j:LANES * (j + 1)]
        o_ref[:, LANES * j:LANES * (j + 1)] = (slab * _silu(z)).astype(BF16)


def _proj_ab_kernel(x_ref, g_ref, w_ref, cos_ref, sa_ref, sb_ref, bd_ref, gains_ref,
                    qa_ref, ka_ref, va_ref, qi_ref, ki_ref, wi_ref, za_ref,
                    qb_ref, kb_ref, vb_ref, zb_ref):
    xn = _row_rms(x_ref[...], g_ref[...]).astype(BF16)
    cos, sa, sb, bd = cos_ref[...], sa_ref[...], sb_ref[...], bd_ref[...]
    lo_half = lax.broadcasted_iota(I32, (TM, LANES), 1) < HEAD_DIM

    def proj(c0, n):
        return _nn(xn, w_ref[:, c0:c0 + n])

    def k_slab(c0, gain):
        return _rope(_seg_norm(proj(c0, LANES), bd, gain, HEAD_DIM), cos, sa, sb, HEAD_DIM // 2)

    _place_q_heads(proj(0, 512), bd, gains_ref[0:1, :], cos, sa, sb, qa_ref, HEAD_DIM ** -0.5)
    ka_ref[...] = k_slab(512, gains_ref[1:2, :]).astype(BF16)
    va_ref[...] = _with_ones(proj(640, LANES))
    qi = proj(768, 256)
    for j in range(2):
        s = _rope(qi[:, LANES * j:LANES * (j + 1)], cos, sa, sb, HEAD_DIM // 2)
        qi_ref[2 * j] = jnp.where(lo_half, s, 0.0).astype(BF16)
        qi_ref[2 * j + 1] = jnp.where(lo_half, pltpu.roll(s, HEAD_DIM, 1), 0.0).astype(BF16)
    kiw = proj(1024, LANES)
    wi_ref[...] = kiw
    ki_ref[...] = _rope(_seg_norm(kiw, bd, gains_ref[2:3, :], HEAD_DIM), cos, sa, sb, HEAD_DIM // 2).astype(BF16)
    za_ref[...] = proj(1152, 512)
    _place_q_heads(proj(1664, 512), bd, gains_ref[3:4, :], cos, sa, sb, qb_ref, HEAD_DIM ** -0.5)
    kb_ref[...] = k_slab(2176, gains_ref[4:5, :]).astype(BF16)
    vb_ref[...] = _with_ones(proj(2304, LANES))
    zb_ref[...] = proj(2432, 512)


def _count_chunks(keys_ref, nch, rows, pred):
    def body(c, acc):
        kk = keys_ref[c]
        for j in range(CH // LANES):
            acc = acc + pred(kk[:, j * LANES:(j + 1) * LANES])
        return acc

    acc = lax.fori_loop(0, nch, body, jnp.zeros((rows, LANES), F32))
    return jnp.sum(acc, axis=1, keepdims=True)


def _kth_threshold(count, rows, k):
    def bit_body(it, tb):
        cand_b = tb | jnp.left_shift(jnp.int32(1), 31 - it)
        cand_s = cand_b ^ INT_MIN
        cnt = count(lambda kk: jnp.where(kk >= cand_s, 1.0, 0.0))
        return jnp.where(cnt >= k, cand_b, tb)

    tb = lax.fori_loop(0, 32, bit_body, jnp.zeros((rows, 1), I32))
    thr = tb ^ INT_MIN
    need = k - count(lambda kk: jnp.where(kk > thr, 1.0, 0.0))
    return thr, need


def _tie_prefix(kk, thr, tri, before):
    eq = jnp.where(kk == thr, 1.0, 0.0).astype(BF16)
    return _nn(eq, tri) + before


def _dsa_kernel(qi_ref, wi_ref, ki_ref, qa_ref, ka_ref, va_ref, za_ref, tri_ref, o_ref,
                keys_ref, m_ref, acc_ref, *, topk):
    i = pl.program_id(1)
    t0 = i * TQ
    nch = (t0 + TQ + CH - 1) // CH
    row_c = t0 + lax.broadcasted_iota(I32, (TQ, CH), 0)
    col_c = lax.broadcasted_iota(I32, (TQ, CH), 1)

    qi = qi_ref[...].reshape(IDX_HEADS * TQ, LANES)
    w = wi_ref[...]
    idx_scale = (HEAD_DIM * IDX_HEADS) ** -0.5
    ws = [w[:, HEAD_DIM + h:HEAD_DIM + h + 1] * idx_scale for h in range(IDX_HEADS)]

    def score_chunk(c, carry):
        off = pl.multiple_of(c * CH, CH)
        lg = _nt(qi, ki_ref[pl.ds(off, CH), :])
        sc = ws[0] * jnp.maximum(lg[0:TQ], 0.0)
        for h in range(1, IDX_HEADS):
            sc = sc + ws[h] * jnp.maximum(lg[h * TQ:(h + 1) * TQ], 0.0)
        sc = jnp.where(sc == 0.0, 0.0, sc)
        keys_ref[c] = jnp.where(col_c + off <= row_c, _order_key(sc), INT_MIN)
        return carry

    lax.fori_loop(0, nch, score_chunk, 0)

    count = functools.partial(_count_chunks, keys_ref, nch, TQ)
    thr, need = _kth_threshold(count, TQ, float(topk))

    tri = tri_ref[...]

    def select_chunk(c, before):
        kk = keys_ref[c]
        pref = _tie_prefix(kk, thr, tri, before)
        tie = jnp.where(kk == thr, jnp.where(pref <= need, 0.0, NEG_INF), NEG_INF)
        bias = jnp.where(col_c + c * CH <= row_c, jnp.where(kk > thr, 0.0, tie), NEG_INF)
        keys_ref[c] = lax.bitcast_convert_type(bias, I32)
        return pref[:, CH - 1:CH]

    lax.fori_loop(0, nch, select_chunk, jnp.zeros((TQ, 1), F32))

    _flash_init(m_ref, acc_ref)
    q_all = qa_ref[...].reshape(N_HEADS * TQ, LANES)

    def att_chunk(c):
        off = pl.multiple_of(c * CH, CH)
        bias = _stack_rows(lax.bitcast_convert_type(keys_ref[c], F32), N_HEADS)
        s = _nt(q_all, ka_ref[pl.ds(off, CH), :]) + bias
        _flash_update(s, va_ref[pl.ds(off, CH), :], m_ref, acc_ref, slice(None))

    _unrolled_loop(nch, att_chunk)
    _gated_store(_flash_result(acc_ref), za_ref, o_ref, TQ)


def _swa_kernel(q_ref, kp_ref, kc_ref, vp_ref, vc_ref, sink_ref, z_ref, o_ref):
    i = pl.program_id(1)
    q = q_ref[...].reshape(N_HEADS * TQ, LANES)
    qi = _stack_rows(lax.broadcasted_iota(I32, (TQ, TQ), 0), N_HEADS)
    kj = lax.broadcasted_iota(I32, (N_HEADS * TQ, TQ), 1)
    no_prev = jnp.where(i > 0, 0, 1 << 20)
    sp = jnp.where(kj > qi + no_prev, _nt(q, kp_ref[...]), NEG_INF)
    sc = jnp.where(kj <= qi, _nt(q, kc_ref[...]), NEG_INF)
    sink = sink_ref[...]
    m = jnp.maximum(jnp.maximum(jnp.max(sp, axis=1, keepdims=True), jnp.max(sc, axis=1, keepdims=True)), sink)
    pp = jnp.exp(sp - m)
    pc = jnp.exp(sc - m)
    o = _nn(pp.astype(BF16), vp_ref[...]) + _nn(pc.astype(BF16), vc_ref[...])
    den = o[:, LANES:2 * LANES] + jnp.exp(sink - m)
    _gated_store(o[:, 0:LANES] / den, z_ref, o_ref, TQ)


def _out_proj_kernel(x_ref, ma_ref, mb_ref, w_ref, o_ref):
    half = ma_ref.shape[1]
    o_ref[...] = x_ref[...] + _nn(ma_ref[...], w_ref[0:half, :]) + _nn(mb_ref[...], w_ref[half:2 * half, :])


def _proj_cd_kernel(x_ref, g_ref, w_ref, cos_ref, sa_ref, sb_ref, bd_ref, gains_ref,
                    qc_ref, kc_ref, vc_ref, ks_ref, vs_ref, kw_ref, vw_ref, gt_ref, zc_ref,
                    cq_ref, ckv_ref, kr_ref, zd_ref):
    xn = _row_rms(x_ref[...], g_ref[...]).astype(BF16)
    cos, sa, sb, bd = cos_ref[...], sa_ref[...], sb_ref[...], bd_ref[...]

    def proj(c0, n):
        return _nn(xn, w_ref[:, c0:c0 + n])

    def k_slab(c0, gain):
        return _rope(_seg_norm(proj(c0, LANES), bd, gain, HEAD_DIM), cos, sa, sb, HEAD_DIM // 2)

    _place_q_heads(proj(0, 512), bd, gains_ref[0:1, :], cos, sa, sb, qc_ref, HEAD_DIM ** -0.5)
    kc_ref[...] = proj(512, LANES)
    vc_ref[...] = proj(640, LANES)
    ks_ref[...] = k_slab(768, gains_ref[1:2, :]).astype(BF16)
    vs_ref[...] = _with_ones(proj(896, LANES))
    kw_ref[...] = k_slab(1024, gains_ref[2:3, :]).astype(BF16)
    vw_ref[...] = _with_ones(proj(1152, LANES))
    gt_ref[...] = _sigmoid(proj(1280, LANES))
    zc_ref[...] = proj(1408, 512)
    cq_ref[...] = proj(1920, 256)
    ckv_ref[...] = proj(2176, LANES)
    kr_ref[...] = proj(2304, LANES)
    zd_ref[...] = proj(2432, 512)


def _compress_kernel(c_ref, pe_ref, w1a_ref, w1b_ref, w2_ref, gain_ref, cos_ref, sa_ref, sb_ref, bd_ref,
                     o_ref, *, is_key):
    c = c_ref[...]
    n = c.shape[0]
    a = _nn((c + pe_ref[0:1, :]).astype(BF16), w1a_ref[...])
    b = _nn((c + pe_ref[1:2, :]).astype(BF16), w1b_ref[...])
    h = _silu(a + pltpu.roll(b, n - 1, 0))
    y = _nn(h.astype(BF16), w2_ref[...])
    if is_key:
        y = _rope(_seg_norm(y, bd_ref[...], gain_ref[...], HEAD_DIM), cos_ref[...], sa_ref[...], sb_ref[...],
                  HEAD_DIM // 2)
    o_ref[...] = y.astype(BF16)


def _nsa_kernel(q_ref, kc_ref, vc_ref, ks_ref, vs_ref, kw_ref, vw_ref, gt_ref, z_ref, wsel_ref, tri_ref, o_ref,
                m_ref, acc_ref, *, n_sel):
    i = pl.program_id(1)
    t0 = i * TQ
    nch = (t0 + TQ + CH - 1) // CH
    ncmp = kc_ref.shape[0]
    rows_all = N_HEADS * TQ
    q = q_ref[...].reshape(rows_all, LANES)

    t_cmp = _stack_rows(t0 + lax.broadcasted_iota(I32, (TQ, ncmp), 0), N_HEADS)
    cend = lax.broadcasted_iota(I32, (rows_all, ncmp), 1) * CMP_STRIDE + (2 * CMP_STRIDE - 1)
    s = jnp.where(cend <= t_cmp, _nt(q, kc_ref[...]), NEG_INF)
    m = jnp.max(s, axis=1, keepdims=True)
    m = jnp.where(m > NEG_INF, m, 0.0)
    p = jnp.exp(s - m)
    p = p / jnp.maximum(jnp.sum(p, axis=1, keepdims=True), 1e-30)
    o_cmp = _nn(p.astype(BF16), vc_ref[...])

    lane = lax.broadcasted_iota(I32, (TQ, LANES), 1)
    row_l = t0 + lax.broadcasted_iota(I32, (TQ, LANES), 0)
    cur = row_l >> 6
    admiss = lane <= cur
    wsel = wsel_ref[...]
    keys = []
    for g in range(N_KV):
        imp = p[(GRP * g) * TQ:(GRP * g + 1) * TQ]
        for r in range(1, GRP):
            imp = imp + p[(GRP * g + r) * TQ:(GRP * g + r + 1) * TQ]
        hi = imp.astype(BF16)
        lo = (imp - hi.astype(F32)).astype(BF16)
        imp_s = _nn(hi, wsel) + _nn(lo, wsel)
        forced = jnp.where(lane == 0, jnp.inf, jnp.where(lane >= cur - 1, jnp.inf, imp_s))
        keys.append(_order_key(jnp.where(admiss, forced, NEG_INF)))
    kk = jnp.concatenate(keys, axis=0)

    def count(pred):
        return jnp.sum(pred(kk), axis=1, keepdims=True)

    thr, need = _kth_threshold(count, N_KV * TQ, float(n_sel))
    pref = _tie_prefix(kk, thr, tri_ref[0:LANES, 0:LANES], 0.0)
    tie = jnp.where(kk == thr, jnp.where(pref <= need, 1.0, 0.0), 0.0)
    sel = jnp.where(kk > thr, 1.0, tie).astype(BF16)

    _flash_init(m_ref, acc_ref)
    blk_row =lax.broadcasted_iota(I32, (LANES, CH), 0)
    blk_col = lax.broadcasted_iota(I32, (LANES, CH), 1) >> 6
    row_c = t0 + lax.broadcasted_iota(I32, (TQ, CH), 0)
    col_c = lax.broadcasted_iota(I32, (TQ, CH), 1)

    def slc_chunk(c):
        off = pl.multiple_of(c * CH, CH)
        expand = jnp.where(blk_row == blk_col + c * (CH // SLC_BLOCK), 1.0, 0.0).astype(BF16)
        tok = _nn(sel, expand)
        causal = col_c + off <= row_c
        parts = []
        for g in range(N_KV):
            bias_g = jnp.where(causal, jnp.where(tok[g * TQ:(g + 1) * TQ] > 0.5, 0.0, NEG_INF), NEG_INF)
            parts.extend([bias_g] * GRP)
        sc = _nt(q, ks_ref[pl.ds(off, CH), :]) + jnp.concatenate(parts, axis=0)
        _flash_update(sc, vs_ref[pl.ds(off, CH), :], m_ref, acc_ref, slice(None))

    _unrolled_loop(nch, slc_chunk)
    o_slc = _flash_result(acc_ref)

    span = NSA_WINDOW + TQ
    start = pl.multiple_of(jnp.maximum(t0 - NSA_WINDOW, 0), TQ)
    t_w = _stack_rows(t0 + lax.broadcasted_iota(I32, (TQ, span), 0), N_HEADS)
    diff = t_w - (start + lax.broadcasted_iota(I32, (rows_all, span), 1))
    sw = _nt(q, kw_ref[pl.ds(start, span), :])
    sw = jnp.where(diff >= 0, jnp.where(diff < NSA_WINDOW, sw, NEG_INF), NEG_INF)
    pw = jnp.exp(sw - jnp.max(sw, axis=1, keepdims=True))
    o_win = _nn(pw.astype(BF16), vw_ref[pl.ds(start, span), :])
    o_win = o_win[:, 0:LANES] / o_win[:, LANES:2 * LANES]

    gates = gt_ref[...]
    outs = []
    for h in range(N_HEADS):
        rs = slice(h * TQ, (h + 1) * TQ)
        outs.append(gates[:, 3 * h:3 * h + 1] * o_cmp[rs] + gates[:, 3 * h + 1:3 * h + 2] * o_slc[rs]
                    + gates[:, 3 * h + 2:3 * h + 3] * o_win[rs])
    _gated_store(jnp.concatenate(outs, axis=0), z_ref, o_ref, TQ)


def _mla_prep_kernel(cq_ref, ckv_ref, kr_ref, wq_ref, wkv_ref, lat_ref, gains_ref, cos_ref, sa_ref, sb_ref,
                     bd64_ref, bd32_ref, q_ref, k_ref, v_ref):
    lane = lax.broadcasted_iota(I32, (TM, LANES), 1)
    lo_half = lane < NOPE_DIM
    cos, sa, sb = cos_ref[...], sa_ref[...], sb_ref[...]
    bd64, bd32 = bd64_ref[...], bd32_ref[...]
    scale = (NOPE_DIM + ROPE_DIM) ** -0.5

    def rope_slab(s, gain):
        return _rope(_seg_norm(s, bd32, gain, ROPE_DIM), cos, sa, sb, ROPE_DIM // 2)

    q = _nn(_row_rms(cq_ref[...], lat_ref[0:1, :]).astype(BF16), wq_ref[...])
    kv = _nn(_row_rms(ckv_ref[...], lat_ref[1:2, 0:KV_LORA]).astype(BF16), wkv_ref[...])
    k_rope = pltpu.roll(rope_slab(kr_ref[...], gains_ref[3:4, :]), NOPE_DIM, 1)
    q_rope = [rope_slab(q[:, 512 + LANES * j:512 + LANES * (j + 1)], gains_ref[1:2, :]) * scale for j in range(2)]
    for j in range(4):
        qn = _seg_norm(q[:, LANES * j:LANES * (j + 1)], bd64, gains_ref[0:1, :], NOPE_DIM) * scale
        kn = _seg_norm(kv[:, LANES * j:LANES * (j + 1)], bd64, gains_ref[2:3, :], NOPE_DIM)
        for e in range(2):
            h = 2 * j + e
            qn_h = qn if e == 0 else pltpu.roll(qn, NOPE_DIM, 1)
            kn_h = kn if e == 0 else pltpu.roll(kn, NOPE_DIM, 1)
            shift = (NOPE_DIM - ROPE_DIM * (h % 4)) % LANES
            qr = q_rope[h // 4]
            qr_h = qr if shift == 0 else pltpu.roll(qr, shift, 1)
            q_ref[h] = jnp.where(lo_half, qn_h, jnp.where(lane < NOPE_DIM + ROPE_DIM, qr_h, 0.0)).astype(BF16)
            k_ref[h] = jnp.where(lo_half, kn_h, k_rope).astype(BF16)
    for j in range(4):
        v_ref[j] = _with_ones(kv[:, 512 + LANES * j:512 + LANES * (j + 1)])


def _mla_kernel(q_ref, k_ref, v_ref, z_ref, o_ref, m_ref, acc_ref):
    i = pl.program_id(2)
    tq = TQ_MLA
    _flash_init(m_ref, acc_ref)

    def step(c, masked):
        off = pl.multiple_of(c * CH, CH)
        v = v_ref[0, pl.ds(off, CH), :]
        for e in range(2):
            s = _nt(q_ref[e], k_ref[e, pl.ds(off, CH), :])
            if masked:
                s = jnp.where(lax.broadcasted_iota(I32, (tq, CH), 1) <= lax.broadcasted_iota(I32, (tq, CH), 0),
                              s, NEG_INF)
            _flash_update(s, v, m_ref, acc_ref, slice(e * tq, (e + 1) * tq))

    _unrolled_loop(i, lambda c: step(c, False))
    step(i, True)
    o = _flash_result(acc_ref)
    lo_half = lax.broadcasted_iota(I32, (tq, LANES), 1) < NOPE_DIM
    slab = jnp.where(lo_half, o[0:tq], o[tq:2 * tq])
    o_ref[...] = (slab * _silu(z_ref[...])).astype(BF16)


def _rope_tables(pos, dim):
    inv = jnp.power(jnp.float32(ROPE_THETA), -jnp.arange(0, dim, 2, dtype=F32) / dim)
    ang = pos.astype(F32)[:, None] * inv[None, :]
    cos, sin = jnp.cos(ang), jnp.sin(ang)
    reps = LANES // dim
    zero = jnp.zeros_like(sin)
    cos_t = jnp.tile(jnp.concatenate([cos, cos], axis=1), (1, reps))
    sa_t = jnp.tile(jnp.concatenate([zero, sin], axis=1), (1, reps))
    sb_t = jnp.tile(jnp.concatenate([-sin, zero], axis=1), (1, reps))
    return cos_t, sa_t, sb_t


def _block_diag(group):
    r = jnp.arange(LANES) // group
    return (r[:, None] == r[None, :]).astype(BF16)


def _tile_gain(g, width=LANES):
    return jnp.tile(g.astype(F32), width // g.shape[0])


def _pad_lanes(g, width=LANES):
    return jnp.concatenate([g.astype(F32), jnp.zeros((width - g.shape[0],), F32)])


def _pad_cols(w, width):
    return jnp.concatenate([w, jnp.zeros((w.shape[0], width - w.shape[1]), w.dtype)], axis=1)


def _params(*sem, flags=None):
    return pltpu.CompilerParams(dimension_semantics=sem, vmem_limit_bytes=VMEM_LIMIT, flags=flags)


def kernel(x, ab_norm, ab_w_in, a_qk_norm, a_kidx_norm, b_qk_norm, b_sinks, ab_w_out, cd_norm, cd_w_in, c_q_norm, c_k_norm, c_cmp_pe, c_cmp_w1, c_cmp_w2, d_q_lat_norm, d_kv_lat_norm, d_w_uq, d_w_ukv, d_nope_norm, d_rope_norm, cd_w_out):
    bsz, seq, d = x.shape
    n = bsz * seq
    assert d == D_MODEL and seq % CH == 0 and seq >= NSA_WINDOW + TQ and seq // SLC_BLOCK <= LANES
    nq = seq // TQ
    nrow = n // TM
    srow = seq // TM
    xf = x.reshape(n, d)
    pos = jnp.arange(seq)
    cos64, sa64, sb64 = _rope_tables(pos, HEAD_DIM)
    cos32, sa32, sb32 = _rope_tables(pos, ROPE_DIM)
    bd64, bd32 = _block_diag(HEAD_DIM), _block_diag(ROPE_DIM)
    tri = (jnp.arange(CH)[:, None] <= jnp.arange(CH)[None, :]).astype(BF16)
    tri_spec = pl.BlockSpec((CH, CH), lambda b, i: (0, 0))

    def row_spec(width, rows=TM):
        return pl.BlockSpec((rows, width), lambda i: (i, 0))

    def const_spec(shape):
        return pl.BlockSpec(shape, lambda i: tuple(0 for _ in shape))

    tab_spec = pl.BlockSpec((TM, LANES), lambda i: (i % srow, 0))
    head_spec = lambda nh: pl.BlockSpec((nh, TM, LANES), lambda i: (0, i, 0))

    wa = ab_w_in[0]
    w_ab = jnp.concatenate([wa[:, 0:1088], _pad_cols(wa[:, 1088:1092], 64), wa[:, 1092:]], axis=1).astype(BF16)
    gains_ab = jnp.stack([_tile_gain(a_qk_norm[0, 0]), _tile_gain(a_qk_norm[0, 1]), _pad_lanes(a_kidx_norm[0]),
                          _tile_gain(b_qk_norm[0, 0]), _tile_gain(b_qk_norm[0, 1])]
                         + [jnp.zeros((LANES,), F32)] * 3)
    sds = jax.ShapeDtypeStruct
    qa, ka, va, qi, ki, wi, za, qb, kb, vb, zb = pl.pallas_call(
        _proj_ab_kernel,
        grid=(nrow,),
        in_specs=[row_spec(d), const_spec((1, d)), const_spec((d, AB_W)), tab_spec, tab_spec, tab_spec,
                  const_spec((LANES, LANES)), const_spec((8, LANES))],
        out_specs=[head_spec(8), row_spec(LANES), row_spec(2 * LANES), head_spec(4), row_spec(LANES), row_spec(LANES),
                   row_spec(512), head_spec(8), row_spec(LANES), row_spec(2 * LANES), row_spec(512)],
        out_shape=[sds((8, n, LANES), BF16), sds((n, LANES), BF16), sds((n, 2 * LANES), BF16),
                   sds((4, n, LANES), BF16), sds((n, LANES), BF16), sds((n, LANES), F32), sds((n, 512), F32),
                   sds((8, n, LANES), BF16), sds((n, LANES), BF16), sds((n, 2 * LANES), BF16), sds((n, 512), F32)],
        compiler_params=_params("parallel"),
    )(xf, ab_norm[0][None, :], w_ab, cos64, sa64, sb64, bd64, gains_ab)

    qt_heads = lambda nh: pl.BlockSpec((nh, TQ, LANES), lambda b, i: (0, b * nq + i, 0))
    qt_rows = lambda width: pl.BlockSpec((TQ, width), lambda b, i: (b * nq + i, 0))
    seq_rows = pl.BlockSpec((seq, LANES), lambda b, i: (b, 0))
    seq_rows_v = pl.BlockSpec((seq, 2 * LANES), lambda b, i: (b, 0))
    att_scratch = [pltpu.VMEM((N_HEADS * TQ, LANES), F32), pltpu.VMEM((N_HEADS * TQ, 2 * LANES), F32)]

    mix_a = pl.pallas_call(
        functools.partial(_dsa_kernel, topk=min(DSA_TOPK, seq // 4)),
        grid=(bsz, nq),
        in_specs=[qt_heads(4), qt_rows(LANES), seq_rows, qt_heads(8), seq_rows, seq_rows_v, qt_rows(512), tri_spec],
        out_specs=qt_rows(512),
        out_shape=sds((n, 512), BF16),
        scratch_shapes=[pltpu.VMEM((seq // CH, TQ, CH), I32)] + att_scratch,
        compiler_params=_params("parallel", "arbitrary"),
    )(qi, wi, ki, qa, ka, va, za, tri)

    prev_rows = lambda width: pl.BlockSpec((TQ, width), lambda b, i: (b * nq + jnp.maximum(i - 1, 0), 0))
    sink_rows = jnp.broadcast_to(jnp.repeat(b_sinks[0].astype(F32), TQ)[:, None], (N_HEADS * TQ, LANES))
    mix_b = pl.pallas_call(
        _swa_kernel,
        grid=(bsz, nq),
        in_specs=[qt_heads(8), prev_rows(LANES), qt_rows(LANES), prev_rows(2 * LANES), qt_rows(2 * LANES),
                  pl.BlockSpec((N_HEADS * TQ, LANES), lambda b, i: (0, 0)), qt_rows(512)],
        out_specs=qt_rows(512),
        out_shape=sds((n, 512), BF16),
        compiler_params=_params("parallel", "arbitrary"),
    )(qb, kb, kb, vb, vb, sink_rows, zb)

    def out_proj(xin, m0, m1, w):
        return pl.pallas_call(
            _out_proj_kernel,
            grid=(nrow,),
            in_specs=[row_spec(d), row_spec(512), row_spec(512), const_spec((2 * 512, d))],
            out_specs=row_spec(d),
            out_shape=sds((n, d), F32),
            compiler_params=_params("parallel"),
        )(xin, m0, m1, w.astype(BF16))

    x1 = out_proj(xf, mix_a, mix_b, ab_w_out[0])

    wc = cd_w_in[0]
    w_cd = jnp.concatenate([wc[:, 0:1280], _pad_cols(wc[:, 1280:1304], LANES), wc[:, 1304:2200],
                            _pad_cols(wc[:, 2200:2232], LANES), wc[:, 2232:]], axis=1).astype(BF16)
    gains_cd = jnp.stack([_tile_gain(c_q_norm[0]), _tile_gain(c_k_norm[0, 1]), _tile_gain(c_k_norm[0, 2])]
                         + [jnp.zeros((LANES,), F32)] * 5)
    qc, kc_raw, vc_raw, ks, vs, kw, vw, gates, zc, cq, ckv, kr, zd = pl.pallas_call(
        _proj_cd_kernel,
        grid=(nrow,),
        in_specs=[row_spec(d), const_spec((1, d)), const_spec((d, AB_W)), tab_spec, tab_spec, tab_spec,
                  const_spec((LANES, LANES)), const_spec((8, LANES))],
        out_specs=[head_spec(8), row_spec(LANES), row_spec(LANES), row_spec(LANES), row_spec(2 * LANES),
                   row_spec(LANES), row_spec(2 * LANES), row_spec(LANES), row_spec(512), row_spec(256),
                   row_spec(LANES), row_spec(LANES), row_spec(512)],
        out_shape=[sds((8, n, LANES), BF16), sds((n, LANES), F32), sds((n, LANES), F32), sds((n, LANES), BF16),
                   sds((n, 2 * LANES), BF16), sds((n, LANES), BF16), sds((n, 2 * LANES), BF16), sds((n, LANES), F32),
                   sds((n, 512), F32), sds((n, 256), F32), sds((n, LANES), F32), sds((n, LANES), F32),
                   sds((n, 512), F32)],
        compiler_params=_params("parallel"),
    )(x1, cd_norm[0][None, :], w_cd, cos64, sa64, sb64, bd64, gains_cd)

    ncmp = seq // CMP_STRIDE
    cw = CMP_STRIDE * N_KV * HEAD_DIM
    eye = jnp.eye(N_KV, dtype=F32)
    ccos, csa, csb = _rope_tables(jnp.arange(ncmp) * CMP_STRIDE + (2 * CMP_STRIDE - 1), HEAD_DIM)

    def compress(raw, which, is_key):
        w1 = c_cmp_w1[0, which].reshape(2, CMP_STRIDE, HEAD_DIM, -1)
        hid = w1.shape[-1]
        w1e = jnp.einsum("tjdu,gh->tjgdhu", w1, eye).reshape(2, cw, N_KV * hid).astype(BF16)
        w2e = jnp.einsum("ud,gh->guhd", c_cmp_w2[0, which], eye).reshape(N_KV * hid, N_KV * HEAD_DIM).astype(BF16)
        pe = c_cmp_pe[0, which].reshape(2, CMP_STRIDE, 1, HEAD_DIM)
        pe = jnp.broadcast_to(pe, (2, CMP_STRIDE, N_KV, HEAD_DIM)).reshape(2, cw)
        return pl.pallas_call(
            functools.partial(_compress_kernel, is_key=is_key),
            grid=(bsz,),
            in_specs=[pl.BlockSpec((ncmp, cw), lambda b: (b, 0)), const_spec((2, cw)),
                      const_spec((cw, N_KV * hid)), const_spec((cw, N_KV * hid)),
                      const_spec((N_KV * hid, LANES)), const_spec((1, LANES)),
                      const_spec((ncmp, LANES)), const_spec((ncmp, LANES)), const_spec((ncmp, LANES)),
                      const_spec((LANES, LANES))],
            out_specs=pl.BlockSpec((ncmp, LANES), lambda b: (b, 0)),
            out_shape=sds((bsz * ncmp, LANES), BF16),
            compiler_params=_params("parallel"),
        )(raw.reshape(bsz * ncmp, cw), pe, w1e[0], w1e[1], w2e, _tile_gain(c_k_norm[0, 0])[None, :],
          ccos, csa, csb, bd64)

    kc = compress(kc_raw, 0, True)
    vc = compress(vc_raw, 1, False)

    ratio = SLC_BLOCK // CMP_STRIDE
    mm = jnp.arange(ncmp)[:, None]
    jj = jnp.arange(LANES)[None, :]
    wsel = (jnp.where((mm == ratio * jj - 1) | (mm == ratio * jj + ratio - 1), 1.0, 0.0)
            + jnp.where((mm >= ratio * jj) & (mm < ratio * jj + ratio - 1), 2.0, 0.0))
    wsel = jnp.where((mm < ncmp - 1) & (jj < seq // SLC_BLOCK), wsel, 0.0).astype(BF16)

    cmp_rows = pl.BlockSpec((ncmp, LANES), lambda b, i: (b, 0))
    mix_c = pl.pallas_call(
        functools.partial(_nsa_kernel, n_sel=min(SLC_TOPN, seq // SLC_BLOCK)),
        grid=(bsz, nq),
        in_specs=[qt_heads(8), cmp_rows, cmp_rows, seq_rows, seq_rows_v, seq_rows, seq_rows_v, qt_rows(LANES),
                  qt_rows(512), pl.BlockSpec((ncmp, LANES), lambda b, i: (0, 0)), tri_spec],
        out_specs=qt_rows(512),
        out_shape=sds((n, 512), BF16),
        scratch_shapes=att_scratch,
        compiler_params=_params("parallel", "arbitrary"),
    )(qc, kc, vc, ks, vs, kw, vw, gates, zc, wsel, tri)

    wq = d_w_uq[0].reshape(Q_LORA, N_HEADS, NOPE_DIM + ROPE_DIM)
    wq = jnp.concatenate([wq[:, :, :NOPE_DIM].reshape(Q_LORA, -1), wq[:, :, NOPE_DIM:].reshape(Q_LORA, -1)],
                         axis=1).astype(BF16)
    wkv = d_w_ukv[0].reshape(KV_LORA, N_HEADS, NOPE_DIM + HEAD_DIM)
    wkv = jnp.concatenate([wkv[:, :, :NOPE_DIM].reshape(KV_LORA, -1), wkv[:, :, NOPE_DIM:].reshape(KV_LORA, -1)],
                          axis=1).astype(BF16)
    lat_gains = jnp.stack([d_q_lat_norm[0].astype(F32), _pad_lanes(d_kv_lat_norm[0], Q_LORA)]
                          + [jnp.zeros((Q_LORA,), F32)] * 6)
    gains_d = jnp.stack([_tile_gain(d_nope_norm[0, 0]), _tile_gain(d_rope_norm[0, 0]), _tile_gain(d_nope_norm[0, 1]),
                         _pad_lanes(d_rope_norm[0, 1])] + [jnp.zeros((LANES,), F32)] * 4)
    q_cat, k_cat, v_d = pl.pallas_call(
        _mla_prep_kernel,
        grid=(nrow,),
        in_specs=[row_spec(Q_LORA), row_spec(LANES), row_spec(LANES), const_spec((Q_LORA, 768)),
                  const_spec((KV_LORA, 1024)), const_spec((8, Q_LORA)), const_spec((8, LANES)),
                  tab_spec, tab_spec, tab_spec, const_spec((LANES, LANES)), const_spec((LANES, LANES))],
        out_specs=[head_spec(8), head_spec(8), pl.BlockSpec((4, TM, 2 * LANES), lambda i: (0, i, 0))],
        out_shape=[sds((8, n, LANES), BF16), sds((8, n, LANES), BF16), sds((4, n, 2 * LANES), BF16)],
        compiler_params=_params("parallel"),
    )(cq, ckv, kr, wq, wkv, lat_gains, gains_d, cos32, sa32, sb32, bd64, bd32)

    nqm = seq // TQ_MLA
    mix_d = pl.pallas_call(
        _mla_kernel,
        grid=(bsz, N_HEADS // 2, nqm),
        in_specs=[pl.BlockSpec((2, TQ_MLA, LANES), lambda b, hp, i: (hp, b * nqm + i, 0)),
                  pl.BlockSpec((2, seq, LANES), lambda b, hp, i: (hp, b, 0)),
                  pl.BlockSpec((1, seq, 2 * LANES), lambda b, hp, i: (hp, b, 0)),
                  pl.BlockSpec((TQ_MLA, LANES), lambda b, hp, i: (b * nqm + i, hp))],
        out_specs=pl.BlockSpec((TQ_MLA, LANES), lambda b, hp, i: (b * nqm + i, hp)),
        out_shape=sds((n, 512), BF16),
        scratch_shapes=[pltpu.VMEM((2 * TQ_MLA, LANES), F32), pltpu.VMEM((2 * TQ_MLA, 2 * LANES), F32)],
        compiler_params=_params("parallel", "parallel", "arbitrary"),
    )(q_cat, k_cat, v_d, zd)

    x2 = out_proj(x1, mix_c, mix_d, cd_w_out[0])
    return x2.reshape(bsz, seq, d)
```

```python
import functools

import jax
import jax.numpy as jnp
from jax import lax
from jax.experimental import pallas as pl
from jax.experimental.pallas import tpu as pltpu

F32, BF16, I32 = jnp.float32, jnp.bfloat16, jnp.int32

D_MODEL = 1024
HEAD_DIM = 64
N_HEADS = 8
N_KV = 2
GRP = N_HEADS // N_KV
IDX_HEADS = 4
DSA_TOPK = 256
SWA_WINDOW = 128
CMP_STRIDE = 16
SLC_BLOCK = 64
SLC_TOPN = 16
NSA_WINDOW = 512
Q_LORA = 256
KV_LORA = 128
NOPE_DIM = 64
ROPE_DIM = 32
ROPE_THETA = 10000.0
EPS = 1e-6

LANES = 128
TM = 512
TQ = 128
CH = 512
TQ_MLA = 512
UNROLL = 4
COUNT_ROWS = 64
VMEM_LIMIT = 56 * 1024 * 1024

NEG_INF = float("-inf")
M_INIT = -1e30
INT_MIN = -(2 ** 31)

AB_W = 2944
LOG2E = 1.4426950408889634
Q_SCALE = HEAD_DIM ** -0.5 * LOG2E


def _nn(a, b):
    return jnp.dot(a, b, preferred_element_type=F32)


def _nt(a, b):
    return lax.dot_general(a, b, (((1,), (1,)), ((), ())), preferred_element_type=F32)


def _sigmoid(z):
    return 1.0 / (1.0 + jnp.exp(-z))


def _silu(z):
    return z * _sigmoid(z)


def _row_rms(x, g):
    return x * lax.rsqrt(jnp.mean(x * x, axis=-1, keepdims=True) + EPS) * g


def _seg_norm(s, bd, gain, group):
    s2 = s * s
    hi = s2.astype(BF16)
    lo = (s2 - hi.astype(F32)).astype(BF16)
    ss = _nn(hi, bd) + _nn(lo, bd)
    return s * lax.rsqrt(ss * (1.0 / group) + EPS) * gain


def _rope(s, cos, sa, sb, half):
    return s * cos + pltpu.roll(s, half, 1) * sa + pltpu.roll(s, LANES - half, 1) * sb


def _order_key(v):
    bits = lax.bitcast_convert_type(v, I32)
    return bits ^ ((bits >> 31) & 0x7FFFFFFF)


def _stack_rows(x, n):
    return jnp.concatenate([x] * n, axis=0)


def _with_ones(v):
    return jnp.concatenate([v.astype(BF16), jnp.ones(v.shape, BF16)], axis=1)


def _flash_update(s, v_ext, m_ref, acc_ref, rows):
    m_prev = m_ref[rows]
    m_new = jnp.maximum(m_prev, jnp.max(s, axis=1, keepdims=True))
    alpha = jnp.exp2(m_prev - m_new)
    p = jnp.exp2(s - jnp.concatenate([m_new] * (s.shape[1] // LANES), axis=1))
    acc_ref[rows] = jnp.concatenate([alpha, alpha], axis=1) * acc_ref[rows] + _nn(p.astype(BF16), v_ext)
    m_ref[rows] = m_new


def _unrolled_loop(n, step, unroll=UNROLL):
    def body(j, carry):
        for u in range(unroll):
            step(j * unroll + u)
        return carry

    lax.fori_loop(0, n // unroll, body, 0)
    base = (n // unroll) * unroll
    p = unroll // 2
    while p >= 1:
        def tail(base=base, p=p):
            for u in range(p):
                step(base + u)

        pl.when((n & p) != 0)(tail)
        base = base + (n & p)
        p //= 2


def _flash_init(m_ref, acc_ref):
    m_ref[...] = jnp.full(m_ref.shape, M_INIT, F32)
    acc_ref[...] = jnp.zeros(acc_ref.shape, F32)


def _flash_result(acc_ref):
    acc = acc_ref[...]
    return acc[:, 0:LANES] / jnp.maximum(acc[:, LANES:2 * LANES], 1e-30)


def _place_q_heads(y, bd, gain, cos, sa, sb, out_ref, scale):
    lo_half = lax.broadcasted_iota(I32, (y.shape[0], LANES), 1) < HEAD_DIM
    for j in range(4):
        s = y[:, LANES * j:LANES * (j + 1)]
        s = _rope(_seg_norm(s, bd, gain, HEAD_DIM), cos, sa, sb, HEAD_DIM // 2) * scale
        r = pltpu.roll(s, HEAD_DIM, 1)
        if j < 2:
            out_ref[2 * j] = jnp.where(lo_half, s, 0.0).astype(BF16)
            out_ref[2 * j + 1] = jnp.where(lo_half, r, 0.0).astype(BF16)
        else:
            out_ref[2 * j] = jnp.where(lo_half, 0.0, r).astype(BF16)
            out_ref[2 * j + 1] = jnp.where(lo_half, 0.0, s).astype(BF16)


def _gated_store(o, z_ref, o_ref, tq):
    lo_half = lax.broadcasted_iota(I32, (tq, LANES), 1) < HEAD_DIM
    for j in range(4):
        a = o[(2 * j) * tq:(2 * j + 1) * tq]
        b = o[(2 * j + 1) * tq:(2 * j + 2) * tq]
        if j < 2:
            slab = jnp.where(lo_half, a, pltpu.roll(b, HEAD_DIM, 1))
        else:
            slab = jnp.where(lo_half, pltpu.roll(a, HEAD_DIM, 1), b)
        z = z_ref[:, LANES * j:LANES * (j + 1)]
        o_ref[:, LANES * j:LANES * (j + 1)] = (slab * _silu(z)).astype(BF16)


def _proj_ab_kernel(x_ref, g_ref, w_ref, cos_ref, sa_ref, sb_ref, bd_ref, gains_ref,
                    qa_ref, ka_ref, va_ref, qi_ref, ki_ref, wi_ref, za_ref,
                    qb_ref, kb_ref, vb_ref, zb_ref):
    xn = _row_rms(x_ref[...], g_ref[...]).astype(BF16)
    cos, sa, sb, bd = cos_ref[...], sa_ref[...], sb_ref[...], bd_ref[...]
    lo_half = lax.broadcasted_iota(I32, (TM, LANES), 1) < HEAD_DIM

    def proj(c0, n):
        return _nn(xn, w_ref[:, c0:c0 + n])

    def k_slab(c0, gain):
        return _rope(_seg_norm(proj(c0, LANES), bd, gain, HEAD_DIM), cos, sa, sb, HEAD_DIM // 2)

    _place_q_heads(proj(0, 512), bd, gains_ref[0:1, :], cos, sa, sb, qa_ref, Q_SCALE)
    ka_ref[...] = k_slab(512, gains_ref[1:2, :]).astype(BF16)
    va_ref[...] = _with_ones(proj(640, LANES))
    qi = proj(768, 256)
    for j in range(2):
        s = _rope(qi[:, LANES * j:LANES * (j + 1)], cos, sa, sb, HEAD_DIM // 2)
        qi_ref[2 * j] = jnp.where(lo_half, s, 0.0).astype(BF16)
        qi_ref[2 * j + 1] = jnp.where(lo_half, pltpu.roll(s, HEAD_DIM, 1), 0.0).astype(BF16)
    kiw = proj(1024, LANES)
    wi_ref[...] = kiw
    ki_ref[...] = _rope(_seg_norm(kiw, bd, gains_ref[2:3, :], HEAD_DIM), cos, sa, sb, HEAD_DIM // 2).astype(BF16)
    za_ref[...] = proj(1152, 512)
    _place_q_heads(proj(1664, 512), bd, gains_ref[3:4, :], cos, sa, sb, qb_ref, Q_SCALE)
    kb_ref[...] = k_slab(2176, gains_ref[4:5, :]).astype(BF16)
    vb_ref[...] = _with_ones(proj(2304, LANES))
    zb_ref[...] = proj(2432, 512)


def _count_chunks(keys_ref, nch, nq, pred):
    def body(c, acc):
        return acc + jnp.sum(pred(keys_ref[c]).reshape(CH // COUNT_ROWS, COUNT_ROWS, nq), axis=0)

    acc = lax.fori_loop(0, nch, body, jnp.zeros((COUNT_ROWS, nq), F32))
    return jnp.sum(acc, axis=0, keepdims=True)


def _kth_threshold(count, nq, k):
    def bit_body(it, tb):
        cand_b = tb | jnp.left_shift(jnp.int32(1), 31 - it)
        cand_s = cand_b ^ INT_MIN
        cnt = count(lambda kk: jnp.where(kk >= cand_s, 1.0, 0.0))
        return jnp.where(cnt >= k, cand_b, tb)

    tb = lax.fori_loop(0, 32, bit_body, jnp.zeros((1, nq), I32))
    thr = tb ^ INT_MIN
    need = k - count(lambda kk: jnp.where(kk > thr, 1.0, 0.0))
    return thr, need


def _dsa_kernel(qi_ref, wi_ref, ki_ref, qa_ref, ka_ref, va_ref, za_ref, tril_ref, o_ref,
                keys_ref, bias_ref, ties_ref, m_ref, acc_ref, *, topk):
    i = pl.program_id(1)
    t0 = i * TQ
    nch = (t0 + TQ + CH - 1) // CH
    key_c = lax.broadcasted_iota(I32, (CH, TQ), 0)
    qry_c = t0 + lax.broadcasted_iota(I32, (CH, TQ), 1)

    qi = qi_ref[...].reshape(IDX_HEADS * TQ, LANES)
    w_t = wi_ref[...].T * ((HEAD_DIM * IDX_HEADS) ** -0.5)
    ws = [w_t[HEAD_DIM + h:HEAD_DIM + h + 1, :] for h in range(IDX_HEADS)]

    def score_chunk(c, carry):
        off = pl.multiple_of(c * CH, CH)
        lg = _nt(ki_ref[pl.ds(off, CH), :], qi)
        sc = ws[0] * jnp.maximum(lg[:, 0:TQ], 0.0)
        for h in range(1, IDX_HEADS):
            sc = sc + ws[h] * jnp.maximum(lg[:, h * TQ:(h + 1) * TQ], 0.0)
        sc = jnp.where(sc == 0.0, 0.0, sc)
        keys_ref[c] = jnp.where(key_c + off <= qry_c, _order_key(sc), INT_MIN)
        return carry

    lax.fori_loop(0, nch, score_chunk, 0)

    count = functools.partial(_count_chunks, keys_ref, nch, TQ)
    thr, need = _kth_threshold(count, TQ, float(topk))

    tril = tril_ref[...]

    ties_ref[...] = jnp.zeros(ties_ref.shape, F32)

    def select_chunk(c):
        kk = keys_ref[c]
        pref = _nn(tril, jnp.where(kk == thr, 1.0, 0.0).astype(BF16)) + ties_ref[0:1, :]
        ties_ref[0:1, :] = pref[CH - 1:CH, :]
        tie = jnp.where(kk == thr, jnp.where(pref <= need, 0.0, NEG_INF), NEG_INF)
        bias = jnp.where(key_c + c * CH <= qry_c, jnp.where(kk > thr, 0.0, tie), NEG_INF)
        bias_ref[c] = bias.T

    _unrolled_loop(nch, select_chunk, unroll=2)

    _flash_init(m_ref, acc_ref)
    q_all = qa_ref[...].reshape(N_HEADS * TQ, LANES)

    def att_chunk(c):
        off = pl.multiple_of(c * CH, CH)
        s = _nt(q_all, ka_ref[pl.ds(off, CH), :]) + _stack_rows(bias_ref[c], N_HEADS)
        _flash_update(s, va_ref[pl.ds(off, CH), :], m_ref, acc_ref, slice(None))

    _unrolled_loop(nch, att_chunk)
    _gated_store(_flash_result(acc_ref), za_ref, o_ref, TQ)


def _swa_kernel(q_ref, kp_ref, kc_ref, vp_ref, vc_ref, sink_ref, z_ref, o_ref):
    i = pl.program_id(1)
    q = q_ref[...].reshape(N_HEADS * TQ, LANES)
    qi = _stack_rows(lax.broadcasted_iota(I32, (TQ, TQ), 0), N_HEADS)
    kj = lax.broadcasted_iota(I32, (N_HEADS * TQ, TQ), 1)
    no_prev = jnp.where(i > 0, 0, 1 << 20)
    sp = jnp.where(kj > qi + no_prev, _nt(q, kp_ref[...]), NEG_INF)
    sc = jnp.where(kj <= qi, _nt(q, kc_ref[...]), NEG_INF)
    sink = sink_ref[...]
    m = jnp.maximum(jnp.maximum(jnp.max(sp, axis=1, keepdims=True), jnp.max(sc, axis=1, keepdims=True)), sink)
    pp = jnp.exp2(sp - m)
    pc = jnp.exp2(sc - m)
    o = _nn(pp.astype(BF16), vp_ref[...]) + _nn(pc.astype(BF16), vc_ref[...])
    den = o[:, LANES:2 * LANES] + jnp.exp2(sink - m)
    _gated_store(o[:, 0:LANES] / den, z_ref, o_ref, TQ)


def _out_proj_kernel(x_ref, ma_ref, mb_ref, w_ref, o_ref):
    half = ma_ref.shape[1]
    o_ref[...] = x_ref[...] + _nn(ma_ref[...], w_ref[0:half, :]) + _nn(mb_ref[...], w_ref[half:2 * half, :])


def _proj_cd_kernel(x_ref, g_ref, w_ref, cos_ref, sa_ref, sb_ref, bd_ref, gains_ref,
                    qc_ref, kc_ref, vc_ref, ks_ref, vs_ref, kw_ref, vw_ref, gt_ref, zc_ref,
                    cq_ref, ckv_ref, kr_ref, zd_ref):
    xn = _row_rms(x_ref[...], g_ref[...]).astype(BF16)
    cos, sa, sb, bd = cos_ref[...], sa_ref[...], sb_ref[...], bd_ref[...]

    def proj(c0, n):
        return _nn(xn, w_ref[:, c0:c0 + n])

    def k_slab(c0, gain):
        return _rope(_seg_norm(proj(c0, LANES), bd, gain, HEAD_DIM), cos, sa, sb, HEAD_DIM // 2)

    _place_q_heads(proj(0, 512), bd, gains_ref[0:1, :], cos, sa, sb, qc_ref, Q_SCALE)
    kc_ref[...] = proj(512, LANES)
    vc_ref[...] = proj(640, LANES)
    ks_ref[...] = k_slab(768, gains_ref[1:2, :]).astype(BF16)
    vs_ref[...] = _with_ones(proj(896, LANES))
    kw_ref[...] = k_slab(1024, gains_ref[2:3, :]).astype(BF16)
    vw_ref[...] = _with_ones(proj(1152, LANES))
    gt_ref[...] = _sigmoid(proj(1280, LANES))
    zc_ref[...] = proj(1408, 512)
    cq_ref[...] = proj(1920, 256)
    ckv_ref[...] = proj(2176, LANES)
    kr_ref[...] = proj(2304, LANES)
    zd_ref[...] = proj(2432, 512)


def _compress_kernel(c_ref, pe_ref, w1a_ref, w1b_ref, w2_ref, gain_ref, cos_ref, sa_ref, sb_ref, bd_ref,
                     o_ref, *, is_key):
    c = c_ref[...]
    n = c.shape[0]
    a = _nn((c + pe_ref[0:1, :]).astype(BF16), w1a_ref[...])
    b = _nn((c + pe_ref[1:2, :]).astype(BF16), w1b_ref[...])
    h = _silu(a + pltpu.roll(b, n - 1, 0))
    y = _nn(h.astype(BF16), w2_ref[...])
    if is_key:
        y = _rope(_seg_norm(y, bd_ref[...], gain_ref[...], HEAD_DIM), cos_ref[...], sa_ref[...], sb_ref[...],
                  HEAD_DIM // 2)
    o_ref[...] = y.astype(BF16)


def _nsa_kernel(q_ref, kc_ref, vc_ref, ks_ref, vs_ref, kw_ref, vw_ref, gt_ref, z_ref, wsel_ref, o_ref,
                m_ref, acc_ref, *, n_sel):
    i = pl.program_id(1)
    t0 = i * TQ
    nch = (t0 + TQ + CH - 1) // CH
    ncmp = kc_ref.shape[0]
    rows_all = N_HEADS * TQ
    q = q_ref[...].reshape(rows_all, LANES)

    t_cmp = _stack_rows(t0 + lax.broadcasted_iota(I32, (TQ, ncmp), 0), N_HEADS)
    cend = lax.broadcasted_iota(I32, (rows_all, ncmp), 1) * CMP_STRIDE + (2 * CMP_STRIDE - 1)
    s = jnp.where(cend <= t_cmp, _nt(q, kc_ref[...]), NEG_INF)
    m = jnp.max(s, axis=1, keepdims=True)
    m = jnp.where(m > NEG_INF, m, 0.0)
    p = jnp.exp2(s - m)
    p = p / jnp.maximum(jnp.sum(p, axis=1, keepdims=True), 1e-30)
    o_cmp = _nn(p.astype(BF16), vc_ref[...])

    blk = lax.broadcasted_iota(I32, (LANES, TQ), 0)
    cur = (t0 + lax.broadcasted_iota(I32, (LANES, TQ), 1)) >> 6
    wsel_t = wsel_ref[...]
    keys = []
    for g in range(N_KV):
        imp = p[(GRP * g) * TQ:(GRP * g + 1) * TQ]
        for r in range(1, GRP):
            imp = imp + p[(GRP * g + r) * TQ:(GRP * g + r + 1) * TQ]
        hi = imp.astype(BF16)
        lo = (imp - hi.astype(F32)).astype(BF16)
        imp_s = _nt(wsel_t, hi) + _nt(wsel_t, lo)
        forced = jnp.where(blk == 0, jnp.inf, jnp.where(blk >= cur - 1, jnp.inf, imp_s))
        keys.append(_order_key(jnp.where(blk <= cur, forced, NEG_INF)))
    kk = jnp.concatenate(keys, axis=1)

    def bit_body(it, tb):
        cand_b = tb | jnp.left_shift(jnp.int32(1), 31 - it)
        cnt = jnp.sum(jnp.where(kk >= (cand_b ^ INT_MIN), 1.0, 0.0), axis=0, keepdims=True)
        return jnp.where(cnt >= n_sel, cand_b, tb)

    thr = lax.fori_loop(0, 32, bit_body, jnp.zeros((1, N_KV * TQ), I32)) ^ INT_MIN
    need = n_sel - jnp.sum(jnp.where(kk > thr, 1.0, 0.0), axis=0, keepdims=True)
    tril = jnp.where(lax.broadcasted_iota(I32, (LANES, LANES), 0) >= lax.broadcasted_iota(I32, (LANES, LANES), 1),
                     1.0, 0.0).astype(BF16)
    pref = _nn(tril, jnp.where(kk == thr, 1.0, 0.0).astype(BF16))
    tie = jnp.where(kk == thr, jnp.where(pref <= need, 1.0, 0.0), 0.0)
    sel_t = jnp.where(kk > thr, 1.0, tie)
    sel = jnp.concatenate([sel_t[:, g * TQ:(g + 1) * TQ].T for g in range(N_KV)], axis=0).astype(BF16)

    _flash_init(m_ref, acc_ref)
    blk_row =lax.broadcasted_iota(I32, (LANES, CH), 0)
    blk_col = lax.broadcasted_iota(I32, (LANES, CH), 1) >> 6
    row_c = t0 + lax.broadcasted_iota(I32, (TQ, CH), 0)
    col_c = lax.broadcasted_iota(I32, (TQ, CH), 1)

    def slc_chunk(c):
        off = pl.multiple_of(c * CH, CH)
        expand = jnp.where(blk_row == blk_col + c * (CH // SLC_BLOCK), 1.0, 0.0).astype(BF16)
        tok = _nn(sel, expand)
        causal = col_c + off <= row_c
        parts = []
        for g in range(N_KV):
            bias_g = jnp.where(causal, jnp.where(tok[g * TQ:(g + 1) * TQ] > 0.5, 0.0, NEG_INF), NEG_INF)
            parts.extend([bias_g] * GRP)
        sc = _nt(q, ks_ref[pl.ds(off, CH), :]) + jnp.concatenate(parts, axis=0)
        _flash_update(sc, vs_ref[pl.ds(off, CH), :], m_ref, acc_ref, slice(None))

    _unrolled_loop(nch, slc_chunk)
    o_slc = _flash_result(acc_ref)

    span = NSA_WINDOW + TQ
    start = pl.multiple_of(jnp.maximum(t0 - NSA_WINDOW, 0), TQ)
    t_w = _stack_rows(t0 + lax.broadcasted_iota(I32, (TQ, span), 0), N_HEADS)
    diff = t_w - (start + lax.broadcasted_iota(I32, (rows_all, span), 1))
    sw = _nt(q, kw_ref[pl.ds(start, span), :])
    sw = jnp.where(diff >= 0, jnp.where(diff < NSA_WINDOW, sw, NEG_INF), NEG_INF)
    pw = jnp.exp2(sw - jnp.max(sw, axis=1, keepdims=True))
    o_win = _nn(pw.astype(BF16), vw_ref[pl.ds(start, span), :])
    o_win = o_win[:, 0:LANES] / o_win[:, LANES:2 * LANES]

    gates = gt_ref[...]
    outs = []
    for h in range(N_HEADS):
        rs = slice(h * TQ, (h + 1) * TQ)
        outs.append(gates[:, 3 * h:3 * h + 1] * o_cmp[rs] + gates[:, 3 * h + 1:3 * h + 2] * o_slc[rs]
                    + gates[:, 3 * h + 2:3 * h + 3] * o_win[rs])
    _gated_store(jnp.concatenate(outs, axis=0), z_ref, o_ref, TQ)


def _mla_prep_kernel(cq_ref, ckv_ref, kr_ref, wq_ref, wkv_ref, lat_ref, gains_ref, cos_ref, sa_ref, sb_ref,
                     bd64_ref, bd32_ref, q_ref, k_ref, v_ref):
    lane = lax.broadcasted_iota(I32, (TM, LANES), 1)
    lo_half = lane < NOPE_DIM
    cos, sa, sb = cos_ref[...], sa_ref[...], sb_ref[...]
    bd64, bd32 = bd64_ref[...], bd32_ref[...]
    scale = (NOPE_DIM + ROPE_DIM) ** -0.5 * LOG2E

    def rope_slab(s, gain):
        return _rope(_seg_norm(s, bd32, gain, ROPE_DIM), cos, sa, sb, ROPE_DIM // 2)

    q = _nn(_row_rms(cq_ref[...], lat_ref[0:1, :]).astype(BF16), wq_ref[...])
    kv = _nn(_row_rms(ckv_ref[...], lat_ref[1:2, 0:KV_LORA]).astype(BF16), wkv_ref[...])
    k_rope = pltpu.roll(rope_slab(kr_ref[...], gains_ref[3:4, :]), NOPE_DIM, 1)
    q_rope = [rope_slab(q[:, 512 + LANES * j:512 + LANES * (j + 1)], gains_ref[1:2, :]) * scale for j in range(2)]
    for j in range(4):
        qn = _seg_norm(q[:, LANES * j:LANES * (j + 1)], bd64, gains_ref[0:1, :], NOPE_DIM) * scale
        kn = _seg_norm(kv[:, LANES * j:LANES * (j + 1)], bd64, gains_ref[2:3, :], NOPE_DIM)
        for e in range(2):
            h = 2 * j + e
            qn_h = qn if e == 0 else pltpu.roll(qn, NOPE_DIM, 1)
            kn_h = kn if e == 0 else pltpu.roll(kn, NOPE_DIM, 1)
            shift = (NOPE_DIM - ROPE_DIM * (h % 4)) % LANES
            qr = q_rope[h // 4]
            qr_h = qr if shift == 0 else pltpu.roll(qr, shift, 1)
            q_ref[h] = jnp.where(lo_half, qn_h, jnp.where(lane < NOPE_DIM + ROPE_DIM, qr_h, 0.0)).astype(BF16)
            k_ref[h] = jnp.where(lo_half, kn_h, k_rope).astype(BF16)
    for j in range(4):
        v_ref[j] = _with_ones(kv[:, 512 + LANES * j:512 + LANES * (j + 1)])


def _mla_kernel(q_ref, k_ref, v_ref, z_ref, o_ref, m_ref, acc_ref):
    i = pl.program_id(2)
    tq = TQ_MLA
    _flash_init(m_ref, acc_ref)

    def step(c, masked):
        off = pl.multiple_of(c * CH, CH)
        v = v_ref[0, pl.ds(off, CH), :]
        for e in range(2):
            s = _nt(q_ref[e], k_ref[e, pl.ds(off, CH), :])
            if masked:
                s = jnp.where(lax.broadcasted_iota(I32, (tq, CH), 1) <= lax.broadcasted_iota(I32, (tq, CH), 0),
                              s, NEG_INF)
            _flash_update(s, v, m_ref, acc_ref, slice(e * tq, (e + 1) * tq))

    _unrolled_loop(i, lambda c: step(c, False))
    step(i, True)
    o = _flash_result(acc_ref)
    lo_half = lax.broadcasted_iota(I32, (tq, LANES), 1) < NOPE_DIM
    slab = jnp.where(lo_half, o[0:tq], o[tq:2 * tq])
    o_ref[...] = (slab * _silu(z_ref[...])).astype(BF16)


def _rope_tables(pos, dim):
    inv = jnp.power(jnp.float32(ROPE_THETA), -jnp.arange(0, dim, 2, dtype=F32) / dim)
    ang = pos.astype(F32)[:, None] * inv[None, :]
    cos, sin = jnp.cos(ang), jnp.sin(ang)
    reps = LANES // dim
    zero = jnp.zeros_like(sin)
    cos_t = jnp.tile(jnp.concatenate([cos, cos], axis=1), (1, reps))
    sa_t = jnp.tile(jnp.concatenate([zero, sin], axis=1), (1, reps))
    sb_t = jnp.tile(jnp.concatenate([-sin, zero], axis=1), (1, reps))
    return cos_t, sa_t, sb_t


def _block_diag(group):
    r = jnp.arange(LANES) // group
    return (r[:, None] == r[None, :]).astype(BF16)


def _tile_gain(g, width=LANES):
    return jnp.tile(g.astype(F32), width // g.shape[0])


def _pad_lanes(g, width=LANES):
    return jnp.concatenate([g.astype(F32), jnp.zeros((width - g.shape[0],), F32)])


def _pad_cols(w, width):
    return jnp.concatenate([w, jnp.zeros((w.shape[0], width - w.shape[1]), w.dtype)], axis=1)


def _params(*sem, flags=None):
    return pltpu.CompilerParams(dimension_semantics=sem, vmem_limit_bytes=VMEM_LIMIT, flags=flags)


def kernel(x, ab_norm, ab_w_in, a_qk_norm, a_kidx_norm, b_qk_norm, b_sinks, ab_w_out, cd_norm, cd_w_in, c_q_norm, c_k_norm, c_cmp_pe, c_cmp_w1, c_cmp_w2, d_q_lat_norm, d_kv_lat_norm, d_w_uq, d_w_ukv, d_nope_norm, d_rope_norm, cd_w_out):
    bsz, seq, d = x.shape
    n = bsz * seq
    assert d == D_MODEL and seq % CH == 0 and seq >= NSA_WINDOW + TQ and seq // SLC_BLOCK <= LANES
    nq = seq // TQ
    nrow = n // TM
    srow = seq // TM
    xf = x.reshape(n, d)
    pos = jnp.arange(seq)
    cos64, sa64, sb64 = _rope_tables(pos, HEAD_DIM)
    cos32, sa32, sb32 = _rope_tables(pos, ROPE_DIM)
    bd64, bd32 = _block_diag(HEAD_DIM), _block_diag(ROPE_DIM)
    tril = (jnp.arange(CH)[:, None] >= jnp.arange(CH)[None, :]).astype(BF16)

    def row_spec(width, rows=TM):
        return pl.BlockSpec((rows, width), lambda i: (i, 0))

    def const_spec(shape):
        return pl.BlockSpec(shape, lambda i: tuple(0 for _ in shape))

    tab_spec = pl.BlockSpec((TM, LANES), lambda i: (i % srow, 0))
    head_spec = lambda nh: pl.BlockSpec((nh, TM, LANES), lambda i: (0, i, 0))

    wa = ab_w_in[0]
    w_ab = jnp.concatenate([wa[:, 0:1088], _pad_cols(wa[:, 1088:1092], 64), wa[:, 1092:]], axis=1).astype(BF16)
    gains_ab = jnp.stack([_tile_gain(a_qk_norm[0, 0]), _tile_gain(a_qk_norm[0, 1]), _pad_lanes(a_kidx_norm[0]),
                          _tile_gain(b_qk_norm[0, 0]), _tile_gain(b_qk_norm[0, 1])]
                         + [jnp.zeros((LANES,), F32)] * 3)
    sds = jax.ShapeDtypeStruct
    qa, ka, va, qi, ki, wi, za, qb, kb, vb, zb = pl.pallas_call(
        _proj_ab_kernel,
        grid=(nrow,),
        in_specs=[row_spec(d), const_spec((1, d)), const_spec((d, AB_W)), tab_spec, tab_spec, tab_spec,
                  const_spec((LANES, LANES)), const_spec((8, LANES))],
        out_specs=[head_spec(8), row_spec(LANES), row_spec(2 * LANES), head_spec(4), row_spec(LANES), row_spec(LANES),
                   row_spec(512), head_spec(8), row_spec(LANES), row_spec(2 * LANES), row_spec(512)],
        out_shape=[sds((8, n, LANES), BF16), sds((n, LANES), BF16), sds((n, 2 * LANES), BF16),
                   sds((4, n, LANES), BF16), sds((n, LANES), BF16), sds((n, LANES), F32), sds((n, 512), F32),
                   sds((8, n, LANES), BF16), sds((n, LANES), BF16), sds((n, 2 * LANES), BF16), sds((n, 512), F32)],
        compiler_params=_params("parallel"),
    )(xf, ab_norm[0][None, :], w_ab, cos64, sa64, sb64, bd64, gains_ab)

    qt_heads = lambda nh: pl.BlockSpec((nh, TQ, LANES), lambda b, i: (0, b * nq + i, 0))
    qt_rows = lambda width: pl.BlockSpec((TQ, width), lambda b, i: (b * nq + i, 0))
    seq_rows = pl.BlockSpec((seq, LANES), lambda b, i: (b, 0))
    seq_rows_v = pl.BlockSpec((seq, 2 * LANES), lambda b, i: (b, 0))
    att_scratch = [pltpu.VMEM((N_HEADS * TQ, LANES), F32), pltpu.VMEM((N_HEADS * TQ, 2 * LANES), F32)]

    mix_a = pl.pallas_call(
        functools.partial(_dsa_kernel, topk=min(DSA_TOPK, seq // 4)),
        grid=(bsz, nq),
        in_specs=[qt_heads(4), qt_rows(LANES), seq_rows, qt_heads(8), seq_rows, seq_rows_v, qt_rows(512),
                  pl.BlockSpec((CH, CH), lambda b, i: (0, 0))],
        out_specs=qt_rows(512),
        out_shape=sds((n, 512), BF16),
        scratch_shapes=[pltpu.VMEM((seq // CH, CH, TQ), I32), pltpu.VMEM((seq // CH, TQ, CH), F32),
                        pltpu.VMEM((8, TQ), F32)] + att_scratch,
        compiler_params=_params("parallel", "arbitrary"),
    )(qi, wi, ki, qa, ka, va, za, tril)

    prev_rows = lambda width: pl.BlockSpec((TQ, width), lambda b, i: (b * nq + jnp.maximum(i - 1, 0), 0))
    sink_rows = jnp.broadcast_to(jnp.repeat(b_sinks[0].astype(F32) * LOG2E, TQ)[:, None], (N_HEADS * TQ, LANES))
    mix_b = pl.pallas_call(
        _swa_kernel,
        grid=(bsz, nq),
        in_specs=[qt_heads(8), prev_rows(LANES), qt_rows(LANES), prev_rows(2 * LANES), qt_rows(2 * LANES),
                  pl.BlockSpec((N_HEADS * TQ, LANES), lambda b, i: (0, 0)), qt_rows(512)],
        out_specs=qt_rows(512),
        out_shape=sds((n, 512), BF16),
        compiler_params=_params("parallel", "arbitrary"),
    )(qb, kb, kb, vb, vb, sink_rows, zb)

    def out_proj(xin, m0, m1, w):
        return pl.pallas_call(
            _out_proj_kernel,
            grid=(nrow,),
            in_specs=[row_spec(d), row_spec(512), row_spec(512), const_spec((2 * 512, d))],
            out_specs=row_spec(d),
            out_shape=sds((n, d), F32),
            compiler_params=_params("parallel"),
        )(xin, m0, m1, w.astype(BF16))

    x1 = out_proj(xf, mix_a, mix_b, ab_w_out[0])

    wc = cd_w_in[0]
    w_cd = jnp.concatenate([wc[:, 0:1280], _pad_cols(wc[:, 1280:1304], LANES), wc[:, 1304:2200],
                            _pad_cols(wc[:, 2200:2232], LANES), wc[:, 2232:]], axis=1).astype(BF16)
    gains_cd = jnp.stack([_tile_gain(c_q_norm[0]), _tile_gain(c_k_norm[0, 1]), _tile_gain(c_k_norm[0, 2])]
                         + [jnp.zeros((LANES,), F32)] * 5)
    qc, kc_raw, vc_raw, ks, vs, kw, vw, gates, zc, cq, ckv, kr, zd = pl.pallas_call(
        _proj_cd_kernel,
        grid=(nrow,),
        in_specs=[row_spec(d), const_spec((1, d)), const_spec((d, AB_W)), tab_spec, tab_spec, tab_spec,
                  const_spec((LANES, LANES)), const_spec((8, LANES))],
        out_specs=[head_spec(8), row_spec(LANES), row_spec(LANES), row_spec(LANES), row_spec(2 * LANES),
                   row_spec(LANES), row_spec(2 * LANES), row_spec(LANES), row_spec(512), row_spec(256),
                   row_spec(LANES), row_spec(LANES), row_spec(512)],
        out_shape=[sds((8, n, LANES), BF16), sds((n, LANES), F32), sds((n, LANES), F32), sds((n, LANES), BF16),
                   sds((n, 2 * LANES), BF16), sds((n, LANES), BF16), sds((n, 2 * LANES), BF16), sds((n, LANES), F32),
                   sds((n, 512), F32), sds((n, 256), F32), sds((n, LANES), F32), sds((n, LANES), F32),
                   sds((n, 512), F32)],
        compiler_params=_params("parallel"),
    )(x1, cd_norm[0][None, :], w_cd, cos64, sa64, sb64, bd64, gains_cd)

    ncmp = seq // CMP_STRIDE
    cw = CMP_STRIDE * N_KV * HEAD_DIM
    eye = jnp.eye(N_KV, dtype=F32)
    ccos, csa, csb = _rope_tables(jnp.arange(ncmp) * CMP_STRIDE + (2 * CMP_STRIDE - 1), HEAD_DIM)

    def compress(raw, which, is_key):
        w1 = c_cmp_w1[0, which].reshape(2, CMP_STRIDE, HEAD_DIM, -1)
        hid = w1.shape[-1]
        w1e = jnp.einsum("tjdu,gh->tjgdhu", w1, eye).reshape(2, cw, N_KV * hid).astype(BF16)
        w2e = jnp.einsum("ud,gh->guhd", c_cmp_w2[0, which], eye).reshape(N_KV * hid, N_KV * HEAD_DIM).astype(BF16)
        pe = c_cmp_pe[0, which].reshape(2, CMP_STRIDE, 1, HEAD_DIM)
        pe = jnp.broadcast_to(pe, (2, CMP_STRIDE, N_KV, HEAD_DIM)).reshape(2, cw)
        return pl.pallas_call(
            functools.partial(_compress_kernel, is_key=is_key),
            grid=(bsz,),
            in_specs=[pl.BlockSpec((ncmp, cw), lambda b: (b, 0)), const_spec((2, cw)),
                      const_spec((cw, N_KV * hid)), const_spec((cw, N_KV * hid)),
                      const_spec((N_KV * hid, LANES)), const_spec((1, LANES)),
                      const_spec((ncmp, LANES)), const_spec((ncmp, LANES)), const_spec((ncmp, LANES)),
                      const_spec((LANES, LANES))],
            out_specs=pl.BlockSpec((ncmp, LANES), lambda b: (b, 0)),
            out_shape=sds((bsz * ncmp, LANES), BF16),
            compiler_params=_params("parallel"),
        )(raw.reshape(bsz * ncmp, cw), pe, w1e[0], w1e[1], w2e, _tile_gain(c_k_norm[0, 0])[None, :],
          ccos, csa, csb, bd64)

    kc = compress(kc_raw, 0, True)
    vc = compress(vc_raw, 1, False)

    ratio = SLC_BLOCK // CMP_STRIDE
    mm = jnp.arange(ncmp)[:, None]
    jj = jnp.arange(LANES)[None, :]
    wsel = (jnp.where((mm == ratio * jj - 1) | (mm == ratio * jj + ratio - 1), 1.0, 0.0)
            + jnp.where((mm >= ratio * jj) & (mm < ratio * jj + ratio - 1), 2.0, 0.0))
    wsel_t = jnp.where((mm < ncmp - 1) & (jj < seq // SLC_BLOCK), wsel, 0.0).astype(BF16).T

    cmp_rows = pl.BlockSpec((ncmp, LANES), lambda b, i: (b, 0))
    mix_c = pl.pallas_call(
        functools.partial(_nsa_kernel, n_sel=min(SLC_TOPN, seq // SLC_BLOCK)),
        grid=(bsz, nq),
        in_specs=[qt_heads(8), cmp_rows, cmp_rows, seq_rows, seq_rows_v, seq_rows, seq_rows_v, qt_rows(LANES),
                  qt_rows(512), pl.BlockSpec((LANES, ncmp), lambda b, i: (0, 0))],
        out_specs=qt_rows(512),
        out_shape=sds((n, 512), BF16),
        scratch_shapes=att_scratch,
        compiler_params=_params("parallel", "arbitrary"),
    )(qc, kc, vc, ks, vs, kw, vw, gates, zc, wsel_t)

    wq = d_w_uq[0].reshape(Q_LORA, N_HEADS, NOPE_DIM + ROPE_DIM)
    wq = jnp.concatenate([wq[:, :, :NOPE_DIM].reshape(Q_LORA, -1), wq[:, :, NOPE_DIM:].reshape(Q_LORA, -1)],
                         axis=1).astype(BF16)
    wkv = d_w_ukv[0].reshape(KV_LORA, N_HEADS, NOPE_DIM + HEAD_DIM)
    wkv = jnp.concatenate([wkv[:, :, :NOPE_DIM].reshape(KV_LORA, -1), wkv[:, :, NOPE_DIM:].reshape(KV_LORA, -1)],
                          axis=1).astype(BF16)
    lat_gains = jnp.stack([d_q_lat_norm[0].astype(F32), _pad_lanes(d_kv_lat_norm[0], Q_LORA)]
                          + [jnp.zeros((Q_LORA,), F32)] * 6)
    gains_d = jnp.stack([_tile_gain(d_nope_norm[0, 0]), _tile_gain(d_rope_norm[0, 0]), _tile_gain(d_nope_norm[0, 1]),
                         _pad_lanes(d_rope_norm[0, 1])] + [jnp.zeros((LANES,), F32)] * 4)
    q_cat, k_cat, v_d = pl.pallas_call(
        _mla_prep_kernel,
        grid=(nrow,),
        in_specs=[row_spec(Q_LORA), row_spec(LANES), row_spec(LANES), const_spec((Q_LORA, 768)),
                  const_spec((KV_LORA, 1024)), const_spec((8, Q_LORA)), const_spec((8, LANES)),
                  tab_spec, tab_spec, tab_spec, const_spec((LANES, LANES)), const_spec((LANES, LANES))],
        out_specs=[head_spec(8), head_spec(8), pl.BlockSpec((4, TM, 2 * LANES), lambda i: (0, i, 0))],
        out_shape=[sds((8, n, LANES), BF16), sds((8, n, LANES), BF16), sds((4, n, 2 * LANES), BF16)],
        compiler_params=_params("parallel"),
    )(cq, ckv, kr, wq, wkv, lat_gains, gains_d, cos32, sa32, sb32, bd64, bd32)

    nqm = seq // TQ_MLA
    mix_d = pl.pallas_call(
        _mla_kernel,
        grid=(bsz, N_HEADS // 2, nqm),
        in_specs=[pl.BlockSpec((2, TQ_MLA, LANES), lambda b, hp, i: (hp, b * nqm + i, 0)),
                  pl.BlockSpec((2, seq, LANES), lambda b, hp, i: (hp, b, 0)),
                  pl.BlockSpec((1, seq, 2 * LANES), lambda b, hp, i: (hp, b, 0)),
                  pl.BlockSpec((TQ_MLA, LANES), lambda b, hp, i: (b * nqm + i, hp))],
        out_specs=pl.BlockSpec((TQ_MLA, LANES), lambda b, hp, i: (b * nqm + i, hp)),
        out_shape=sds((n, 512), BF16),
        scratch_shapes=[pltpu.VMEM((2 * TQ_MLA, LANES), F32), pltpu.VMEM((2 * TQ_MLA, 2 * LANES), F32)],
        compiler_params=_params("parallel", "parallel", "arbitrary"),
    )(q_cat, k_cat, v_d, zd)

    x2 = out_proj(x1, mix_c, mix_d, cd_w_out[0])
    return x2.reshape(bsz, seq, d)
```

```python
import functools

import jax
import jax.numpy as jnp
from jax import lax
from jax.experimental import pallas as pl
from jax.experimental.pallas import tpu as pltpu

F32, BF16, I32 = jnp.float32, jnp.bfloat16, jnp.int32

D_MODEL = 1024
HEAD_DIM = 64
N_HEADS = 8
N_KV = 2
GRP = N_HEADS // N_KV
IDX_HEADS = 4
DSA_TOPK = 256
SWA_WINDOW = 128
CMP_STRIDE = 16
SLC_BLOCK = 64
SLC_TOPN = 16
NSA_WINDOW = 512
Q_LORA = 256
KV_LORA = 128
NOPE_DIM = 64
ROPE_DIM = 32
ROPE_THETA = 10000.0
EPS = 1e-6

LANES = 128
TM = 512
TQ = 128
CH = 512
TQ_MLA = 512
UNROLL = 4
COUNT_ROWS = 64
VMEM_LIMIT = 56 * 1024 * 1024

NEG_INF = float("-inf")
M_INIT = -1e30
INT_MIN = -(2 ** 31)

AB_W = 2944
LOG2E = 1.4426950408889634
Q_SCALE = HEAD_DIM ** -0.5 * LOG2E


def _nn(a, b):
    return jnp.dot(a, b, preferred_element_type=F32)


def _nt(a, b):
    return lax.dot_general(a, b, (((1,), (1,)), ((), ())), preferred_element_type=F32)


def _sigmoid(z):
    return 1.0 / (1.0 + jnp.exp(-z))


def _silu(z):
    return z * _sigmoid(z)


def _row_rms(x, g):
    return x * lax.rsqrt(jnp.mean(x * x, axis=-1, keepdims=True) + EPS) * g


def _seg_norm(s, bd, gain, group):
    s2 = s * s
    hi = s2.astype(BF16)
    lo = (s2 - hi.astype(F32)).astype(BF16)
    ss = _nn(hi, bd) + _nn(lo, bd)
    return s * lax.rsqrt(ss * (1.0 / group) + EPS) * gain


def _rope(s, cos, sa, sb, half):
    return s * cos + pltpu.roll(s, half, 1) * sa + pltpu.roll(s, LANES - half, 1) * sb


def _order_key(v):
    bits = lax.bitcast_convert_type(v, I32)
    return bits ^ ((bits >> 31) & 0x7FFFFFFF)


def _stack_rows(x, n):
    return jnp.concatenate([x] * n, axis=0)


def _with_ones(v):
    return jnp.concatenate([v.astype(BF16), jnp.ones(v.shape, BF16)], axis=1)


def _flash_update(s, v_ext, m_ref, acc_ref, rows, m_cur=None):
    m_prev = m_ref[rows]
    m_new = jnp.maximum(m_prev, jnp.max(s, axis=1, keepdims=True) if m_cur is None else m_cur)
    alpha = jnp.exp2(m_prev - m_new)
    p = jnp.exp2(s - jnp.concatenate([m_new] * (s.shape[1] // LANES), axis=1))
    acc_ref[rows] = jnp.concatenate([alpha, alpha], axis=1) * acc_ref[rows] + _nn(p.astype(BF16), v_ext)
    m_ref[rows] = m_new


def _unrolled_loop(n, step, unroll=UNROLL):
    def body(j, carry):
        for u in range(unroll):
            step(j * unroll + u)
        return carry

    lax.fori_loop(0, n // unroll, body, 0)
    base = (n // unroll) * unroll
    p = unroll // 2
    while p >= 1:
        def tail(base=base, p=p):
            for u in range(p):
                step(base + u)

        pl.when((n & p) != 0)(tail)
        base = base + (n & p)
        p //= 2


def _pipelined_chunks(n, qk, consume, s0_ref, s1_ref):
    def produce(ref, c):
        s = qk(c)
        ref[:, 0:CH] = s
        ref[:, CH:CH + LANES] = jnp.broadcast_to(jnp.max(s, axis=1, keepdims=True), (s.shape[0], LANES))

    def use(ref, c, last):
        consume(c, ref[:, 0:CH], ref[:, CH:CH + LANES], last)

    produce(s0_ref, 0)
    pairs = (n - 1) // 2

    def body(j, carry):
        produce(s1_ref, 2 * j + 1)
        use(s0_ref, 2 * j, False)
        produce(s0_ref, 2 * j + 2)
        use(s1_ref, 2 * j + 1, False)
        return carry

    lax.fori_loop(0, pairs, body, 0)
    done = 2 * pairs

    @pl.when(n - done == 2)
    def _():
        produce(s1_ref, done + 1)
        use(s0_ref, done, False)
        use(s1_ref, done + 1, True)

    @pl.when(n - done == 1)
    def _():
        use(s0_ref, done, True)


def _flash_init(m_ref, acc_ref):
    m_ref[...] = jnp.full(m_ref.shape, M_INIT, F32)
    acc_ref[...] = jnp.zeros(acc_ref.shape, F32)


def _flash_result(acc_ref):
    acc = acc_ref[...]
    return acc[:, 0:LANES] / jnp.maximum(acc[:, LANES:2 * LANES], 1e-30)


def _place_q_heads(y, bd, gain, cos, sa, sb, out_ref, scale):
    lo_half = lax.broadcasted_iota(I32, (y.shape[0], LANES), 1) < HEAD_DIM
    for j in range(4):
        s = y[:, LANES * j:LANES * (j + 1)]
        s = _rope(_seg_norm(s, bd, gain, HEAD_DIM), cos, sa, sb, HEAD_DIM // 2) * scale
        r = pltpu.roll(s, HEAD_DIM, 1)
        if j < 2:
            out_ref[2 * j] = jnp.where(lo_half, s, 0.0).astype(BF16)
            out_ref[2 * j + 1] = jnp.where(lo_half, r, 0.0).astype(BF16)
        else:
            out_ref[2 * j] = jnp.where(lo_half, 0.0, r).astype(BF16)
            out_ref[2 * j + 1] = jnp.where(lo_half, 0.0, s).astype(BF16)


def _gated_store(o, z_ref, o_ref, tq):
    lo_half = lax.broadcasted_iota(I32, (tq, LANES), 1) < HEAD_DIM
    for j in range(4):
        a = o[(2 * j) * tq:(2 * j + 1) * tq]
        b = o[(2 * j + 1) * tq:(2 * j + 2) * tq]
        if j < 2:
            slab = jnp.where(lo_half, a, pltpu.roll(b, HEAD_DIM, 1))
        else:
            slab = jnp.where(lo_half, pltpu.roll(a, HEAD_DIM, 1), b)
        z = z_ref[:, LANES * j:LANES * (j + 1)]
        o_ref[:, LANES * j:LANES * (j + 1)] = (slab * _silu(z)).astype(BF16)


def _proj_ab_kernel(x_ref, g_ref, w_ref, cos_ref, sa_ref, sb_ref, bd_ref, gains_ref,
                    qa_ref, ka_ref, va_ref, qi_ref, ki_ref, wi_ref, za_ref,
                    qb_ref, kb_ref, vb_ref, zb_ref):
    xn = _row_rms(x_ref[...], g_ref[...]).astype(BF16)
    cos, sa, sb, bd = cos_ref[...], sa_ref[...], sb_ref[...], bd_ref[...]
    lo_half = lax.broadcasted_iota(I32, (TM, LANES), 1) < HEAD_DIM

    def proj(c0, n):
        return _nn(xn, w_ref[:, c0:c0 + n])

    def k_slab(c0, gain):
        return _rope(_seg_norm(proj(c0, LANES), bd, gain, HEAD_DIM), cos, sa, sb, HEAD_DIM // 2)

    _place_q_heads(proj(0, 512), bd, gains_ref[0:1, :], cos, sa, sb, qa_ref, Q_SCALE)
    ka_ref[...] = k_slab(512, gains_ref[1:2, :]).astype(BF16)
    va_ref[...] = _with_ones(proj(640, LANES))
    qi = proj(768, 256)
    for j in range(2):
        s = _rope(qi[:, LANES * j:LANES * (j + 1)], cos, sa, sb, HEAD_DIM // 2)
        qi_ref[2 * j] = jnp.where(lo_half, s, 0.0).astype(BF16)
        qi_ref[2 * j + 1] = jnp.where(lo_half, pltpu.roll(s, HEAD_DIM, 1), 0.0).astype(BF16)
    kiw = proj(1024, LANES)
    wi_ref[...] = kiw
    ki_ref[...] = _rope(_seg_norm(kiw, bd, gains_ref[2:3, :], HEAD_DIM), cos, sa, sb, HEAD_DIM // 2).astype(BF16)
    za_ref[...] = proj(1152, 512)
    _place_q_heads(proj(1664, 512), bd, gains_ref[3:4, :], cos, sa, sb, qb_ref, Q_SCALE)
    kb_ref[...] = k_slab(2176, gains_ref[4:5, :]).astype(BF16)
    vb_ref[...] = _with_ones(proj(2304, LANES))
    zb_ref[...] = proj(2432, 512)


def _count_chunks(keys_ref, nch, nq, pred):
    def body(c, acc):
        return acc + jnp.sum(pred(keys_ref[c]).reshape(CH // COUNT_ROWS, COUNT_ROWS, nq), axis=0)

    acc = lax.fori_loop(0, nch, body, jnp.zeros((COUNT_ROWS, nq), F32))
    return jnp.sum(acc, axis=0, keepdims=True)


def _kth_threshold(count, nq, k):
    def bit_body(it, tb):
        cand_b = tb | jnp.left_shift(jnp.int32(1), 31 - it)
        cand_s = cand_b ^ INT_MIN
        cnt = count(lambda kk: jnp.where(kk >= cand_s, 1.0, 0.0))
        return jnp.where(cnt >= k, cand_b, tb)

    tb = lax.fori_loop(0, 32, bit_body, jnp.zeros((1, nq), I32))
    thr = tb ^ INT_MIN
    need = k - count(lambda kk: jnp.where(kk > thr, 1.0, 0.0))
    return thr, need


def _dsa_kernel(qi_ref, wi_ref, ki_ref, qa_ref, ka_ref, va_ref, za_ref, tril_ref, o_ref,
                keys_ref, bias_ref, ties_ref, m_ref, acc_ref, *, topk):
    i = pl.program_id(1)
    t0 = i * TQ
    nch = (t0 + TQ + CH - 1) // CH
    key_c = lax.broadcasted_iota(I32, (CH, TQ), 0)
    qry_c = t0 + lax.broadcasted_iota(I32, (CH, TQ), 1)

    qi = qi_ref[...].reshape(IDX_HEADS * TQ, LANES)
    w_t = wi_ref[...].T * ((HEAD_DIM * IDX_HEADS) ** -0.5)
    ws = [w_t[HEAD_DIM + h:HEAD_DIM + h + 1, :] for h in range(IDX_HEADS)]

    def score_chunk(c, carry):
        off = pl.multiple_of(c * CH, CH)
        lg = _nt(ki_ref[pl.ds(off, CH), :], qi)
        sc = ws[0] * jnp.maximum(lg[:, 0:TQ], 0.0)
        for h in range(1, IDX_HEADS):
            sc = sc + ws[h] * jnp.maximum(lg[:, h * TQ:(h + 1) * TQ], 0.0)
        sc = jnp.where(sc == 0.0, 0.0, sc)
        keys_ref[c] = jnp.where(key_c + off <= qry_c, _order_key(sc), INT_MIN)
        return carry

    lax.fori_loop(0, nch, score_chunk, 0)

    count = functools.partial(_count_chunks, keys_ref, nch, TQ)
    thr, need = _kth_threshold(count, TQ, float(topk))

    tril = tril_ref[...]

    ties_ref[...] = jnp.zeros(ties_ref.shape, F32)

    def select_chunk(c):
        kk = keys_ref[c]
        pref = _nn(tril, jnp.where(kk == thr, 1.0, 0.0).astype(BF16)) + ties_ref[0:1, :]
        ties_ref[0:1, :] = pref[CH - 1:CH, :]
        tie = jnp.where(kk == thr, jnp.where(pref <= need, 0.0, NEG_INF), NEG_INF)
        bias = jnp.where(key_c + c * CH <= qry_c, jnp.where(kk > thr, 0.0, tie), NEG_INF)
        bias_ref[c] = bias.T

    _unrolled_loop(nch, select_chunk, unroll=2)

    _flash_init(m_ref, acc_ref)
    q_all = qa_ref[...].reshape(N_HEADS * TQ, LANES)

    def att_chunk(c):
        off = pl.multiple_of(c * CH, CH)
        s = _nt(q_all, ka_ref[pl.ds(off, CH), :]) + _stack_rows(bias_ref[c], N_HEADS)
        _flash_update(s, va_ref[pl.ds(off, CH), :], m_ref, acc_ref, slice(None))

    _unrolled_loop(nch, att_chunk)
    _gated_store(_flash_result(acc_ref), za_ref, o_ref, TQ)


def _swa_kernel(q_ref, kp_ref, kc_ref, vp_ref, vc_ref, sink_ref, z_ref, o_ref):
    i = pl.program_id(1)
    q = q_ref[...].reshape(N_HEADS * TQ, LANES)
    qi = _stack_rows(lax.broadcasted_iota(I32, (TQ, TQ), 0), N_HEADS)
    kj = lax.broadcasted_iota(I32, (N_HEADS * TQ, TQ), 1)
    no_prev = jnp.where(i > 0, 0, 1 << 20)
    sp = jnp.where(kj > qi + no_prev, _nt(q, kp_ref[...]), NEG_INF)
    sc = jnp.where(kj <= qi, _nt(q, kc_ref[...]), NEG_INF)
    sink = sink_ref[...]
    m = jnp.maximum(jnp.maximum(jnp.max(sp, axis=1, keepdims=True), jnp.max(sc, axis=1, keepdims=True)), sink)
    pp = jnp.exp2(sp - m)
    pc = jnp.exp2(sc - m)
    o = _nn(pp.astype(BF16), vp_ref[...]) + _nn(pc.astype(BF16), vc_ref[...])
    den = o[:, LANES:2 * LANES] + jnp.exp2(sink - m)
    _gated_store(o[:, 0:LANES] / den, z_ref, o_ref, TQ)


def _out_proj_kernel(x_ref, ma_ref, mb_ref, w_ref, o_ref):
    half = ma_ref.shape[1]
    o_ref[...] = x_ref[...] + _nn(ma_ref[...], w_ref[0:half, :]) + _nn(mb_ref[...], w_ref[half:2 * half, :])


def _proj_cd_kernel(x_ref, g_ref, w_ref, cos_ref, sa_ref, sb_ref, bd_ref, gains_ref,
                    qc_ref, kc_ref, vc_ref, ks_ref, vs_ref, kw_ref, vw_ref, gt_ref, zc_ref,
                    cq_ref, ckv_ref, kr_ref, zd_ref):
    xn = _row_rms(x_ref[...], g_ref[...]).astype(BF16)
    cos, sa, sb, bd = cos_ref[...], sa_ref[...], sb_ref[...], bd_ref[...]

    def proj(c0, n):
        return _nn(xn, w_ref[:, c0:c0 + n])

    def k_slab(c0, gain):
        return _rope(_seg_norm(proj(c0, LANES), bd, gain, HEAD_DIM), cos, sa, sb, HEAD_DIM // 2)

    _place_q_heads(proj(0, 512), bd, gains_ref[0:1, :], cos, sa, sb, qc_ref, Q_SCALE)
    kc_ref[...] = proj(512, LANES)
    vc_ref[...] = proj(640, LANES)
    ks_ref[...] = k_slab(768, gains_ref[1:2, :]).astype(BF16)
    vs_ref[...] = _with_ones(proj(896, LANES))
    kw_ref[...] = k_slab(1024, gains_ref[2:3, :]).astype(BF16)
    vw_ref[...] = _with_ones(proj(1152, LANES))
    gt_ref[...] = _sigmoid(proj(1280, LANES))
    zc_ref[...] = proj(1408, 512)
    cq_ref[...] = proj(1920, 256)
    ckv_ref[...] = proj(2176, LANES)
    kr_ref[...] = proj(2304, LANES)
    zd_ref[...] = proj(2432, 512)


def _compress_kernel(c_ref, pe_ref, w1a_ref, w1b_ref, w2_ref, gain_ref, cos_ref, sa_ref, sb_ref, bd_ref,
                     o_ref, *, is_key):
    c = c_ref[...]
    n = c.shape[0]
    a = _nn((c + pe_ref[0:1, :]).astype(BF16), w1a_ref[...])
    b = _nn((c + pe_ref[1:2, :]).astype(BF16), w1b_ref[...])
    h = _silu(a + pltpu.roll(b, n - 1, 0))
    y = _nn(h.astype(BF16), w2_ref[...])
    if is_key:
        y = _rope(_seg_norm(y, bd_ref[...], gain_ref[...], HEAD_DIM), cos_ref[...], sa_ref[...], sb_ref[...],
                  HEAD_DIM // 2)
    o_ref[...] = y.astype(BF16)


def _nsa_kernel(q_ref, kc_ref, vc_ref, ks_ref, vs_ref, kw_ref, vw_ref, gt_ref, z_ref, wsel_ref, o_ref,
                m_ref, acc_ref, *, n_sel):
    i = pl.program_id(1)
    t0 = i * TQ
    nch = (t0 + TQ + CH - 1) // CH
    ncmp = kc_ref.shape[0]
    rows_all = N_HEADS * TQ
    q = q_ref[...].reshape(rows_all, LANES)

    t_cmp = _stack_rows(t0 + lax.broadcasted_iota(I32, (TQ, ncmp), 0), N_HEADS)
    cend = lax.broadcasted_iota(I32, (rows_all, ncmp), 1) * CMP_STRIDE + (2 * CMP_STRIDE - 1)
    s = jnp.where(cend <= t_cmp, _nt(q, kc_ref[...]), NEG_INF)
    m = jnp.max(s, axis=1, keepdims=True)
    m = jnp.where(m > NEG_INF, m, 0.0)
    p = jnp.exp2(s - m)
    p = p / jnp.maximum(jnp.sum(p, axis=1, keepdims=True), 1e-30)
    o_cmp = _nn(p.astype(BF16), vc_ref[...])

    span = NSA_WINDOW + TQ
    start = pl.multiple_of(jnp.maximum(t0 - NSA_WINDOW, 0), TQ)
    t_w = _stack_rows(t0 + lax.broadcasted_iota(I32, (TQ, span), 0), N_HEADS)
    diff = t_w - (start + lax.broadcasted_iota(I32, (rows_all, span), 1))
    sw = _nt(q, kw_ref[pl.ds(start, span), :])
    sw = jnp.where(diff >= 0, jnp.where(diff < NSA_WINDOW, sw, NEG_INF), NEG_INF)
    pw = jnp.exp2(sw - jnp.max(sw, axis=1, keepdims=True))
    o_win = _nn(pw.astype(BF16), vw_ref[pl.ds(start, span), :])
    o_win = o_win[:, 0:LANES] / o_win[:, LANES:2 * LANES]

    blk = lax.broadcasted_iota(I32, (LANES, TQ), 0)
    cur = (t0 + lax.broadcasted_iota(I32, (LANES, TQ), 1)) >> 6
    wsel_t = wsel_ref[...]
    keys = []
    for g in range(N_KV):
        imp = p[(GRP * g) * TQ:(GRP * g + 1) * TQ]
        for r in range(1, GRP):
            imp = imp + p[(GRP * g + r) * TQ:(GRP * g + r + 1) * TQ]
        hi = imp.astype(BF16)
        lo = (imp - hi.astype(F32)).astype(BF16)
        imp_s = _nt(wsel_t, hi) + _nt(wsel_t, lo)
        forced = jnp.where(blk == 0, jnp.inf, jnp.where(blk >= cur - 1, jnp.inf, imp_s))
        keys.append(_order_key(jnp.where(blk <= cur, forced, NEG_INF)))
    kk = jnp.concatenate(keys, axis=1)

    def bit_body(it, tb):
        cand_b = tb | jnp.left_shift(jnp.int32(1), 31 - it)
        cnt = jnp.sum(jnp.where(kk >= (cand_b ^ INT_MIN), 1.0, 0.0), axis=0, keepdims=True)
        return jnp.where(cnt >= n_sel, cand_b, tb)

    thr = lax.fori_loop(0, 32, bit_body, jnp.zeros((1, N_KV * TQ), I32)) ^ INT_MIN
    need = n_sel - jnp.sum(jnp.where(kk > thr, 1.0, 0.0), axis=0, keepdims=True)
    tril = jnp.where(lax.broadcasted_iota(I32, (LANES, LANES), 0) >= lax.broadcasted_iota(I32, (LANES, LANES), 1),
                     1.0, 0.0).astype(BF16)
    pref = _nn(tril, jnp.where(kk == thr, 1.0, 0.0).astype(BF16))
    tie = jnp.where(kk == thr, jnp.where(pref <= need, 1.0, 0.0), 0.0)
    sel_t = jnp.where(kk > thr, 1.0, tie)
    sel = jnp.concatenate([sel_t[:, g * TQ:(g + 1) * TQ].T for g in range(N_KV)], axis=0).astype(BF16)

    _flash_init(m_ref, acc_ref)
    blk_row =lax.broadcasted_iota(I32, (LANES, CH), 0)
    blk_col = lax.broadcasted_iota(I32, (LANES, CH), 1) >> 6
    row_c = t0 + lax.broadcasted_iota(I32, (TQ, CH), 0)
    col_c = lax.broadcasted_iota(I32, (TQ, CH), 1)

    def slc_chunk(c):
        off = pl.multiple_of(c * CH, CH)
        expand = jnp.where(blk_row == blk_col + c * (CH // SLC_BLOCK), 1.0, 0.0).astype(BF16)
        tok = _nn(sel, expand)
        causal = col_c + off <= row_c
        parts = []
        for g in range(N_KV):
            bias_g = jnp.where(causal, jnp.where(tok[g * TQ:(g + 1) * TQ] > 0.5, 0.0, NEG_INF), NEG_INF)
            parts.extend([bias_g] * GRP)
        sc = _nt(q, ks_ref[pl.ds(off, CH), :]) + jnp.concatenate(parts, axis=0)
        _flash_update(sc, vs_ref[pl.ds(off, CH), :], m_ref, acc_ref, slice(None))

    _unrolled_loop(nch, slc_chunk)
    o_slc = _flash_result(acc_ref)

    gates = gt_ref[...]
    outs = []
    for h in range(N_HEADS):
        rs = slice(h * TQ, (h + 1) * TQ)
        outs.append(gates[:, 3 * h:3 * h + 1] * o_cmp[rs] + gates[:, 3 * h + 1:3 * h + 2] * o_slc[rs]
                    + gates[:, 3 * h + 2:3 * h + 3] * o_win[rs])
    _gated_store(jnp.concatenate(outs, axis=0), z_ref, o_ref, TQ)


def _mla_prep_kernel(cq_ref, ckv_ref, kr_ref, wq_ref, wkv_ref, lat_ref, gains_ref, cos_ref, sa_ref, sb_ref,
                     bd64_ref, bd32_ref, q_ref, k_ref, v_ref):
    lane = lax.broadcasted_iota(I32, (TM, LANES), 1)
    lo_half = lane < NOPE_DIM
    cos, sa, sb = cos_ref[...], sa_ref[...], sb_ref[...]
    bd64, bd32 = bd64_ref[...], bd32_ref[...]
    scale = (NOPE_DIM + ROPE_DIM) ** -0.5 * LOG2E

    def rope_slab(s, gain):
        return _rope(_seg_norm(s, bd32, gain, ROPE_DIM), cos, sa, sb, ROPE_DIM // 2)

    q = _nn(_row_rms(cq_ref[...], lat_ref[0:1, :]).astype(BF16), wq_ref[...])
    kv = _nn(_row_rms(ckv_ref[...], lat_ref[1:2, 0:KV_LORA]).astype(BF16), wkv_ref[...])
    k_rope = pltpu.roll(rope_slab(kr_ref[...], gains_ref[3:4, :]), NOPE_DIM, 1)
    q_rope = [rope_slab(q[:, 512 + LANES * j:512 + LANES * (j + 1)], gains_ref[1:2, :]) * scale for j in range(2)]
    for j in range(4):
        qn = _seg_norm(q[:, LANES * j:LANES * (j + 1)], bd64, gains_ref[0:1, :], NOPE_DIM) * scale
        kn = _seg_norm(kv[:, LANES * j:LANES * (j + 1)], bd64, gains_ref[2:3, :], NOPE_DIM)
        for e in range(2):
            h = 2 * j + e
            qn_h = qn if e == 0 else pltpu.roll(qn, NOPE_DIM, 1)
            kn_h = kn if e == 0 else pltpu.roll(kn, NOPE_DIM, 1)
            shift = (NOPE_DIM - ROPE_DIM * (h % 4)) % LANES
            qr = q_rope[h // 4]
            qr_h = qr if shift == 0 else pltpu.roll(qr, shift, 1)
            q_ref[h] = jnp.where(lo_half, qn_h, jnp.where(lane < NOPE_DIM + ROPE_DIM, qr_h, 0.0)).astype(BF16)
            k_ref[h] = jnp.where(lo_half, kn_h, k_rope).astype(BF16)
    for j in range(4):
        v_ref[j] = _with_ones(kv[:, 512 + LANES * j:512 + LANES * (j + 1)])


def _mla_kernel(q_ref, k_ref, v_ref, z_ref, o_ref, m_ref, acc_ref, s0_ref, s1_ref):
    i = pl.program_id(2)
    tq = TQ_MLA
    _flash_init(m_ref, acc_ref)

    def qk(c):
        off = pl.multiple_of(c * CH, CH)
        return jnp.concatenate([_nt(q_ref[e], k_ref[e, pl.ds(off, CH), :]) for e in range(2)], axis=0)

    def consume(c, s, m_cur, last):
        if last:
            qpos = _stack_rows(lax.broadcasted_iota(I32, (tq, CH), 0), 2)
            s = jnp.where(lax.broadcasted_iota(I32, (2 * tq, CH), 1) <= qpos, s, NEG_INF)
            m_cur = None
        _flash_update(s, v_ref[0, pl.ds(pl.multiple_of(c * CH, CH), CH), :], m_ref, acc_ref, slice(None), m_cur)

    _pipelined_chunks(i + 1, qk, consume, s0_ref, s1_ref)
    o = _flash_result(acc_ref)
    lo_half = lax.broadcasted_iota(I32, (tq, LANES), 1) < NOPE_DIM
    slab = jnp.where(lo_half, o[0:tq], o[tq:2 * tq])
    o_ref[...] = (slab * _silu(z_ref[...])).astype(BF16)


def _rope_tables(pos, dim):
    inv = jnp.power(jnp.float32(ROPE_THETA), -jnp.arange(0, dim, 2, dtype=F32) / dim)
    ang = pos.astype(F32)[:, None] * inv[None, :]
    cos, sin = jnp.cos(ang), jnp.sin(ang)
    reps = LANES // dim
    zero = jnp.zeros_like(sin)
    cos_t = jnp.tile(jnp.concatenate([cos, cos], axis=1), (1, reps))
    sa_t = jnp.tile(jnp.concatenate([zero, sin], axis=1), (1, reps))
    sb_t = jnp.tile(jnp.concatenate([-sin, zero], axis=1), (1, reps))
    return cos_t, sa_t, sb_t


def _block_diag(group):
    r = jnp.arange(LANES) // group
    return (r[:, None] == r[None, :]).astype(BF16)


def _tile_gain(g, width=LANES):
    return jnp.tile(g.astype(F32), width // g.shape[0])


def _pad_lanes(g, width=LANES):
    return jnp.concatenate([g.astype(F32), jnp.zeros((width - g.shape[0],), F32)])


def _pad_cols(w, width):
    return jnp.concatenate([w, jnp.zeros((w.shape[0], width - w.shape[1]), w.dtype)], axis=1)


def _params(*sem, flags=None):
    return pltpu.CompilerParams(dimension_semantics=sem, vmem_limit_bytes=VMEM_LIMIT, flags=flags)


def kernel(x, ab_norm, ab_w_in, a_qk_norm, a_kidx_norm, b_qk_norm, b_sinks, ab_w_out, cd_norm, cd_w_in, c_q_norm, c_k_norm, c_cmp_pe, c_cmp_w1, c_cmp_w2, d_q_lat_norm, d_kv_lat_norm, d_w_uq, d_w_ukv, d_nope_norm, d_rope_norm, cd_w_out):
    bsz, seq, d = x.shape
    n = bsz * seq
    assert d == D_MODEL and seq % CH == 0 and seq >= NSA_WINDOW + TQ and seq // SLC_BLOCK <= LANES
    nq = seq // TQ
    nrow = n // TM
    srow = seq // TM
    xf = x.reshape(n, d)
    pos = jnp.arange(seq)
    cos64, sa64, sb64 = _rope_tables(pos, HEAD_DIM)
    cos32, sa32, sb32 = _rope_tables(pos, ROPE_DIM)
    bd64, bd32 = _block_diag(HEAD_DIM), _block_diag(ROPE_DIM)
    tril = (jnp.arange(CH)[:, None] >= jnp.arange(CH)[None, :]).astype(BF16)

    def row_spec(width, rows=TM):
        return pl.BlockSpec((rows, width), lambda i: (i, 0))

    def const_spec(shape):
        return pl.BlockSpec(shape, lambda i: tuple(0 for _ in shape))

    tab_spec = pl.BlockSpec((TM, LANES), lambda i: (i % srow, 0))
    head_spec = lambda nh: pl.BlockSpec((nh, TM, LANES), lambda i: (0, i, 0))

    wa = ab_w_in[0]
    w_ab = jnp.concatenate([wa[:, 0:1088], _pad_cols(wa[:, 1088:1092], 64), wa[:, 1092:]], axis=1).astype(BF16)
    gains_ab = jnp.stack([_tile_gain(a_qk_norm[0, 0]), _tile_gain(a_qk_norm[0, 1]), _pad_lanes(a_kidx_norm[0]),
                          _tile_gain(b_qk_norm[0, 0]), _tile_gain(b_qk_norm[0, 1])]
                         + [jnp.zeros((LANES,), F32)] * 3)
    sds = jax.ShapeDtypeStruct
    qa, ka, va, qi, ki, wi, za, qb, kb, vb, zb = pl.pallas_call(
        _proj_ab_kernel,
        grid=(nrow,),
        in_specs=[row_spec(d), const_spec((1, d)), const_spec((d, AB_W)), tab_spec, tab_spec, tab_spec,
                  const_spec((LANES, LANES)), const_spec((8, LANES))],
        out_specs=[head_spec(8), row_spec(LANES), row_spec(2 * LANES), head_spec(4), row_spec(LANES), row_spec(LANES),
                   row_spec(512), head_spec(8), row_spec(LANES), row_spec(2 * LANES), row_spec(512)],
        out_shape=[sds((8, n, LANES), BF16), sds((n, LANES), BF16), sds((n, 2 * LANES), BF16),
                   sds((4, n, LANES), BF16), sds((n, LANES), BF16), sds((n, LANES), F32), sds((n, 512), F32),
                   sds((8, n, LANES), BF16), sds((n, LANES), BF16), sds((n, 2 * LANES), BF16), sds((n, 512), F32)],
        compiler_params=_params("parallel"),
    )(xf, ab_norm[0][None, :], w_ab, cos64, sa64, sb64, bd64, gains_ab)

    qt_heads = lambda nh: pl.BlockSpec((nh, TQ, LANES), lambda b, i: (0, b * nq + i, 0))
    qt_rows = lambda width: pl.BlockSpec((TQ, width), lambda b, i: (b * nq + i, 0))
    seq_rows = pl.BlockSpec((seq, LANES), lambda b, i: (b, 0))
    seq_rows_v = pl.BlockSpec((seq, 2 * LANES), lambda b, i: (b, 0))
    att_scratch = [pltpu.VMEM((N_HEADS * TQ, LANES), F32), pltpu.VMEM((N_HEADS * TQ, 2 * LANES), F32)]

    mix_a = pl.pallas_call(
        functools.partial(_dsa_kernel, topk=min(DSA_TOPK, seq // 4)),
        grid=(bsz, nq),
        in_specs=[qt_heads(4), qt_rows(LANES), seq_rows, qt_heads(8), seq_rows, seq_rows_v, qt_rows(512),
                  pl.BlockSpec((CH, CH), lambda b, i: (0, 0))],
        out_specs=qt_rows(512),
        out_shape=sds((n, 512), BF16),
        scratch_shapes=[pltpu.VMEM((seq // CH, CH, TQ), I32), pltpu.VMEM((seq // CH, TQ, CH), F32),
                        pltpu.VMEM((8, TQ), F32)] + att_scratch,
        compiler_params=_params("parallel", "arbitrary"),
    )(qi, wi, ki, qa, ka, va, za, tril)

    prev_rows = lambda width: pl.BlockSpec((TQ, width), lambda b, i: (b * nq + jnp.maximum(i - 1, 0), 0))
    sink_rows = jnp.broadcast_to(jnp.repeat(b_sinks[0].astype(F32) * LOG2E, TQ)[:, None], (N_HEADS * TQ, LANES))
    mix_b = pl.pallas_call(
        _swa_kernel,
        grid=(bsz, nq),
        in_specs=[qt_heads(8), prev_rows(LANES), qt_rows(LANES), prev_rows(2 * LANES), qt_rows(2 * LANES),
                  pl.BlockSpec((N_HEADS * TQ, LANES), lambda b, i: (0, 0)), qt_rows(512)],
        out_specs=qt_rows(512),
        out_shape=sds((n, 512), BF16),
        compiler_params=_params("parallel", "arbitrary"),
    )(qb, kb, kb, vb, vb, sink_rows, zb)

    def out_proj(xin, m0, m1, w):
        return pl.pallas_call(
            _out_proj_kernel,
            grid=(nrow,),
            in_specs=[row_spec(d), row_spec(512), row_spec(512), const_spec((2 * 512, d))],
            out_specs=row_spec(d),
            out_shape=sds((n, d), F32),
            compiler_params=_params("parallel"),
        )(xin, m0, m1, w.astype(BF16))

    x1 = out_proj(xf, mix_a, mix_b, ab_w_out[0])

    wc = cd_w_in[0]
    w_cd = jnp.concatenate([wc[:, 0:1280], _pad_cols(wc[:, 1280:1304], LANES), wc[:, 1304:2200],
                            _pad_cols(wc[:, 2200:2232], LANES), wc[:, 2232:]], axis=1).astype(BF16)
    gains_cd = jnp.stack([_tile_gain(c_q_norm[0]), _tile_gain(c_k_norm[0, 1]), _tile_gain(c_k_norm[0, 2])]
                         + [jnp.zeros((LANES,), F32)] * 5)
    qc, kc_raw, vc_raw, ks, vs, kw, vw, gates, zc, cq, ckv, kr, zd = pl.pallas_call(
        _proj_cd_kernel,
        grid=(nrow,),
        in_specs=[row_spec(d), const_spec((1, d)), const_spec((d, AB_W)), tab_spec, tab_spec, tab_spec,
                  const_spec((LANES, LANES)), const_spec((8, LANES))],
        out_specs=[head_spec(8), row_spec(LANES), row_spec(LANES), row_spec(LANES), row_spec(2 * LANES),
                   row_spec(LANES), row_spec(2 * LANES), row_spec(LANES), row_spec(512), row_spec(256),
                   row_spec(LANES), row_spec(LANES), row_spec(512)],
        out_shape=[sds((8, n, LANES), BF16), sds((n, LANES), F32), sds((n, LANES), F32), sds((n, LANES), BF16),
                   sds((n, 2 * LANES), BF16), sds((n, LANES), BF16), sds((n, 2 * LANES), BF16), sds((n, LANES), F32),
                   sds((n, 512), F32), sds((n, 256), F32), sds((n, LANES), F32), sds((n, LANES), F32),
                   sds((n, 512), F32)],
        compiler_params=_params("parallel"),
    )(x1, cd_norm[0][None, :], w_cd, cos64, sa64, sb64, bd64, gains_cd)

    ncmp = seq // CMP_STRIDE
    cw = CMP_STRIDE * N_KV * HEAD_DIM
    eye = jnp.eye(N_KV, dtype=F32)
    ccos, csa, csb = _rope_tables(jnp.arange(ncmp) * CMP_STRIDE + (2 * CMP_STRIDE - 1), HEAD_DIM)

    def compress(raw, which, is_key):
        w1 = c_cmp_w1[0, which].reshape(2, CMP_STRIDE, HEAD_DIM, -1)
        hid = w1.shape[-1]
        w1e = jnp.einsum("tjdu,gh->tjgdhu", w1, eye).reshape(2, cw, N_KV * hid).astype(BF16)
        w2e = jnp.einsum("ud,gh->guhd", c_cmp_w2[0, which], eye).reshape(N_KV * hid, N_KV * HEAD_DIM).astype(BF16)
        pe = c_cmp_pe[0, which].reshape(2, CMP_STRIDE, 1, HEAD_DIM)
        pe = jnp.broadcast_to(pe, (2, CMP_STRIDE, N_KV, HEAD_DIM)).reshape(2, cw)
        return pl.pallas_call(
            functools.partial(_compress_kernel, is_key=is_key),
            grid=(bsz,),
            in_specs=[pl.BlockSpec((ncmp, cw), lambda b: (b, 0)), const_spec((2, cw)),
                      const_spec((cw, N_KV * hid)), const_spec((cw, N_KV * hid)),
                      const_spec((N_KV * hid, LANES)), const_spec((1, LANES)),
                      const_spec((ncmp, LANES)), const_spec((ncmp, LANES)), const_spec((ncmp, LANES)),
                      const_spec((LANES, LANES))],
            out_specs=pl.BlockSpec((ncmp, LANES), lambda b: (b, 0)),
            out_shape=sds((bsz * ncmp, LANES), BF16),
            compiler_params=_params("parallel"),
        )(raw.reshape(bsz * ncmp, cw), pe, w1e[0], w1e[1], w2e, _tile_gain(c_k_norm[0, 0])[None, :],
          ccos, csa, csb, bd64)

    kc = compress(kc_raw, 0, True)
    vc = compress(vc_raw, 1, False)

    ratio = SLC_BLOCK // CMP_STRIDE
    mm = jnp.arange(ncmp)[:, None]
    jj = jnp.arange(LANES)[None, :]
    wsel = (jnp.where((mm == ratio * jj - 1) | (mm == ratio * jj + ratio - 1), 1.0, 0.0)
            + jnp.where((mm >= ratio * jj) & (mm < ratio * jj + ratio - 1), 2.0, 0.0))
    wsel_t = jnp.where((mm < ncmp - 1) & (jj < seq // SLC_BLOCK), wsel, 0.0).astype(BF16).T

    cmp_rows = pl.BlockSpec((ncmp, LANES), lambda b, i: (b, 0))
    mix_c = pl.pallas_call(
        functools.partial(_nsa_kernel, n_sel=min(SLC_TOPN, seq // SLC_BLOCK)),
        grid=(bsz, nq),
        in_specs=[qt_heads(8), cmp_rows, cmp_rows, seq_rows, seq_rows_v, seq_rows, seq_rows_v, qt_rows(LANES),
                  qt_rows(512), pl.BlockSpec((LANES, ncmp), lambda b, i: (0, 0))],
        out_specs=qt_rows(512),
        out_shape=sds((n, 512), BF16),
        scratch_shapes=att_scratch,
        compiler_params=_params("parallel", "arbitrary"),
    )(qc, kc, vc, ks, vs, kw, vw, gates, zc, wsel_t)

    wq = d_w_uq[0].reshape(Q_LORA, N_HEADS, NOPE_DIM + ROPE_DIM)
    wq = jnp.concatenate([wq[:, :, :NOPE_DIM].reshape(Q_LORA, -1), wq[:, :, NOPE_DIM:].reshape(Q_LORA, -1)],
                         axis=1).astype(BF16)
    wkv = d_w_ukv[0].reshape(KV_LORA, N_HEADS, NOPE_DIM + HEAD_DIM)
    wkv = jnp.concatenate([wkv[:, :, :NOPE_DIM].reshape(KV_LORA, -1), wkv[:, :, NOPE_DIM:].reshape(KV_LORA, -1)],
                          axis=1).astype(BF16)
    lat_gains = jnp.stack([d_q_lat_norm[0].astype(F32), _pad_lanes(d_kv_lat_norm[0], Q_LORA)]
                          + [jnp.zeros((Q_LORA,), F32)] * 6)
    gains_d = jnp.stack([_tile_gain(d_nope_norm[0, 0]), _tile_gain(d_rope_norm[0, 0]), _tile_gain(d_nope_norm[0, 1]),
                         _pad_lanes(d_rope_norm[0, 1])] + [jnp.zeros((LANES,), F32)] * 4)
    q_cat, k_cat, v_d = pl.pallas_call(
        _mla_prep_kernel,
        grid=(nrow,),
        in_specs=[row_spec(Q_LORA), row_spec(LANES), row_spec(LANES), const_spec((Q_LORA, 768)),
                  const_spec((KV_LORA, 1024)), const_spec((8, Q_LORA)), const_spec((8, LANES)),
                  tab_spec, tab_spec, tab_spec, const_spec((LANES, LANES)), const_spec((LANES, LANES))],
        out_specs=[head_spec(8), head_spec(8), pl.BlockSpec((4, TM, 2 * LANES), lambda i: (0, i, 0))],
        out_shape=[sds((8, n, LANES), BF16), sds((8, n, LANES), BF16), sds((4, n, 2 * LANES), BF16)],
        compiler_params=_params("parallel"),
    )(cq, ckv, kr, wq, wkv, lat_gains, gains_d, cos32, sa32, sb32, bd64, bd32)

    nqm = seq // TQ_MLA
    mix_d = pl.pallas_call(
        _mla_kernel,
        grid=(bsz, N_HEADS // 2, nqm),
        in_specs=[pl.BlockSpec((2, TQ_MLA, LANES), lambda b, hp, i: (hp, b * nqm + i, 0)),
                  pl.BlockSpec((2, seq, LANES), lambda b, hp, i: (hp, b, 0)),
                  pl.BlockSpec((1, seq, 2 * LANES), lambda b, hp, i: (hp, b, 0)),
                  pl.BlockSpec((TQ_MLA, LANES), lambda b, hp, i: (b * nqm + i, hp))],
        out_specs=pl.BlockSpec((TQ_MLA, LANES), lambda b, hp, i: (b * nqm + i, hp)),
        out_shape=sds((n, 512), BF16),
        scratch_shapes=[pltpu.VMEM((2 * TQ_MLA, LANES), F32), pltpu.VMEM((2 * TQ_MLA, 2 * LANES), F32),
                        pltpu.VMEM((2 * TQ_MLA, CH + LANES), F32), pltpu.VMEM((2 * TQ_MLA, CH + LANES), F32)],
        compiler_params=_params("parallel", "parallel", "arbitrary"),
    )(q_cat, k_cat, v_d, zd)

    x2 = out_proj(x1, mix_c, mix_d, cd_w_out[0])
    return x2.reshape(bsz, seq, d)
```

```python
import functools

import jax
import jax.numpy as jnp
from jax import lax
from jax.experimental import pallas as pl
from jax.experimental.pallas import tpu as pltpu

F32, BF16, I32 = jnp.float32, jnp.bfloat16, jnp.int32

D_MODEL = 1024
HEAD_DIM = 64
N_HEADS = 8
N_KV = 2
GRP = N_HEADS // N_KV
IDX_HEADS = 4
DSA_TOPK = 256
SWA_WINDOW = 128
CMP_STRIDE = 16
SLC_BLOCK = 64
SLC_TOPN = 16
NSA_WINDOW = 512
Q_LORA = 256
KV_LORA = 128
NOPE_DIM = 64
ROPE_DIM = 32
ROPE_THETA = 10000.0
EPS = 1e-6

LANES = 128
TM = 512
TQ = 128
CH = 512
TQ_MLA = 512
UNROLL = 4
COUNT_ROWS = 64
ROW_BLOCK = 128
VMEM_LIMIT = 56 * 1024 * 1024

NEG_INF = float("-inf")
M_INIT = -1e30
INT_MIN = -(2 ** 31)

AB_W = 2944
LOG2E = 1.4426950408889634
Q_SCALE = HEAD_DIM ** -0.5 * LOG2E


def _nn(a, b):
    return jnp.dot(a, b, preferred_element_type=F32)


def _nt(a, b):
    return lax.dot_general(a, b, (((1,), (1,)), ((), ())), preferred_element_type=F32)


def _sigmoid(z):
    return 1.0 / (1.0 + jnp.exp(-z))


def _silu(z):
    return z * _sigmoid(z)


def _row_rms(x, g):
    return x * lax.rsqrt(jnp.mean(x * x, axis=-1, keepdims=True) + EPS) * g


def _seg_norm(s, bd, gain, group):
    s2 = s * s
    hi = s2.astype(BF16)
    lo = (s2 - hi.astype(F32)).astype(BF16)
    ss = _nn(hi, bd) + _nn(lo, bd)
    return s * lax.rsqrt(ss * (1.0 / group) + EPS) * gain


def _rope(s, cos, sa, sb, half):
    return s * cos + pltpu.roll(s, half, 1) * sa + pltpu.roll(s, LANES - half, 1) * sb


def _order_key(v):
    bits = lax.bitcast_convert_type(v, I32)
    return bits ^ ((bits >> 31) & 0x7FFFFFFF)


def _stack_rows(x, n):
    return jnp.concatenate([x] * n, axis=0)


def _with_ones(v):
    return jnp.concatenate([v.astype(BF16), jnp.ones(v.shape, BF16)], axis=1)


def _flash_update(s, v_ext, m_ref, acc_ref, rows):
    m_prev = m_ref[rows]
    sb = s.astype(BF16)
    m_new = jnp.maximum(m_prev, jnp.max(sb, axis=1, keepdims=True).astype(F32))
    alpha = jnp.exp2(m_prev - m_new)
    p = jnp.exp2(sb - jnp.concatenate([m_new.astype(BF16)] * (s.shape[1] // LANES), axis=1))
    acc_ref[rows] = jnp.concatenate([alpha, alpha], axis=1) * acc_ref[rows] + _nn(p, v_ext)
    m_ref[rows] = m_new


def _unrolled_loop(n, step, unroll=UNROLL):
    def body(j, carry):
        for u in range(unroll):
            step(j * unroll + u)
        return carry

    lax.fori_loop(0, n // unroll, body, 0)
    base = (n // unroll) * unroll
    p = unroll // 2
    while p >= 1:
        def tail(base=base, p=p):
            for u in range(p):
                step(base + u)

        pl.when((n & p) != 0)(tail)
        base = base + (n & p)
        p //= 2


def _flash_init(m_ref, acc_ref):
    m_ref[...] = jnp.full(m_ref.shape, M_INIT, F32)
    acc_ref[...] = jnp.zeros(acc_ref.shape, F32)


def _flash_result(acc_ref):
    acc = acc_ref[...]
    return acc[:, 0:LANES] / jnp.maximum(acc[:, LANES:2 * LANES], 1e-30)


def _place_q_heads(y, bd, gain, cos, sa, sb, out_ref, scale):
    lo_half = lax.broadcasted_iota(I32, (y.shape[0], LANES), 1) < HEAD_DIM
    for j in range(4):
        s = y[:, LANES * j:LANES * (j + 1)]
        s = _rope(_seg_norm(s, bd, gain, HEAD_DIM), cos, sa, sb, HEAD_DIM // 2) * scale
        r = pltpu.roll(s, HEAD_DIM, 1)
        if j < 2:
            out_ref[2 * j] = jnp.where(lo_half, s, 0.0).astype(BF16)
            out_ref[2 * j + 1] = jnp.where(lo_half, r, 0.0).astype(BF16)
        else:
            out_ref[2 * j] = jnp.where(lo_half, 0.0, r).astype(BF16)
            out_ref[2 * j + 1] = jnp.where(lo_half, 0.0, s).astype(BF16)


def _gated_store(o, z_ref, o_ref, tq):
    lo_half = lax.broadcasted_iota(I32, (tq, LANES), 1) < HEAD_DIM
    for j in range(4):
        a = o[(2 * j) * tq:(2 * j + 1) * tq]
        b = o[(2 * j + 1) * tq:(2 * j + 2) * tq]
        if j < 2:
            slab = jnp.where(lo_half, a, pltpu.roll(b, HEAD_DIM, 1))
        else:
            slab = jnp.where(lo_half, pltpu.roll(a, HEAD_DIM, 1), b)
        z = z_ref[:, LANES * j:LANES * (j + 1)]
        o_ref[:, LANES * j:LANES * (j + 1)] = (slab * _silu(z)).astype(BF16)


def _proj_ab_kernel(x_ref, g_ref, w_ref, cos_ref, sa_ref, sb_ref, bd_ref, gains_ref,
                    qa_ref, ka_ref, va_ref, qi_ref, ki_ref, wi_ref, za_ref,
                    qb_ref, kb_ref, vb_ref, zb_ref):
    xn = _row_rms(x_ref[...], g_ref[...]).astype(BF16)
    cos, sa, sb, bd = cos_ref[...], sa_ref[...], sb_ref[...], bd_ref[...]
    lo_half = lax.broadcasted_iota(I32, (TM, LANES), 1) < HEAD_DIM

    def proj(c0, n):
        return _nn(xn, w_ref[:, c0:c0 + n])

    def k_slab(c0, gain):
        return _rope(_seg_norm(proj(c0, LANES), bd, gain, HEAD_DIM), cos, sa, sb, HEAD_DIM // 2)

    _place_q_heads(proj(0, 512), bd, gains_ref[0:1, :], cos, sa, sb, qa_ref, Q_SCALE)
    ka_ref[...] = k_slab(512, gains_ref[1:2, :]).astype(BF16)
    va_ref[...] = _with_ones(proj(640, LANES))
    qi = proj(768, 256)
    for j in range(2):
        s = _rope(qi[:, LANES * j:LANES * (j + 1)], cos, sa, sb, HEAD_DIM // 2)
        qi_ref[2 * j] = jnp.where(lo_half, s, 0.0).astype(BF16)
        qi_ref[2 * j + 1] = jnp.where(lo_half, pltpu.roll(s, HEAD_DIM, 1), 0.0).astype(BF16)
    kiw = proj(1024, LANES)
    wi_ref[...] = kiw
    ki_ref[...] = _rope(_seg_norm(kiw, bd, gains_ref[2:3, :], HEAD_DIM), cos, sa, sb, HEAD_DIM // 2).astype(BF16)
    za_ref[...] = proj(1152, 512)
    _place_q_heads(proj(1664, 512), bd, gains_ref[3:4, :], cos, sa, sb, qb_ref, Q_SCALE)
    kb_ref[...] = k_slab(2176, gains_ref[4:5, :]).astype(BF16)
    vb_ref[...] = _with_ones(proj(2304, LANES))
    zb_ref[...] = proj(2432, 512)


def _count_chunks(keys_ref, nch, nq, pred):
    def body(c, acc):
        return acc + jnp.sum(pred(keys_ref[c]).reshape(CH // COUNT_ROWS, COUNT_ROWS, nq), axis=0)

    acc = lax.fori_loop(0, nch, body, jnp.zeros((COUNT_ROWS, nq), F32))
    return jnp.sum(acc, axis=0, keepdims=True)


def _kth_threshold(count, nq, k):
    def bit_body(it, tb):
        cand_b = tb | jnp.left_shift(jnp.int32(1), 31 - it)
        cand_s = cand_b ^ INT_MIN
        cnt = count(lambda kk: jnp.where(kk >= cand_s, 1.0, 0.0))
        return jnp.where(cnt >= k, cand_b, tb)

    tb = lax.fori_loop(0, 32, bit_body, jnp.zeros((1, nq), I32))
    thr = tb ^ INT_MIN
    need = k - count(lambda kk: jnp.where(kk > thr, 1.0, 0.0))
    return thr, need


def _dsa_kernel(qi_ref, wi_ref, ki_ref, qa_ref, ka_ref, va_ref, za_ref, tril_ref, o_ref,
                keys_ref, bias_ref, ties_ref, m_ref, acc_ref, *, topk):
    i = pl.program_id(1)
    t0 = i * TQ
    nch = (t0 + TQ + CH - 1) // CH
    key_c = lax.broadcasted_iota(I32, (CH, TQ), 0)
    qry_c = t0 + lax.broadcasted_iota(I32, (CH, TQ), 1)

    qi = qi_ref[...].reshape(IDX_HEADS * TQ, LANES)
    w_t = wi_ref[...].T * ((HEAD_DIM * IDX_HEADS) ** -0.5)
    ws = [w_t[HEAD_DIM + h:HEAD_DIM + h + 1, :] for h in range(IDX_HEADS)]

    def score_chunk(c):
        off = pl.multiple_of(c * CH, CH)
        lg = _nt(ki_ref[pl.ds(off, CH), :], qi)
        sc = ws[0] * jnp.maximum(lg[:, 0:TQ], 0.0)
        for h in range(1, IDX_HEADS):
            sc = sc + ws[h] * jnp.maximum(lg[:, h * TQ:(h + 1) * TQ], 0.0)
        sc = jnp.where(sc == 0.0, 0.0, sc)
        keys_ref[c] = jnp.where(key_c + off <= qry_c, _order_key(sc), INT_MIN)

    _unrolled_loop(nch, score_chunk, unroll=2)

    count = functools.partial(_count_chunks, keys_ref, nch, TQ)
    thr, need = _kth_threshold(count, TQ, float(topk))

    tril = tril_ref[...]

    ties_ref[...] = jnp.zeros(ties_ref.shape, F32)

    def select_chunk(c):
        kk = keys_ref[c]
        pref = _nn(tril, jnp.where(kk == thr, 1.0, 0.0).astype(BF16)) + ties_ref[0:1, :]
        ties_ref[0:1, :] = pref[CH - 1:CH, :]
        tie = jnp.where(kk == thr, jnp.where(pref <= need, 0.0, NEG_INF), NEG_INF)
        bias = jnp.where(key_c + c * CH <= qry_c, jnp.where(kk > thr, 0.0, tie), NEG_INF)
        bias_ref[c] = bias.T

    _unrolled_loop(nch, select_chunk)

    _flash_init(m_ref, acc_ref)
    q_all = qa_ref[...].reshape(N_HEADS * TQ, LANES)

    def att_chunk(c):
        off = pl.multiple_of(c * CH, CH)
        s = _nt(q_all, ka_ref[pl.ds(off, CH), :]) + _stack_rows(bias_ref[c], N_HEADS)
        _flash_update(s, va_ref[pl.ds(off, CH), :], m_ref, acc_ref, slice(None))

    _unrolled_loop(nch, att_chunk)
    _gated_store(_flash_result(acc_ref), za_ref, o_ref, TQ)


def _swa_kernel(q_ref, kp_ref, kc_ref, vp_ref, vc_ref, sink_ref, z_ref, o_ref):
    i = pl.program_id(1)
    q = q_ref[...].reshape(N_HEADS * TQ, LANES)
    qi = lax.broadcasted_iota(I32, (TQ, TQ), 0)
    kj = lax.broadcasted_iota(I32, (TQ, TQ), 1)
    no_prev = jnp.where(i > 0, 0, 1 << 20)
    sp = _nt(q, kp_ref[...]) + _stack_rows(jnp.where(kj > qi + no_prev, 0.0, NEG_INF), N_HEADS)
    sc = _nt(q, kc_ref[...]) + _stack_rows(jnp.where(kj <= qi, 0.0, NEG_INF), N_HEADS)
    sink = sink_ref[...]
    m = jnp.maximum(jnp.maximum(jnp.max(sp, axis=1, keepdims=True), jnp.max(sc, axis=1, keepdims=True)), sink)
    pp = jnp.exp2(sp - m)
    pc = jnp.exp2(sc - m)
    o = _nn(pp.astype(BF16), vp_ref[...]) + _nn(pc.astype(BF16), vc_ref[...])
    den = o[:, LANES:2 * LANES] + jnp.exp2(sink - m)
    _gated_store(o[:, 0:LANES] / den, z_ref, o_ref, TQ)


def _out_proj_kernel(x_ref, ma_ref, mb_ref, w_ref, o_ref):
    half = ma_ref.shape[1]
    o_ref[...] = x_ref[...] + _nn(ma_ref[...], w_ref[0:half, :]) + _nn(mb_ref[...], w_ref[half:2 * half, :])


def _proj_cd_kernel(x_ref, g_ref, w_ref, cos_ref, sa_ref, sb_ref, bd_ref, gains_ref,
                    qc_ref, kc_ref, vc_ref, ks_ref, vs_ref, kw_ref, vw_ref, gt_ref, zc_ref,
                    cq_ref, ckv_ref, kr_ref, zd_ref):
    xn = _row_rms(x_ref[...], g_ref[...]).astype(BF16)
    cos, sa, sb, bd = cos_ref[...], sa_ref[...], sb_ref[...], bd_ref[...]

    def proj(c0, n):
        return _nn(xn, w_ref[:, c0:c0 + n])

    def k_slab(c0, gain):
        return _rope(_seg_norm(proj(c0, LANES), bd, gain, HEAD_DIM), cos, sa, sb, HEAD_DIM // 2)

    _place_q_heads(proj(0, 512), bd, gains_ref[0:1, :], cos, sa, sb, qc_ref, Q_SCALE)
    kc_ref[...] = proj(512, LANES)
    vc_ref[...] = proj(640, LANES)
    ks_ref[...] = k_slab(768, gains_ref[1:2, :]).astype(BF16)
    vs_ref[...] = _with_ones(proj(896, LANES))
    kw_ref[...] = k_slab(1024, gains_ref[2:3, :]).astype(BF16)
    vw_ref[...] = _with_ones(proj(1152, LANES))
    gt_ref[...] = _sigmoid(proj(1280, LANES))
    zc_ref[...] = proj(1408, 512)
    cq_ref[...] = proj(1920, 256)
    ckv_ref[...] = proj(2176, LANES)
    kr_ref[...] = proj(2304, LANES)
    zd_ref[...] = proj(2432, 512)


def _compress_kernel(c_ref, pe_ref, w1a_ref, w1b_ref, w2_ref, gain_ref, cos_ref, sa_ref, sb_ref, bd_ref,
                     o_ref, *, is_key):
    c = c_ref[...]
    n = c.shape[0]
    a = _nn((c + pe_ref[0:1, :]).astype(BF16), w1a_ref[...])
    b = _nn((c + pe_ref[1:2, :]).astype(BF16), w1b_ref[...])
    h = _silu(a + pltpu.roll(b, n - 1, 0))
    y = _nn(h.astype(BF16), w2_ref[...])
    if is_key:
        y = _rope(_seg_norm(y, bd_ref[...], gain_ref[...], HEAD_DIM), cos_ref[...], sa_ref[...], sb_ref[...],
                  HEAD_DIM // 2)
    o_ref[...] = y.astype(BF16)


def _nsa_kernel(q_ref, kc_ref, vc_ref, ks_ref, vs_ref, kw_ref, vw_ref, gt_ref, z_ref, wsel_ref, o_ref,
                m_ref, acc_ref, *, n_sel):
    i = pl.program_id(1)
    t0 = i * TQ
    nch = (t0 + TQ + CH - 1) // CH
    ncmp = kc_ref.shape[0]
    rows_all = N_HEADS * TQ
    q = q_ref[...].reshape(rows_all, LANES)

    cend = lax.broadcasted_iota(I32, (TQ, ncmp), 1) * CMP_STRIDE + (2 * CMP_STRIDE - 1)
    bias_c = jnp.where(cend <= t0 + lax.broadcasted_iota(I32, (TQ, ncmp), 0), 0.0, NEG_INF)
    s = _nt(q, kc_ref[...]) + _stack_rows(bias_c, N_HEADS)
    m = jnp.max(s, axis=1, keepdims=True)
    m = jnp.where(m > NEG_INF, m, 0.0)
    p = jnp.exp2(s - m)
    p = p / jnp.maximum(jnp.sum(p, axis=1, keepdims=True), 1e-30)
    o_cmp = _nn(p.astype(BF16), vc_ref[...])

    span = NSA_WINDOW + TQ
    start = pl.multiple_of(jnp.maximum(t0 - NSA_WINDOW, 0), TQ)
    diff = t0 + lax.broadcasted_iota(I32, (TQ, span), 0) - (start + lax.broadcasted_iota(I32, (TQ, span), 1))
    bias_w = jnp.where(diff >= 0, jnp.where(diff < NSA_WINDOW, 0.0, NEG_INF), NEG_INF)
    sw = _nt(q, kw_ref[pl.ds(start, span), :]) + _stack_rows(bias_w, N_HEADS)
    pw = jnp.exp2(sw - jnp.max(sw, axis=1, keepdims=True))
    o_win = _nn(pw.astype(BF16), vw_ref[pl.ds(start, span), :])
    o_win = o_win[:, 0:LANES] / o_win[:, LANES:2 * LANES]

    blk = lax.broadcasted_iota(I32, (LANES, TQ), 0)
    cur = (t0 + lax.broadcasted_iota(I32, (LANES, TQ), 1)) >> 6
    wsel_t = wsel_ref[...]
    keys = []
    for g in range(N_KV):
        imp = p[(GRP * g) * TQ:(GRP * g + 1) * TQ]
        for r in range(1, GRP):
            imp = imp + p[(GRP * g + r) * TQ:(GRP * g + r + 1) * TQ]
        hi = imp.astype(BF16)
        lo = (imp - hi.astype(F32)).astype(BF16)
        imp_s = _nt(wsel_t, hi) + _nt(wsel_t, lo)
        forced = jnp.where(blk == 0, jnp.inf, jnp.where(blk >= cur - 1, jnp.inf, imp_s))
        keys.append(_order_key(jnp.where(blk <= cur, forced, NEG_INF)))
    kk = jnp.concatenate(keys, axis=1)

    def bit_body(it, tb):
        cand_b = tb | jnp.left_shift(jnp.int32(1), 31 - it)
        cnt = jnp.sum(jnp.where(kk >= (cand_b ^ INT_MIN), 1.0, 0.0), axis=0, keepdims=True)
        return jnp.where(cnt >= n_sel, cand_b, tb)

    thr = lax.fori_loop(0, 32, bit_body, jnp.zeros((1, N_KV * TQ), I32)) ^ INT_MIN
    need = n_sel - jnp.sum(jnp.where(kk > thr, 1.0, 0.0), axis=0, keepdims=True)
    tril = jnp.where(lax.broadcasted_iota(I32, (LANES, LANES), 0) >= lax.broadcasted_iota(I32, (LANES, LANES), 1),
                     1.0, 0.0).astype(BF16)
    pref = _nn(tril, jnp.where(kk == thr, 1.0, 0.0).astype(BF16))
    tie = jnp.where(kk == thr, jnp.where(pref <= need, 1.0, 0.0), 0.0)
    sel_t = jnp.where(kk > thr, 1.0, tie)
    sel = jnp.concatenate([sel_t[:, g * TQ:(g + 1) * TQ].T for g in range(N_KV)], axis=0).astype(BF16)

    _flash_init(m_ref, acc_ref)
    blk_row =lax.broadcasted_iota(I32, (LANES, CH), 0)
    blk_col = lax.broadcasted_iota(I32, (LANES, CH), 1) >> 6
    row_c = t0 + lax.broadcasted_iota(I32, (TQ, CH), 0)
    col_c = lax.broadcasted_iota(I32, (TQ, CH), 1)

    def slc_chunk(c):
        off = pl.multiple_of(c * CH, CH)
        expand = jnp.where(blk_row == blk_col + c * (CH // SLC_BLOCK), 1.0, 0.0).astype(BF16)
        tok = _nn(sel, expand)
        causal = col_c + off <= row_c
        parts = []
        for g in range(N_KV):
            bias_g = jnp.where(causal, jnp.where(tok[g * TQ:(g + 1) * TQ] > 0.5, 0.0, NEG_INF), NEG_INF)
            parts.extend([bias_g] * GRP)
        sc = _nt(q, ks_ref[pl.ds(off, CH), :]) + jnp.concatenate(parts, axis=0)
        _flash_update(sc, vs_ref[pl.ds(off, CH), :], m_ref, acc_ref, slice(None))

    _unrolled_loop(nch, slc_chunk)
    o_slc = _flash_result(acc_ref)

    gates = gt_ref[...]
    outs = []
    for h in range(N_HEADS):
        rs = slice(h * TQ, (h + 1) * TQ)
        outs.append(gates[:, 3 * h:3 * h + 1] * o_cmp[rs] + gates[:, 3 * h + 1:3 * h + 2] * o_slc[rs]
                    + gates[:, 3 * h + 2:3 * h + 3] * o_win[rs])
    _gated_store(jnp.concatenate(outs, axis=0), z_ref, o_ref, TQ)


def _mla_prep_kernel(cq_ref, ckv_ref, kr_ref, wq_ref, wkv_ref, lat_ref, gains_ref, cos_ref, sa_ref, sb_ref,
                     bd64_ref, bd32_ref, q_ref, k_ref, v_ref):
    lane = lax.broadcasted_iota(I32, (TM, LANES), 1)
    lo_half = lane < NOPE_DIM
    cos, sa, sb = cos_ref[...], sa_ref[...], sb_ref[...]
    bd64, bd32 = bd64_ref[...], bd32_ref[...]
    scale = (NOPE_DIM + ROPE_DIM) ** -0.5 * LOG2E

    def rope_slab(s, gain):
        return _rope(_seg_norm(s, bd32, gain, ROPE_DIM), cos, sa, sb, ROPE_DIM // 2)

    q = _nn(_row_rms(cq_ref[...], lat_ref[0:1, :]).astype(BF16), wq_ref[...])
    kv = _nn(_row_rms(ckv_ref[...], lat_ref[1:2, 0:KV_LORA]).astype(BF16), wkv_ref[...])
    k_rope = pltpu.roll(rope_slab(kr_ref[...], gains_ref[3:4, :]), NOPE_DIM, 1)
    q_rope = [rope_slab(q[:, 512 + LANES * j:512 + LANES * (j + 1)], gains_ref[1:2, :]) * scale for j in range(2)]
    for j in range(4):
        qn = _seg_norm(q[:, LANES * j:LANES * (j + 1)], bd64, gains_ref[0:1, :], NOPE_DIM) * scale
        kn = _seg_norm(kv[:, LANES * j:LANES * (j + 1)], bd64, gains_ref[2:3, :], NOPE_DIM)
        for e in range(2):
            h = 2 * j + e
            qn_h = qn if e == 0 else pltpu.roll(qn, NOPE_DIM, 1)
            kn_h = kn if e == 0 else pltpu.roll(kn, NOPE_DIM, 1)
            shift = (NOPE_DIM - ROPE_DIM * (h % 4)) % LANES
            qr = q_rope[h // 4]
            qr_h = qr if shift == 0 else pltpu.roll(qr, shift, 1)
            q_ref[h] = jnp.where(lo_half, qn_h, jnp.where(lane < NOPE_DIM + ROPE_DIM, qr_h, 0.0)).astype(BF16)
            k_ref[h] = jnp.where(lo_half, kn_h, k_rope).astype(BF16)
    for j in range(4):
        v_ref[j] = _with_ones(kv[:, 512 + LANES * j:512 + LANES * (j + 1)])


def _mla_kernel(q_ref, k_ref, v_ref, z_ref, o_ref, m_ref, acc_ref):
    i = pl.program_id(2)
    tq = TQ_MLA
    _flash_init(m_ref, acc_ref)

    def step(c, masked):
        off = pl.multiple_of(c * CH, CH)
        v = v_ref[0, pl.ds(off, CH), :]
        for e in range(2):
            k = k_ref[e, pl.ds(off, CH), :]
            for rb in range(tq // ROW_BLOCK):
                s = _nt(q_ref[e, rb * ROW_BLOCK:(rb + 1) * ROW_BLOCK, :], k)
                if masked:
                    qpos = rb * ROW_BLOCK + lax.broadcasted_iota(I32, (ROW_BLOCK, CH), 0)
                    s = jnp.where(lax.broadcasted_iota(I32, (ROW_BLOCK, CH), 1) <= qpos, s, NEG_INF)
                _flash_update(s, v, m_ref, acc_ref, slice(e * tq + rb * ROW_BLOCK, e * tq + (rb + 1) * ROW_BLOCK))

    _unrolled_loop(i, lambda c: step(c, False))
    step(i, True)
    o = _flash_result(acc_ref)
    lo_half = lax.broadcasted_iota(I32, (tq, LANES), 1) < NOPE_DIM
    slab = jnp.where(lo_half, o[0:tq], o[tq:2 * tq])
    o_ref[...] = (slab * _silu(z_ref[...])).astype(BF16)


def _rope_tables(pos, dim):
    inv = jnp.power(jnp.float32(ROPE_THETA), -jnp.arange(0, dim, 2, dtype=F32) / dim)
    ang = pos.astype(F32)[:, None] * inv[None, :]
    cos, sin = jnp.cos(ang), jnp.sin(ang)
    reps = LANES // dim
    zero = jnp.zeros_like(sin)
    cos_t = jnp.tile(jnp.concatenate([cos, cos], axis=1), (1, reps))
    sa_t = jnp.tile(jnp.concatenate([zero, sin], axis=1), (1, reps))
    sb_t = jnp.tile(jnp.concatenate([-sin, zero], axis=1), (1, reps))
    return cos_t, sa_t, sb_t


def _block_diag(group):
    r = jnp.arange(LANES) // group
    return (r[:, None] == r[None, :]).astype(BF16)


def _tile_gain(g, width=LANES):
    return jnp.tile(g.astype(F32), width // g.shape[0])


def _pad_lanes(g, width=LANES):
    return jnp.concatenate([g.astype(F32), jnp.zeros((width - g.shape[0],), F32)])


def _pad_cols(w, width):
    return jnp.concatenate([w, jnp.zeros((w.shape[0], width - w.shape[1]), w.dtype)], axis=1)


def _params(*sem, flags=None):
    return pltpu.CompilerParams(dimension_semantics=sem, vmem_limit_bytes=VMEM_LIMIT, flags=flags)


def kernel(x, ab_norm, ab_w_in, a_qk_norm, a_kidx_norm, b_qk_norm, b_sinks, ab_w_out, cd_norm, cd_w_in, c_q_norm, c_k_norm, c_cmp_pe, c_cmp_w1, c_cmp_w2, d_q_lat_norm, d_kv_lat_norm, d_w_uq, d_w_ukv, d_nope_norm, d_rope_norm, cd_w_out):
    bsz, seq, d = x.shape
    n = bsz * seq
    assert d == D_MODEL and seq % CH == 0 and seq >= NSA_WINDOW + TQ and seq // SLC_BLOCK <= LANES
    nq = seq // TQ
    nrow = n // TM
    srow = seq // TM
    xf = x.reshape(n, d)
    pos = jnp.arange(seq)
    cos64, sa64, sb64 = _rope_tables(pos, HEAD_DIM)
    cos32, sa32, sb32 = _rope_tables(pos, ROPE_DIM)
    bd64, bd32 = _block_diag(HEAD_DIM), _block_diag(ROPE_DIM)
    tril = (jnp.arange(CH)[:, None] >= jnp.arange(CH)[None, :]).astype(BF16)

    def row_spec(width, rows=TM):
        return pl.BlockSpec((rows, width), lambda i: (i, 0))

    def const_spec(shape):
        return pl.BlockSpec(shape, lambda i: tuple(0 for _ in shape))

    tab_spec = pl.BlockSpec((TM, LANES), lambda i: (i % srow, 0))
    head_spec = lambda nh: pl.BlockSpec((nh, TM, LANES), lambda i: (0, i, 0))

    wa = ab_w_in[0]
    w_ab = jnp.concatenate([wa[:, 0:1088], _pad_cols(wa[:, 1088:1092], 64), wa[:, 1092:]], axis=1).astype(BF16)
    gains_ab = jnp.stack([_tile_gain(a_qk_norm[0, 0]), _tile_gain(a_qk_norm[0, 1]), _pad_lanes(a_kidx_norm[0]),
                          _tile_gain(b_qk_norm[0, 0]), _tile_gain(b_qk_norm[0, 1])]
                         + [jnp.zeros((LANES,), F32)] * 3)
    sds = jax.ShapeDtypeStruct
    qa, ka, va, qi, ki, wi, za, qb, kb, vb, zb = pl.pallas_call(
        _proj_ab_kernel,
        grid=(nrow,),
        in_specs=[row_spec(d), const_spec((1, d)), const_spec((d, AB_W)), tab_spec, tab_spec, tab_spec,
                  const_spec((LANES, LANES)), const_spec((8, LANES))],
        out_specs=[head_spec(8), row_spec(LANES), row_spec(2 * LANES), head_spec(4), row_spec(LANES), row_spec(LANES),
                   row_spec(512), head_spec(8), row_spec(LANES), row_spec(2 * LANES), row_spec(512)],
        out_shape=[sds((8, n, LANES), BF16), sds((n, LANES), BF16), sds((n, 2 * LANES), BF16),
                   sds((4, n, LANES), BF16), sds((n, LANES), BF16), sds((n, LANES), F32), sds((n, 512), F32),
                   sds((8, n, LANES), BF16), sds((n, LANES), BF16), sds((n, 2 * LANES), BF16), sds((n, 512), F32)],
        compiler_params=_params("parallel"),
    )(xf, ab_norm[0][None, :], w_ab, cos64, sa64, sb64, bd64, gains_ab)

    qt_heads = lambda nh: pl.BlockSpec((nh, TQ, LANES), lambda b, i: (0, b * nq + i, 0))
    qt_rows = lambda width: pl.BlockSpec((TQ, width), lambda b, i: (b * nq + i, 0))
    seq_rows = pl.BlockSpec((seq, LANES), lambda b, i: (b, 0))
    seq_rows_v = pl.BlockSpec((seq, 2 * LANES), lambda b, i: (b, 0))
    att_scratch = [pltpu.VMEM((N_HEADS * TQ, LANES), F32), pltpu.VMEM((N_HEADS * TQ, 2 * LANES), F32)]

    mix_a = pl.pallas_call(
        functools.partial(_dsa_kernel, topk=min(DSA_TOPK, seq // 4)),
        grid=(bsz, nq),
        in_specs=[qt_heads(4), qt_rows(LANES), seq_rows, qt_heads(8), seq_rows, seq_rows_v, qt_rows(512),
                  pl.BlockSpec((CH, CH), lambda b, i: (0, 0))],
        out_specs=qt_rows(512),
        out_shape=sds((n, 512), BF16),
        scratch_shapes=[pltpu.VMEM((seq // CH, CH, TQ), I32), pltpu.VMEM((seq // CH, TQ, CH), F32),
                        pltpu.VMEM((8, TQ), F32)] + att_scratch,
        compiler_params=_params("parallel", "arbitrary"),
    )(qi, wi, ki, qa, ka, va, za, tril)

    prev_rows = lambda width: pl.BlockSpec((TQ, width), lambda b, i: (b * nq + jnp.maximum(i - 1, 0), 0))
    sink_rows = jnp.broadcast_to(jnp.repeat(b_sinks[0].astype(F32) * LOG2E, TQ)[:, None], (N_HEADS * TQ, LANES))
    mix_b = pl.pallas_call(
        _swa_kernel,
        grid=(bsz, nq),
        in_specs=[qt_heads(8), prev_rows(LANES), qt_rows(LANES), prev_rows(2 * LANES), qt_rows(2 * LANES),
                  pl.BlockSpec((N_HEADS * TQ, LANES), lambda b, i: (0, 0)), qt_rows(512)],
        out_specs=qt_rows(512),
        out_shape=sds((n, 512), BF16),
        compiler_params=_params("parallel", "arbitrary"),
    )(qb, kb, kb, vb, vb, sink_rows, zb)

    def out_proj(xin, m0, m1, w):
        return pl.pallas_call(
            _out_proj_kernel,
            grid=(nrow,),
            in_specs=[row_spec(d), row_spec(512), row_spec(512), const_spec((2 * 512, d))],
            out_specs=row_spec(d),
            out_shape=sds((n, d), F32),
            compiler_params=_params("parallel"),
        )(xin, m0, m1, w.astype(BF16))

    x1 = out_proj(xf, mix_a, mix_b, ab_w_out[0])

    wc = cd_w_in[0]
    w_cd = jnp.concatenate([wc[:, 0:1280], _pad_cols(wc[:, 1280:1304], LANES), wc[:, 1304:2200],
                            _pad_cols(wc[:, 2200:2232], LANES), wc[:, 2232:]], axis=1).astype(BF16)
    gains_cd = jnp.stack([_tile_gain(c_q_norm[0]), _tile_gain(c_k_norm[0, 1]), _tile_gain(c_k_norm[0, 2])]
                         + [jnp.zeros((LANES,), F32)] * 5)
    qc, kc_raw, vc_raw, ks, vs, kw, vw, gates, zc, cq, ckv, kr, zd = pl.pallas_call(
        _proj_cd_kernel,
        grid=(nrow,),
        in_specs=[row_spec(d), const_spec((1, d)), const_spec((d, AB_W)), tab_spec, tab_spec, tab_spec,
                  const_spec((LANES, LANES)), const_spec((8, LANES))],
        out_specs=[head_spec(8), row_spec(LANES), row_spec(LANES), row_spec(LANES), row_spec(2 * LANES),
                   row_spec(LANES), row_spec(2 * LANES), row_spec(LANES), row_spec(512), row_spec(256),
                   row_spec(LANES), row_spec(LANES), row_spec(512)],
        out_shape=[sds((8, n, LANES), BF16), sds((n, LANES), F32), sds((n, LANES), F32), sds((n, LANES), BF16),
                   sds((n, 2 * LANES), BF16), sds((n, LANES), BF16), sds((n, 2 * LANES), BF16), sds((n, LANES), F32),
                   sds((n, 512), F32), sds((n, 256), F32), sds((n, LANES), F32), sds((n, LANES), F32),
                   sds((n, 512), F32)],
        compiler_params=_params("parallel"),
    )(x1, cd_norm[0][None, :], w_cd, cos64, sa64, sb64, bd64, gains_cd)

    ncmp = seq // CMP_STRIDE
    cw = CMP_STRIDE * N_KV * HEAD_DIM
    eye = jnp.eye(N_KV, dtype=F32)
    ccos, csa, csb = _rope_tables(jnp.arange(ncmp) * CMP_STRIDE + (2 * CMP_STRIDE - 1), HEAD_DIM)

    def compress(raw, which, is_key):
        w1 = c_cmp_w1[0, which].reshape(2, CMP_STRIDE, HEAD_DIM, -1)
        hid = w1.shape[-1]
        w1e = jnp.einsum("tjdu,gh->tjgdhu", w1, eye).reshape(2, cw, N_KV * hid).astype(BF16)
        w2e = jnp.einsum("ud,gh->guhd", c_cmp_w2[0, which], eye).reshape(N_KV * hid, N_KV * HEAD_DIM).astype(BF16)
        pe = c_cmp_pe[0, which].reshape(2, CMP_STRIDE, 1, HEAD_DIM)
        pe = jnp.broadcast_to(pe, (2, CMP_STRIDE, N_KV, HEAD_DIM)).reshape(2, cw)
        return pl.pallas_call(
            functools.partial(_compress_kernel, is_key=is_key),
            grid=(bsz,),
            in_specs=[pl.BlockSpec((ncmp, cw), lambda b: (b, 0)), const_spec((2, cw)),
                      const_spec((cw, N_KV * hid)), const_spec((cw, N_KV * hid)),
                      const_spec((N_KV * hid, LANES)), const_spec((1, LANES)),
                      const_spec((ncmp, LANES)), const_spec((ncmp, LANES)), const_spec((ncmp, LANES)),
                      const_spec((LANES, LANES))],
            out_specs=pl.BlockSpec((ncmp, LANES), lambda b: (b, 0)),
            out_shape=sds((bsz * ncmp, LANES), BF16),
            compiler_params=_params("parallel"),
        )(raw.reshape(bsz * ncmp, cw), pe, w1e[0], w1e[1], w2e, _tile_gain(c_k_norm[0, 0])[None, :],
          ccos, csa, csb, bd64)

    kc = compress(kc_raw, 0, True)
    vc = compress(vc_raw, 1, False)

    ratio = SLC_BLOCK // CMP_STRIDE
    mm = jnp.arange(ncmp)[:, None]
    jj = jnp.arange(LANES)[None, :]
    wsel = (jnp.where((mm == ratio * jj - 1) | (mm == ratio * jj + ratio - 1), 1.0, 0.0)
            + jnp.where((mm >= ratio * jj) & (mm < ratio * jj + ratio - 1), 2.0, 0.0))
    wsel_t = jnp.where((mm < ncmp - 1) & (jj < seq // SLC_BLOCK), wsel, 0.0).astype(BF16).T

    cmp_rows = pl.BlockSpec((ncmp, LANES), lambda b, i: (b, 0))
    mix_c = pl.pallas_call(
        functools.partial(_nsa_kernel, n_sel=min(SLC_TOPN, seq // SLC_BLOCK)),
        grid=(bsz, nq),
        in_specs=[qt_heads(8), cmp_rows, cmp_rows, seq_rows, seq_rows_v, seq_rows, seq_rows_v, qt_rows(LANES),
                  qt_rows(512), pl.BlockSpec((LANES, ncmp), lambda b, i: (0, 0))],
        out_specs=qt_rows(512),
        out_shape=sds((n, 512), BF16),
        scratch_shapes=att_scratch,
        compiler_params=_params("parallel", "arbitrary"),
    )(qc, kc, vc, ks, vs, kw, vw, gates, zc, wsel_t)

    wq = d_w_uq[0].reshape(Q_LORA, N_HEADS, NOPE_DIM + ROPE_DIM)
    wq = jnp.concatenate([wq[:, :, :NOPE_DIM].reshape(Q_LORA, -1), wq[:, :, NOPE_DIM:].reshape(Q_LORA, -1)],
                         axis=1).astype(BF16)
    wkv = d_w_ukv[0].reshape(KV_LORA, N_HEADS, NOPE_DIM + HEAD_DIM)
    wkv = jnp.concatenate([wkv[:, :, :NOPE_DIM].reshape(KV_LORA, -1), wkv[:, :, NOPE_DIM:].reshape(KV_LORA, -1)],
                          axis=1).astype(BF16)
    lat_gains = jnp.stack([d_q_lat_norm[0].astype(F32), _pad_lanes(d_kv_lat_norm[0], Q_LORA)]
                          + [jnp.zeros((Q_LORA,), F32)] * 6)
    gains_d = jnp.stack([_tile_gain(d_nope_norm[0, 0]), _tile_gain(d_rope_norm[0, 0]), _tile_gain(d_nope_norm[0, 1]),
                         _pad_lanes(d_rope_norm[0, 1])] + [jnp.zeros((LANES,), F32)] * 4)
    q_cat, k_cat, v_d = pl.pallas_call(
        _mla_prep_kernel,
        grid=(nrow,),
        in_specs=[row_spec(Q_LORA), row_spec(LANES), row_spec(LANES), const_spec((Q_LORA, 768)),
                  const_spec((KV_LORA, 1024)), const_spec((8, Q_LORA)), const_spec((8, LANES)),
                  tab_spec, tab_spec, tab_spec, const_spec((LANES, LANES)), const_spec((LANES, LANES))],
        out_specs=[head_spec(8), head_spec(8), pl.BlockSpec((4, TM, 2 * LANES), lambda i: (0, i, 0))],
        out_shape=[sds((8, n, LANES), BF16), sds((8, n, LANES), BF16), sds((4, n, 2 * LANES), BF16)],
        compiler_params=_params("parallel"),
    )(cq, ckv, kr, wq, wkv, lat_gains, gains_d, cos32, sa32, sb32, bd64, bd32)

    nqm = seq // TQ_MLA
    mix_d = pl.pallas_call(
        _mla_kernel,
        grid=(bsz, N_HEADS // 2, nqm),
        in_specs=[pl.BlockSpec((2, TQ_MLA, LANES), lambda b, hp, i: (hp, b * nqm + i, 0)),
                  pl.BlockSpec((2, seq, LANES), lambda b, hp, i: (hp, b, 0)),
                  pl.BlockSpec((1, seq, 2 * LANES), lambda b, hp, i: (hp, b, 0)),
                  pl.BlockSpec((TQ_MLA, LANES), lambda b, hp, i: (b * nqm + i, hp))],
        out_specs=pl.BlockSpec((TQ_MLA, LANES), lambda b, hp, i: (b * nqm + i, hp)),
        out_shape=sds((n, 512), BF16),
        scratch_shapes=[pltpu.VMEM((2 * TQ_MLA, LANES), F32), pltpu.VMEM((2 * TQ_MLA, 2 * LANES), F32)],
        compiler_params=_params("parallel", "parallel", "arbitrary"),
    )(q_cat, k_cat, v_d, zd)

    x2 = out_proj(x1, mix_c, mix_d, cd_w_out[0])
    return x2.reshape(bsz, seq, d)
```

```python
import functools

import jax
import jax.numpy as jnp
from jax import lax
from jax.experimental import pallas as pl
from jax.experimental.pallas import tpu as pltpu

F32, BF16, I32 = jnp.float32, jnp.bfloat16, jnp.int32

D_MODEL = 1024
HEAD_DIM = 64
N_HEADS = 8
N_KV = 2
GRP = N_HEADS // N_KV
IDX_HEADS = 4
DSA_TOPK = 256
SWA_WINDOW = 128
CMP_STRIDE = 16
SLC_BLOCK = 64
SLC_TOPN = 16
NSA_WINDOW = 512
Q_LORA = 256
KV_LORA = 128
NOPE_DIM = 64
ROPE_DIM = 32
ROPE_THETA = 10000.0
EPS = 1e-6

LANES = 128
TM = 512
TQ = 128
CH = 512
TQ_MLA = 512
UNROLL = 4
COUNT_ROWS = 64
ROW_BLOCK = 128
VMEM_LIMIT = 56 * 1024 * 1024

NEG_INF = float("-inf")
M_INIT = -1e30
INT_MIN = -(2 ** 31)

AB_W = 2944
LOG2E = 1.4426950408889634
Q_SCALE = HEAD_DIM ** -0.5 * LOG2E


def _nn(a, b):
    return jnp.dot(a, b, preferred_element_type=F32)


def _nt(a, b):
    return lax.dot_general(a, b, (((1,), (1,)), ((), ())), preferred_element_type=F32)


def _sigmoid(z):
    return 1.0 / (1.0 + jnp.exp(-z))


def _silu(z):
    return z * _sigmoid(z)


def _row_rms(x, g):
    return x * lax.rsqrt(jnp.mean(x * x, axis=-1, keepdims=True) + EPS) * g


def _seg_norm(s, bd, gain, group):
    s2 = s * s
    hi = s2.astype(BF16)
    lo = (s2 - hi.astype(F32)).astype(BF16)
    ss = _nn(hi, bd) + _nn(lo, bd)
    return s * lax.rsqrt(ss * (1.0 / group) + EPS) * gain


def _rope(s, cos, sa, sb, half):
    return s * cos + pltpu.roll(s, half, 1) * sa + pltpu.roll(s, LANES - half, 1) * sb


def _order_key(v):
    bits = lax.bitcast_convert_type(v, I32)
    return bits ^ ((bits >> 31) & 0x7FFFFFFF)


def _stack_rows(x, n):
    return jnp.concatenate([x] * n, axis=0)


def _with_ones(v):
    return jnp.concatenate([v.astype(BF16), jnp.ones(v.shape, BF16)], axis=1)


def _flash_update(s, v_ext, m_ref, acc_ref, rows):
    m_prev = m_ref[rows]
    sb = s.astype(BF16)
    m_new = jnp.maximum(m_prev, jnp.max(sb, axis=1, keepdims=True).astype(F32))
    alpha = jnp.exp2(m_prev - m_new)
    p = jnp.exp2(sb - jnp.concatenate([m_new.astype(BF16)] * (s.shape[1] // LANES), axis=1))
    acc_ref[rows] = jnp.concatenate([alpha, alpha], axis=1) * acc_ref[rows] + _nn(p, v_ext)
    m_ref[rows] = m_new


def _unrolled_loop(n, step, unroll=UNROLL):
    def body(j, carry):
        for u in range(unroll):
            step(j * unroll + u)
        return carry

    lax.fori_loop(0, n // unroll, body, 0)
    base = (n // unroll) * unroll
    p = unroll // 2
    while p >= 1:
        def tail(base=base, p=p):
            for u in range(p):
                step(base + u)

        pl.when((n & p) != 0)(tail)
        base = base + (n & p)
        p //= 2


def _flash_init(m_ref, acc_ref):
    m_ref[...] = jnp.full(m_ref.shape, M_INIT, F32)
    acc_ref[...] = jnp.zeros(acc_ref.shape, F32)


def _flash_result(acc_ref):
    acc = acc_ref[...]
    return acc[:, 0:LANES] / jnp.maximum(acc[:, LANES:2 * LANES], 1e-30)


def _place_q_heads(y, bd, gain, cos, sa, sb, out_ref, scale):
    lo_half = lax.broadcasted_iota(I32, (y.shape[0], LANES), 1) < HEAD_DIM
    for j in range(4):
        s = y[:, LANES * j:LANES * (j + 1)]
        s = _rope(_seg_norm(s, bd, gain, HEAD_DIM), cos, sa, sb, HEAD_DIM // 2) * scale
        r = pltpu.roll(s, HEAD_DIM, 1)
        if j < 2:
            out_ref[2 * j] = jnp.where(lo_half, s, 0.0).astype(BF16)
            out_ref[2 * j + 1] = jnp.where(lo_half, r, 0.0).astype(BF16)
        else:
            out_ref[2 * j] = jnp.where(lo_half, 0.0, r).astype(BF16)
            out_ref[2 * j + 1] = jnp.where(lo_half, 0.0, s).astype(BF16)


def _gated_store(o, z_ref, o_ref, tq):
    lo_half = lax.broadcasted_iota(I32, (tq, LANES), 1) < HEAD_DIM
    for j in range(4):
        a = o[(2 * j) * tq:(2 * j + 1) * tq]
        b = o[(2 * j + 1) * tq:(2 * j + 2) * tq]
        if j < 2:
            slab = jnp.where(lo_half, a, pltpu.roll(b, HEAD_DIM, 1))
        else:
            slab = jnp.where(lo_half, pltpu.roll(a, HEAD_DIM, 1), b)
        z = z_ref[:, LANES * j:LANES * (j + 1)]
        o_ref[:, LANES * j:LANES * (j + 1)] = (slab * _silu(z)).astype(BF16)


def _proj_ab_kernel(x_ref, g_ref, w_ref, cos_ref, sa_ref, sb_ref, bd_ref, gains_ref,
                    qa_ref, ka_ref, va_ref, qi_ref, ki_ref, wi_ref, za_ref,
                    qb_ref, kb_ref, vb_ref, zb_ref):
    xn = _row_rms(x_ref[...], g_ref[...]).astype(BF16)
    cos, sa, sb, bd = cos_ref[...], sa_ref[...], sb_ref[...], bd_ref[...]
    lo_half = lax.broadcasted_iota(I32, (TM, LANES), 1) < HEAD_DIM

    def proj(c0, n):
        return _nn(xn, w_ref[:, c0:c0 + n])

    def k_slab(c0, gain):
        return _rope(_seg_norm(proj(c0, LANES), bd, gain, HEAD_DIM), cos, sa, sb, HEAD_DIM // 2)

    _place_q_heads(proj(0, 512), bd, gains_ref[0:1, :], cos, sa, sb, qa_ref, Q_SCALE)
    ka_ref[...] = k_slab(512, gains_ref[1:2, :]).astype(BF16)
    va_ref[...] = _with_ones(proj(640, LANES))
    qi = proj(768, 256)
    for j in range(2):
        s = _rope(qi[:, LANES * j:LANES * (j + 1)], cos, sa, sb, HEAD_DIM // 2)
        qi_ref[2 * j] = jnp.where(lo_half, s, 0.0).astype(BF16)
        qi_ref[2 * j + 1] = jnp.where(lo_half, pltpu.roll(s, HEAD_DIM, 1), 0.0).astype(BF16)
    kiw = proj(1024, LANES)
    wi_ref[...] = kiw
    ki_ref[...] = _rope(_seg_norm(kiw, bd, gains_ref[2:3, :], HEAD_DIM), cos, sa, sb, HEAD_DIM // 2).astype(BF16)
    za_ref[...] = proj(1152, 512)
    _place_q_heads(proj(1664, 512), bd, gains_ref[3:4, :], cos, sa, sb, qb_ref, Q_SCALE)
    kb_ref[...] = k_slab(2176, gains_ref[4:5, :]).astype(BF16)
    vb_ref[...] = _with_ones(proj(2304, LANES))
    zb_ref[...] = proj(2432, 512)


def _count_chunks(keys_ref, nch, nq, pred):
    def body(c, acc):
        return acc + jnp.sum(pred(keys_ref[c]).reshape(CH // COUNT_ROWS, COUNT_ROWS, nq), axis=0)

    acc = lax.fori_loop(0, nch, body, jnp.zeros((COUNT_ROWS, nq), F32))
    return jnp.sum(acc, axis=0, keepdims=True)


def _kth_threshold(count, nq, k):
    def bit_body(it, tb):
        cand_b = tb | jnp.left_shift(jnp.int32(1), 31 - it)
        cand_s = cand_b ^ INT_MIN
        cnt = count(lambda kk: jnp.where(kk >= cand_s, 1.0, 0.0))
        return jnp.where(cnt >= k, cand_b, tb)

    tb = lax.fori_loop(0, 32, bit_body, jnp.zeros((1, nq), I32))
    thr = tb ^ INT_MIN
    need = k - count(lambda kk: jnp.where(kk > thr, 1.0, 0.0))
    return thr, need


def _dsa_kernel(qi_ref, wi_ref, ki_ref, qa_ref, ka_ref, va_ref, za_ref, tril_ref, o_ref,
                keys_ref, bias_ref, ties_ref, m_ref, acc_ref, *, topk):
    i = pl.program_id(1)
    t0 = i * TQ
    nch = (t0 + TQ + CH - 1) // CH
    key_c = lax.broadcasted_iota(I32, (CH, TQ), 0)
    qry_c = t0 + lax.broadcasted_iota(I32, (CH, TQ), 1)

    qi = qi_ref[...].reshape(IDX_HEADS * TQ, LANES)
    w_t = wi_ref[...].T * ((HEAD_DIM * IDX_HEADS) ** -0.5)
    ws = [w_t[HEAD_DIM + h:HEAD_DIM + h + 1, :] for h in range(IDX_HEADS)]

    def score_chunk(c):
        off = pl.multiple_of(c * CH, CH)
        lg = _nt(ki_ref[pl.ds(off, CH), :], qi)
        sc = ws[0] * jnp.maximum(lg[:, 0:TQ], 0.0)
        for h in range(1, IDX_HEADS):
            sc = sc + ws[h] * jnp.maximum(lg[:, h * TQ:(h + 1) * TQ], 0.0)
        sc = jnp.where(sc == 0.0, 0.0, sc)
        keys_ref[c] = jnp.where(key_c + off <= qry_c, _order_key(sc), INT_MIN)

    _unrolled_loop(nch, score_chunk, unroll=2)

    count = functools.partial(_count_chunks, keys_ref, nch, TQ)
    thr, need = _kth_threshold(count, TQ, float(topk))

    tril = tril_ref[...]

    ties_ref[...] = jnp.zeros(ties_ref.shape, F32)

    def select_chunk(c):
        kk = keys_ref[c]
        pref = _nn(tril, jnp.where(kk == thr, 1.0, 0.0).astype(BF16)) + ties_ref[0:1, :]
        ties_ref[0:1, :] = pref[CH - 1:CH, :]
        tie = jnp.where(kk == thr, jnp.where(pref <= need, 0.0, NEG_INF), NEG_INF)
        bias = jnp.where(key_c + c * CH <= qry_c, jnp.where(kk > thr, 0.0, tie), NEG_INF)
        bias_ref[c] = bias.T.astype(BF16)

    _unrolled_loop(nch, select_chunk)

    _flash_init(m_ref, acc_ref)
    q_all = qa_ref[...].reshape(N_HEADS * TQ, LANES)

    def att_chunk(c):
        off = pl.multiple_of(c * CH, CH)
        s = _nt(q_all, ka_ref[pl.ds(off, CH), :]).astype(BF16) + _stack_rows(bias_ref[c], N_HEADS)
        _flash_update(s, va_ref[pl.ds(off, CH), :], m_ref, acc_ref, slice(None))

    _unrolled_loop(nch, att_chunk)
    _gated_store(_flash_result(acc_ref), za_ref, o_ref, TQ)


def _swa_kernel(q_ref, kp_ref, kc_ref, vp_ref, vc_ref, sink_ref, z_ref, o_ref):
    i = pl.program_id(1)
    q = q_ref[...].reshape(N_HEADS * TQ, LANES)
    qi = lax.broadcasted_iota(I32, (TQ, TQ), 0)
    kj = lax.broadcasted_iota(I32, (TQ, TQ), 1)
    no_prev = jnp.where(i > 0, 0, 1 << 20)
    sp = _nt(q, kp_ref[...]) + _stack_rows(jnp.where(kj > qi + no_prev, 0.0, NEG_INF), N_HEADS)
    sc = _nt(q, kc_ref[...]) + _stack_rows(jnp.where(kj <= qi, 0.0, NEG_INF), N_HEADS)
    sink = sink_ref[...]
    m = jnp.maximum(jnp.maximum(jnp.max(sp, axis=1, keepdims=True), jnp.max(sc, axis=1, keepdims=True)), sink)
    pp = jnp.exp2(sp - m)
    pc = jnp.exp2(sc - m)
    o = _nn(pp.astype(BF16), vp_ref[...]) + _nn(pc.astype(BF16), vc_ref[...])
    den = o[:, LANES:2 * LANES] + jnp.exp2(sink - m)
    _gated_store(o[:, 0:LANES] / den, z_ref, o_ref, TQ)


def _out_proj_kernel(x_ref, ma_ref, mb_ref, w_ref, o_ref):
    half = ma_ref.shape[1]
    o_ref[...] = x_ref[...] + _nn(ma_ref[...], w_ref[0:half, :]) + _nn(mb_ref[...], w_ref[half:2 * half, :])


def _proj_cd_kernel(x_ref, g_ref, w_ref, cos_ref, sa_ref, sb_ref, bd_ref, gains_ref,
                    qc_ref, kc_ref, vc_ref, ks_ref, vs_ref, kw_ref, vw_ref, gt_ref, zc_ref,
                    cq_ref, ckv_ref, kr_ref, zd_ref):
    xn = _row_rms(x_ref[...], g_ref[...]).astype(BF16)
    cos, sa, sb, bd = cos_ref[...], sa_ref[...], sb_ref[...], bd_ref[...]

    def proj(c0, n):
        return _nn(xn, w_ref[:, c0:c0 + n])

    def k_slab(c0, gain):
        return _rope(_seg_norm(proj(c0, LANES), bd, gain, HEAD_DIM), cos, sa, sb, HEAD_DIM // 2)

    _place_q_heads(proj(0, 512), bd, gains_ref[0:1, :], cos, sa, sb, qc_ref, Q_SCALE)
    kc_ref[...] = proj(512, LANES)
    vc_ref[...] = proj(640, LANES)
    ks_ref[...] = k_slab(768, gains_ref[1:2, :]).astype(BF16)
    vs_ref[...] = _with_ones(proj(896, LANES))
    kw_ref[...] = k_slab(1024, gains_ref[2:3, :]).astype(BF16)
    vw_ref[...] = _with_ones(proj(1152, LANES))
    gt_ref[...] = _sigmoid(proj(1280, LANES))
    zc_ref[...] = proj(1408, 512)
    cq_ref[...] = proj(1920, 256)
    ckv_ref[...] = proj(2176, LANES)
    kr_ref[...] = proj(2304, LANES)
    zd_ref[...] = proj(2432, 512)


def _compress_kernel(c_ref, pe_ref, w1a_ref, w1b_ref, w2_ref, gain_ref, cos_ref, sa_ref, sb_ref, bd_ref,
                     o_ref, *, is_key):
    c = c_ref[...]
    n = c.shape[0]
    a = _nn((c + pe_ref[0:1, :]).astype(BF16), w1a_ref[...])
    b = _nn((c + pe_ref[1:2, :]).astype(BF16), w1b_ref[...])
    h = _silu(a + pltpu.roll(b, n - 1, 0))
    y = _nn(h.astype(BF16), w2_ref[...])
    if is_key:
        y = _rope(_seg_norm(y, bd_ref[...], gain_ref[...], HEAD_DIM), cos_ref[...], sa_ref[...], sb_ref[...],
                  HEAD_DIM // 2)
    o_ref[...] = y.astype(BF16)


def _nsa_kernel(q_ref, kc_ref, vc_ref, ks_ref, vs_ref, kw_ref, vw_ref, gt_ref, z_ref, wsel_ref, o_ref,
                m_ref, acc_ref, *, n_sel):
    i = pl.program_id(1)
    t0 = i * TQ
    nch = (t0 + TQ + CH - 1) // CH
    ncmp = kc_ref.shape[0]
    rows_all = N_HEADS * TQ
    q = q_ref[...].reshape(rows_all, LANES)

    cend = lax.broadcasted_iota(I32, (TQ, ncmp), 1) * CMP_STRIDE + (2 * CMP_STRIDE - 1)
    bias_c = jnp.where(cend <= t0 + lax.broadcasted_iota(I32, (TQ, ncmp), 0), 0.0, NEG_INF)
    s = _nt(q, kc_ref[...]) + _stack_rows(bias_c, N_HEADS)
    m = jnp.max(s, axis=1, keepdims=True)
    m = jnp.where(m > NEG_INF, m, 0.0)
    p = jnp.exp2(s - m)
    p = p / jnp.maximum(jnp.sum(p, axis=1, keepdims=True), 1e-30)
    o_cmp = _nn(p.astype(BF16), vc_ref[...])

    span = NSA_WINDOW + TQ
    start = pl.multiple_of(jnp.maximum(t0 - NSA_WINDOW, 0), TQ)
    diff = t0 + lax.broadcasted_iota(I32, (TQ, span), 0) - (start + lax.broadcasted_iota(I32, (TQ, span), 1))
    bias_w = jnp.where(diff >= 0, jnp.where(diff < NSA_WINDOW, 0.0, NEG_INF), NEG_INF)
    sw = _nt(q, kw_ref[pl.ds(start, span), :]).astype(BF16) + _stack_rows(bias_w.astype(BF16), N_HEADS)
    pw = jnp.exp2(sw - jnp.max(sw, axis=1, keepdims=True))
    o_win = _nn(pw, vw_ref[pl.ds(start, span), :])
    o_win = o_win[:, 0:LANES] / o_win[:, LANES:2 * LANES]

    blk = lax.broadcasted_iota(I32, (LANES, TQ), 0)
    cur = (t0 + lax.broadcasted_iota(I32, (LANES, TQ), 1)) >> 6
    wsel_t = wsel_ref[...]
    keys = []
    for g in range(N_KV):
        imp = p[(GRP * g) * TQ:(GRP * g + 1) * TQ]
        for r in range(1, GRP):
            imp = imp + p[(GRP * g + r) * TQ:(GRP * g + r + 1) * TQ]
        hi = imp.astype(BF16)
        lo = (imp - hi.astype(F32)).astype(BF16)
        imp_s = _nt(wsel_t, hi) + _nt(wsel_t, lo)
        forced = jnp.where(blk == 0, jnp.inf, jnp.where(blk >= cur - 1, jnp.inf, imp_s))
        keys.append(_order_key(jnp.where(blk <= cur, forced, NEG_INF)))
    kk = jnp.concatenate(keys, axis=1)

    def bit_body(it, tb):
        cand_b = tb | jnp.left_shift(jnp.int32(1), 31 - it)
        cnt = jnp.sum(jnp.where(kk >= (cand_b ^ INT_MIN), 1.0, 0.0), axis=0, keepdims=True)
        return jnp.where(cnt >= n_sel, cand_b, tb)

    thr = lax.fori_loop(0, 32, bit_body, jnp.zeros((1, N_KV * TQ), I32)) ^ INT_MIN
    need = n_sel - jnp.sum(jnp.where(kk > thr, 1.0, 0.0), axis=0, keepdims=True)
    tril = jnp.where(lax.broadcasted_iota(I32, (LANES, LANES), 0) >= lax.broadcasted_iota(I32, (LANES, LANES), 1),
                     1.0, 0.0).astype(BF16)
    pref = _nn(tril, jnp.where(kk == thr, 1.0, 0.0).astype(BF16))
    tie = jnp.where(kk == thr, jnp.where(pref <= need, 1.0, 0.0), 0.0)
    sel_t = jnp.where(kk > thr, 1.0, tie)
    sel = jnp.concatenate([sel_t[:, g * TQ:(g + 1) * TQ].T for g in range(N_KV)], axis=0).astype(BF16)

    _flash_init(m_ref, acc_ref)
    blk_row =lax.broadcasted_iota(I32, (LANES, CH), 0)
    blk_col = lax.broadcasted_iota(I32, (LANES, CH), 1) >> 6
    row_c = t0 + lax.broadcasted_iota(I32, (TQ, CH), 0)
    col_c = lax.broadcasted_iota(I32, (TQ, CH), 1)

    def slc_chunk(c):
        off = pl.multiple_of(c * CH, CH)
        expand = jnp.where(blk_row == blk_col + c * (CH // SLC_BLOCK), 1.0, 0.0).astype(BF16)
        tok = _nn(sel, expand)
        causal = col_c + off <= row_c
        parts = []
        for g in range(N_KV):
            bias_g = jnp.where(causal, jnp.where(tok[g * TQ:(g + 1) * TQ] > 0.5, 0.0, NEG_INF), NEG_INF)
            parts.extend([bias_g.astype(BF16)] * GRP)
        sc = _nt(q, ks_ref[pl.ds(off, CH), :]).astype(BF16) + jnp.concatenate(parts, axis=0)
        _flash_update(sc, vs_ref[pl.ds(off, CH), :], m_ref, acc_ref, slice(None))

    _unrolled_loop(nch, slc_chunk)
    o_slc = _flash_result(acc_ref)

    gates = gt_ref[...]
    outs = []
    for h in range(N_HEADS):
        rs = slice(h * TQ, (h + 1) * TQ)
        outs.append(gates[:, 3 * h:3 * h + 1] * o_cmp[rs] + gates[:, 3 * h + 1:3 * h + 2] * o_slc[rs]
                    + gates[:, 3 * h + 2:3 * h + 3] * o_win[rs])
    _gated_store(jnp.concatenate(outs, axis=0), z_ref, o_ref, TQ)


def _mla_prep_kernel(cq_ref, ckv_ref, kr_ref, wq_ref, wkv_ref, lat_ref, gains_ref, cos_ref, sa_ref, sb_ref,
                     bd64_ref, bd32_ref, q_ref, k_ref, v_ref):
    lane = lax.broadcasted_iota(I32, (TM, LANES), 1)
    lo_half = lane < NOPE_DIM
    cos, sa, sb = cos_ref[...], sa_ref[...], sb_ref[...]
    bd64, bd32 = bd64_ref[...], bd32_ref[...]
    scale = (NOPE_DIM + ROPE_DIM) ** -0.5 * LOG2E

    def rope_slab(s, gain):
        return _rope(_seg_norm(s, bd32, gain, ROPE_DIM), cos, sa, sb, ROPE_DIM // 2)

    q = _nn(_row_rms(cq_ref[...], lat_ref[0:1, :]).astype(BF16), wq_ref[...])
    kv = _nn(_row_rms(ckv_ref[...], lat_ref[1:2, 0:KV_LORA]).astype(BF16), wkv_ref[...])
    k_rope = pltpu.roll(rope_slab(kr_ref[...], gains_ref[3:4, :]), NOPE_DIM, 1)
    q_rope = [rope_slab(q[:, 512 + LANES * j:512 + LANES * (j + 1)], gains_ref[1:2, :]) * scale for j in range(2)]
    for j in range(4):
        qn = _seg_norm(q[:, LANES * j:LANES * (j + 1)], bd64, gains_ref[0:1, :], NOPE_DIM) * scale
        kn = _seg_norm(kv[:, LANES * j:LANES * (j + 1)], bd64, gains_ref[2:3, :], NOPE_DIM)
        for e in range(2):
            h = 2 * j + e
            qn_h = qn if e == 0 else pltpu.roll(qn, NOPE_DIM, 1)
            kn_h = kn if e == 0 else pltpu.roll(kn, NOPE_DIM, 1)
            shift = (NOPE_DIM - ROPE_DIM * (h % 4)) % LANES
            qr = q_rope[h // 4]
            qr_h = qr if shift == 0 else pltpu.roll(qr, shift, 1)
            q_ref[h] = jnp.where(lo_half, qn_h, jnp.where(lane < NOPE_DIM + ROPE_DIM, qr_h, 0.0)).astype(BF16)
            k_ref[h] = jnp.where(lo_half, kn_h, k_rope).astype(BF16)
    for j in range(4):
        v_ref[j] = _with_ones(kv[:, 512 + LANES * j:512 + LANES * (j + 1)])


def _mla_kernel(q_ref, k_ref, v_ref, z_ref, o_ref, m_ref, acc_ref):
    i = pl.program_id(2)
    tq = TQ_MLA
    _flash_init(m_ref, acc_ref)

    def step(c, masked):
        off = pl.multiple_of(c * CH, CH)
        v = v_ref[0, pl.ds(off, CH), :]
        for e in range(2):
            k = k_ref[e, pl.ds(off, CH), :]
            for rb in range(tq // ROW_BLOCK):
                s = _nt(q_ref[e, rb * ROW_BLOCK:(rb + 1) * ROW_BLOCK, :], k)
                if masked:
                    qpos = rb * ROW_BLOCK + lax.broadcasted_iota(I32, (ROW_BLOCK, CH), 0)
                    s = jnp.where(lax.broadcasted_iota(I32, (ROW_BLOCK, CH), 1) <= qpos, s, NEG_INF)
                _flash_update(s, v, m_ref, acc_ref, slice(e * tq + rb * ROW_BLOCK, e * tq + (rb + 1) * ROW_BLOCK))

    _unrolled_loop(i, lambda c: step(c, False))
    step(i, True)
    o = _flash_result(acc_ref)
    lo_half = lax.broadcasted_iota(I32, (tq, LANES), 1) < NOPE_DIM
    slab = jnp.where(lo_half, o[0:tq], o[tq:2 * tq])
    o_ref[...] = (slab * _silu(z_ref[...])).astype(BF16)


def _rope_tables(pos, dim):
    inv = jnp.power(jnp.float32(ROPE_THETA), -jnp.arange(0, dim, 2, dtype=F32) / dim)
    ang = pos.astype(F32)[:, None] * inv[None, :]
    cos, sin = jnp.cos(ang), jnp.sin(ang)
    reps = LANES // dim
    zero = jnp.zeros_like(sin)
    cos_t = jnp.tile(jnp.concatenate([cos, cos], axis=1), (1, reps))
    sa_t = jnp.tile(jnp.concatenate([zero, sin], axis=1), (1, reps))
    sb_t = jnp.tile(jnp.concatenate([-sin, zero], axis=1), (1, reps))
    return cos_t, sa_t, sb_t


def _block_diag(group):
    r = jnp.arange(LANES) // group
    return (r[:, None] == r[None, :]).astype(BF16)


def _tile_gain(g, width=LANES):
    return jnp.tile(g.astype(F32), width // g.shape[0])


def _pad_lanes(g, width=LANES):
    return jnp.concatenate([g.astype(F32), jnp.zeros((width - g.shape[0],), F32)])


def _pad_cols(w, width):
    return jnp.concatenate([w, jnp.zeros((w.shape[0], width - w.shape[1]), w.dtype)], axis=1)


def _params(*sem, flags=None):
    return pltpu.CompilerParams(dimension_semantics=sem, vmem_limit_bytes=VMEM_LIMIT, flags=flags)


def kernel(x, ab_norm, ab_w_in, a_qk_norm, a_kidx_norm, b_qk_norm, b_sinks, ab_w_out, cd_norm, cd_w_in, c_q_norm, c_k_norm, c_cmp_pe, c_cmp_w1, c_cmp_w2, d_q_lat_norm, d_kv_lat_norm, d_w_uq, d_w_ukv, d_nope_norm, d_rope_norm, cd_w_out):
    bsz, seq, d = x.shape
    n = bsz * seq
    assert d == D_MODEL and seq % CH == 0 and seq >= NSA_WINDOW + TQ and seq // SLC_BLOCK <= LANES
    nq = seq // TQ
    nrow = n // TM
    srow = seq // TM
    xf = x.reshape(n, d)
    pos = jnp.arange(seq)
    cos64, sa64, sb64 = _rope_tables(pos, HEAD_DIM)
    cos32, sa32, sb32 = _rope_tables(pos, ROPE_DIM)
    bd64, bd32 = _block_diag(HEAD_DIM), _block_diag(ROPE_DIM)
    tril = (jnp.arange(CH)[:, None] >= jnp.arange(CH)[None, :]).astype(BF16)

    def row_spec(width, rows=TM):
        return pl.BlockSpec((rows, width), lambda i: (i, 0))

    def const_spec(shape):
        return pl.BlockSpec(shape, lambda i: tuple(0 for _ in shape))

    tab_spec = pl.BlockSpec((TM, LANES), lambda i: (i % srow, 0))
    head_spec = lambda nh: pl.BlockSpec((nh, TM, LANES), lambda i: (0, i, 0))

    wa = ab_w_in[0]
    w_ab = jnp.concatenate([wa[:, 0:1088], _pad_cols(wa[:, 1088:1092], 64), wa[:, 1092:]], axis=1).astype(BF16)
    gains_ab = jnp.stack([_tile_gain(a_qk_norm[0, 0]), _tile_gain(a_qk_norm[0, 1]), _pad_lanes(a_kidx_norm[0]),
                          _tile_gain(b_qk_norm[0, 0]), _tile_gain(b_qk_norm[0, 1])]
                         + [jnp.zeros((LANES,), F32)] * 3)
    sds = jax.ShapeDtypeStruct
    qa, ka, va, qi, ki, wi, za, qb, kb, vb, zb = pl.pallas_call(
        _proj_ab_kernel,
        grid=(nrow,),
        in_specs=[row_spec(d), const_spec((1, d)), const_spec((d, AB_W)), tab_spec, tab_spec, tab_spec,
                  const_spec((LANES, LANES)), const_spec((8, LANES))],
        out_specs=[head_spec(8), row_spec(LANES), row_spec(2 * LANES), head_spec(4), row_spec(LANES), row_spec(LANES),
                   row_spec(512), head_spec(8), row_spec(LANES), row_spec(2 * LANES), row_spec(512)],
        out_shape=[sds((8, n, LANES), BF16), sds((n, LANES), BF16), sds((n, 2 * LANES), BF16),
                   sds((4, n, LANES), BF16), sds((n, LANES), BF16), sds((n, LANES), F32), sds((n, 512), F32),
                   sds((8, n, LANES), BF16), sds((n, LANES), BF16), sds((n, 2 * LANES), BF16), sds((n, 512), F32)],
        compiler_params=_params("parallel"),
    )(xf, ab_norm[0][None, :], w_ab, cos64, sa64, sb64, bd64, gains_ab)

    qt_heads = lambda nh: pl.BlockSpec((nh, TQ, LANES), lambda b, i: (0, b * nq + i, 0))
    qt_rows = lambda width: pl.BlockSpec((TQ, width), lambda b, i: (b * nq + i, 0))
    seq_rows = pl.BlockSpec((seq, LANES), lambda b, i: (b, 0))
    seq_rows_v = pl.BlockSpec((seq, 2 * LANES), lambda b, i: (b, 0))
    att_scratch = [pltpu.VMEM((N_HEADS * TQ, LANES), F32), pltpu.VMEM((N_HEADS * TQ, 2 * LANES), F32)]

    mix_a = pl.pallas_call(
        functools.partial(_dsa_kernel, topk=min(DSA_TOPK, seq // 4)),
        grid=(bsz, nq),
        in_specs=[qt_heads(4), qt_rows(LANES), seq_rows, qt_heads(8), seq_rows, seq_rows_v, qt_rows(512),
                  pl.BlockSpec((CH, CH), lambda b, i: (0, 0))],
        out_specs=qt_rows(512),
        out_shape=sds((n, 512), BF16),
        scratch_shapes=[pltpu.VMEM((seq // CH, CH, TQ), I32), pltpu.VMEM((seq // CH, TQ, CH), BF16),
                        pltpu.VMEM((8, TQ), F32)] + att_scratch,
        compiler_params=_params("parallel", "arbitrary"),
    )(qi, wi, ki, qa, ka, va, za, tril)

    prev_rows = lambda width: pl.BlockSpec((TQ, width), lambda b, i: (b * nq + jnp.maximum(i - 1, 0), 0))
    sink_rows = jnp.broadcast_to(jnp.repeat(b_sinks[0].astype(F32) * LOG2E, TQ)[:, None], (N_HEADS * TQ, LANES))
    mix_b = pl.pallas_call(
        _swa_kernel,
        grid=(bsz, nq),
        in_specs=[qt_heads(8), prev_rows(LANES), qt_rows(LANES), prev_rows(2 * LANES), qt_rows(2 * LANES),
                  pl.BlockSpec((N_HEADS * TQ, LANES), lambda b, i: (0, 0)), qt_rows(512)],
        out_specs=qt_rows(512),
        out_shape=sds((n, 512), BF16),
        compiler_params=_params("parallel", "arbitrary"),
    )(qb, kb, kb, vb, vb, sink_rows, zb)

    def out_proj(xin, m0, m1, w):
        return pl.pallas_call(
            _out_proj_kernel,
            grid=(nrow,),
            in_specs=[row_spec(d), row_spec(512), row_spec(512), const_spec((2 * 512, d))],
            out_specs=row_spec(d),
            out_shape=sds((n, d), F32),
            compiler_params=_params("parallel"),
        )(xin, m0, m1, w.astype(BF16))

    x1 = out_proj(xf, mix_a, mix_b, ab_w_out[0])

    wc = cd_w_in[0]
    w_cd = jnp.concatenate([wc[:, 0:1280], _pad_cols(wc[:, 1280:1304], LANES), wc[:, 1304:2200],
                            _pad_cols(wc[:, 2200:2232], LANES), wc[:, 2232:]], axis=1).astype(BF16)
    gains_cd = jnp.stack([_tile_gain(c_q_norm[0]), _tile_gain(c_k_norm[0, 1]), _tile_gain(c_k_norm[0, 2])]
                         + [jnp.zeros((LANES,), F32)] * 5)
    qc, kc_raw, vc_raw, ks, vs, kw, vw, gates, zc, cq, ckv, kr, zd = pl.pallas_call(
        _proj_cd_kernel,
        grid=(nrow,),
        in_specs=[row_spec(d), const_spec((1, d)), const_spec((d, AB_W)), tab_spec, tab_spec, tab_spec,
                  const_spec((LANES, LANES)), const_spec((8, LANES))],
        out_specs=[head_spec(8), row_spec(LANES), row_spec(LANES), row_spec(LANES), row_spec(2 * LANES),
                   row_spec(LANES), row_spec(2 * LANES), row_spec(LANES), row_spec(512), row_spec(256),
                   row_spec(LANES), row_spec(LANES), row_spec(512)],
        out_shape=[sds((8, n, LANES), BF16), sds((n, LANES), F32), sds((n, LANES), F32), sds((n, LANES), BF16),
                   sds((n, 2 * LANES), BF16), sds((n, LANES), BF16), sds((n, 2 * LANES), BF16), sds((n, LANES), F32),
                   sds((n, 512), F32), sds((n, 256), F32), sds((n, LANES), F32), sds((n, LANES), F32),
                   sds((n, 512), F32)],
        compiler_params=_params("parallel"),
    )(x1, cd_norm[0][None, :], w_cd, cos64, sa64, sb64, bd64, gains_cd)

    ncmp = seq // CMP_STRIDE
    cw = CMP_STRIDE * N_KV * HEAD_DIM
    eye = jnp.eye(N_KV, dtype=F32)
    ccos, csa, csb = _rope_tables(jnp.arange(ncmp) * CMP_STRIDE + (2 * CMP_STRIDE - 1), HEAD_DIM)

    def compress(raw, which, is_key):
        w1 = c_cmp_w1[0, which].reshape(2, CMP_STRIDE, HEAD_DIM, -1)
        hid = w1.shape[-1]
        w1e = jnp.einsum("tjdu,gh->tjgdhu", w1, eye).reshape(2, cw, N_KV * hid).astype(BF16)
        w2e = jnp.einsum("ud,gh->guhd", c_cmp_w2[0, which], eye).reshape(N_KV * hid, N_KV * HEAD_DIM).astype(BF16)
        pe = c_cmp_pe[0, which].reshape(2, CMP_STRIDE, 1, HEAD_DIM)
        pe = jnp.broadcast_to(pe, (2, CMP_STRIDE, N_KV, HEAD_DIM)).reshape(2, cw)
        return pl.pallas_call(
            functools.partial(_compress_kernel, is_key=is_key),
            grid=(bsz,),
            in_specs=[pl.BlockSpec((ncmp, cw), lambda b: (b, 0)), const_spec((2, cw)),
                      const_spec((cw, N_KV * hid)), const_spec((cw, N_KV * hid)),
                      const_spec((N_KV * hid, LANES)), const_spec((1, LANES)),
                      const_spec((ncmp, LANES)), const_spec((ncmp, LANES)), const_spec((ncmp, LANES)),
                      const_spec((LANES, LANES))],
            out_specs=pl.BlockSpec((ncmp, LANES), lambda b: (b, 0)),
            out_shape=sds((bsz * ncmp, LANES), BF16),
            compiler_params=_params("parallel"),
        )(raw.reshape(bsz * ncmp, cw), pe, w1e[0], w1e[1], w2e, _tile_gain(c_k_norm[0, 0])[None, :],
          ccos, csa, csb, bd64)

    kc = compress(kc_raw, 0, True)
    vc = compress(vc_raw, 1, False)

    ratio = SLC_BLOCK // CMP_STRIDE
    mm = jnp.arange(ncmp)[:, None]
    jj = jnp.arange(LANES)[None, :]
    wsel = (jnp.where((mm == ratio * jj - 1) | (mm == ratio * jj + ratio - 1), 1.0, 0.0)
            + jnp.where((mm >= ratio * jj) & (mm < ratio * jj + ratio - 1), 2.0, 0.0))
    wsel_t = jnp.where((mm < ncmp - 1) & (jj < seq // SLC_BLOCK), wsel, 0.0).astype(BF16).T

    cmp_rows = pl.BlockSpec((ncmp, LANES), lambda b, i: (b, 0))
    mix_c = pl.pallas_call(
        functools.partial(_nsa_kernel, n_sel=min(SLC_TOPN, seq // SLC_BLOCK)),
        grid=(bsz, nq),
        in_specs=[qt_heads(8), cmp_rows, cmp_rows, seq_rows, seq_rows_v, seq_rows, seq_rows_v, qt_rows(LANES),
                  qt_rows(512), pl.BlockSpec((LANES, ncmp), lambda b, i: (0, 0))],
        out_specs=qt_rows(512),
        out_shape=sds((n, 512), BF16),
        scratch_shapes=att_scratch,
        compiler_params=_params("parallel", "arbitrary"),
    )(qc, kc, vc, ks, vs, kw, vw, gates, zc, wsel_t)

    wq = d_w_uq[0].reshape(Q_LORA, N_HEADS, NOPE_DIM + ROPE_DIM)
    wq = jnp.concatenate([wq[:, :, :NOPE_DIM].reshape(Q_LORA, -1), wq[:, :, NOPE_DIM:].reshape(Q_LORA, -1)],
                         axis=1).astype(BF16)
    wkv = d_w_ukv[0].reshape(KV_LORA, N_HEADS, NOPE_DIM + HEAD_DIM)
    wkv = jnp.concatenate([wkv[:, :, :NOPE_DIM].reshape(KV_LORA, -1), wkv[:, :, NOPE_DIM:].reshape(KV_LORA, -1)],
                          axis=1).astype(BF16)
    lat_gains = jnp.stack([d_q_lat_norm[0].astype(F32), _pad_lanes(d_kv_lat_norm[0], Q_LORA)]
                          + [jnp.zeros((Q_LORA,), F32)] * 6)
    gains_d = jnp.stack([_tile_gain(d_nope_norm[0, 0]), _tile_gain(d_rope_norm[0, 0]), _tile_gain(d_nope_norm[0, 1]),
                         _pad_lanes(d_rope_norm[0, 1])] + [jnp.zeros((LANES,), F32)] * 4)
    q_cat, k_cat, v_d = pl.pallas_call(
        _mla_prep_kernel,
        grid=(nrow,),
        in_specs=[row_spec(Q_LORA), row_spec(LANES), row_spec(LANES), const_spec((Q_LORA, 768)),
                  const_spec((KV_LORA, 1024)), const_spec((8, Q_LORA)), const_spec((8, LANES)),
                  tab_spec, tab_spec, tab_spec, const_spec((LANES, LANES)), const_spec((LANES, LANES))],
        out_specs=[head_spec(8), head_spec(8), pl.BlockSpec((4, TM, 2 * LANES), lambda i: (0, i, 0))],
        out_shape=[sds((8, n, LANES), BF16), sds((8, n, LANES), BF16), sds((4, n, 2 * LANES), BF16)],
        compiler_params=_params("parallel"),
    )(cq, ckv, kr, wq, wkv, lat_gains, gains_d, cos32, sa32, sb32, bd64, bd32)

    nqm = seq // TQ_MLA
    mix_d = pl.pallas_call(
        _mla_kernel,
        grid=(bsz, N_HEADS // 2, nqm),
        in_specs=[pl.BlockSpec((2, TQ_MLA, LANES), lambda b, hp, i: (hp, b * nqm + i, 0)),
                  pl.BlockSpec((2, seq, LANES), lambda b, hp, i: (hp, b, 0)),
                  pl.BlockSpec((1, seq, 2 * LANES), lambda b, hp, i: (hp, b, 0)),
                  pl.BlockSpec((TQ_MLA, LANES), lambda b, hp, i: (b * nqm + i, hp))],
        out_specs=pl.BlockSpec((TQ_MLA, LANES), lambda b, hp, i: (b * nqm + i, hp)),
        out_shape=sds((n, 512), BF16),
        scratch_shapes=[pltpu.VMEM((2 * TQ_MLA, LANES), F32), pltpu.VMEM((2 * TQ_MLA, 2 * LANES), F32)],
        compiler_params=_params("parallel", "parallel", "arbitrary"),
    )(q_cat, k_cat, v_d, zd)

    x2 = out_proj(x1, mix_c, mix_d, cd_w_out[0])
    return x2.reshape(bsz, seq, d)
```

```python
import functools

import jax
import jax.numpy as jnp
from jax import lax
from jax.experimental import pallas as pl
from jax.experimental.pallas import tpu as pltpu

F32, BF16, I32 = jnp.float32, jnp.bfloat16, jnp.int32

D_MODEL = 1024
HEAD_DIM = 64
N_HEADS = 8
N_KV = 2
GRP = N_HEADS // N_KV
IDX_HEADS = 4
DSA_TOPK = 256
SWA_WINDOW = 128
CMP_STRIDE = 16
SLC_BLOCK = 64
SLC_TOPN = 16
NSA_WINDOW = 512
Q_LORA = 256
KV_LORA = 128
NOPE_DIM = 64
ROPE_DIM = 32
ROPE_THETA = 10000.0
EPS = 1e-6

LANES = 128
TM = 512
TQ = 128
CH = 512
TQ_MLA = 512
UNROLL = 4
COUNT_ROWS = 64
ROW_BLOCK = 128
VMEM_LIMIT = 56 * 1024 * 1024

NEG_INF = float("-inf")
M_INIT = -1e30
INT_MIN = -(2 ** 31)

AB_W = 2944
LOG2E = 1.4426950408889634
Q_SCALE = HEAD_DIM ** -0.5 * LOG2E


def _nn(a, b):
    return jnp.dot(a, b, preferred_element_type=F32)


def _nt(a, b):
    return lax.dot_general(a, b, (((1,), (1,)), ((), ())), preferred_element_type=F32)


def _sigmoid(z):
    return 1.0 / (1.0 + jnp.exp(-z))


def _silu(z):
    return z * _sigmoid(z)


def _row_rms(x, g):
    return x * lax.rsqrt(jnp.mean(x * x, axis=-1, keepdims=True) + EPS) * g


def _seg_norm(s, bd, gain, group):
    s2 = s * s
    hi = s2.astype(BF16)
    lo = (s2 - hi.astype(F32)).astype(BF16)
    ss = _nn(jnp.concatenate([hi, lo], axis=1), bd)
    return s * lax.rsqrt(ss * (1.0 / group) + EPS) * gain


def _rope(s, cos, sa, sb, half):
    return s * cos + pltpu.roll(s, half, 1) * sa + pltpu.roll(s, LANES - half, 1) * sb


def _order_key(v):
    bits = lax.bitcast_convert_type(v, I32)
    return bits ^ ((bits >> 31) & 0x7FFFFFFF)


def _stack_rows(x, n):
    return jnp.concatenate([x] * n, axis=0)


def _with_ones(v):
    return jnp.concatenate([v.astype(BF16), jnp.ones(v.shape, BF16)], axis=1)


def _flash_update(s, v_ext, m_ref, acc_ref, rows):
    m_prev = m_ref[rows]
    sb = s.astype(BF16)
    m_new = jnp.maximum(m_prev, jnp.max(sb, axis=1, keepdims=True).astype(F32))
    alpha = jnp.exp2(m_prev - m_new)
    p = jnp.exp2(sb - jnp.concatenate([m_new.astype(BF16)] * (s.shape[1] // LANES), axis=1))
    acc_ref[rows] = jnp.concatenate([alpha, alpha], axis=1) * acc_ref[rows] + _nn(p, v_ext)
    m_ref[rows] = m_new


def _unrolled_loop(n, step, unroll=UNROLL):
    def body(j, carry):
        for u in range(unroll):
            step(j * unroll + u)
        return carry

    lax.fori_loop(0, n // unroll, body, 0)
    base = (n // unroll) * unroll
    p = unroll // 2
    while p >= 1:
        def tail(base=base, p=p):
            for u in range(p):
                step(base + u)

        pl.when((n & p) != 0)(tail)
        base = base + (n & p)
        p //= 2


def _flash_init(m_ref, acc_ref):
    m_ref[...] = jnp.full(m_ref.shape, M_INIT, F32)
    acc_ref[...] = jnp.zeros(acc_ref.shape, F32)


def _flash_result(acc_ref):
    acc = acc_ref[...]
    return acc[:, 0:LANES] / jnp.maximum(acc[:, LANES:2 * LANES], 1e-30)


def _place_q_heads(y, bd, gain, cos, sa, sb, out_ref, scale):
    lo_half = lax.broadcasted_iota(I32, (y.shape[0], LANES), 1) < HEAD_DIM
    for j in range(4):
        s = y[:, LANES * j:LANES * (j + 1)]
        s = _rope(_seg_norm(s, bd, gain, HEAD_DIM), cos, sa, sb, HEAD_DIM // 2) * scale
        r = pltpu.roll(s, HEAD_DIM, 1)
        if j < 2:
            out_ref[2 * j] = jnp.where(lo_half, s, 0.0).astype(BF16)
            out_ref[2 * j + 1] = jnp.where(lo_half, r, 0.0).astype(BF16)
        else:
            out_ref[2 * j] = jnp.where(lo_half, 0.0, r).astype(BF16)
            out_ref[2 * j + 1] = jnp.where(lo_half, 0.0, s).astype(BF16)


def _gated_store(o, z_ref, o_ref, tq):
    lo_half = lax.broadcasted_iota(I32, (tq, LANES), 1) < HEAD_DIM
    for j in range(4):
        a = o[(2 * j) * tq:(2 * j + 1) * tq]
        b = o[(2 * j + 1) * tq:(2 * j + 2) * tq]
        if j < 2:
            slab = jnp.where(lo_half, a, pltpu.roll(b, HEAD_DIM, 1))
        else:
            slab = jnp.where(lo_half, pltpu.roll(a, HEAD_DIM, 1), b)
        z = z_ref[:, LANES * j:LANES * (j + 1)]
        o_ref[:, LANES * j:LANES * (j + 1)] = (slab * _silu(z)).astype(BF16)


def _proj_ab_kernel(x_ref, g_ref, w_ref, cos_ref, sa_ref, sb_ref, bd_ref, gains_ref,
                    qa_ref, ka_ref, va_ref, qi_ref, ki_ref, wi_ref, za_ref,
                    qb_ref, kb_ref, vb_ref, zb_ref):
    xn = _row_rms(x_ref[...], g_ref[...]).astype(BF16)
    cos, sa, sb, bd = cos_ref[...], sa_ref[...], sb_ref[...], bd_ref[...]
    lo_half = lax.broadcasted_iota(I32, (TM, LANES), 1) < HEAD_DIM

    def proj(c0, n):
        return _nn(xn, w_ref[:, c0:c0 + n])

    def k_slab(c0, gain):
        return _rope(_seg_norm(proj(c0, LANES), bd, gain, HEAD_DIM), cos, sa, sb, HEAD_DIM // 2)

    _place_q_heads(proj(0, 512), bd, gains_ref[0:1, :], cos, sa, sb, qa_ref, Q_SCALE)
    ka_ref[...] = k_slab(512, gains_ref[1:2, :]).astype(BF16)
    va_ref[...] = _with_ones(proj(640, LANES))
    qi = proj(768, 256)
    for j in range(2):
        s = _rope(qi[:, LANES * j:LANES * (j + 1)], cos, sa, sb, HEAD_DIM // 2)
        qi_ref[2 * j] = jnp.where(lo_half, s, 0.0).astype(BF16)
        qi_ref[2 * j + 1] = jnp.where(lo_half, pltpu.roll(s, HEAD_DIM, 1), 0.0).astype(BF16)
    kiw = proj(1024, LANES)
    wi_ref[...] = kiw
    ki_ref[...] = _rope(_seg_norm(kiw, bd, gains_ref[2:3, :], HEAD_DIM), cos, sa, sb, HEAD_DIM // 2).astype(BF16)
    za_ref[...] = proj(1152, 512)
    _place_q_heads(proj(1664, 512), bd, gains_ref[3:4, :], cos, sa, sb, qb_ref, Q_SCALE)
    kb_ref[...] = k_slab(2176, gains_ref[4:5, :]).astype(BF16)
    vb_ref[...] = _with_ones(proj(2304, LANES))
    zb_ref[...] = proj(2432, 512)


def _count_chunks(keys_ref, nch, nq, pred):
    def body(c, acc):
        return acc + jnp.sum(pred(keys_ref[c]).reshape(CH // COUNT_ROWS, COUNT_ROWS, nq), axis=0)

    acc = lax.fori_loop(0, nch, body, jnp.zeros((COUNT_ROWS, nq), F32))
    return jnp.sum(acc, axis=0, keepdims=True)


def _kth_threshold(count, nq, k):
    def bit_body(it, carry):
        tb, above = carry
        cand_b = tb | jnp.left_shift(jnp.int32(1), 31 - it)
        cand_s = cand_b ^ INT_MIN
        cnt = count(lambda kk: jnp.where(kk >= cand_s, 1.0, 0.0))
        ok = cnt >= k
        return jnp.where(ok, cand_b, tb), jnp.where(ok, above, cnt)

    tb, above = lax.fori_loop(0, 32, bit_body, (jnp.zeros((1, nq), I32), jnp.zeros((1, nq), F32)))
    return tb ^ INT_MIN, k - above


def _dsa_kernel(qi_ref, wi_ref, ki_ref, qa_ref, ka_ref, va_ref, za_ref, tril_ref, o_ref,
                keys_ref, bias_ref, ties_ref, m_ref, acc_ref, *, topk):
    i = pl.program_id(1)
    t0 = i * TQ
    nch = (t0 + TQ + CH - 1) // CH
    key_c = lax.broadcasted_iota(I32, (CH, TQ), 0)
    qry_c = t0 + lax.broadcasted_iota(I32, (CH, TQ), 1)

    qi = qi_ref[...].reshape(IDX_HEADS * TQ, LANES)
    w_t = wi_ref[...].T * ((HEAD_DIM * IDX_HEADS) ** -0.5)
    ws = [w_t[HEAD_DIM + h:HEAD_DIM + h + 1, :] for h in range(IDX_HEADS)]

    def score_chunk(c):
        off = pl.multiple_of(c * CH, CH)
        lg = _nt(ki_ref[pl.ds(off, CH), :], qi)
        sc = ws[0] * jnp.maximum(lg[:, 0:TQ], 0.0)
        for h in range(1, IDX_HEADS):
            sc = sc + ws[h] * jnp.maximum(lg[:, h * TQ:(h + 1) * TQ], 0.0)
        sc = jnp.where(sc == 0.0, 0.0, sc)
        keys_ref[c] = jnp.where(key_c + off <= qry_c, _order_key(sc), INT_MIN)

    _unrolled_loop(nch, score_chunk, unroll=2)

    count = functools.partial(_count_chunks, keys_ref, nch, TQ)
    thr, need = _kth_threshold(count, TQ, float(topk))

    tril = tril_ref[...]

    ties_ref[...] = jnp.zeros(ties_ref.shape, F32)

    def select_chunk(c):
        kk = keys_ref[c]
        pref = _nn(tril, jnp.where(kk == thr, 1.0, 0.0).astype(BF16)) + ties_ref[0:1, :]
        ties_ref[0:1, :] = pref[CH - 1:CH, :]
        tie = jnp.where(kk == thr, jnp.where(pref <= need, 0.0, NEG_INF), NEG_INF)
        bias = jnp.where(key_c + c * CH <= qry_c, jnp.where(kk > thr, 0.0, tie), NEG_INF)
        bias_ref[c] = bias.T.astype(BF16)

    _unrolled_loop(nch, select_chunk)

    _flash_init(m_ref, acc_ref)
    q_all = qa_ref[...].reshape(N_HEADS * TQ, LANES)

    def att_chunk(c):
        off = pl.multiple_of(c * CH, CH)
        s = _nt(q_all, ka_ref[pl.ds(off, CH), :]).astype(BF16) + _stack_rows(bias_ref[c], N_HEADS)
        _flash_update(s, va_ref[pl.ds(off, CH), :], m_ref, acc_ref, slice(None))

    _unrolled_loop(nch, att_chunk)
    _gated_store(_flash_result(acc_ref), za_ref, o_ref, TQ)


def _swa_kernel(q_ref, kp_ref, kc_ref, vp_ref, vc_ref, sink_ref, z_ref, o_ref):
    i = pl.program_id(1)
    q = q_ref[...].reshape(N_HEADS * TQ, LANES)
    qi = lax.broadcasted_iota(I32, (TQ, TQ), 0)
    kj = lax.broadcasted_iota(I32, (TQ, TQ), 1)
    no_prev = jnp.where(i > 0, 0, 1 << 20)
    sp = _nt(q, kp_ref[...]) + _stack_rows(jnp.where(kj > qi + no_prev, 0.0, NEG_INF), N_HEADS)
    sc = _nt(q, kc_ref[...]) + _stack_rows(jnp.where(kj <= qi, 0.0, NEG_INF), N_HEADS)
    sink = sink_ref[...]
    m = jnp.maximum(jnp.maximum(jnp.max(sp, axis=1, keepdims=True), jnp.max(sc, axis=1, keepdims=True)), sink)
    pp = jnp.exp2(sp - m)
    pc = jnp.exp2(sc - m)
    o = _nn(pp.astype(BF16), vp_ref[...]) + _nn(pc.astype(BF16), vc_ref[...])
    den = o[:, LANES:2 * LANES] + jnp.exp2(sink - m)
    _gated_store(o[:, 0:LANES] / den, z_ref, o_ref, TQ)


def _out_proj_kernel(x_ref, ma_ref, mb_ref, w_ref, o_ref):
    half = ma_ref.shape[1]
    o_ref[...] = x_ref[...] + _nn(ma_ref[...], w_ref[0:half, :]) + _nn(mb_ref[...], w_ref[half:2 * half, :])


def _proj_cd_kernel(x_ref, g_ref, w_ref, cos_ref, sa_ref, sb_ref, bd_ref, gains_ref,
                    qc_ref, kc_ref, vc_ref, ks_ref, vs_ref, kw_ref, vw_ref, gt_ref, zc_ref,
                    cq_ref, ckv_ref, kr_ref, zd_ref):
    xn = _row_rms(x_ref[...], g_ref[...]).astype(BF16)
    cos, sa, sb, bd = cos_ref[...], sa_ref[...], sb_ref[...], bd_ref[...]

    def proj(c0, n):
        return _nn(xn, w_ref[:, c0:c0 + n])

    def k_slab(c0, gain):
        return _rope(_seg_norm(proj(c0, LANES), bd, gain, HEAD_DIM), cos, sa, sb, HEAD_DIM // 2)

    _place_q_heads(proj(0, 512), bd, gains_ref[0:1, :], cos, sa, sb, qc_ref, Q_SCALE)
    kc_ref[...] = proj(512, LANES)
    vc_ref[...] = proj(640, LANES)
    ks_ref[...] = k_slab(768, gains_ref[1:2, :]).astype(BF16)
    vs_ref[...] = _with_ones(proj(896, LANES))
    kw_ref[...] = k_slab(1024, gains_ref[2:3, :]).astype(BF16)
    vw_ref[...] = _with_ones(proj(1152, LANES))
    gt_ref[...] = _sigmoid(proj(1280, LANES))
    zc_ref[...] = proj(1408, 512)
    cq_ref[...] = proj(1920, 256)
    ckv_ref[...] = proj(2176, LANES)
    kr_ref[...] = proj(2304, LANES)
    zd_ref[...] = proj(2432, 512)


def _compress_kernel(c_ref, pe_ref, w1a_ref, w1b_ref, w2_ref, gain_ref, cos_ref, sa_ref, sb_ref, bd_ref,
                     o_ref, *, is_key):
    c = c_ref[...]
    n = c.shape[0]
    a = _nn((c + pe_ref[0:1, :]).astype(BF16), w1a_ref[...])
    b = _nn((c + pe_ref[1:2, :]).astype(BF16), w1b_ref[...])
    h = _silu(a + pltpu.roll(b, n - 1, 0))
    y = _nn(h.astype(BF16), w2_ref[...])
    if is_key:
        y = _rope(_seg_norm(y, bd_ref[...], gain_ref[...], HEAD_DIM), cos_ref[...], sa_ref[...], sb_ref[...],
                  HEAD_DIM // 2)
    o_ref[...] = y.astype(BF16)


def _nsa_kernel(q_ref, kc_ref, vc_ref, ks_ref, vs_ref, kw_ref, vw_ref, gt_ref, z_ref, wsel_ref, o_ref,
                m_ref, acc_ref, *, n_sel):
    i = pl.program_id(1)
    t0 = i * TQ
    nch = (t0 + TQ + CH - 1) // CH
    ncmp = kc_ref.shape[0]
    rows_all = N_HEADS * TQ
    q = q_ref[...].reshape(rows_all, LANES)

    cend = lax.broadcasted_iota(I32, (TQ, ncmp), 1) * CMP_STRIDE + (2 * CMP_STRIDE - 1)
    bias_c = jnp.where(cend <= t0 + lax.broadcasted_iota(I32, (TQ, ncmp), 0), 0.0, NEG_INF)
    s = _nt(q, kc_ref[...]) + _stack_rows(bias_c, N_HEADS)
    m = jnp.max(s, axis=1, keepdims=True)
    m = jnp.where(m > NEG_INF, m, 0.0)
    p = jnp.exp2(s - m)
    p = p / jnp.maximum(jnp.sum(p, axis=1, keepdims=True), 1e-30)
    o_cmp = _nn(p.astype(BF16), vc_ref[...])

    span = NSA_WINDOW + TQ
    start = pl.multiple_of(jnp.maximum(t0 - NSA_WINDOW, 0), TQ)
    diff = t0 + lax.broadcasted_iota(I32, (TQ, span), 0) - (start + lax.broadcasted_iota(I32, (TQ, span), 1))
    bias_w = jnp.where(diff >= 0, jnp.where(diff < NSA_WINDOW, 0.0, NEG_INF), NEG_INF)
    sw = _nt(q, kw_ref[pl.ds(start, span), :]).astype(BF16) + _stack_rows(bias_w.astype(BF16), N_HEADS)
    pw = jnp.exp2(sw - jnp.max(sw, axis=1, keepdims=True))
    o_win = _nn(pw, vw_ref[pl.ds(start, span), :])
    o_win = o_win[:, 0:LANES] / o_win[:, LANES:2 * LANES]

    blk = lax.broadcasted_iota(I32, (LANES, TQ), 0)
    cur = (t0 + lax.broadcasted_iota(I32, (LANES, TQ), 1)) >> 6
    wsel_t = wsel_ref[...]
    keys = []
    for g in range(N_KV):
        imp = p[(GRP * g) * TQ:(GRP * g + 1) * TQ]
        for r in range(1, GRP):
            imp = imp + p[(GRP * g + r) * TQ:(GRP * g + r + 1) * TQ]
        hi = imp.astype(BF16)
        lo = (imp - hi.astype(F32)).astype(BF16)
        imp_s = _nt(wsel_t, hi) + _nt(wsel_t, lo)
        forced = jnp.where(blk == 0, jnp.inf, jnp.where(blk >= cur - 1, jnp.inf, imp_s))
        keys.append(_order_key(jnp.where(blk <= cur, forced, NEG_INF)))
    kk = jnp.concatenate(keys, axis=1)

    def bit_body(it, tb):
        cand_b = tb | jnp.left_shift(jnp.int32(1), 31 - it)
        cnt = jnp.sum(jnp.where(kk >= (cand_b ^ INT_MIN), 1.0, 0.0), axis=0, keepdims=True)
        return jnp.where(cnt >= n_sel, cand_b, tb)

    thr = lax.fori_loop(0, 32, bit_body, jnp.zeros((1, N_KV * TQ), I32)) ^ INT_MIN
    need = n_sel - jnp.sum(jnp.where(kk > thr, 1.0, 0.0), axis=0, keepdims=True)
    tril = jnp.where(lax.broadcasted_iota(I32, (LANES, LANES), 0) >= lax.broadcasted_iota(I32, (LANES, LANES), 1),
                     1.0, 0.0).astype(BF16)
    pref = _nn(tril, jnp.where(kk == thr, 1.0, 0.0).astype(BF16))
    tie = jnp.where(kk == thr, jnp.where(pref <= need, 1.0, 0.0), 0.0)
    sel_t = jnp.where(kk > thr, 1.0, tie)
    sel = jnp.concatenate([sel_t[:, g * TQ:(g + 1) * TQ].T for g in range(N_KV)], axis=0).astype(BF16)

    _flash_init(m_ref, acc_ref)
    blk_row =lax.broadcasted_iota(I32, (LANES, CH), 0)
    blk_col = lax.broadcasted_iota(I32, (LANES, CH), 1) >> 6
    row_c = t0 + lax.broadcasted_iota(I32, (TQ, CH), 0)
    col_c = lax.broadcasted_iota(I32, (TQ, CH), 1)

    def slc_chunk(c):
        off = pl.multiple_of(c * CH, CH)
        expand = jnp.where(blk_row == blk_col + c * (CH // SLC_BLOCK), 1.0, 0.0).astype(BF16)
        tok = _nn(sel, expand)
        causal = col_c + off <= row_c
        parts = []
        for g in range(N_KV):
            bias_g = jnp.where(causal, jnp.where(tok[g * TQ:(g + 1) * TQ] > 0.5, 0.0, NEG_INF), NEG_INF)
            parts.extend([bias_g.astype(BF16)] * GRP)
        sc = _nt(q, ks_ref[pl.ds(off, CH), :]).astype(BF16) + jnp.concatenate(parts, axis=0)
        _flash_update(sc, vs_ref[pl.ds(off, CH), :], m_ref, acc_ref, slice(None))

    _unrolled_loop(nch, slc_chunk)
    o_slc = _flash_result(acc_ref)

    gates = gt_ref[...]
    outs = []
    for h in range(N_HEADS):
        rs = slice(h * TQ, (h + 1) * TQ)
        outs.append(gates[:, 3 * h:3 * h + 1] * o_cmp[rs] + gates[:, 3 * h + 1:3 * h + 2] * o_slc[rs]
                    + gates[:, 3 * h + 2:3 * h + 3] * o_win[rs])
    _gated_store(jnp.concatenate(outs, axis=0), z_ref, o_ref, TQ)


def _mla_prep_kernel(cq_ref, ckv_ref, kr_ref, wq_ref, wkv_ref, lat_ref, gains_ref, cos_ref, sa_ref, sb_ref,
                     bd64_ref, bd32_ref, q_ref, k_ref, v_ref):
    lane = lax.broadcasted_iota(I32, (TM, LANES), 1)
    lo_half = lane < NOPE_DIM
    cos, sa, sb = cos_ref[...], sa_ref[...], sb_ref[...]
    bd64, bd32 = bd64_ref[...], bd32_ref[...]
    scale = (NOPE_DIM + ROPE_DIM) ** -0.5 * LOG2E

    def rope_slab(s, gain):
        return _rope(_seg_norm(s, bd32, gain, ROPE_DIM), cos, sa, sb, ROPE_DIM // 2)

    q = _nn(_row_rms(cq_ref[...], lat_ref[0:1, :]).astype(BF16), wq_ref[...])
    kv = _nn(_row_rms(ckv_ref[...], lat_ref[1:2, 0:KV_LORA]).astype(BF16), wkv_ref[...])
    k_rope = pltpu.roll(rope_slab(kr_ref[...], gains_ref[3:4, :]), NOPE_DIM, 1)
    q_rope = [rope_slab(q[:, 512 + LANES * j:512 + LANES * (j + 1)], gains_ref[1:2, :]) * scale for j in range(2)]
    for j in range(4):
        qn = _seg_norm(q[:, LANES * j:LANES * (j + 1)], bd64, gains_ref[0:1, :], NOPE_DIM) * scale
        kn = _seg_norm(kv[:, LANES * j:LANES * (j + 1)], bd64, gains_ref[2:3, :], NOPE_DIM)
        for e in range(2):
            h = 2 * j + e
            qn_h = qn if e == 0 else pltpu.roll(qn, NOPE_DIM, 1)
            kn_h = kn if e == 0 else pltpu.roll(kn, NOPE_DIM, 1)
            shift = (NOPE_DIM - ROPE_DIM * (h % 4)) % LANES
            qr = q_rope[h // 4]
            qr_h = qr if shift == 0 else pltpu.roll(qr, shift, 1)
            q_ref[h] = jnp.where(lo_half, qn_h, jnp.where(lane < NOPE_DIM + ROPE_DIM, qr_h, 0.0)).astype(BF16)
            k_ref[h] = jnp.where(lo_half, kn_h, k_rope).astype(BF16)
    for j in range(4):
        v_ref[j] = _with_ones(kv[:, 512 + LANES * j:512 + LANES * (j + 1)])


def _mla_kernel(q_ref, k_ref, v_ref, z_ref, o_ref, m_ref, acc_ref):
    i = pl.program_id(2)
    tq = TQ_MLA
    _flash_init(m_ref, acc_ref)

    def step(c, masked):
        off = pl.multiple_of(c * CH, CH)
        v = v_ref[0, pl.ds(off, CH), :]
        for e in range(2):
            k = k_ref[e, pl.ds(off, CH), :]
            for rb in range(tq // ROW_BLOCK):
                s = _nt(q_ref[e, rb * ROW_BLOCK:(rb + 1) * ROW_BLOCK, :], k)
                if masked:
                    qpos = rb * ROW_BLOCK + lax.broadcasted_iota(I32, (ROW_BLOCK, CH), 0)
                    s = jnp.where(lax.broadcasted_iota(I32, (ROW_BLOCK, CH), 1) <= qpos, s, NEG_INF)
                _flash_update(s, v, m_ref, acc_ref, slice(e * tq + rb * ROW_BLOCK, e * tq + (rb + 1) * ROW_BLOCK))

    _unrolled_loop(i, lambda c: step(c, False))
    step(i, True)
    o = _flash_result(acc_ref)
    lo_half = lax.broadcasted_iota(I32, (tq, LANES), 1) < NOPE_DIM
    slab = jnp.where(lo_half, o[0:tq], o[tq:2 * tq])
    o_ref[...] = (slab * _silu(z_ref[...])).astype(BF16)


def _rope_tables(pos, dim):
    inv = jnp.power(jnp.float32(ROPE_THETA), -jnp.arange(0, dim, 2, dtype=F32) / dim)
    ang = pos.astype(F32)[:, None] * inv[None, :]
    cos, sin = jnp.cos(ang), jnp.sin(ang)
    reps = LANES // dim
    zero = jnp.zeros_like(sin)
    cos_t = jnp.tile(jnp.concatenate([cos, cos], axis=1), (1, reps))
    sa_t = jnp.tile(jnp.concatenate([zero, sin], axis=1), (1, reps))
    sb_t = jnp.tile(jnp.concatenate([-sin, zero], axis=1), (1, reps))
    return cos_t, sa_t, sb_t


def _block_diag(group):
    r = jnp.arange(LANES) // group
    bd = (r[:, None] == r[None, :]).astype(BF16)
    return jnp.concatenate([bd, bd], axis=0)


def _tile_gain(g, width=LANES):
    return jnp.tile(g.astype(F32), width // g.shape[0])


def _pad_lanes(g, width=LANES):
    return jnp.concatenate([g.astype(F32), jnp.zeros((width - g.shape[0],), F32)])


def _pad_cols(w, width):
    return jnp.concatenate([w, jnp.zeros((w.shape[0], width - w.shape[1]), w.dtype)], axis=1)


def _params(*sem, flags=None):
    return pltpu.CompilerParams(dimension_semantics=sem, vmem_limit_bytes=VMEM_LIMIT, flags=flags)


def kernel(x, ab_norm, ab_w_in, a_qk_norm, a_kidx_norm, b_qk_norm, b_sinks, ab_w_out, cd_norm, cd_w_in, c_q_norm, c_k_norm, c_cmp_pe, c_cmp_w1, c_cmp_w2, d_q_lat_norm, d_kv_lat_norm, d_w_uq, d_w_ukv, d_nope_norm, d_rope_norm, cd_w_out):
    bsz, seq, d = x.shape
    n = bsz * seq
    assert d == D_MODEL and seq % CH == 0 and seq >= NSA_WINDOW + TQ and seq // SLC_BLOCK <= LANES
    nq = seq // TQ
    nrow = n // TM
    srow = seq // TM
    xf = x.reshape(n, d)
    pos = jnp.arange(seq)
    cos64, sa64, sb64 = _rope_tables(pos, HEAD_DIM)
    cos32, sa32, sb32 = _rope_tables(pos, ROPE_DIM)
    bd64, bd32 = _block_diag(HEAD_DIM), _block_diag(ROPE_DIM)
    tril = (jnp.arange(CH)[:, None] >= jnp.arange(CH)[None, :]).astype(BF16)

    def row_spec(width, rows=TM):
        return pl.BlockSpec((rows, width), lambda i: (i, 0))

    def const_spec(shape):
        return pl.BlockSpec(shape, lambda i: tuple(0 for _ in shape))

    tab_spec = pl.BlockSpec((TM, LANES), lambda i: (i % srow, 0))
    head_spec = lambda nh: pl.BlockSpec((nh, TM, LANES), lambda i: (0, i, 0))

    wa = ab_w_in[0]
    w_ab = jnp.concatenate([wa[:, 0:1088], _pad_cols(wa[:, 1088:1092], 64), wa[:, 1092:]], axis=1).astype(BF16)
    gains_ab = jnp.stack([_tile_gain(a_qk_norm[0, 0]), _tile_gain(a_qk_norm[0, 1]), _pad_lanes(a_kidx_norm[0]),
                          _tile_gain(b_qk_norm[0, 0]), _tile_gain(b_qk_norm[0, 1])]
                         + [jnp.zeros((LANES,), F32)] * 3)
    sds = jax.ShapeDtypeStruct
    qa, ka, va, qi, ki, wi, za, qb, kb, vb, zb = pl.pallas_call(
        _proj_ab_kernel,
        grid=(nrow,),
        in_specs=[row_spec(d), const_spec((1, d)), const_spec((d, AB_W)), tab_spec, tab_spec, tab_spec,
                  const_spec((2 * LANES, LANES)), const_spec((8, LANES))],
        out_specs=[head_spec(8), row_spec(LANES), row_spec(2 * LANES), head_spec(4), row_spec(LANES), row_spec(LANES),
                   row_spec(512), head_spec(8), row_spec(LANES), row_spec(2 * LANES), row_spec(512)],
        out_shape=[sds((8, n, LANES), BF16), sds((n, LANES), BF16), sds((n, 2 * LANES), BF16),
                   sds((4, n, LANES), BF16), sds((n, LANES), BF16), sds((n, LANES), F32), sds((n, 512), F32),
                   sds((8, n, LANES), BF16), sds((n, LANES), BF16), sds((n, 2 * LANES), BF16), sds((n, 512), F32)],
        compiler_params=_params("parallel"),
    )(xf, ab_norm[0][None, :], w_ab, cos64, sa64, sb64, bd64, gains_ab)

    qt_heads = lambda nh: pl.BlockSpec((nh, TQ, LANES), lambda b, i: (0, b * nq + i, 0))
    qt_rows = lambda width: pl.BlockSpec((TQ, width), lambda b, i: (b * nq + i, 0))
    seq_rows = pl.BlockSpec((seq, LANES), lambda b, i: (b, 0))
    seq_rows_v = pl.BlockSpec((seq, 2 * LANES), lambda b, i: (b, 0))
    att_scratch = [pltpu.VMEM((N_HEADS * TQ, LANES), F32), pltpu.VMEM((N_HEADS * TQ, 2 * LANES), F32)]

    mix_a = pl.pallas_call(
        functools.partial(_dsa_kernel, topk=min(DSA_TOPK, seq // 4)),
        grid=(bsz, nq),
        in_specs=[qt_heads(4), qt_rows(LANES), seq_rows, qt_heads(8), seq_rows, seq_rows_v, qt_rows(512),
                  pl.BlockSpec((CH, CH), lambda b, i: (0, 0))],
        out_specs=qt_rows(512),
        out_shape=sds((n, 512), BF16),
        scratch_shapes=[pltpu.VMEM((seq // CH, CH, TQ), I32), pltpu.VMEM((seq // CH, TQ, CH), BF16),
                        pltpu.VMEM((8, TQ), F32)] + att_scratch,
        compiler_params=_params("parallel", "arbitrary"),
    )(qi, wi, ki, qa, ka, va, za, tril)

    prev_rows = lambda width: pl.BlockSpec((TQ, width), lambda b, i: (b * nq + jnp.maximum(i - 1, 0), 0))
    sink_rows = jnp.broadcast_to(jnp.repeat(b_sinks[0].astype(F32) * LOG2E, TQ)[:, None], (N_HEADS * TQ, LANES))
    mix_b = pl.pallas_call(
        _swa_kernel,
        grid=(bsz, nq),
        in_specs=[qt_heads(8), prev_rows(LANES), qt_rows(LANES), prev_rows(2 * LANES), qt_rows(2 * LANES),
                  pl.BlockSpec((N_HEADS * TQ, LANES), lambda b, i: (0, 0)), qt_rows(512)],
        out_specs=qt_rows(512),
        out_shape=sds((n, 512), BF16),
        compiler_params=_params("parallel", "arbitrary"),
    )(qb, kb, kb, vb, vb, sink_rows, zb)

    def out_proj(xin, m0, m1, w):
        return pl.pallas_call(
            _out_proj_kernel,
            grid=(nrow,),
            in_specs=[row_spec(d), row_spec(512), row_spec(512), const_spec((2 * 512, d))],
            out_specs=row_spec(d),
            out_shape=sds((n, d), F32),
            compiler_params=_params("parallel"),
        )(xin, m0, m1, w.astype(BF16))

    x1 = out_proj(xf, mix_a, mix_b, ab_w_out[0])

    wc = cd_w_in[0]
    w_cd = jnp.concatenate([wc[:, 0:1280], _pad_cols(wc[:, 1280:1304], LANES), wc[:, 1304:2200],
                            _pad_cols(wc[:, 2200:2232], LANES), wc[:, 2232:]], axis=1).astype(BF16)
    gains_cd = jnp.stack([_tile_gain(c_q_norm[0]), _tile_gain(c_k_norm[0, 1]), _tile_gain(c_k_norm[0, 2])]
                         + [jnp.zeros((LANES,), F32)] * 5)
    qc, kc_raw, vc_raw, ks, vs, kw, vw, gates, zc, cq, ckv, kr, zd = pl.pallas_call(
        _proj_cd_kernel,
        grid=(nrow,),
        in_specs=[row_spec(d), const_spec((1, d)), const_spec((d, AB_W)), tab_spec, tab_spec, tab_spec,
                  const_spec((2 * LANES, LANES)), const_spec((8, LANES))],
        out_specs=[head_spec(8), row_spec(LANES), row_spec(LANES), row_spec(LANES), row_spec(2 * LANES),
                   row_spec(LANES), row_spec(2 * LANES), row_spec(LANES), row_spec(512), row_spec(256),
                   row_spec(LANES), row_spec(LANES), row_spec(512)],
        out_shape=[sds((8, n, LANES), BF16), sds((n, LANES), F32), sds((n, LANES), F32), sds((n, LANES), BF16),
                   sds((n, 2 * LANES), BF16), sds((n, LANES), BF16), sds((n, 2 * LANES), BF16), sds((n, LANES), F32),
                   sds((n, 512), F32), sds((n, 256), F32), sds((n, LANES), F32), sds((n, LANES), F32),
                   sds((n, 512), F32)],
        compiler_params=_params("parallel"),
    )(x1, cd_norm[0][None, :], w_cd, cos64, sa64, sb64, bd64, gains_cd)

    ncmp = seq // CMP_STRIDE
    cw = CMP_STRIDE * N_KV * HEAD_DIM
    eye = jnp.eye(N_KV, dtype=F32)
    ccos, csa, csb = _rope_tables(jnp.arange(ncmp) * CMP_STRIDE + (2 * CMP_STRIDE - 1), HEAD_DIM)

    def compress(raw, which, is_key):
        w1 = c_cmp_w1[0, which].reshape(2, CMP_STRIDE, HEAD_DIM, -1)
        hid = w1.shape[-1]
        w1e = jnp.einsum("tjdu,gh->tjgdhu", w1, eye).reshape(2, cw, N_KV * hid).astype(BF16)
        w2e = jnp.einsum("ud,gh->guhd", c_cmp_w2[0, which], eye).reshape(N_KV * hid, N_KV * HEAD_DIM).astype(BF16)
        pe = c_cmp_pe[0, which].reshape(2, CMP_STRIDE, 1, HEAD_DIM)
        pe = jnp.broadcast_to(pe, (2, CMP_STRIDE, N_KV, HEAD_DIM)).reshape(2, cw)
        return pl.pallas_call(
            functools.partial(_compress_kernel, is_key=is_key),
            grid=(bsz,),
            in_specs=[pl.BlockSpec((ncmp, cw), lambda b: (b, 0)), const_spec((2, cw)),
                      const_spec((cw, N_KV * hid)), const_spec((cw, N_KV * hid)),
                      const_spec((N_KV * hid, LANES)), const_spec((1, LANES)),
                      const_spec((ncmp, LANES)), const_spec((ncmp, LANES)), const_spec((ncmp, LANES)),
                      const_spec((2 * LANES, LANES))],
            out_specs=pl.BlockSpec((ncmp, LANES), lambda b: (b, 0)),
            out_shape=sds((bsz * ncmp, LANES), BF16),
            compiler_params=_params("parallel"),
        )(raw.reshape(bsz * ncmp, cw), pe, w1e[0], w1e[1], w2e, _tile_gain(c_k_norm[0, 0])[None, :],
          ccos, csa, csb, bd64)

    kc = compress(kc_raw, 0, True)
    vc = compress(vc_raw, 1, False)

    ratio = SLC_BLOCK // CMP_STRIDE
    mm = jnp.arange(ncmp)[:, None]
    jj = jnp.arange(LANES)[None, :]
    wsel = (jnp.where((mm == ratio * jj - 1) | (mm == ratio * jj + ratio - 1), 1.0, 0.0)
            + jnp.where((mm >= ratio * jj) & (mm < ratio * jj + ratio - 1), 2.0, 0.0))
    wsel_t = jnp.where((mm < ncmp - 1) & (jj < seq // SLC_BLOCK), wsel, 0.0).astype(BF16).T

    cmp_rows = pl.BlockSpec((ncmp, LANES), lambda b, i: (b, 0))
    mix_c = pl.pallas_call(
        functools.partial(_nsa_kernel, n_sel=min(SLC_TOPN, seq // SLC_BLOCK)),
        grid=(bsz, nq),
        in_specs=[qt_heads(8), cmp_rows, cmp_rows, seq_rows, seq_rows_v, seq_rows, seq_rows_v, qt_rows(LANES),
                  qt_rows(512), pl.BlockSpec((LANES, ncmp), lambda b, i: (0, 0))],
        out_specs=qt_rows(512),
        out_shape=sds((n, 512), BF16),
        scratch_shapes=att_scratch,
        compiler_params=_params("parallel", "arbitrary"),
    )(qc, kc, vc, ks, vs, kw, vw, gates, zc, wsel_t)

    wq = d_w_uq[0].reshape(Q_LORA, N_HEADS, NOPE_DIM + ROPE_DIM)
    wq = jnp.concatenate([wq[:, :, :NOPE_DIM].reshape(Q_LORA, -1), wq[:, :, NOPE_DIM:].reshape(Q_LORA, -1)],
                         axis=1).astype(BF16)
    wkv = d_w_ukv[0].reshape(KV_LORA, N_HEADS, NOPE_DIM + HEAD_DIM)
    wkv = jnp.concatenate([wkv[:, :, :NOPE_DIM].reshape(KV_LORA, -1), wkv[:, :, NOPE_DIM:].reshape(KV_LORA, -1)],
                          axis=1).astype(BF16)
    lat_gains = jnp.stack([d_q_lat_norm[0].astype(F32), _pad_lanes(d_kv_lat_norm[0], Q_LORA)]
                          + [jnp.zeros((Q_LORA,), F32)] * 6)
    gains_d = jnp.stack([_tile_gain(d_nope_norm[0, 0]), _tile_gain(d_rope_norm[0, 0]), _tile_gain(d_nope_norm[0, 1]),
                         _pad_lanes(d_rope_norm[0, 1])] + [jnp.zeros((LANES,), F32)] * 4)
    q_cat, k_cat, v_d = pl.pallas_call(
        _mla_prep_kernel,
        grid=(nrow,),
        in_specs=[row_spec(Q_LORA), row_spec(LANES), row_spec(LANES), const_spec((Q_LORA, 768)),
                  const_spec((KV_LORA, 1024)), const_spec((8, Q_LORA)), const_spec((8, LANES)),
                  tab_spec, tab_spec, tab_spec, const_spec((2 * LANES, LANES)), const_spec((2 * LANES, LANES))],
        out_specs=[head_spec(8), head_spec(8), pl.BlockSpec((4, TM, 2 * LANES), lambda i: (0, i, 0))],
        out_shape=[sds((8, n, LANES), BF16), sds((8, n, LANES), BF16), sds((4, n, 2 * LANES), BF16)],
        compiler_params=_params("parallel"),
    )(cq, ckv, kr, wq, wkv, lat_gains, gains_d, cos32, sa32, sb32, bd64, bd32)

    nqm = seq // TQ_MLA
    mix_d = pl.pallas_call(
        _mla_kernel,
        grid=(bsz, N_HEADS // 2, nqm),
        in_specs=[pl.BlockSpec((2, TQ_MLA, LANES), lambda b, hp, i: (hp, b * nqm + i, 0)),
                  pl.BlockSpec((2, seq, LANES), lambda b, hp, i: (hp, b, 0)),
                  pl.BlockSpec((1, seq, 2 * LANES), lambda b, hp, i: (hp, b, 0)),
                  pl.BlockSpec((TQ_MLA, LANES), lambda b, hp, i: (b * nqm + i, hp))],
        out_specs=pl.BlockSpec((TQ_MLA, LANES), lambda b, hp, i: (b * nqm + i, hp)),
        out_shape=sds((n, 512), BF16),
        scratch_shapes=[pltpu.VMEM((2 * TQ_MLA, LANES), F32), pltpu.VMEM((2 * TQ_MLA, 2 * LANES), F32)],
        compiler_params=_params("parallel", "parallel", "arbitrary"),
    )(q_cat, k_cat, v_d, zd)

    x2 = out_proj(x1, mix_c, mix_d, cd_w_out[0])
    return x2.reshape(bsz, seq, d)
```

```python
import functools

import jax
import jax.numpy as jnp
from jax import lax
from jax.experimental import pallas as pl
from jax.experimental.pallas import tpu as pltpu

F32, BF16, I32 = jnp.float32, jnp.bfloat16, jnp.int32

D_MODEL = 1024
HEAD_DIM = 64
N_HEADS = 8
N_KV = 2
GRP = N_HEADS // N_KV
IDX_HEADS = 4
DSA_TOPK = 256
SWA_WINDOW = 128
CMP_STRIDE = 16
SLC_BLOCK = 64
SLC_TOPN = 16
NSA_WINDOW = 512
Q_LORA = 256
KV_LORA = 128
NOPE_DIM = 64
ROPE_DIM = 32
ROPE_THETA = 10000.0
EPS = 1e-6

LANES = 128
TM = 512
TQ = 128
CH = 512
TQ_MLA = 512
UNROLL = 4
COUNT_ROWS = 64
ROW_BLOCK = 128
VMEM_LIMIT = 56 * 1024 * 1024

NEG_INF = float("-inf")
M_INIT = -1e30
INT_MIN = -(2 ** 31)

AB_W = 2944
LOG2E = 1.4426950408889634
Q_SCALE = HEAD_DIM ** -0.5 * LOG2E


def _nn(a, b):
    return jnp.dot(a, b, preferred_element_type=F32)


def _nt(a, b):
    return lax.dot_general(a, b, (((1,), (1,)), ((), ())), preferred_element_type=F32)


def _sigmoid(z):
    return 1.0 / (1.0 + jnp.exp(-z))


def _silu(z):
    return z * _sigmoid(z)


def _row_rms(x, g):
    return x * lax.rsqrt(jnp.mean(x * x, axis=-1, keepdims=True) + EPS) * g


def _seg_norm(s, bd, gain, group):
    s2 = s * s
    hi = s2.astype(BF16)
    lo = (s2 - hi.astype(F32)).astype(BF16)
    ss = _nn(jnp.concatenate([hi, lo], axis=1), bd)
    return s * lax.rsqrt(ss * (1.0 / group) + EPS) * gain


def _rope(s, cos, sa, sb, half):
    return s * cos + pltpu.roll(s, half, 1) * sa + pltpu.roll(s, LANES - half, 1) * sb


def _order_key(v):
    bits = lax.bitcast_convert_type(v, I32)
    return bits ^ ((bits >> 31) & 0x7FFFFFFF)


def _stack_rows(x, n):
    return jnp.concatenate([x] * n, axis=0)


def _with_ones(v):
    return jnp.concatenate([v.astype(BF16), jnp.ones(v.shape, BF16)], axis=1)


def _flash_update(s, v_ext, m_ref, acc_ref, rows):
    m_prev = m_ref[rows]
    sb = s.astype(BF16)
    m_new = jnp.maximum(m_prev, jnp.max(sb, axis=1, keepdims=True).astype(F32))
    alpha = jnp.exp2(m_prev - m_new)
    p = jnp.exp2(sb - jnp.concatenate([m_new.astype(BF16)] * (s.shape[1] // LANES), axis=1))
    acc_ref[rows] = jnp.concatenate([alpha, alpha], axis=1) * acc_ref[rows] + _nn(p, v_ext)
    m_ref[rows] = m_new


def _unrolled_loop(n, step, unroll=UNROLL):
    def body(j, carry):
        for u in range(unroll):
            step(j * unroll + u)
        return carry

    lax.fori_loop(0, n // unroll, body, 0)
    base = (n // unroll) * unroll
    p = unroll // 2
    while p >= 1:
        def tail(base=base, p=p):
            for u in range(p):
                step(base + u)

        pl.when((n & p) != 0)(tail)
        base = base + (n & p)
        p //= 2


def _flash_init(m_ref, acc_ref):
    m_ref[...] = jnp.full(m_ref.shape, M_INIT, F32)
    acc_ref[...] = jnp.zeros(acc_ref.shape, F32)


def _flash_result(acc_ref):
    acc = acc_ref[...]
    return acc[:, 0:LANES] / jnp.maximum(acc[:, LANES:2 * LANES], 1e-30)


def _place_q_heads(y, bd, gain, cos, sa, sb, out_ref, scale):
    lo_half = lax.broadcasted_iota(I32, (y.shape[0], LANES), 1) < HEAD_DIM
    for j in range(4):
        s = y[:, LANES * j:LANES * (j + 1)]
        s = _rope(_seg_norm(s, bd, gain, HEAD_DIM), cos, sa, sb, HEAD_DIM // 2) * scale
        r = pltpu.roll(s, HEAD_DIM, 1)
        if j < 2:
            out_ref[2 * j] = jnp.where(lo_half, s, 0.0).astype(BF16)
            out_ref[2 * j + 1] = jnp.where(lo_half, r, 0.0).astype(BF16)
        else:
            out_ref[2 * j] = jnp.where(lo_half, 0.0, r).astype(BF16)
            out_ref[2 * j + 1] = jnp.where(lo_half, 0.0, s).astype(BF16)


def _gated_store(o, z_ref, o_ref, tq, row0=0):
    lo_half = lax.broadcasted_iota(I32, (tq, LANES), 1) < HEAD_DIM
    for j in range(4):
        a = o[(2 * j) * tq:(2 * j + 1) * tq]
        b = o[(2 * j + 1) * tq:(2 * j + 2) * tq]
        if j < 2:
            slab = jnp.where(lo_half, a, pltpu.roll(b, HEAD_DIM, 1))
        else:
            slab = jnp.where(lo_half, pltpu.roll(a, HEAD_DIM, 1), b)
        z = z_ref[row0:row0 + tq, LANES * j:LANES * (j + 1)]
        o_ref[row0:row0 + tq, LANES * j:LANES * (j + 1)] = (slab * _silu(z)).astype(BF16)


def _proj_ab_kernel(x_ref, g_ref, w_ref, cos_ref, sa_ref, sb_ref, bd_ref, gains_ref,
                    qa_ref, ka_ref, va_ref, qi_ref, ki_ref, wi_ref, za_ref,
                    qb_ref, kb_ref, vb_ref, zb_ref):
    xn = _row_rms(x_ref[...], g_ref[...]).astype(BF16)
    cos, sa, sb, bd = cos_ref[...], sa_ref[...], sb_ref[...], bd_ref[...]
    lo_half = lax.broadcasted_iota(I32, (TM, LANES), 1) < HEAD_DIM

    def proj(c0, n):
        return _nn(xn, w_ref[:, c0:c0 + n])

    def k_slab(c0, gain):
        return _rope(_seg_norm(proj(c0, LANES), bd, gain, HEAD_DIM), cos, sa, sb, HEAD_DIM // 2)

    _place_q_heads(proj(0, 512), bd, gains_ref[0:1, :], cos, sa, sb, qa_ref, Q_SCALE)
    ka_ref[...] = k_slab(512, gains_ref[1:2, :]).astype(BF16)
    va_ref[...] = _with_ones(proj(640, LANES))
    qi = proj(768, 256)
    for j in range(2):
        s = _rope(qi[:, LANES * j:LANES * (j + 1)], cos, sa, sb, HEAD_DIM // 2)
        qi_ref[2 * j] = jnp.where(lo_half, s, 0.0).astype(BF16)
        qi_ref[2 * j + 1] = jnp.where(lo_half, pltpu.roll(s, HEAD_DIM, 1), 0.0).astype(BF16)
    kiw = proj(1024, LANES)
    wi_ref[...] = kiw
    ki_ref[...] = _rope(_seg_norm(kiw, bd, gains_ref[2:3, :], HEAD_DIM), cos, sa, sb, HEAD_DIM // 2).astype(BF16)
    za_ref[...] = proj(1152, 512)
    _place_q_heads(proj(1664, 512), bd, gains_ref[3:4, :], cos, sa, sb, qb_ref, Q_SCALE)
    kb_ref[...] = k_slab(2176, gains_ref[4:5, :]).astype(BF16)
    vb_ref[...] = _with_ones(proj(2304, LANES))
    zb_ref[...] = proj(2432, 512)


def _count_chunks(keys_ref, nch, nq, pred):
    def body(c, acc):
        return acc + jnp.sum(pred(keys_ref[c]).reshape(CH // COUNT_ROWS, COUNT_ROWS, nq), axis=0)

    acc = lax.fori_loop(0, nch, body, jnp.zeros((COUNT_ROWS, nq), F32))
    return jnp.sum(acc, axis=0, keepdims=True)


def _kth_threshold(count, nq, k):
    def bit_body(it, carry):
        tb, above = carry
        cand_b = tb | jnp.left_shift(jnp.int32(1), 31 - it)
        cand_s = cand_b ^ INT_MIN
        cnt = count(lambda kk: jnp.where(kk >= cand_s, 1.0, 0.0))
        ok = cnt >= k
        return jnp.where(ok, cand_b, tb), jnp.where(ok, above, cnt)

    tb, above = lax.fori_loop(0, 32, bit_body, (jnp.zeros((1, nq), I32), jnp.zeros((1, nq), F32)))
    return tb ^ INT_MIN, k - above


def _dsa_kernel(qi_ref, wi_ref, ki_ref, qa_ref, ka_ref, va_ref, za_ref, tril_ref, o_ref,
                keys_ref, bias_ref, ties_ref, m_ref, acc_ref, *, topk):
    i = pl.program_id(1)
    t0 = i * TQ
    nch = (t0 + TQ + CH - 1) // CH
    key_c = lax.broadcasted_iota(I32, (CH, TQ), 0)
    qry_c = t0 + lax.broadcasted_iota(I32, (CH, TQ), 1)

    qi = qi_ref[...].reshape(IDX_HEADS * TQ, LANES)
    w_t = wi_ref[...].T * ((HEAD_DIM * IDX_HEADS) ** -0.5)
    ws = [w_t[HEAD_DIM + h:HEAD_DIM + h + 1, :] for h in range(IDX_HEADS)]

    def score_chunk(c):
        off = pl.multiple_of(c * CH, CH)
        lg = _nt(ki_ref[pl.ds(off, CH), :], qi)
        sc = ws[0] * jnp.maximum(lg[:, 0:TQ], 0.0)
        for h in range(1, IDX_HEADS):
            sc = sc + ws[h] * jnp.maximum(lg[:, h * TQ:(h + 1) * TQ], 0.0)
        sc = jnp.where(sc == 0.0, 0.0, sc)
        keys_ref[c] = jnp.where(key_c + off <= qry_c, _order_key(sc), INT_MIN)

    _unrolled_loop(nch, score_chunk, unroll=2)

    count = functools.partial(_count_chunks, keys_ref, nch, TQ)
    thr, need = _kth_threshold(count, TQ, float(topk))

    tril = tril_ref[...]

    ties_ref[...] = jnp.zeros(ties_ref.shape, F32)

    def select_chunk(c):
        kk = keys_ref[c]
        pref = _nn(tril, jnp.where(kk == thr, 1.0, 0.0).astype(BF16)) + ties_ref[0:1, :]
        ties_ref[0:1, :] = pref[CH - 1:CH, :]
        tie = jnp.where(kk == thr, jnp.where(pref <= need, 0.0, NEG_INF), NEG_INF)
        bias = jnp.where(key_c + c * CH <= qry_c, jnp.where(kk > thr, 0.0, tie), NEG_INF)
        bias_ref[c] = bias.T.astype(BF16)

    _unrolled_loop(nch, select_chunk)

    _flash_init(m_ref, acc_ref)
    q_all = qa_ref[...].reshape(N_HEADS * TQ, LANES)

    def att_chunk(c):
        off = pl.multiple_of(c * CH, CH)
        s = _nt(q_all, ka_ref[pl.ds(off, CH), :]).astype(BF16) + _stack_rows(bias_ref[c], N_HEADS)
        _flash_update(s, va_ref[pl.ds(off, CH), :], m_ref, acc_ref, slice(None))

    _unrolled_loop(nch, att_chunk)
    _gated_store(_flash_result(acc_ref), za_ref, o_ref, TQ)


def _swa_kernel(q_ref, kp_ref, kc_ref, vp_ref, vc_ref, sink_ref, z_ref, o_ref):
    i = pl.program_id(1)
    qi = lax.broadcasted_iota(I32, (TQ, 2 * TQ), 0)
    kj = lax.broadcasted_iota(I32, (TQ, 2 * TQ), 1)
    sink = sink_ref[...]
    for half in range(2):
        q = q_ref[:, half * TQ:(half + 1) * TQ, :].reshape(N_HEADS * TQ, LANES)
        if half == 0:
            k2 = jnp.concatenate([kp_ref[...], kc_ref[0:TQ, :]], axis=0)
            v2 = jnp.concatenate([vp_ref[...], vc_ref[0:TQ, :]], axis=0)
            lo = jnp.where(i > 0, 0, TQ)
        else:
            k2, v2, lo = kc_ref[...], vc_ref[...], 0
        keep = jnp.where(kj >= lo, jnp.where(kj > qi, jnp.where(kj <= qi + TQ, 0.0, NEG_INF), NEG_INF), NEG_INF)
        s = _nt(q, k2) + _stack_rows(keep, N_HEADS)
        m = jnp.maximum(jnp.max(s, axis=1, keepdims=True), sink)
        p = jnp.exp2(s - jnp.concatenate([m, m], axis=1))
        o = _nn(p.astype(BF16), v2)
        den = o[:, LANES:2 * LANES] + jnp.exp2(sink - m)
        _gated_store(o[:, 0:LANES] / den, z_ref, o_ref, TQ, half * TQ)


def _out_proj_kernel(x_ref, ma_ref, mb_ref, w_ref, o_ref):
    half = ma_ref.shape[1]
    o_ref[...] = x_ref[...] + _nn(ma_ref[...], w_ref[0:half, :]) + _nn(mb_ref[...], w_ref[half:2 * half, :])


def _proj_cd_kernel(x_ref, g_ref, w_ref, cos_ref, sa_ref, sb_ref, bd_ref, gains_ref,
                    qc_ref, kc_ref, vc_ref, ks_ref, vs_ref, kw_ref, vw_ref, gt_ref, zc_ref,
                    cq_ref, ckv_ref, kr_ref, zd_ref):
    xn = _row_rms(x_ref[...], g_ref[...]).astype(BF16)
    cos, sa, sb, bd = cos_ref[...], sa_ref[...], sb_ref[...], bd_ref[...]

    def proj(c0, n):
        return _nn(xn, w_ref[:, c0:c0 + n])

    def k_slab(c0, gain):
        return _rope(_seg_norm(proj(c0, LANES), bd, gain, HEAD_DIM), cos, sa, sb, HEAD_DIM // 2)

    _place_q_heads(proj(0, 512), bd, gains_ref[0:1, :], cos, sa, sb, qc_ref, Q_SCALE)
    kc_ref[...] = proj(512, LANES)
    vc_ref[...] = proj(640, LANES)
    ks_ref[...] = k_slab(768, gains_ref[1:2, :]).astype(BF16)
    vs_ref[...] = _with_ones(proj(896, LANES))
    kw_ref[...] = k_slab(1024, gains_ref[2:3, :]).astype(BF16)
    vw_ref[...] = _with_ones(proj(1152, LANES))
    gt_ref[...] = _sigmoid(proj(1280, LANES))
    zc_ref[...] = proj(1408, 512)
    cq_ref[...] = proj(1920, 256)
    ckv_ref[...] = proj(2176, LANES)
    kr_ref[...] = proj(2304, LANES)
    zd_ref[...] = proj(2432, 512)


def _compress_kernel(c_ref, pe_ref, w1a_ref, w1b_ref, w2_ref, gain_ref, cos_ref, sa_ref, sb_ref, bd_ref,
                     o_ref, *, is_key):
    c = c_ref[...]
    n = c.shape[0]
    a = _nn((c + pe_ref[0:1, :]).astype(BF16), w1a_ref[...])
    b = _nn((c + pe_ref[1:2, :]).astype(BF16), w1b_ref[...])
    h = _silu(a + pltpu.roll(b, n - 1, 0))
    y = _nn(h.astype(BF16), w2_ref[...])
    if is_key:
        y = _rope(_seg_norm(y, bd_ref[...], gain_ref[...], HEAD_DIM), cos_ref[...], sa_ref[...], sb_ref[...],
                  HEAD_DIM // 2)
    o_ref[...] = y.astype(BF16)


def _nsa_kernel(q_ref, kc_ref, vc_ref, ks_ref, vs_ref, kw_ref, vw_ref, gt_ref, z_ref, wsel_ref, o_ref,
                m_ref, acc_ref, *, n_sel):
    i = pl.program_id(1)
    t0 = i * TQ
    nch = (t0 + TQ + CH - 1) // CH
    ncmp = kc_ref.shape[0]
    rows_all = N_HEADS * TQ
    q = q_ref[...].reshape(rows_all, LANES)

    cend = lax.broadcasted_iota(I32, (TQ, ncmp), 1) * CMP_STRIDE + (2 * CMP_STRIDE - 1)
    bias_c = jnp.where(cend <= t0 + lax.broadcasted_iota(I32, (TQ, ncmp), 0), 0.0, NEG_INF)
    s = _nt(q, kc_ref[...]) + _stack_rows(bias_c, N_HEADS)
    m = jnp.max(s, axis=1, keepdims=True)
    m = jnp.where(m > NEG_INF, m, 0.0)
    p = jnp.exp2(s - m)
    p = p / jnp.maximum(jnp.sum(p, axis=1, keepdims=True), 1e-30)
    o_cmp = _nn(p.astype(BF16), vc_ref[...])

    span = NSA_WINDOW + TQ
    start = pl.multiple_of(jnp.maximum(t0 - NSA_WINDOW, 0), TQ)
    diff = t0 + lax.broadcasted_iota(I32, (TQ, span), 0) - (start + lax.broadcasted_iota(I32, (TQ, span), 1))
    bias_w = jnp.where(diff >= 0, jnp.where(diff < NSA_WINDOW, 0.0, NEG_INF), NEG_INF)
    sw = _nt(q, kw_ref[pl.ds(start, span), :]).astype(BF16) + _stack_rows(bias_w.astype(BF16), N_HEADS)
    pw = jnp.exp2(sw - jnp.max(sw, axis=1, keepdims=True))
    o_win = _nn(pw, vw_ref[pl.ds(start, span), :])
    o_win = o_win[:, 0:LANES] / o_win[:, LANES:2 * LANES]

    blk = lax.broadcasted_iota(I32, (LANES, TQ), 0)
    cur = (t0 + lax.broadcasted_iota(I32, (LANES, TQ), 1)) >> 6
    wsel_t = wsel_ref[...]
    keys = []
    for g in range(N_KV):
        imp = p[(GRP * g) * TQ:(GRP * g + 1) * TQ]
        for r in range(1, GRP):
            imp = imp + p[(GRP * g + r) * TQ:(GRP * g + r + 1) * TQ]
        hi = imp.astype(BF16)
        lo = (imp - hi.astype(F32)).astype(BF16)
        imp_s = _nt(wsel_t, hi) + _nt(wsel_t, lo)
        forced = jnp.where(blk == 0, jnp.inf, jnp.where(blk >= cur - 1, jnp.inf, imp_s))
        keys.append(_order_key(jnp.where(blk <= cur, forced, NEG_INF)))
    kk = jnp.concatenate(keys, axis=1)

    def bit_body(it, tb):
        cand_b = tb | jnp.left_shift(jnp.int32(1), 31 - it)
        cnt = jnp.sum(jnp.where(kk >= (cand_b ^ INT_MIN), 1.0, 0.0), axis=0, keepdims=True)
        return jnp.where(cnt >= n_sel, cand_b, tb)

    thr = lax.fori_loop(0, 32, bit_body, jnp.zeros((1, N_KV * TQ), I32)) ^ INT_MIN
    need = n_sel - jnp.sum(jnp.where(kk > thr, 1.0, 0.0), axis=0, keepdims=True)
    tril = jnp.where(lax.broadcasted_iota(I32, (LANES, LANES), 0) >= lax.broadcasted_iota(I32, (LANES, LANES), 1),
                     1.0, 0.0).astype(BF16)
    pref = _nn(tril, jnp.where(kk == thr, 1.0, 0.0).astype(BF16))
    tie = jnp.where(kk == thr, jnp.where(pref <= need, 1.0, 0.0), 0.0)
    sel_t = jnp.where(kk > thr, 1.0, tie)
    sel = jnp.concatenate([sel_t[:, g * TQ:(g + 1) * TQ].T for g in range(N_KV)], axis=0).astype(BF16)

    _flash_init(m_ref, acc_ref)
    blk_row =lax.broadcasted_iota(I32, (LANES, CH), 0)
    blk_col = lax.broadcasted_iota(I32, (LANES, CH), 1) >> 6
    row_c = t0 + lax.broadcasted_iota(I32, (TQ, CH), 0)
    col_c = lax.broadcasted_iota(I32, (TQ, CH), 1)

    def slc_chunk(c):
        off = pl.multiple_of(c * CH, CH)
        expand = jnp.where(blk_row == blk_col + c * (CH // SLC_BLOCK), 1.0, 0.0).astype(BF16)
        tok = _nn(sel, expand)
        causal = col_c + off <= row_c
        parts = []
        for g in range(N_KV):
            bias_g = jnp.where(causal, jnp.where(tok[g * TQ:(g + 1) * TQ] > 0.5, 0.0, NEG_INF), NEG_INF)
            parts.extend([bias_g.astype(BF16)] * GRP)
        sc = _nt(q, ks_ref[pl.ds(off, CH), :]).astype(BF16) + jnp.concatenate(parts, axis=0)
        _flash_update(sc, vs_ref[pl.ds(off, CH), :], m_ref, acc_ref, slice(None))

    _unrolled_loop(nch, slc_chunk)
    o_slc = _flash_result(acc_ref)

    gates = gt_ref[...]
    outs = []
    for h in range(N_HEADS):
        rs = slice(h * TQ, (h + 1) * TQ)
        outs.append(gates[:, 3 * h:3 * h + 1] * o_cmp[rs] + gates[:, 3 * h + 1:3 * h + 2] * o_slc[rs]
                    + gates[:, 3 * h + 2:3 * h + 3] * o_win[rs])
    _gated_store(jnp.concatenate(outs, axis=0), z_ref, o_ref, TQ)


def _mla_prep_kernel(cq_ref, ckv_ref, kr_ref, wq_ref, wkv_ref, lat_ref, gains_ref, cos_ref, sa_ref, sb_ref,
                     bd64_ref, bd32_ref, q_ref, k_ref, v_ref):
    lane = lax.broadcasted_iota(I32, (TM, LANES), 1)
    lo_half = lane < NOPE_DIM
    cos, sa, sb = cos_ref[...], sa_ref[...], sb_ref[...]
    bd64, bd32 = bd64_ref[...], bd32_ref[...]
    scale = (NOPE_DIM + ROPE_DIM) ** -0.5 * LOG2E

    def rope_slab(s, gain):
        return _rope(_seg_norm(s, bd32, gain, ROPE_DIM), cos, sa, sb, ROPE_DIM // 2)

    q = _nn(_row_rms(cq_ref[...], lat_ref[0:1, :]).astype(BF16), wq_ref[...])
    kv = _nn(_row_rms(ckv_ref[...], lat_ref[1:2, 0:KV_LORA]).astype(BF16), wkv_ref[...])
    k_rope = pltpu.roll(rope_slab(kr_ref[...], gains_ref[3:4, :]), NOPE_DIM, 1)
    q_rope = [rope_slab(q[:, 512 + LANES * j:512 + LANES * (j + 1)], gains_ref[1:2, :]) * scale for j in range(2)]
    for j in range(4):
        qn = _seg_norm(q[:, LANES * j:LANES * (j + 1)], bd64, gains_ref[0:1, :], NOPE_DIM) * scale
        kn = _seg_norm(kv[:, LANES * j:LANES * (j + 1)], bd64, gains_ref[2:3, :], NOPE_DIM)
        for e in range(2):
            h = 2 * j + e
            qn_h = qn if e == 0 else pltpu.roll(qn, NOPE_DIM, 1)
            kn_h = kn if e == 0 else pltpu.roll(kn, NOPE_DIM, 1)
            shift = (NOPE_DIM - ROPE_DIM * (h % 4)) % LANES
            qr = q_rope[h // 4]
            qr_h = qr if shift == 0 else pltpu.roll(qr, shift, 1)
            q_ref[h] = jnp.where(lo_half, qn_h, jnp.where(lane < NOPE_DIM + ROPE_DIM, qr_h, 0.0)).astype(BF16)
            k_ref[h] = jnp.where(lo_half, kn_h, k_rope).astype(BF16)
    for j in range(4):
        v_ref[j] = _with_ones(kv[:, 512 + LANES * j:512 + LANES * (j + 1)])


def _mla_kernel(q_ref, k_ref, v_ref, z_ref, o_ref, m_ref, acc_ref):
    i = pl.program_id(2)
    tq = TQ_MLA
    _flash_init(m_ref, acc_ref)

    def step(c, masked):
        off = pl.multiple_of(c * CH, CH)
        v = v_ref[0, pl.ds(off, CH), :]
        for e in range(2):
            k = k_ref[e, pl.ds(off, CH), :]
            for rb in range(tq // ROW_BLOCK):
                s = _nt(q_ref[e, rb * ROW_BLOCK:(rb + 1) * ROW_BLOCK, :], k)
                if masked:
                    qpos = rb * ROW_BLOCK + lax.broadcasted_iota(I32, (ROW_BLOCK, CH), 0)
                    s = jnp.where(lax.broadcasted_iota(I32, (ROW_BLOCK, CH), 1) <= qpos, s, NEG_INF)
                _flash_update(s, v, m_ref, acc_ref, slice(e * tq + rb * ROW_BLOCK, e * tq + (rb + 1) * ROW_BLOCK))

    _unrolled_loop(i, lambda c: step(c, False))
    step(i, True)
    o = _flash_result(acc_ref)
    lo_half = lax.broadcasted_iota(I32, (tq, LANES), 1) < NOPE_DIM
    slab = jnp.where(lo_half, o[0:tq], o[tq:2 * tq])
    o_ref[...] = (slab * _silu(z_ref[...])).astype(BF16)


def _rope_tables(pos, dim):
    inv = jnp.power(jnp.float32(ROPE_THETA), -jnp.arange(0, dim, 2, dtype=F32) / dim)
    ang = pos.astype(F32)[:, None] * inv[None, :]
    cos, sin = jnp.cos(ang), jnp.sin(ang)
    reps = LANES // dim
    zero = jnp.zeros_like(sin)
    cos_t = jnp.tile(jnp.concatenate([cos, cos], axis=1), (1, reps))
    sa_t = jnp.tile(jnp.concatenate([zero, sin], axis=1), (1, reps))
    sb_t = jnp.tile(jnp.concatenate([-sin, zero], axis=1), (1, reps))
    return cos_t, sa_t, sb_t


def _block_diag(group):
    r = jnp.arange(LANES) // group
    bd = (r[:, None] == r[None, :]).astype(BF16)
    return jnp.concatenate([bd, bd], axis=0)


def _tile_gain(g, width=LANES):
    return jnp.tile(g.astype(F32), width // g.shape[0])


def _pad_lanes(g, width=LANES):
    return jnp.concatenate([g.astype(F32), jnp.zeros((width - g.shape[0],), F32)])


def _pad_cols(w, width):
    return jnp.concatenate([w, jnp.zeros((w.shape[0], width - w.shape[1]), w.dtype)], axis=1)


def _params(*sem, flags=None):
    return pltpu.CompilerParams(dimension_semantics=sem, vmem_limit_bytes=VMEM_LIMIT, flags=flags)


def kernel(x, ab_norm, ab_w_in, a_qk_norm, a_kidx_norm, b_qk_norm, b_sinks, ab_w_out, cd_norm, cd_w_in, c_q_norm, c_k_norm, c_cmp_pe, c_cmp_w1, c_cmp_w2, d_q_lat_norm, d_kv_lat_norm, d_w_uq, d_w_ukv, d_nope_norm, d_rope_norm, cd_w_out):
    bsz, seq, d = x.shape
    n = bsz * seq
    assert d == D_MODEL and seq % CH == 0 and seq >= NSA_WINDOW + TQ and seq // SLC_BLOCK <= LANES
    nq = seq // TQ
    nrow = n // TM
    srow = seq // TM
    xf = x.reshape(n, d)
    pos = jnp.arange(seq)
    cos64, sa64, sb64 = _rope_tables(pos, HEAD_DIM)
    cos32, sa32, sb32 = _rope_tables(pos, ROPE_DIM)
    bd64, bd32 = _block_diag(HEAD_DIM), _block_diag(ROPE_DIM)
    tril = (jnp.arange(CH)[:, None] >= jnp.arange(CH)[None, :]).astype(BF16)

    def row_spec(width, rows=TM):
        return pl.BlockSpec((rows, width), lambda i: (i, 0))

    def const_spec(shape):
        return pl.BlockSpec(shape, lambda i: tuple(0 for _ in shape))

    tab_spec = pl.BlockSpec((TM, LANES), lambda i: (i % srow, 0))
    head_spec = lambda nh: pl.BlockSpec((nh, TM, LANES), lambda i: (0, i, 0))

    wa = ab_w_in[0]
    w_ab = jnp.concatenate([wa[:, 0:1088], _pad_cols(wa[:, 1088:1092], 64), wa[:, 1092:]], axis=1).astype(BF16)
    gains_ab = jnp.stack([_tile_gain(a_qk_norm[0, 0]), _tile_gain(a_qk_norm[0, 1]), _pad_lanes(a_kidx_norm[0]),
                          _tile_gain(b_qk_norm[0, 0]), _tile_gain(b_qk_norm[0, 1])]
                         + [jnp.zeros((LANES,), F32)] * 3)
    sds = jax.ShapeDtypeStruct
    qa, ka, va, qi, ki, wi, za, qb, kb, vb, zb = pl.pallas_call(
        _proj_ab_kernel,
        grid=(nrow,),
        in_specs=[row_spec(d), const_spec((1, d)), const_spec((d, AB_W)), tab_spec, tab_spec, tab_spec,
                  const_spec((2 * LANES, LANES)), const_spec((8, LANES))],
        out_specs=[head_spec(8), row_spec(LANES), row_spec(2 * LANES), head_spec(4), row_spec(LANES), row_spec(LANES),
                   row_spec(512), head_spec(8), row_spec(LANES), row_spec(2 * LANES), row_spec(512)],
        out_shape=[sds((8, n, LANES), BF16), sds((n, LANES), BF16), sds((n, 2 * LANES), BF16),
                   sds((4, n, LANES), BF16), sds((n, LANES), BF16), sds((n, LANES), F32), sds((n, 512), F32),
                   sds((8, n, LANES), BF16), sds((n, LANES), BF16), sds((n, 2 * LANES), BF16), sds((n, 512), F32)],
        compiler_params=_params("parallel"),
    )(xf, ab_norm[0][None, :], w_ab, cos64, sa64, sb64, bd64, gains_ab)

    qt_heads = lambda nh: pl.BlockSpec((nh, TQ, LANES), lambda b, i: (0, b * nq + i, 0))
    qt_rows = lambda width: pl.BlockSpec((TQ, width), lambda b, i: (b * nq + i, 0))
    seq_rows = pl.BlockSpec((seq, LANES), lambda b, i: (b, 0))
    seq_rows_v = pl.BlockSpec((seq, 2 * LANES), lambda b, i: (b, 0))
    att_scratch = [pltpu.VMEM((N_HEADS * TQ, LANES), F32), pltpu.VMEM((N_HEADS * TQ, 2 * LANES), F32)]

    mix_a = pl.pallas_call(
        functools.partial(_dsa_kernel, topk=min(DSA_TOPK, seq // 4)),
        grid=(bsz, nq),
        in_specs=[qt_heads(4), qt_rows(LANES), seq_rows, qt_heads(8), seq_rows, seq_rows_v, qt_rows(512),
                  pl.BlockSpec((CH, CH), lambda b, i: (0, 0))],
        out_specs=qt_rows(512),
        out_shape=sds((n, 512), BF16),
        scratch_shapes=[pltpu.VMEM((seq // CH, CH, TQ), I32), pltpu.VMEM((seq // CH, TQ, CH), BF16),
                        pltpu.VMEM((8, TQ), F32)] + att_scratch,
        compiler_params=_params("parallel", "arbitrary"),
    )(qi, wi, ki, qa, ka, va, za, tril)

    nq2 = nq // 2
    prev_rows = lambda width: pl.BlockSpec((TQ, width), lambda b, i: (b * nq + jnp.maximum(2 * i - 1, 0), 0))
    pair_rows = lambda width: pl.BlockSpec((2 * TQ, width), lambda b, i: (b * nq2 + i, 0))
    sink_rows = jnp.broadcast_to(jnp.repeat(b_sinks[0].astype(F32) * LOG2E, TQ)[:, None], (N_HEADS * TQ, LANES))
    mix_b = pl.pallas_call(
        _swa_kernel,
        grid=(bsz, nq2),
        in_specs=[pl.BlockSpec((N_HEADS, 2 * TQ, LANES), lambda b, i: (0, b * nq2 + i, 0)),
                  prev_rows(LANES), pair_rows(LANES), prev_rows(2 * LANES), pair_rows(2 * LANES),
                  pl.BlockSpec((N_HEADS * TQ, LANES), lambda b, i: (0, 0)), pair_rows(512)],
        out_specs=pair_rows(512),
        out_shape=sds((n, 512), BF16),
        compiler_params=_params("parallel", "arbitrary"),
    )(qb, kb, kb, vb, vb, sink_rows, zb)

    def out_proj(xin, m0, m1, w):
        return pl.pallas_call(
            _out_proj_kernel,
            grid=(nrow,),
            in_specs=[row_spec(d), row_spec(512), row_spec(512), const_spec((2 * 512, d))],
            out_specs=row_spec(d),
            out_shape=sds((n, d), F32),
            compiler_params=_params("parallel"),
        )(xin, m0, m1, w.astype(BF16))

    x1 = out_proj(xf, mix_a, mix_b, ab_w_out[0])

    wc = cd_w_in[0]
    w_cd = jnp.concatenate([wc[:, 0:1280], _pad_cols(wc[:, 1280:1304], LANES), wc[:, 1304:2200],
                            _pad_cols(wc[:, 2200:2232], LANES), wc[:, 2232:]], axis=1).astype(BF16)
    gains_cd = jnp.stack([_tile_gain(c_q_norm[0]), _tile_gain(c_k_norm[0, 1]), _tile_gain(c_k_norm[0, 2])]
                         + [jnp.zeros((LANES,), F32)] * 5)
    qc, kc_raw, vc_raw, ks, vs, kw, vw, gates, zc, cq, ckv, kr, zd = pl.pallas_call(
        _proj_cd_kernel,
        grid=(nrow,),
        in_specs=[row_spec(d), const_spec((1, d)), const_spec((d, AB_W)), tab_spec, tab_spec, tab_spec,
                  const_spec((2 * LANES, LANES)), const_spec((8, LANES))],
        out_specs=[head_spec(8), row_spec(LANES), row_spec(LANES), row_spec(LANES), row_spec(2 * LANES),
                   row_spec(LANES), row_spec(2 * LANES), row_spec(LANES), row_spec(512), row_spec(256),
                   row_spec(LANES), row_spec(LANES), row_spec(512)],
        out_shape=[sds((8, n, LANES), BF16), sds((n, LANES), F32), sds((n, LANES), F32), sds((n, LANES), BF16),
                   sds((n, 2 * LANES), BF16), sds((n, LANES), BF16), sds((n, 2 * LANES), BF16), sds((n, LANES), F32),
                   sds((n, 512), F32), sds((n, 256), F32), sds((n, LANES), F32), sds((n, LANES), F32),
                   sds((n, 512), F32)],
        compiler_params=_params("parallel"),
    )(x1, cd_norm[0][None, :], w_cd, cos64, sa64, sb64, bd64, gains_cd)

    ncmp = seq // CMP_STRIDE
    cw = CMP_STRIDE * N_KV * HEAD_DIM
    eye = jnp.eye(N_KV, dtype=F32)
    ccos, csa, csb = _rope_tables(jnp.arange(ncmp) * CMP_STRIDE + (2 * CMP_STRIDE - 1), HEAD_DIM)

    def compress(raw, which, is_key):
        w1 = c_cmp_w1[0, which].reshape(2, CMP_STRIDE, HEAD_DIM, -1)
        hid = w1.shape[-1]
        w1e = jnp.einsum("tjdu,gh->tjgdhu", w1, eye).reshape(2, cw, N_KV * hid).astype(BF16)
        w2e = jnp.einsum("ud,gh->guhd", c_cmp_w2[0, which], eye).reshape(N_KV * hid, N_KV * HEAD_DIM).astype(BF16)
        pe = c_cmp_pe[0, which].reshape(2, CMP_STRIDE, 1, HEAD_DIM)
        pe = jnp.broadcast_to(pe, (2, CMP_STRIDE, N_KV, HEAD_DIM)).reshape(2, cw)
        return pl.pallas_call(
            functools.partial(_compress_kernel, is_key=is_key),
            grid=(bsz,),
            in_specs=[pl.BlockSpec((ncmp, cw), lambda b: (b, 0)), const_spec((2, cw)),
                      const_spec((cw, N_KV * hid)), const_spec((cw, N_KV * hid)),
                      const_spec((N_KV * hid, LANES)), const_spec((1, LANES)),
                      const_spec((ncmp, LANES)), const_spec((ncmp, LANES)), const_spec((ncmp, LANES)),
                      const_spec((2 * LANES, LANES))],
            out_specs=pl.BlockSpec((ncmp, LANES), lambda b: (b, 0)),
            out_shape=sds((bsz * ncmp, LANES), BF16),
            compiler_params=_params("parallel"),
        )(raw.reshape(bsz * ncmp, cw), pe, w1e[0], w1e[1], w2e, _tile_gain(c_k_norm[0, 0])[None, :],
          ccos, csa, csb, bd64)

    kc = compress(kc_raw, 0, True)
    vc = compress(vc_raw, 1, False)

    ratio = SLC_BLOCK // CMP_STRIDE
    mm = jnp.arange(ncmp)[:, None]
    jj = jnp.arange(LANES)[None, :]
    wsel = (jnp.where((mm == ratio * jj - 1) | (mm == ratio * jj + ratio - 1), 1.0, 0.0)
            + jnp.where((mm >= ratio * jj) & (mm < ratio * jj + ratio - 1), 2.0, 0.0))
    wsel_t = jnp.where((mm < ncmp - 1) & (jj < seq // SLC_BLOCK), wsel, 0.0).astype(BF16).T

    cmp_rows = pl.BlockSpec((ncmp, LANES), lambda b, i: (b, 0))
    mix_c = pl.pallas_call(
        functools.partial(_nsa_kernel, n_sel=min(SLC_TOPN, seq // SLC_BLOCK)),
        grid=(bsz, nq),
        in_specs=[qt_heads(8), cmp_rows, cmp_rows, seq_rows, seq_rows_v, seq_rows, seq_rows_v, qt_rows(LANES),
                  qt_rows(512), pl.BlockSpec((LANES, ncmp), lambda b, i: (0, 0))],
        out_specs=qt_rows(512),
        out_shape=sds((n, 512), BF16),
        scratch_shapes=att_scratch,
        compiler_params=_params("parallel", "arbitrary"),
    )(qc, kc, vc, ks, vs, kw, vw, gates, zc, wsel_t)

    wq = d_w_uq[0].reshape(Q_LORA, N_HEADS, NOPE_DIM + ROPE_DIM)
    wq = jnp.concatenate([wq[:, :, :NOPE_DIM].reshape(Q_LORA, -1), wq[:, :, NOPE_DIM:].reshape(Q_LORA, -1)],
                         axis=1).astype(BF16)
    wkv = d_w_ukv[0].reshape(KV_LORA, N_HEADS, NOPE_DIM + HEAD_DIM)
    wkv = jnp.concatenate([wkv[:, :, :NOPE_DIM].reshape(KV_LORA, -1), wkv[:, :, NOPE_DIM:].reshape(KV_LORA, -1)],
                          axis=1).astype(BF16)
    lat_gains = jnp.stack([d_q_lat_norm[0].astype(F32), _pad_lanes(d_kv_lat_norm[0], Q_LORA)]
                          + [jnp.zeros((Q_LORA,), F32)] * 6)
    gains_d = jnp.stack([_tile_gain(d_nope_norm[0, 0]), _tile_gain(d_rope_norm[0, 0]), _tile_gain(d_nope_norm[0, 1]),
                         _pad_lanes(d_rope_norm[0, 1])] + [jnp.zeros((LANES,), F32)] * 4)
    q_cat, k_cat, v_d = pl.pallas_call(
        _mla_prep_kernel,
        grid=(nrow,),
        in_specs=[row_spec(Q_LORA), row_spec(LANES), row_spec(LANES), const_spec((Q_LORA, 768)),
                  const_spec((KV_LORA, 1024)), const_spec((8, Q_LORA)), const_spec((8, LANES)),
                  tab_spec, tab_spec, tab_spec, const_spec((2 * LANES, LANES)), const_spec((2 * LANES, LANES))],
        out_specs=[head_spec(8), head_spec(8), pl.BlockSpec((4, TM, 2 * LANES), lambda i: (0, i, 0))],
        out_shape=[sds((8, n, LANES), BF16), sds((8, n, LANES), BF16), sds((4, n, 2 * LANES), BF16)],
        compiler_params=_params("parallel"),
    )(cq, ckv, kr, wq, wkv, lat_gains, gains_d, cos32, sa32, sb32, bd64, bd32)

    nqm = seq // TQ_MLA
    mix_d = pl.pallas_call(
        _mla_kernel,
        grid=(bsz, N_HEADS // 2, nqm),
        in_specs=[pl.BlockSpec((2, TQ_MLA, LANES), lambda b, hp, i: (hp, b * nqm + i, 0)),
                  pl.BlockSpec((2, seq, LANES), lambda b, hp, i: (hp, b, 0)),
                  pl.BlockSpec((1, seq, 2 * LANES), lambda b, hp, i: (hp, b, 0)),
                  pl.BlockSpec((TQ_MLA, LANES), lambda b, hp, i: (b * nqm + i, hp))],
        out_specs=pl.BlockSpec((TQ_MLA, LANES), lambda b, hp, i: (b * nqm + i, hp)),
        out_shape=sds((n, 512), BF16),
        scratch_shapes=[pltpu.VMEM((2 * TQ_MLA, LANES), F32), pltpu.VMEM((2 * TQ_MLA, 2 * LANES), F32)],
        compiler_params=_params("parallel", "parallel", "arbitrary"),
    )(q_cat, k_cat, v_d, zd)

    x2 = out_proj(x1, mix_c, mix_d, cd_w_out[0])
    return x2.reshape(bsz, seq, d)
```

```python
import functools

import jax
import jax.numpy as jnp
from jax import lax
from jax.experimental import pallas as pl
from jax.experimental.pallas import tpu as pltpu

F32, BF16, I32 = jnp.float32, jnp.bfloat16, jnp.int32

D_MODEL = 1024
HEAD_DIM = 64
N_HEADS = 8
N_KV = 2
GRP = N_HEADS // N_KV
IDX_HEADS = 4
DSA_TOPK = 256
SWA_WINDOW = 128
CMP_STRIDE = 16
SLC_BLOCK = 64
SLC_TOPN = 16
NSA_WINDOW = 512
Q_LORA = 256
KV_LORA = 128
NOPE_DIM = 64
ROPE_DIM = 32
ROPE_THETA = 10000.0
EPS = 1e-6

LANES = 128
TM = 512
TQ = 128
CH = 512
TQ_DSA = 256
TQ_MLA = 512
UNROLL = 4
COUNT_ROWS = 16
ROW_BLOCK = 128
VMEM_LIMIT = 56 * 1024 * 1024

NEG_INF = float("-inf")
M_INIT = -1e30
INT_MIN = -(2 ** 31)

AB_W = 2944
LOG2E = 1.4426950408889634
Q_SCALE = HEAD_DIM ** -0.5 * LOG2E


def _nn(a, b):
    return jnp.dot(a, b, preferred_element_type=F32)


def _nt(a, b):
    return lax.dot_general(a, b, (((1,), (1,)), ((), ())), preferred_element_type=F32)


def _sigmoid(z):
    return 1.0 / (1.0 + jnp.exp(-z))


def _silu(z):
    return z * _sigmoid(z)


def _row_rms(x, g):
    return x * lax.rsqrt(jnp.mean(x * x, axis=-1, keepdims=True) + EPS) * g


def _seg_norm(s, bd, gain, group):
    s2 = s * s
    hi = s2.astype(BF16)
    lo = (s2 - hi.astype(F32)).astype(BF16)
    ss = _nn(jnp.concatenate([hi, lo], axis=1), bd)
    return s * lax.rsqrt(ss * (1.0 / group) + EPS) * gain


def _rope(s, cos, sa, sb, half):
    return s * cos + pltpu.roll(s, half, 1) * sa + pltpu.roll(s, LANES - half, 1) * sb


def _order_key(v):
    bits = lax.bitcast_convert_type(v, I32)
    return bits ^ ((bits >> 31) & 0x7FFFFFFF)


def _stack_rows(x, n):
    return jnp.concatenate([x] * n, axis=0)


def _with_ones(v):
    return jnp.concatenate([v.astype(BF16), jnp.ones(v.shape, BF16)], axis=1)


def _flash_update(s, v_ext, m_ref, acc_ref, rows):
    m_prev = m_ref[rows]
    sb = s.astype(BF16)
    m_new = jnp.maximum(m_prev, jnp.max(sb, axis=1, keepdims=True).astype(F32))
    alpha = jnp.exp2(m_prev - m_new)
    p = jnp.exp2(sb - jnp.concatenate([m_new.astype(BF16)] * (s.shape[1] // LANES), axis=1))
    acc_ref[rows] = jnp.concatenate([alpha, alpha], axis=1) * acc_ref[rows] + _nn(p, v_ext)
    m_ref[rows] = m_new


def _unrolled_loop(n, step, unroll=UNROLL):
    def body(j, carry):
        for u in range(unroll):
            step(j * unroll + u)
        return carry

    lax.fori_loop(0, n // unroll, body, 0)
    base = (n // unroll) * unroll
    p = unroll // 2
    while p >= 1:
        def tail(base=base, p=p):
            for u in range(p):
                step(base + u)

        pl.when((n & p) != 0)(tail)
        base = base + (n & p)
        p //= 2


def _flash_init(m_ref, acc_ref):
    m_ref[...] = jnp.full(m_ref.shape, M_INIT, F32)
    acc_ref[...] = jnp.zeros(acc_ref.shape, F32)


def _flash_result(acc_ref):
    acc = acc_ref[...]
    return acc[:, 0:LANES] / jnp.maximum(acc[:, LANES:2 * LANES], 1e-30)


def _place_q_heads(y, bd, gain, cos, sa, sb, out_ref, scale):
    lo_half = lax.broadcasted_iota(I32, (y.shape[0], LANES), 1) < HEAD_DIM
    for j in range(4):
        s = y[:, LANES * j:LANES * (j + 1)]
        s = _rope(_seg_norm(s, bd, gain, HEAD_DIM), cos, sa, sb, HEAD_DIM // 2) * scale
        r = pltpu.roll(s, HEAD_DIM, 1)
        if j < 2:
            out_ref[2 * j] = jnp.where(lo_half, s, 0.0).astype(BF16)
            out_ref[2 * j + 1] = jnp.where(lo_half, r, 0.0).astype(BF16)
        else:
            out_ref[2 * j] = jnp.where(lo_half, 0.0, r).astype(BF16)
            out_ref[2 * j + 1] = jnp.where(lo_half, 0.0, s).astype(BF16)


def _gated_store(o, z_ref, o_ref, tq, row0=0):
    lo_half = lax.broadcasted_iota(I32, (tq, LANES), 1) < HEAD_DIM
    for j in range(4):
        a = o[(2 * j) * tq:(2 * j + 1) * tq]
        b = o[(2 * j + 1) * tq:(2 * j + 2) * tq]
        if j < 2:
            slab = jnp.where(lo_half, a, pltpu.roll(b, HEAD_DIM, 1))
        else:
            slab = jnp.where(lo_half, pltpu.roll(a, HEAD_DIM, 1), b)
        z = z_ref[row0:row0 + tq, LANES * j:LANES * (j + 1)]
        o_ref[row0:row0 + tq, LANES * j:LANES * (j + 1)] = (slab * _silu(z)).astype(BF16)


def _proj_ab_kernel(x_ref, g_ref, w_ref, cos_ref, sa_ref, sb_ref, bd_ref, gains_ref,
                    qa_ref, ka_ref, va_ref, qi_ref, ki_ref, wi_ref, za_ref,
                    qb_ref, kb_ref, vb_ref, zb_ref):
    xn = _row_rms(x_ref[...], g_ref[...]).astype(BF16)
    cos, sa, sb, bd = cos_ref[...], sa_ref[...], sb_ref[...], bd_ref[...]
    lo_half = lax.broadcasted_iota(I32, (TM, LANES), 1) < HEAD_DIM

    def proj(c0, n):
        return _nn(xn, w_ref[:, c0:c0 + n])

    def k_slab(c0, gain):
        return _rope(_seg_norm(proj(c0, LANES), bd, gain, HEAD_DIM), cos, sa, sb, HEAD_DIM // 2)

    _place_q_heads(proj(0, 512), bd, gains_ref[0:1, :], cos, sa, sb, qa_ref, Q_SCALE)
    ka_ref[...] = k_slab(512, gains_ref[1:2, :]).astype(BF16)
    va_ref[...] = _with_ones(proj(640, LANES))
    qi = proj(768, 256)
    for j in range(2):
        s = _rope(qi[:, LANES * j:LANES * (j + 1)], cos, sa, sb, HEAD_DIM // 2)
        qi_ref[2 * j] = jnp.where(lo_half, s, 0.0).astype(BF16)
        qi_ref[2 * j + 1] = jnp.where(lo_half, pltpu.roll(s, HEAD_DIM, 1), 0.0).astype(BF16)
    kiw = proj(1024, LANES)
    wi_ref[...] = kiw
    ki_ref[...] = _rope(_seg_norm(kiw, bd, gains_ref[2:3, :], HEAD_DIM), cos, sa, sb, HEAD_DIM // 2).astype(BF16)
    za_ref[...] = proj(1152, 512)
    _place_q_heads(proj(1664, 512), bd, gains_ref[3:4, :], cos, sa, sb, qb_ref, Q_SCALE)
    kb_ref[...] = k_slab(2176, gains_ref[4:5, :]).astype(BF16)
    vb_ref[...] = _with_ones(proj(2304, LANES))
    zb_ref[...] = proj(2432, 512)


def _count_chunks(keys_ref, nch, nq, pred):
    def body(c, acc):
        return acc + jnp.sum(pred(keys_ref[c]).reshape(CH // COUNT_ROWS, COUNT_ROWS, nq), axis=0)

    acc = lax.fori_loop(0, nch, body, jnp.zeros((COUNT_ROWS, nq), F32))
    return jnp.sum(acc, axis=0, keepdims=True)


def _kth_threshold(count, nq, k):
    def bit_body(it, carry):
        tb, above = carry
        cand_b = tb | jnp.left_shift(jnp.int32(1), 31 - it)
        cand_s = cand_b ^ INT_MIN
        cnt = count(lambda kk: jnp.where(kk >= cand_s, 1.0, 0.0))
        ok = cnt >= k
        return jnp.where(ok, cand_b, tb), jnp.where(ok, above, cnt)

    tb, above = lax.fori_loop(0, 32, bit_body, (jnp.zeros((1, nq), I32), jnp.zeros((1, nq), F32)))
    return tb ^ INT_MIN, k - above


def _dsa_kernel(qi_ref, wi_ref, ki_ref, qa_ref, ka_ref, va_ref, za_ref, tril_ref, o_ref,
                keys_ref, bias_ref, ties_ref, m_ref, acc_ref, *, topk):
    TQ = TQ_DSA
    i = pl.program_id(1)
    t0 = i * TQ
    nch = (t0 + TQ + CH - 1) // CH
    key_c = lax.broadcasted_iota(I32, (CH, TQ), 0)
    qry_c = t0 + lax.broadcasted_iota(I32, (CH, TQ), 1)

    qi = qi_ref[...].reshape(IDX_HEADS * TQ, LANES)
    w_t = wi_ref[...].T * ((HEAD_DIM * IDX_HEADS) ** -0.5)
    ws = [w_t[HEAD_DIM + h:HEAD_DIM + h + 1, :] for h in range(IDX_HEADS)]

    def score_chunk(c):
        off = pl.multiple_of(c * CH, CH)
        lg = _nt(ki_ref[pl.ds(off, CH), :], qi)
        sc = ws[0] * jnp.maximum(lg[:, 0:TQ], 0.0)
        for h in range(1, IDX_HEADS):
            sc = sc + ws[h] * jnp.maximum(lg[:, h * TQ:(h + 1) * TQ], 0.0)
        sc = jnp.where(sc == 0.0, 0.0, sc)
        keys_ref[c] = jnp.where(key_c + off <= qry_c, _order_key(sc), INT_MIN)

    _unrolled_loop(nch, score_chunk, unroll=2)

    count = functools.partial(_count_chunks, keys_ref, nch, TQ)
    thr, need = _kth_threshold(count, TQ, float(topk))

    tril = tril_ref[...]

    ties_ref[...] = jnp.zeros(ties_ref.shape, F32)

    def select_chunk(c):
        kk = keys_ref[c]
        pref = _nn(tril, jnp.where(kk == thr, 1.0, 0.0).astype(BF16)) + ties_ref[0:1, :]
        ties_ref[0:1, :] = pref[CH - 1:CH, :]
        tie = jnp.where(kk == thr, jnp.where(pref <= need, 0.0, NEG_INF), NEG_INF)
        bias = jnp.where(key_c + c * CH <= qry_c, jnp.where(kk > thr, 0.0, tie), NEG_INF)
        bias_ref[c] = bias.T.astype(BF16)

    _unrolled_loop(nch, select_chunk)

    _flash_init(m_ref, acc_ref)
    q_all = qa_ref[...].reshape(N_HEADS * TQ, LANES)

    def att_chunk(c):
        off = pl.multiple_of(c * CH, CH)
        s = _nt(q_all, ka_ref[pl.ds(off, CH), :]).astype(BF16) + _stack_rows(bias_ref[c], N_HEADS)
        _flash_update(s, va_ref[pl.ds(off, CH), :], m_ref, acc_ref, slice(None))

    _unrolled_loop(nch, att_chunk)
    _gated_store(_flash_result(acc_ref), za_ref, o_ref, TQ)


def _swa_kernel(q_ref, kp_ref, kc_ref, vp_ref, vc_ref, sink_ref, z_ref, o_ref):
    i = pl.program_id(1)
    qi = lax.broadcasted_iota(I32, (TQ, 2 * TQ), 0)
    kj = lax.broadcasted_iota(I32, (TQ, 2 * TQ), 1)
    sink = sink_ref[...]
    for half in range(2):
        q = q_ref[:, half * TQ:(half + 1) * TQ, :].reshape(N_HEADS * TQ, LANES)
        if half == 0:
            k2 = jnp.concatenate([kp_ref[...], kc_ref[0:TQ, :]], axis=0)
            v2 = jnp.concatenate([vp_ref[...], vc_ref[0:TQ, :]], axis=0)
            lo = jnp.where(i > 0, 0, TQ)
        else:
            k2, v2, lo = kc_ref[...], vc_ref[...], 0
        keep = jnp.where(kj >= lo, jnp.where(kj > qi, jnp.where(kj <= qi + TQ, 0.0, NEG_INF), NEG_INF), NEG_INF)
        s = _nt(q, k2) + _stack_rows(keep, N_HEADS)
        m = jnp.maximum(jnp.max(s, axis=1, keepdims=True), sink)
        p = jnp.exp2(s - jnp.concatenate([m, m], axis=1))
        o = _nn(p.astype(BF16), v2)
        den = o[:, LANES:2 * LANES] + jnp.exp2(sink - m)
        _gated_store(o[:, 0:LANES] / den, z_ref, o_ref, TQ, half * TQ)


def _out_proj_kernel(x_ref, ma_ref, mb_ref, w_ref, o_ref):
    half = ma_ref.shape[1]
    o_ref[...] = x_ref[...] + _nn(ma_ref[...], w_ref[0:half, :]) + _nn(mb_ref[...], w_ref[half:2 * half, :])


def _proj_cd_kernel(x_ref, g_ref, w_ref, cos_ref, sa_ref, sb_ref, bd_ref, gains_ref,
                    qc_ref, kc_ref, vc_ref, ks_ref, vs_ref, kw_ref, vw_ref, gt_ref, zc_ref,
                    cq_ref, ckv_ref, kr_ref, zd_ref):
    xn = _row_rms(x_ref[...], g_ref[...]).astype(BF16)
    cos, sa, sb, bd = cos_ref[...], sa_ref[...], sb_ref[...], bd_ref[...]

    def proj(c0, n):
        return _nn(xn, w_ref[:, c0:c0 + n])

    def k_slab(c0, gain):
        return _rope(_seg_norm(proj(c0, LANES), bd, gain, HEAD_DIM), cos, sa, sb, HEAD_DIM // 2)

    _place_q_heads(proj(0, 512), bd, gains_ref[0:1, :], cos, sa, sb, qc_ref, Q_SCALE)
    kc_ref[...] = proj(512, LANES)
    vc_ref[...] = proj(640, LANES)
    ks_ref[...] = k_slab(768, gains_ref[1:2, :]).astype(BF16)
    vs_ref[...] = _with_ones(proj(896, LANES))
    kw_ref[...] = k_slab(1024, gains_ref[2:3, :]).astype(BF16)
    vw_ref[...] = _with_ones(proj(1152, LANES))
    gt_ref[...] = _sigmoid(proj(1280, LANES))
    zc_ref[...] = proj(1408, 512)
    cq_ref[...] = proj(1920, 256)
    ckv_ref[...] = proj(2176, LANES)
    kr_ref[...] = proj(2304, LANES)
    zd_ref[...] = proj(2432, 512)


def _compress_kernel(c_ref, pe_ref, w1a_ref, w1b_ref, w2_ref, gain_ref, cos_ref, sa_ref, sb_ref, bd_ref,
                     o_ref, *, is_key):
    c = c_ref[...]
    n = c.shape[0]
    a = _nn((c + pe_ref[0:1, :]).astype(BF16), w1a_ref[...])
    b = _nn((c + pe_ref[1:2, :]).astype(BF16), w1b_ref[...])
    h = _silu(a + pltpu.roll(b, n - 1, 0))
    y = _nn(h.astype(BF16), w2_ref[...])
    if is_key:
        y = _rope(_seg_norm(y, bd_ref[...], gain_ref[...], HEAD_DIM), cos_ref[...], sa_ref[...], sb_ref[...],
                  HEAD_DIM // 2)
    o_ref[...] = y.astype(BF16)


def _nsa_kernel(q_ref, kc_ref, vc_ref, ks_ref, vs_ref, kw_ref, vw_ref, gt_ref, z_ref, wsel_ref, o_ref,
                m_ref, acc_ref, *, n_sel):
    i = pl.program_id(1)
    t0 = i * TQ
    nch = (t0 + TQ + CH - 1) // CH
    ncmp = kc_ref.shape[0]
    rows_all = N_HEADS * TQ
    q = q_ref[...].reshape(rows_all, LANES)

    cend = lax.broadcasted_iota(I32, (TQ, ncmp), 1) * CMP_STRIDE + (2 * CMP_STRIDE - 1)
    bias_c = jnp.where(cend <= t0 + lax.broadcasted_iota(I32, (TQ, ncmp), 0), 0.0, NEG_INF)
    s = _nt(q, kc_ref[...]) + _stack_rows(bias_c, N_HEADS)
    m = jnp.max(s, axis=1, keepdims=True)
    m = jnp.where(m > NEG_INF, m, 0.0)
    p = jnp.exp2(s - m)
    p = p / jnp.maximum(jnp.sum(p, axis=1, keepdims=True), 1e-30)
    o_cmp = _nn(p.astype(BF16), vc_ref[...])

    span = NSA_WINDOW + TQ
    start = pl.multiple_of(jnp.maximum(t0 - NSA_WINDOW, 0), TQ)
    diff = t0 + lax.broadcasted_iota(I32, (TQ, span), 0) - (start + lax.broadcasted_iota(I32, (TQ, span), 1))
    bias_w = jnp.where(diff >= 0, jnp.where(diff < NSA_WINDOW, 0.0, NEG_INF), NEG_INF)
    sw = _nt(q, kw_ref[pl.ds(start, span), :]).astype(BF16) + _stack_rows(bias_w.astype(BF16), N_HEADS)
    pw = jnp.exp2(sw - jnp.max(sw, axis=1, keepdims=True))
    o_win = _nn(pw, vw_ref[pl.ds(start, span), :])
    o_win = o_win[:, 0:LANES] / o_win[:, LANES:2 * LANES]

    blk = lax.broadcasted_iota(I32, (LANES, TQ), 0)
    cur = (t0 + lax.broadcasted_iota(I32, (LANES, TQ), 1)) >> 6
    wsel_t = wsel_ref[...]
    keys = []
    for g in range(N_KV):
        imp = p[(GRP * g) * TQ:(GRP * g + 1) * TQ]
        for r in range(1, GRP):
            imp = imp + p[(GRP * g + r) * TQ:(GRP * g + r + 1) * TQ]
        hi = imp.astype(BF16)
        lo = (imp - hi.astype(F32)).astype(BF16)
        imp_s = _nt(wsel_t, hi) + _nt(wsel_t, lo)
        forced = jnp.where(blk == 0, jnp.inf, jnp.where(blk >= cur - 1, jnp.inf, imp_s))
        keys.append(_order_key(jnp.where(blk <= cur, forced, NEG_INF)))
    kk = jnp.concatenate(keys, axis=1)

    def bit_body(it, tb):
        cand_b = tb | jnp.left_shift(jnp.int32(1), 31 - it)
        cnt = jnp.sum(jnp.where(kk >= (cand_b ^ INT_MIN), 1.0, 0.0), axis=0, keepdims=True)
        return jnp.where(cnt >= n_sel, cand_b, tb)

    thr = lax.fori_loop(0, 32, bit_body, jnp.zeros((1, N_KV * TQ), I32)) ^ INT_MIN
    need = n_sel - jnp.sum(jnp.where(kk > thr, 1.0, 0.0), axis=0, keepdims=True)
    tril = jnp.where(lax.broadcasted_iota(I32, (LANES, LANES), 0) >= lax.broadcasted_iota(I32, (LANES, LANES), 1),
                     1.0, 0.0).astype(BF16)
    pref = _nn(tril, jnp.where(kk == thr, 1.0, 0.0).astype(BF16))
    tie = jnp.where(kk == thr, jnp.where(pref <= need, 1.0, 0.0), 0.0)
    sel_t = jnp.where(kk > thr, 1.0, tie)
    sel = jnp.concatenate([sel_t[:, g * TQ:(g + 1) * TQ].T for g in range(N_KV)], axis=0).astype(BF16)

    _flash_init(m_ref, acc_ref)
    blk_row =lax.broadcasted_iota(I32, (LANES, CH), 0)
    blk_col = lax.broadcasted_iota(I32, (LANES, CH), 1) >> 6
    row_c = t0 + lax.broadcasted_iota(I32, (TQ, CH), 0)
    col_c = lax.broadcasted_iota(I32, (TQ, CH), 1)

    def slc_chunk(c):
        off = pl.multiple_of(c * CH, CH)
        expand = jnp.where(blk_row == blk_col + c * (CH // SLC_BLOCK), 1.0, 0.0).astype(BF16)
        tok = _nn(sel, expand)
        causal = col_c + off <= row_c
        parts = []
        for g in range(N_KV):
            bias_g = jnp.where(causal, jnp.where(tok[g * TQ:(g + 1) * TQ] > 0.5, 0.0, NEG_INF), NEG_INF)
            parts.extend([bias_g.astype(BF16)] * GRP)
        sc = _nt(q, ks_ref[pl.ds(off, CH), :]).astype(BF16) + jnp.concatenate(parts, axis=0)
        _flash_update(sc, vs_ref[pl.ds(off, CH), :], m_ref, acc_ref, slice(None))

    _unrolled_loop(nch, slc_chunk)
    o_slc = _flash_result(acc_ref)

    gates = gt_ref[...]
    outs = []
    for h in range(N_HEADS):
        rs = slice(h * TQ, (h + 1) * TQ)
        outs.append(gates[:, 3 * h:3 * h + 1] * o_cmp[rs] + gates[:, 3 * h + 1:3 * h + 2] * o_slc[rs]
                    + gates[:, 3 * h + 2:3 * h + 3] * o_win[rs])
    _gated_store(jnp.concatenate(outs, axis=0), z_ref, o_ref, TQ)


def _mla_prep_kernel(cq_ref, ckv_ref, kr_ref, wq_ref, wkv_ref, lat_ref, gains_ref, cos_ref, sa_ref, sb_ref,
                     bd64_ref, bd32_ref, q_ref, k_ref, v_ref):
    lane = lax.broadcasted_iota(I32, (TM, LANES), 1)
    lo_half = lane < NOPE_DIM
    cos, sa, sb = cos_ref[...], sa_ref[...], sb_ref[...]
    bd64, bd32 = bd64_ref[...], bd32_ref[...]
    scale = (NOPE_DIM + ROPE_DIM) ** -0.5 * LOG2E

    def rope_slab(s, gain):
        return _rope(_seg_norm(s, bd32, gain, ROPE_DIM), cos, sa, sb, ROPE_DIM // 2)

    q = _nn(_row_rms(cq_ref[...], lat_ref[0:1, :]).astype(BF16), wq_ref[...])
    kv = _nn(_row_rms(ckv_ref[...], lat_ref[1:2, 0:KV_LORA]).astype(BF16), wkv_ref[...])
    k_rope = pltpu.roll(rope_slab(kr_ref[...], gains_ref[3:4, :]), NOPE_DIM, 1)
    q_rope = [rope_slab(q[:, 512 + LANES * j:512 + LANES * (j + 1)], gains_ref[1:2, :]) * scale for j in range(2)]
    for j in range(4):
        qn = _seg_norm(q[:, LANES * j:LANES * (j + 1)], bd64, gains_ref[0:1, :], NOPE_DIM) * scale
        kn = _seg_norm(kv[:, LANES * j:LANES * (j + 1)], bd64, gains_ref[2:3, :], NOPE_DIM)
        for e in range(2):
            h = 2 * j + e
            qn_h = qn if e == 0 else pltpu.roll(qn, NOPE_DIM, 1)
            kn_h = kn if e == 0 else pltpu.roll(kn, NOPE_DIM, 1)
            shift = (NOPE_DIM - ROPE_DIM * (h % 4)) % LANES
            qr = q_rope[h // 4]
            qr_h = qr if shift == 0 else pltpu.roll(qr, shift, 1)
            q_ref[h] = jnp.where(lo_half, qn_h, jnp.where(lane < NOPE_DIM + ROPE_DIM, qr_h, 0.0)).astype(BF16)
            k_ref[h] = jnp.where(lo_half, kn_h, k_rope).astype(BF16)
    for j in range(4):
        v_ref[j] = _with_ones(kv[:, 512 + LANES * j:512 + LANES * (j + 1)])


def _mla_kernel(q_ref, k_ref, v_ref, z_ref, o_ref, m_ref, acc_ref):
    i = pl.program_id(2)
    tq = TQ_MLA
    _flash_init(m_ref, acc_ref)

    def step(c, masked):
        off = pl.multiple_of(c * CH, CH)
        v = v_ref[0, pl.ds(off, CH), :]
        for e in range(2):
            k = k_ref[e, pl.ds(off, CH), :]
            for rb in range(tq // ROW_BLOCK):
                s = _nt(q_ref[e, rb * ROW_BLOCK:(rb + 1) * ROW_BLOCK, :], k)
                if masked:
                    qpos = rb * ROW_BLOCK + lax.broadcasted_iota(I32, (ROW_BLOCK, CH), 0)
                    s = jnp.where(lax.broadcasted_iota(I32, (ROW_BLOCK, CH), 1) <= qpos, s, NEG_INF)
                _flash_update(s, v, m_ref, acc_ref, slice(e * tq + rb * ROW_BLOCK, e * tq + (rb + 1) * ROW_BLOCK))

    _unrolled_loop(i, lambda c: step(c, False))
    step(i, True)
    o = _flash_result(acc_ref)
    lo_half = lax.broadcasted_iota(I32, (tq, LANES), 1) < NOPE_DIM
    slab = jnp.where(lo_half, o[0:tq], o[tq:2 * tq])
    o_ref[...] = (slab * _silu(z_ref[...])).astype(BF16)


def _rope_tables(pos, dim):
    inv = jnp.power(jnp.float32(ROPE_THETA), -jnp.arange(0, dim, 2, dtype=F32) / dim)
    ang = pos.astype(F32)[:, None] * inv[None, :]
    cos, sin = jnp.cos(ang), jnp.sin(ang)
    reps = LANES // dim
    zero = jnp.zeros_like(sin)
    cos_t = jnp.tile(jnp.concatenate([cos, cos], axis=1), (1, reps))
    sa_t = jnp.tile(jnp.concatenate([zero, sin], axis=1), (1, reps))
    sb_t = jnp.tile(jnp.concatenate([-sin, zero], axis=1), (1, reps))
    return cos_t, sa_t, sb_t


def _block_diag(group):
    r = jnp.arange(LANES) // group
    bd = (r[:, None] == r[None, :]).astype(BF16)
    return jnp.concatenate([bd, bd], axis=0)


def _tile_gain(g, width=LANES):
    return jnp.tile(g.astype(F32), width // g.shape[0])


def _pad_lanes(g, width=LANES):
    return jnp.concatenate([g.astype(F32), jnp.zeros((width - g.shape[0],), F32)])


def _pad_cols(w, width):
    return jnp.concatenate([w, jnp.zeros((w.shape[0], width - w.shape[1]), w.dtype)], axis=1)


def _params(*sem, flags=None):
    return pltpu.CompilerParams(dimension_semantics=sem, vmem_limit_bytes=VMEM_LIMIT, flags=flags)


def kernel(x, ab_norm, ab_w_in, a_qk_norm, a_kidx_norm, b_qk_norm, b_sinks, ab_w_out, cd_norm, cd_w_in, c_q_norm, c_k_norm, c_cmp_pe, c_cmp_w1, c_cmp_w2, d_q_lat_norm, d_kv_lat_norm, d_w_uq, d_w_ukv, d_nope_norm, d_rope_norm, cd_w_out):
    bsz, seq, d = x.shape
    n = bsz * seq
    assert d == D_MODEL and seq % CH == 0 and seq >= NSA_WINDOW + TQ and seq // SLC_BLOCK <= LANES
    nq = seq // TQ
    nrow = n // TM
    srow = seq // TM
    xf = x.reshape(n, d)
    pos = jnp.arange(seq)
    cos64, sa64, sb64 = _rope_tables(pos, HEAD_DIM)
    cos32, sa32, sb32 = _rope_tables(pos, ROPE_DIM)
    bd64, bd32 = _block_diag(HEAD_DIM), _block_diag(ROPE_DIM)
    tril = (jnp.arange(CH)[:, None] >= jnp.arange(CH)[None, :]).astype(BF16)

    def row_spec(width, rows=TM):
        return pl.BlockSpec((rows, width), lambda i: (i, 0))

    def const_spec(shape):
        return pl.BlockSpec(shape, lambda i: tuple(0 for _ in shape))

    tab_spec = pl.BlockSpec((TM, LANES), lambda i: (i % srow, 0))
    head_spec = lambda nh: pl.BlockSpec((nh, TM, LANES), lambda i: (0, i, 0))

    wa = ab_w_in[0]
    w_ab = jnp.concatenate([wa[:, 0:1088], _pad_cols(wa[:, 1088:1092], 64), wa[:, 1092:]], axis=1).astype(BF16)
    gains_ab = jnp.stack([_tile_gain(a_qk_norm[0, 0]), _tile_gain(a_qk_norm[0, 1]), _pad_lanes(a_kidx_norm[0]),
                          _tile_gain(b_qk_norm[0, 0]), _tile_gain(b_qk_norm[0, 1])]
                         + [jnp.zeros((LANES,), F32)] * 3)
    sds = jax.ShapeDtypeStruct
    qa, ka, va, qi, ki, wi, za, qb, kb, vb, zb = pl.pallas_call(
        _proj_ab_kernel,
        grid=(nrow,),
        in_specs=[row_spec(d), const_spec((1, d)), const_spec((d, AB_W)), tab_spec, tab_spec, tab_spec,
                  const_spec((2 * LANES, LANES)), const_spec((8, LANES))],
        out_specs=[head_spec(8), row_spec(LANES), row_spec(2 * LANES), head_spec(4), row_spec(LANES), row_spec(LANES),
                   row_spec(512), head_spec(8), row_spec(LANES), row_spec(2 * LANES), row_spec(512)],
        out_shape=[sds((8, n, LANES), BF16), sds((n, LANES), BF16), sds((n, 2 * LANES), BF16),
                   sds((4, n, LANES), BF16), sds((n, LANES), BF16), sds((n, LANES), F32), sds((n, 512), F32),
                   sds((8, n, LANES), BF16), sds((n, LANES), BF16), sds((n, 2 * LANES), BF16), sds((n, 512), F32)],
        compiler_params=_params("parallel"),
    )(xf, ab_norm[0][None, :], w_ab, cos64, sa64, sb64, bd64, gains_ab)

    qt_heads = lambda nh: pl.BlockSpec((nh, TQ, LANES), lambda b, i: (0, b * nq + i, 0))
    qt_rows = lambda width: pl.BlockSpec((TQ, width), lambda b, i: (b * nq + i, 0))
    seq_rows = pl.BlockSpec((seq, LANES), lambda b, i: (b, 0))
    seq_rows_v = pl.BlockSpec((seq, 2 * LANES), lambda b, i: (b, 0))
    att_scratch = [pltpu.VMEM((N_HEADS * TQ, LANES), F32), pltpu.VMEM((N_HEADS * TQ, 2 * LANES), F32)]

    nqd = seq // TQ_DSA
    dsa_heads = lambda nh: pl.BlockSpec((nh, TQ_DSA, LANES), lambda b, i: (0, b * nqd + i, 0))
    dsa_rows = lambda width: pl.BlockSpec((TQ_DSA, width), lambda b, i: (b * nqd + i, 0))
    mix_a = pl.pallas_call(
        functools.partial(_dsa_kernel, topk=min(DSA_TOPK, seq // 4)),
        grid=(bsz, nqd),
        in_specs=[dsa_heads(4), dsa_rows(LANES), seq_rows, dsa_heads(8), seq_rows, seq_rows_v, dsa_rows(512),
                  pl.BlockSpec((CH, CH), lambda b, i: (0, 0))],
        out_specs=dsa_rows(512),
        out_shape=sds((n, 512), BF16),
        scratch_shapes=[pltpu.VMEM((seq // CH, CH, TQ_DSA), I32), pltpu.VMEM((seq // CH, TQ_DSA, CH), BF16),
                        pltpu.VMEM((8, TQ_DSA), F32), pltpu.VMEM((N_HEADS * TQ_DSA, LANES), F32),
                        pltpu.VMEM((N_HEADS * TQ_DSA, 2 * LANES), F32)],
        compiler_params=_params("parallel", "arbitrary"),
    )(qi, wi, ki, qa, ka, va, za, tril)

    nq2 = nq // 2
    prev_rows = lambda width: pl.BlockSpec((TQ, width), lambda b, i: (b * nq + jnp.maximum(2 * i - 1, 0), 0))
    pair_rows = lambda width: pl.BlockSpec((2 * TQ, width), lambda b, i: (b * nq2 + i, 0))
    sink_rows = jnp.broadcast_to(jnp.repeat(b_sinks[0].astype(F32) * LOG2E, TQ)[:, None], (N_HEADS * TQ, LANES))
    mix_b = pl.pallas_call(
        _swa_kernel,
        grid=(bsz, nq2),
        in_specs=[pl.BlockSpec((N_HEADS, 2 * TQ, LANES), lambda b, i: (0, b * nq2 + i, 0)),
                  prev_rows(LANES), pair_rows(LANES), prev_rows(2 * LANES), pair_rows(2 * LANES),
                  pl.BlockSpec((N_HEADS * TQ, LANES), lambda b, i: (0, 0)), pair_rows(512)],
        out_specs=pair_rows(512),
        out_shape=sds((n, 512), BF16),
        compiler_params=_params("parallel", "arbitrary"),
    )(qb, kb, kb, vb, vb, sink_rows, zb)

    def out_proj(xin, m0, m1, w):
        return pl.pallas_call(
            _out_proj_kernel,
            grid=(nrow,),
            in_specs=[row_spec(d), row_spec(512), row_spec(512), const_spec((2 * 512, d))],
            out_specs=row_spec(d),
            out_shape=sds((n, d), F32),
            compiler_params=_params("parallel"),
        )(xin, m0, m1, w.astype(BF16))

    x1 = out_proj(xf, mix_a, mix_b, ab_w_out[0])

    wc = cd_w_in[0]
    w_cd = jnp.concatenate([wc[:, 0:1280], _pad_cols(wc[:, 1280:1304], LANES), wc[:, 1304:2200],
                            _pad_cols(wc[:, 2200:2232], LANES), wc[:, 2232:]], axis=1).astype(BF16)
    gains_cd = jnp.stack([_tile_gain(c_q_norm[0]), _tile_gain(c_k_norm[0, 1]), _tile_gain(c_k_norm[0, 2])]
                         + [jnp.zeros((LANES,), F32)] * 5)
    qc, kc_raw, vc_raw, ks, vs, kw, vw, gates, zc, cq, ckv, kr, zd = pl.pallas_call(
        _proj_cd_kernel,
        grid=(nrow,),
        in_specs=[row_spec(d), const_spec((1, d)), const_spec((d, AB_W)), tab_spec, tab_spec, tab_spec,
                  const_spec((2 * LANES, LANES)), const_spec((8, LANES))],
        out_specs=[head_spec(8), row_spec(LANES), row_spec(LANES), row_spec(LANES), row_spec(2 * LANES),
                   row_spec(LANES), row_spec(2 * LANES), row_spec(LANES), row_spec(512), row_spec(256),
                   row_spec(LANES), row_spec(LANES), row_spec(512)],
        out_shape=[sds((8, n, LANES), BF16), sds((n, LANES), F32), sds((n, LANES), F32), sds((n, LANES), BF16),
                   sds((n, 2 * LANES), BF16), sds((n, LANES), BF16), sds((n, 2 * LANES), BF16), sds((n, LANES), F32),
                   sds((n, 512), F32), sds((n, 256), F32), sds((n, LANES), F32), sds((n, LANES), F32),
                   sds((n, 512), F32)],
        compiler_params=_params("parallel"),
    )(x1, cd_norm[0][None, :], w_cd, cos64, sa64, sb64, bd64, gains_cd)

    ncmp = seq // CMP_STRIDE
    cw = CMP_STRIDE * N_KV * HEAD_DIM
    eye = jnp.eye(N_KV, dtype=F32)
    ccos, csa, csb = _rope_tables(jnp.arange(ncmp) * CMP_STRIDE + (2 * CMP_STRIDE - 1), HEAD_DIM)

    def compress(raw, which, is_key):
        w1 = c_cmp_w1[0, which].reshape(2, CMP_STRIDE, HEAD_DIM, -1)
        hid = w1.shape[-1]
        w1e = jnp.einsum("tjdu,gh->tjgdhu", w1, eye).reshape(2, cw, N_KV * hid).astype(BF16)
        w2e = jnp.einsum("ud,gh->guhd", c_cmp_w2[0, which], eye).reshape(N_KV * hid, N_KV * HEAD_DIM).astype(BF16)
        pe = c_cmp_pe[0, which].reshape(2, CMP_STRIDE, 1, HEAD_DIM)
        pe = jnp.broadcast_to(pe, (2, CMP_STRIDE, N_KV, HEAD_DIM)).reshape(2, cw)
        return pl.pallas_call(
            functools.partial(_compress_kernel, is_key=is_key),
            grid=(bsz,),
            in_specs=[pl.BlockSpec((ncmp, cw), lambda b: (b, 0)), const_spec((2, cw)),
                      const_spec((cw, N_KV * hid)), const_spec((cw, N_KV * hid)),
                      const_spec((N_KV * hid, LANES)), const_spec((1, LANES)),
                      const_spec((ncmp, LANES)), const_spec((ncmp, LANES)), const_spec((ncmp, LANES)),
                      const_spec((2 * LANES, LANES))],
            out_specs=pl.BlockSpec((ncmp, LANES), lambda b: (b, 0)),
            out_shape=sds((bsz * ncmp, LANES), BF16),
            compiler_params=_params("parallel"),
        )(raw.reshape(bsz * ncmp, cw), pe, w1e[0], w1e[1], w2e, _tile_gain(c_k_norm[0, 0])[None, :],
          ccos, csa, csb, bd64)

    kc = compress(kc_raw, 0, True)
    vc = compress(vc_raw, 1, False)

    ratio = SLC_BLOCK // CMP_STRIDE
    mm = jnp.arange(ncmp)[:, None]
    jj = jnp.arange(LANES)[None, :]
    wsel = (jnp.where((mm == ratio * jj - 1) | (mm == ratio * jj + ratio - 1), 1.0, 0.0)
            + jnp.where((mm >= ratio * jj) & (mm < ratio * jj + ratio - 1), 2.0, 0.0))
    wsel_t = jnp.where((mm < ncmp - 1) & (jj < seq // SLC_BLOCK), wsel, 0.0).astype(BF16).T

    cmp_rows = pl.BlockSpec((ncmp, LANES), lambda b, i: (b, 0))
    mix_c = pl.pallas_call(
        functools.partial(_nsa_kernel, n_sel=min(SLC_TOPN, seq // SLC_BLOCK)),
        grid=(bsz, nq),
        in_specs=[qt_heads(8), cmp_rows, cmp_rows, seq_rows, seq_rows_v, seq_rows, seq_rows_v, qt_rows(LANES),
                  qt_rows(512), pl.BlockSpec((LANES, ncmp), lambda b, i: (0, 0))],
        out_specs=qt_rows(512),
        out_shape=sds((n, 512), BF16),
        scratch_shapes=att_scratch,
        compiler_params=_params("parallel", "arbitrary"),
    )(qc, kc, vc, ks, vs, kw, vw, gates, zc, wsel_t)

    wq = d_w_uq[0].reshape(Q_LORA, N_HEADS, NOPE_DIM + ROPE_DIM)
    wq = jnp.concatenate([wq[:, :, :NOPE_DIM].reshape(Q_LORA, -1), wq[:, :, NOPE_DIM:].reshape(Q_LORA, -1)],
                         axis=1).astype(BF16)
    wkv = d_w_ukv[0].reshape(KV_LORA, N_HEADS, NOPE_DIM + HEAD_DIM)
    wkv = jnp.concatenate([wkv[:, :, :NOPE_DIM].reshape(KV_LORA, -1), wkv[:, :, NOPE_DIM:].reshape(KV_LORA, -1)],
                          axis=1).astype(BF16)
    lat_gains = jnp.stack([d_q_lat_norm[0].astype(F32), _pad_lanes(d_kv_lat_norm[0], Q_LORA)]
                          + [jnp.zeros((Q_LORA,), F32)] * 6)
    gains_d = jnp.stack([_tile_gain(d_nope_norm[0, 0]), _tile_gain(d_rope_norm[0, 0]), _tile_gain(d_nope_norm[0, 1]),
                         _pad_lanes(d_rope_norm[0, 1])] + [jnp.zeros((LANES,), F32)] * 4)
    q_cat, k_cat, v_d = pl.pallas_call(
        _mla_prep_kernel,
        grid=(nrow,),
        in_specs=[row_spec(Q_LORA), row_spec(LANES), row_spec(LANES), const_spec((Q_LORA, 768)),
                  const_spec((KV_LORA, 1024)), const_spec((8, Q_LORA)), const_spec((8, LANES)),
                  tab_spec, tab_spec, tab_spec, const_spec((2 * LANES, LANES)), const_spec((2 * LANES, LANES))],
        out_specs=[head_spec(8), head_spec(8), pl.BlockSpec((4, TM, 2 * LANES), lambda i: (0, i, 0))],
        out_shape=[sds((8, n, LANES), BF16), sds((8, n, LANES), BF16), sds((4, n, 2 * LANES), BF16)],
        compiler_params=_params("parallel"),
    )(cq, ckv, kr, wq, wkv, lat_gains, gains_d, cos32, sa32, sb32, bd64, bd32)

    nqm = seq // TQ_MLA
    mix_d = pl.pallas_call(
        _mla_kernel,
        grid=(bsz, N_HEADS // 2, nqm),
        in_specs=[pl.BlockSpec((2, TQ_MLA, LANES), lambda b, hp, i: (hp, b * nqm + i, 0)),
                  pl.BlockSpec((2, seq, LANES), lambda b, hp, i: (hp, b, 0)),
                  pl.BlockSpec((1, seq, 2 * LANES), lambda b, hp, i: (hp, b, 0)),
                  pl.BlockSpec((TQ_MLA, LANES), lambda b, hp, i: (b * nqm + i, hp))],
        out_specs=pl.BlockSpec((TQ_MLA, LANES), lambda b, hp, i: (b * nqm + i, hp)),
        out_shape=sds((n, 512), BF16),
        scratch_shapes=[pltpu.VMEM((2 * TQ_MLA, LANES), F32), pltpu.VMEM((2 * TQ_MLA, 2 * LANES), F32)],
        compiler_params=_params("parallel", "parallel", "arbitrary"),
    )(q_cat, k_cat, v_d, zd)

    x2 = out_proj(x1, mix_c, mix_d, cd_w_out[0])
    return x2.reshape(bsz, seq, d)
```

```python
import functools

import jax
import jax.numpy as jnp
from jax import lax
from jax.experimental import pallas as pl
from jax.experimental.pallas import tpu as pltpu

F32, BF16, I32 = jnp.float32, jnp.bfloat16, jnp.int32

D_MODEL = 1024
HEAD_DIM = 64
N_HEADS = 8
N_KV = 2
GRP = N_HEADS // N_KV
IDX_HEADS = 4
DSA_TOPK = 256
SWA_WINDOW = 128
CMP_STRIDE = 16
SLC_BLOCK = 64
SLC_TOPN = 16
NSA_WINDOW = 512
Q_LORA = 256
KV_LORA = 128
NOPE_DIM = 64
ROPE_DIM = 32
ROPE_THETA = 10000.0
EPS = 1e-6

LANES = 128
TM = 512
TQ = 128
CH = 512
TQ_ATT = 256
TQ_MLA = 512
UNROLL = 4
COUNT_ROWS = 16
ROW_BLOCK = 128
VMEM_LIMIT = 56 * 1024 * 1024

NEG_INF = float("-inf")
M_INIT = -1e30
INT_MIN = -(2 ** 31)

AB_W = 2944
LOG2E = 1.4426950408889634
Q_SCALE = HEAD_DIM ** -0.5 * LOG2E


def _nn(a, b):
    return jnp.dot(a, b, preferred_element_type=F32)


def _nt(a, b):
    return lax.dot_general(a, b, (((1,), (1,)), ((), ())), preferred_element_type=F32)


def _sigmoid(z):
    return 1.0 / (1.0 + jnp.exp(-z))


def _silu(z):
    return z * _sigmoid(z)


def _row_rms(x, g):
    return x * lax.rsqrt(jnp.mean(x * x, axis=-1, keepdims=True) + EPS) * g


def _seg_norm(s, bd, gain, group):
    s2 = s * s
    hi = s2.astype(BF16)
    lo = (s2 - hi.astype(F32)).astype(BF16)
    ss = _nn(jnp.concatenate([hi, lo], axis=1), bd)
    return s * lax.rsqrt(ss * (1.0 / group) + EPS) * gain


def _rope(s, cos, sa, sb, half):
    return s * cos + pltpu.roll(s, half, 1) * sa + pltpu.roll(s, LANES - half, 1) * sb


def _order_key(v):
    bits = lax.bitcast_convert_type(v, I32)
    return bits ^ ((bits >> 31) & 0x7FFFFFFF)


def _stack_rows(x, n):
    return jnp.concatenate([x] * n, axis=0)


def _with_ones(v):
    return jnp.concatenate([v.astype(BF16), jnp.ones(v.shape, BF16)], axis=1)


def _flash_update(s, v_ext, m_ref, acc_ref, rows):
    m_prev = m_ref[rows]
    sb = s.astype(BF16)
    m_new = jnp.maximum(m_prev, jnp.max(sb, axis=1, keepdims=True).astype(F32))
    alpha = jnp.exp2(m_prev - m_new)
    p = jnp.exp2(sb - jnp.concatenate([m_new.astype(BF16)] * (s.shape[1] // LANES), axis=1))
    acc_ref[rows] = jnp.concatenate([alpha, alpha], axis=1) * acc_ref[rows] + _nn(p, v_ext)
    m_ref[rows] = m_new


def _unrolled_loop(n, step, unroll=UNROLL):
    def body(j, carry):
        for u in range(unroll):
            step(j * unroll + u)
        return carry

    lax.fori_loop(0, n // unroll, body, 0)
    base = (n // unroll) * unroll
    p = unroll // 2
    while p >= 1:
        def tail(base=base, p=p):
            for u in range(p):
                step(base + u)

        pl.when((n & p) != 0)(tail)
        base = base + (n & p)
        p //= 2


def _flash_init(m_ref, acc_ref):
    m_ref[...] = jnp.full(m_ref.shape, M_INIT, F32)
    acc_ref[...] = jnp.zeros(acc_ref.shape, F32)


def _flash_result(acc_ref):
    acc = acc_ref[...]
    return acc[:, 0:LANES] / jnp.maximum(acc[:, LANES:2 * LANES], 1e-30)


def _place_q_heads(y, bd, gain, cos, sa, sb, out_ref, scale):
    lo_half = lax.broadcasted_iota(I32, (y.shape[0], LANES), 1) < HEAD_DIM
    for j in range(4):
        s = y[:, LANES * j:LANES * (j + 1)]
        s = _rope(_seg_norm(s, bd, gain, HEAD_DIM), cos, sa, sb, HEAD_DIM // 2) * scale
        r = pltpu.roll(s, HEAD_DIM, 1)
        if j < 2:
            out_ref[2 * j] = jnp.where(lo_half, s, 0.0).astype(BF16)
            out_ref[2 * j + 1] = jnp.where(lo_half, r, 0.0).astype(BF16)
        else:
            out_ref[2 * j] = jnp.where(lo_half, 0.0, r).astype(BF16)
            out_ref[2 * j + 1] = jnp.where(lo_half, 0.0, s).astype(BF16)


def _gated_store(o, z_ref, o_ref, tq, row0=0):
    lo_half = lax.broadcasted_iota(I32, (tq, LANES), 1) < HEAD_DIM
    for j in range(4):
        a = o[(2 * j) * tq:(2 * j + 1) * tq]
        b = o[(2 * j + 1) * tq:(2 * j + 2) * tq]
        if j < 2:
            slab = jnp.where(lo_half, a, pltpu.roll(b, HEAD_DIM, 1))
        else:
            slab = jnp.where(lo_half, pltpu.roll(a, HEAD_DIM, 1), b)
        z = z_ref[row0:row0 + tq, LANES * j:LANES * (j + 1)]
        o_ref[row0:row0 + tq, LANES * j:LANES * (j + 1)] = (slab * _silu(z)).astype(BF16)


def _proj_ab_kernel(x_ref, g_ref, w_ref, cos_ref, sa_ref, sb_ref, bd_ref, gains_ref,
                    qa_ref, ka_ref, va_ref, qi_ref, ki_ref, wi_ref, za_ref,
                    qb_ref, kb_ref, vb_ref, zb_ref):
    xn = _row_rms(x_ref[...], g_ref[...]).astype(BF16)
    cos, sa, sb, bd = cos_ref[...], sa_ref[...], sb_ref[...], bd_ref[...]
    lo_half = lax.broadcasted_iota(I32, (TM, LANES), 1) < HEAD_DIM

    def proj(c0, n):
        return _nn(xn, w_ref[:, c0:c0 + n])

    def k_slab(c0, gain):
        return _rope(_seg_norm(proj(c0, LANES), bd, gain, HEAD_DIM), cos, sa, sb, HEAD_DIM // 2)

    _place_q_heads(proj(0, 512), bd, gains_ref[0:1, :], cos, sa, sb, qa_ref, Q_SCALE)
    ka_ref[...] = k_slab(512, gains_ref[1:2, :]).astype(BF16)
    va_ref[...] = _with_ones(proj(640, LANES))
    qi = proj(768, 256)
    for j in range(2):
        s = _rope(qi[:, LANES * j:LANES * (j + 1)], cos, sa, sb, HEAD_DIM // 2)
        qi_ref[2 * j] = jnp.where(lo_half, s, 0.0).astype(BF16)
        qi_ref[2 * j + 1] = jnp.where(lo_half, pltpu.roll(s, HEAD_DIM, 1), 0.0).astype(BF16)
    kiw = proj(1024, LANES)
    wi_ref[...] = kiw
    ki_ref[...] = _rope(_seg_norm(kiw, bd, gains_ref[2:3, :], HEAD_DIM), cos, sa, sb, HEAD_DIM // 2).astype(BF16)
    za_ref[...] = proj(1152, 512)
    _place_q_heads(proj(1664, 512), bd, gains_ref[3:4, :], cos, sa, sb, qb_ref, Q_SCALE)
    kb_ref[...] = k_slab(2176, gains_ref[4:5, :]).astype(BF16)
    vb_ref[...] = _with_ones(proj(2304, LANES))
    zb_ref[...] = proj(2432, 512)


def _count_chunks(keys_ref, nch, nq, pred):
    def body(c, acc):
        return acc + jnp.sum(pred(keys_ref[c]).reshape(CH // COUNT_ROWS, COUNT_ROWS, nq), axis=0)

    acc = lax.fori_loop(0, nch, body, jnp.zeros((COUNT_ROWS, nq), F32))
    return jnp.sum(acc, axis=0, keepdims=True)


def _kth_threshold(count, nq, k):
    def bit_body(it, carry):
        tb, above = carry
        cand_b = tb | jnp.left_shift(jnp.int32(1), 31 - it)
        cand_s = cand_b ^ INT_MIN
        cnt = count(lambda kk: jnp.where(kk >= cand_s, 1.0, 0.0))
        ok = cnt >= k
        return jnp.where(ok, cand_b, tb), jnp.where(ok, above, cnt)

    tb, above = lax.fori_loop(0, 32, bit_body, (jnp.zeros((1, nq), I32), jnp.zeros((1, nq), F32)))
    return tb ^ INT_MIN, k - above


def _dsa_kernel(qi_ref, wi_ref, ki_ref, qa_ref, ka_ref, va_ref, za_ref, tril_ref, o_ref,
                keys_ref, bias_ref, ties_ref, m_ref, acc_ref, *, topk):
    TQ = TQ_ATT
    i = pl.program_id(1)
    t0 = i * TQ
    nch = (t0 + TQ + CH - 1) // CH
    key_c = lax.broadcasted_iota(I32, (CH, TQ), 0)
    qry_c = t0 + lax.broadcasted_iota(I32, (CH, TQ), 1)

    qi = qi_ref[...].reshape(IDX_HEADS * TQ, LANES)
    w_t = wi_ref[...].T * ((HEAD_DIM * IDX_HEADS) ** -0.5)
    ws = [w_t[HEAD_DIM + h:HEAD_DIM + h + 1, :] for h in range(IDX_HEADS)]

    def score_chunk(c):
        off = pl.multiple_of(c * CH, CH)
        lg = _nt(ki_ref[pl.ds(off, CH), :], qi)
        sc = ws[0] * jnp.maximum(lg[:, 0:TQ], 0.0)
        for h in range(1, IDX_HEADS):
            sc = sc + ws[h] * jnp.maximum(lg[:, h * TQ:(h + 1) * TQ], 0.0)
        sc = jnp.where(sc == 0.0, 0.0, sc)
        keys_ref[c] = jnp.where(key_c + off <= qry_c, _order_key(sc), INT_MIN)

    _unrolled_loop(nch, score_chunk, unroll=2)

    count = functools.partial(_count_chunks, keys_ref, nch, TQ)
    thr, need = _kth_threshold(count, TQ, float(topk))

    tril = tril_ref[...]

    ties_ref[...] = jnp.zeros(ties_ref.shape, F32)

    def select_chunk(c):
        kk = keys_ref[c]
        pref = _nn(tril, jnp.where(kk == thr, 1.0, 0.0).astype(BF16)) + ties_ref[0:1, :]
        ties_ref[0:1, :] = pref[CH - 1:CH, :]
        tie = jnp.where(kk == thr, jnp.where(pref <= need, 0.0, NEG_INF), NEG_INF)
        bias = jnp.where(key_c + c * CH <= qry_c, jnp.where(kk > thr, 0.0, tie), NEG_INF)
        bias_ref[c] = bias.T.astype(BF16)

    _unrolled_loop(nch, select_chunk)

    _flash_init(m_ref, acc_ref)
    q_all = qa_ref[...].reshape(N_HEADS * TQ, LANES)

    def att_chunk(c):
        off = pl.multiple_of(c * CH, CH)
        s = _nt(q_all, ka_ref[pl.ds(off, CH), :]).astype(BF16) + _stack_rows(bias_ref[c], N_HEADS)
        _flash_update(s, va_ref[pl.ds(off, CH), :], m_ref, acc_ref, slice(None))

    _unrolled_loop(nch, att_chunk)
    _gated_store(_flash_result(acc_ref), za_ref, o_ref, TQ)


def _swa_kernel(q_ref, kp_ref, kc_ref, vp_ref, vc_ref, sink_ref, z_ref, o_ref):
    i = pl.program_id(1)
    qi = lax.broadcasted_iota(I32, (TQ, 2 * TQ), 0)
    kj = lax.broadcasted_iota(I32, (TQ, 2 * TQ), 1)
    sink = sink_ref[...]
    for half in range(2):
        q = q_ref[:, half * TQ:(half + 1) * TQ, :].reshape(N_HEADS * TQ, LANES)
        if half == 0:
            k2 = jnp.concatenate([kp_ref[...], kc_ref[0:TQ, :]], axis=0)
            v2 = jnp.concatenate([vp_ref[...], vc_ref[0:TQ, :]], axis=0)
            lo = jnp.where(i > 0, 0, TQ)
        else:
            k2, v2, lo = kc_ref[...], vc_ref[...], 0
        keep = jnp.where(kj >= lo, jnp.where(kj > qi, jnp.where(kj <= qi + TQ, 0.0, NEG_INF), NEG_INF), NEG_INF)
        s = _nt(q, k2) + _stack_rows(keep, N_HEADS)
        m = jnp.maximum(jnp.max(s, axis=1, keepdims=True), sink)
        p = jnp.exp2(s - jnp.concatenate([m, m], axis=1))
        o = _nn(p.astype(BF16), v2)
        den = o[:, LANES:2 * LANES] + jnp.exp2(sink - m)
        _gated_store(o[:, 0:LANES] / den, z_ref, o_ref, TQ, half * TQ)


def _out_proj_kernel(x_ref, ma_ref, mb_ref, w_ref, o_ref):
    half = ma_ref.shape[1]
    o_ref[...] = x_ref[...] + _nn(ma_ref[...], w_ref[0:half, :]) + _nn(mb_ref[...], w_ref[half:2 * half, :])


def _proj_cd_kernel(x_ref, g_ref, w_ref, cos_ref, sa_ref, sb_ref, bd_ref, gains_ref,
                    qc_ref, kc_ref, vc_ref, ks_ref, vs_ref, kw_ref, vw_ref, gt_ref, zc_ref,
                    cq_ref, ckv_ref, kr_ref, zd_ref):
    xn = _row_rms(x_ref[...], g_ref[...]).astype(BF16)
    cos, sa, sb, bd = cos_ref[...], sa_ref[...], sb_ref[...], bd_ref[...]

    def proj(c0, n):
        return _nn(xn, w_ref[:, c0:c0 + n])

    def k_slab(c0, gain):
        return _rope(_seg_norm(proj(c0, LANES), bd, gain, HEAD_DIM), cos, sa, sb, HEAD_DIM // 2)

    _place_q_heads(proj(0, 512), bd, gains_ref[0:1, :], cos, sa, sb, qc_ref, Q_SCALE)
    kc_ref[...] = proj(512, LANES)
    vc_ref[...] = proj(640, LANES)
    ks_ref[...] = k_slab(768, gains_ref[1:2, :]).astype(BF16)
    vs_ref[...] = _with_ones(proj(896, LANES))
    kw_ref[...] = k_slab(1024, gains_ref[2:3, :]).astype(BF16)
    vw_ref[...] = _with_ones(proj(1152, LANES))
    gt_ref[...] = _sigmoid(proj(1280, LANES))
    zc_ref[...] = proj(1408, 512)
    cq_ref[...] = proj(1920, 256)
    ckv_ref[...] = proj(2176, LANES)
    kr_ref[...] = proj(2304, LANES)
    zd_ref[...] = proj(2432, 512)


def _compress_kernel(c_ref, pe_ref, w1a_ref, w1b_ref, w2_ref, gain_ref, cos_ref, sa_ref, sb_ref, bd_ref,
                     o_ref, *, is_key):
    c = c_ref[...]
    n = c.shape[0]
    a = _nn((c + pe_ref[0:1, :]).astype(BF16), w1a_ref[...])
    b = _nn((c + pe_ref[1:2, :]).astype(BF16), w1b_ref[...])
    h = _silu(a + pltpu.roll(b, n - 1, 0))
    y = _nn(h.astype(BF16), w2_ref[...])
    if is_key:
        y = _rope(_seg_norm(y, bd_ref[...], gain_ref[...], HEAD_DIM), cos_ref[...], sa_ref[...], sb_ref[...],
                  HEAD_DIM // 2)
    o_ref[...] = y.astype(BF16)


def _nsa_kernel(q_ref, kc_ref, vc_ref, ks_ref, vs_ref, kw_ref, vw_ref, gt_ref, z_ref, wsel_ref, o_ref,
                m_ref, acc_ref, *, n_sel):
    TQ = TQ_ATT
    i = pl.program_id(1)
    t0 = i * TQ
    nch = (t0 + TQ + CH - 1) // CH
    ncmp = kc_ref.shape[0]
    rows_all = N_HEADS * TQ
    q = q_ref[...].reshape(rows_all, LANES)

    cend = lax.broadcasted_iota(I32, (TQ, ncmp), 1) * CMP_STRIDE + (2 * CMP_STRIDE - 1)
    bias_c = jnp.where(cend <= t0 + lax.broadcasted_iota(I32, (TQ, ncmp), 0), 0.0, NEG_INF)
    s = _nt(q, kc_ref[...]) + _stack_rows(bias_c, N_HEADS)
    m = jnp.max(s, axis=1, keepdims=True)
    m = jnp.where(m > NEG_INF, m, 0.0)
    p = jnp.exp2(s - m)
    p = p / jnp.maximum(jnp.sum(p, axis=1, keepdims=True), 1e-30)
    o_cmp = _nn(p.astype(BF16), vc_ref[...])

    span = NSA_WINDOW + TQ
    start = pl.multiple_of(jnp.maximum(t0 - NSA_WINDOW, 0), TQ)
    diff = t0 + lax.broadcasted_iota(I32, (TQ, span), 0) - (start + lax.broadcasted_iota(I32, (TQ, span), 1))
    bias_w = jnp.where(diff >= 0, jnp.where(diff < NSA_WINDOW, 0.0, NEG_INF), NEG_INF)
    sw = _nt(q, kw_ref[pl.ds(start, span), :]).astype(BF16) + _stack_rows(bias_w.astype(BF16), N_HEADS)
    pw = jnp.exp2(sw - jnp.max(sw, axis=1, keepdims=True))
    o_win = _nn(pw, vw_ref[pl.ds(start, span), :])
    o_win = o_win[:, 0:LANES] / o_win[:, LANES:2 * LANES]

    blk = lax.broadcasted_iota(I32, (LANES, TQ), 0)
    cur = (t0 + lax.broadcasted_iota(I32, (LANES, TQ), 1)) >> 6
    wsel_t = wsel_ref[...]
    keys = []
    for g in range(N_KV):
        imp = p[(GRP * g) * TQ:(GRP * g + 1) * TQ]
        for r in range(1, GRP):
            imp = imp + p[(GRP * g + r) * TQ:(GRP * g + r + 1) * TQ]
        hi = imp.astype(BF16)
        lo = (imp - hi.astype(F32)).astype(BF16)
        imp_s = _nt(wsel_t, hi) + _nt(wsel_t, lo)
        forced = jnp.where(blk == 0, jnp.inf, jnp.where(blk >= cur - 1, jnp.inf, imp_s))
        keys.append(_order_key(jnp.where(blk <= cur, forced, NEG_INF)))
    kk = jnp.concatenate(keys, axis=1)

    def bit_body(it, tb):
        cand_b = tb | jnp.left_shift(jnp.int32(1), 31 - it)
        cnt = jnp.sum(jnp.where(kk >= (cand_b ^ INT_MIN), 1.0, 0.0), axis=0, keepdims=True)
        return jnp.where(cnt >= n_sel, cand_b, tb)

    thr = lax.fori_loop(0, 32, bit_body, jnp.zeros((1, N_KV * TQ), I32)) ^ INT_MIN
    need = n_sel - jnp.sum(jnp.where(kk > thr, 1.0, 0.0), axis=0, keepdims=True)
    tril = jnp.where(lax.broadcasted_iota(I32, (LANES, LANES), 0) >= lax.broadcasted_iota(I32, (LANES, LANES), 1),
                     1.0, 0.0).astype(BF16)
    pref = _nn(tril, jnp.where(kk == thr, 1.0, 0.0).astype(BF16))
    tie = jnp.where(kk == thr, jnp.where(pref <= need, 1.0, 0.0), 0.0)
    sel_t = jnp.where(kk > thr, 1.0, tie)
    sel = jnp.concatenate([sel_t[:, g * TQ:(g + 1) * TQ].T for g in range(N_KV)], axis=0).astype(BF16)

    _flash_init(m_ref, acc_ref)
    blk_row =lax.broadcasted_iota(I32, (LANES, CH), 0)
    blk_col = lax.broadcasted_iota(I32, (LANES, CH), 1) >> 6
    row_c = t0 + lax.broadcasted_iota(I32, (TQ, CH), 0)
    col_c = lax.broadcasted_iota(I32, (TQ, CH), 1)

    def slc_chunk(c):
        off = pl.multiple_of(c * CH, CH)
        expand = jnp.where(blk_row == blk_col + c * (CH // SLC_BLOCK), 1.0, 0.0).astype(BF16)
        tok = _nn(sel, expand)
        causal = col_c + off <= row_c
        parts = []
        for g in range(N_KV):
            bias_g = jnp.where(causal, jnp.where(tok[g * TQ:(g + 1) * TQ] > 0.5, 0.0, NEG_INF), NEG_INF)
            parts.extend([bias_g.astype(BF16)] * GRP)
        sc = _nt(q, ks_ref[pl.ds(off, CH), :]).astype(BF16) + jnp.concatenate(parts, axis=0)
        _flash_update(sc, vs_ref[pl.ds(off, CH), :], m_ref, acc_ref, slice(None))

    _unrolled_loop(nch, slc_chunk)
    o_slc = _flash_result(acc_ref)

    gates = gt_ref[...]
    outs = []
    for h in range(N_HEADS):
        rs = slice(h * TQ, (h + 1) * TQ)
        outs.append(gates[:, 3 * h:3 * h + 1] * o_cmp[rs] + gates[:, 3 * h + 1:3 * h + 2] * o_slc[rs]
                    + gates[:, 3 * h + 2:3 * h + 3] * o_win[rs])
    _gated_store(jnp.concatenate(outs, axis=0), z_ref, o_ref, TQ)


def _mla_prep_kernel(cq_ref, ckv_ref, kr_ref, wq_ref, wkv_ref, lat_ref, gains_ref, cos_ref, sa_ref, sb_ref,
                     bd64_ref, bd32_ref, q_ref, k_ref, v_ref):
    lane = lax.broadcasted_iota(I32, (TM, LANES), 1)
    lo_half = lane < NOPE_DIM
    cos, sa, sb = cos_ref[...], sa_ref[...], sb_ref[...]
    bd64, bd32 = bd64_ref[...], bd32_ref[...]
    scale = (NOPE_DIM + ROPE_DIM) ** -0.5 * LOG2E

    def rope_slab(s, gain):
        return _rope(_seg_norm(s, bd32, gain, ROPE_DIM), cos, sa, sb, ROPE_DIM // 2)

    q = _nn(_row_rms(cq_ref[...], lat_ref[0:1, :]).astype(BF16), wq_ref[...])
    kv = _nn(_row_rms(ckv_ref[...], lat_ref[1:2, 0:KV_LORA]).astype(BF16), wkv_ref[...])
    k_rope = pltpu.roll(rope_slab(kr_ref[...], gains_ref[3:4, :]), NOPE_DIM, 1)
    q_rope = [rope_slab(q[:, 512 + LANES * j:512 + LANES * (j + 1)], gains_ref[1:2, :]) * scale for j in range(2)]
    for j in range(4):
        qn = _seg_norm(q[:, LANES * j:LANES * (j + 1)], bd64, gains_ref[0:1, :], NOPE_DIM) * scale
        kn = _seg_norm(kv[:, LANES * j:LANES * (j + 1)], bd64, gains_ref[2:3, :], NOPE_DIM)
        for e in range(2):
            h = 2 * j + e
            qn_h = qn if e == 0 else pltpu.roll(qn, NOPE_DIM, 1)
            kn_h = kn if e == 0 else pltpu.roll(kn, NOPE_DIM, 1)
            shift = (NOPE_DIM - ROPE_DIM * (h % 4)) % LANES
            qr = q_rope[h // 4]
            qr_h = qr if shift == 0 else pltpu.roll(qr, shift, 1)
            q_ref[h] = jnp.where(lo_half, qn_h, jnp.where(lane < NOPE_DIM + ROPE_DIM, qr_h, 0.0)).astype(BF16)
            k_ref[h] = jnp.where(lo_half, kn_h, k_rope).astype(BF16)
    for j in range(4):
        v_ref[j] = _with_ones(kv[:, 512 + LANES * j:512 + LANES * (j + 1)])


def _mla_kernel(q_ref, k_ref, v_ref, z_ref, o_ref, m_ref, acc_ref):
    i = pl.program_id(2)
    tq = TQ_MLA
    _flash_init(m_ref, acc_ref)

    def step(c, masked):
        off = pl.multiple_of(c * CH, CH)
        v = v_ref[0, pl.ds(off, CH), :]
        for e in range(2):
            k = k_ref[e, pl.ds(off, CH), :]
            for rb in range(tq // ROW_BLOCK):
                s = _nt(q_ref[e, rb * ROW_BLOCK:(rb + 1) * ROW_BLOCK, :], k)
                if masked:
                    qpos = rb * ROW_BLOCK + lax.broadcasted_iota(I32, (ROW_BLOCK, CH), 0)
                    s = jnp.where(lax.broadcasted_iota(I32, (ROW_BLOCK, CH), 1) <= qpos, s, NEG_INF)
                _flash_update(s, v, m_ref, acc_ref, slice(e * tq + rb * ROW_BLOCK, e * tq + (rb + 1) * ROW_BLOCK))

    _unrolled_loop(i, lambda c: step(c, False))
    step(i, True)
    o = _flash_result(acc_ref)
    lo_half = lax.broadcasted_iota(I32, (tq, LANES), 1) < NOPE_DIM
    slab = jnp.where(lo_half, o[0:tq], o[tq:2 * tq])
    o_ref[...] = (slab * _silu(z_ref[...])).astype(BF16)


def _rope_tables(pos, dim):
    inv = jnp.power(jnp.float32(ROPE_THETA), -jnp.arange(0, dim, 2, dtype=F32) / dim)
    ang = pos.astype(F32)[:, None] * inv[None, :]
    cos, sin = jnp.cos(ang), jnp.sin(ang)
    reps = LANES // dim
    zero = jnp.zeros_like(sin)
    cos_t = jnp.tile(jnp.concatenate([cos, cos], axis=1), (1, reps))
    sa_t = jnp.tile(jnp.concatenate([zero, sin], axis=1), (1, reps))
    sb_t = jnp.tile(jnp.concatenate([-sin, zero], axis=1), (1, reps))
    return cos_t, sa_t, sb_t


def _block_diag(group):
    r = jnp.arange(LANES) // group
    bd = (r[:, None] == r[None, :]).astype(BF16)
    return jnp.concatenate([bd, bd], axis=0)


def _tile_gain(g, width=LANES):
    return jnp.tile(g.astype(F32), width // g.shape[0])


def _pad_lanes(g, width=LANES):
    return jnp.concatenate([g.astype(F32), jnp.zeros((width - g.shape[0],), F32)])


def _pad_cols(w, width):
    return jnp.concatenate([w, jnp.zeros((w.shape[0], width - w.shape[1]), w.dtype)], axis=1)


def _params(*sem, flags=None):
    return pltpu.CompilerParams(dimension_semantics=sem, vmem_limit_bytes=VMEM_LIMIT, flags=flags)


def kernel(x, ab_norm, ab_w_in, a_qk_norm, a_kidx_norm, b_qk_norm, b_sinks, ab_w_out, cd_norm, cd_w_in, c_q_norm, c_k_norm, c_cmp_pe, c_cmp_w1, c_cmp_w2, d_q_lat_norm, d_kv_lat_norm, d_w_uq, d_w_ukv, d_nope_norm, d_rope_norm, cd_w_out):
    bsz, seq, d = x.shape
    n = bsz * seq
    assert d == D_MODEL and seq % CH == 0 and seq >= NSA_WINDOW + TQ_ATT and seq // SLC_BLOCK <= LANES
    nq = seq // TQ
    nrow = n // TM
    srow = seq // TM
    xf = x.reshape(n, d)
    pos = jnp.arange(seq)
    cos64, sa64, sb64 = _rope_tables(pos, HEAD_DIM)
    cos32, sa32, sb32 = _rope_tables(pos, ROPE_DIM)
    bd64, bd32 = _block_diag(HEAD_DIM), _block_diag(ROPE_DIM)
    tril = (jnp.arange(CH)[:, None] >= jnp.arange(CH)[None, :]).astype(BF16)

    def row_spec(width, rows=TM):
        return pl.BlockSpec((rows, width), lambda i: (i, 0))

    def const_spec(shape):
        return pl.BlockSpec(shape, lambda i: tuple(0 for _ in shape))

    tab_spec = pl.BlockSpec((TM, LANES), lambda i: (i % srow, 0))
    head_spec = lambda nh: pl.BlockSpec((nh, TM, LANES), lambda i: (0, i, 0))

    wa = ab_w_in[0]
    w_ab = jnp.concatenate([wa[:, 0:1088], _pad_cols(wa[:, 1088:1092], 64), wa[:, 1092:]], axis=1).astype(BF16)
    gains_ab = jnp.stack([_tile_gain(a_qk_norm[0, 0]), _tile_gain(a_qk_norm[0, 1]), _pad_lanes(a_kidx_norm[0]),
                          _tile_gain(b_qk_norm[0, 0]), _tile_gain(b_qk_norm[0, 1])]
                         + [jnp.zeros((LANES,), F32)] * 3)
    sds = jax.ShapeDtypeStruct
    qa, ka, va, qi, ki, wi, za, qb, kb, vb, zb = pl.pallas_call(
        _proj_ab_kernel,
        grid=(nrow,),
        in_specs=[row_spec(d), const_spec((1, d)), const_spec((d, AB_W)), tab_spec, tab_spec, tab_spec,
                  const_spec((2 * LANES, LANES)), const_spec((8, LANES))],
        out_specs=[head_spec(8), row_spec(LANES), row_spec(2 * LANES), head_spec(4), row_spec(LANES), row_spec(LANES),
                   row_spec(512), head_spec(8), row_spec(LANES), row_spec(2 * LANES), row_spec(512)],
        out_shape=[sds((8, n, LANES), BF16), sds((n, LANES), BF16), sds((n, 2 * LANES), BF16),
                   sds((4, n, LANES), BF16), sds((n, LANES), BF16), sds((n, LANES), F32), sds((n, 512), F32),
                   sds((8, n, LANES), BF16), sds((n, LANES), BF16), sds((n, 2 * LANES), BF16), sds((n, 512), F32)],
        compiler_params=_params("parallel"),
    )(xf, ab_norm[0][None, :], w_ab, cos64, sa64, sb64, bd64, gains_ab)

    nqa = seq // TQ_ATT
    qt_heads = lambda nh: pl.BlockSpec((nh, TQ_ATT, LANES), lambda b, i: (0, b * nqa + i, 0))
    qt_rows = lambda width: pl.BlockSpec((TQ_ATT, width), lambda b, i: (b * nqa + i, 0))
    seq_rows = pl.BlockSpec((seq, LANES), lambda b, i: (b, 0))
    seq_rows_v = pl.BlockSpec((seq, 2 * LANES), lambda b, i: (b, 0))
    att_scratch = [pltpu.VMEM((N_HEADS * TQ_ATT, LANES), F32), pltpu.VMEM((N_HEADS * TQ_ATT, 2 * LANES), F32)]

    mix_a = pl.pallas_call(
        functools.partial(_dsa_kernel, topk=min(DSA_TOPK, seq // 4)),
        grid=(bsz, nqa),
        in_specs=[qt_heads(4), qt_rows(LANES), seq_rows, qt_heads(8), seq_rows, seq_rows_v, qt_rows(512),
                  pl.BlockSpec((CH, CH), lambda b, i: (0, 0))],
        out_specs=qt_rows(512),
        out_shape=sds((n, 512), BF16),
        scratch_shapes=[pltpu.VMEM((seq // CH, CH, TQ_ATT), I32), pltpu.VMEM((seq // CH, TQ_ATT, CH), BF16),
                        pltpu.VMEM((8, TQ_ATT), F32)] + att_scratch,
        compiler_params=_params("parallel", "arbitrary"),
    )(qi, wi, ki, qa, ka, va, za, tril)

    nq2 = nq // 2
    prev_rows = lambda width: pl.BlockSpec((TQ, width), lambda b, i: (b * nq + jnp.maximum(2 * i - 1, 0), 0))
    pair_rows = lambda width: pl.BlockSpec((2 * TQ, width), lambda b, i: (b * nq2 + i, 0))
    sink_rows = jnp.broadcast_to(jnp.repeat(b_sinks[0].astype(F32) * LOG2E, TQ)[:, None], (N_HEADS * TQ, LANES))
    mix_b = pl.pallas_call(
        _swa_kernel,
        grid=(bsz, nq2),
        in_specs=[pl.BlockSpec((N_HEADS, 2 * TQ, LANES), lambda b, i: (0, b * nq2 + i, 0)),
                  prev_rows(LANES), pair_rows(LANES), prev_rows(2 * LANES), pair_rows(2 * LANES),
                  pl.BlockSpec((N_HEADS * TQ, LANES), lambda b, i: (0, 0)), pair_rows(512)],
        out_specs=pair_rows(512),
        out_shape=sds((n, 512), BF16),
        compiler_params=_params("parallel", "arbitrary"),
    )(qb, kb, kb, vb, vb, sink_rows, zb)

    def out_proj(xin, m0, m1, w):
        return pl.pallas_call(
            _out_proj_kernel,
            grid=(nrow,),
            in_specs=[row_spec(d), row_spec(512), row_spec(512), const_spec((2 * 512, d))],
            out_specs=row_spec(d),
            out_shape=sds((n, d), F32),
            compiler_params=_params("parallel"),
        )(xin, m0, m1, w.astype(BF16))

    x1 = out_proj(xf, mix_a, mix_b, ab_w_out[0])

    wc = cd_w_in[0]
    w_cd = jnp.concatenate([wc[:, 0:1280], _pad_cols(wc[:, 1280:1304], LANES), wc[:, 1304:2200],
                            _pad_cols(wc[:, 2200:2232], LANES), wc[:, 2232:]], axis=1).astype(BF16)
    gains_cd = jnp.stack([_tile_gain(c_q_norm[0]), _tile_gain(c_k_norm[0, 1]), _tile_gain(c_k_norm[0, 2])]
                         + [jnp.zeros((LANES,), F32)] * 5)
    qc, kc_raw, vc_raw, ks, vs, kw, vw, gates, zc, cq, ckv, kr, zd = pl.pallas_call(
        _proj_cd_kernel,
        grid=(nrow,),
        in_specs=[row_spec(d), const_spec((1, d)), const_spec((d, AB_W)), tab_spec, tab_spec, tab_spec,
                  const_spec((2 * LANES, LANES)), const_spec((8, LANES))],
        out_specs=[head_spec(8), row_spec(LANES), row_spec(LANES), row_spec(LANES), row_spec(2 * LANES),
                   row_spec(LANES), row_spec(2 * LANES), row_spec(LANES), row_spec(512), row_spec(256),
                   row_spec(LANES), row_spec(LANES), row_spec(512)],
        out_shape=[sds((8, n, LANES), BF16), sds((n, LANES), F32), sds((n, LANES), F32), sds((n, LANES), BF16),
                   sds((n, 2 * LANES), BF16), sds((n, LANES), BF16), sds((n, 2 * LANES), BF16), sds((n, LANES), F32),
                   sds((n, 512), F32), sds((n, 256), F32), sds((n, LANES), F32), sds((n, LANES), F32),
                   sds((n, 512), F32)],
        compiler_params=_params("parallel"),
    )(x1, cd_norm[0][None, :], w_cd, cos64, sa64, sb64, bd64, gains_cd)

    ncmp = seq // CMP_STRIDE
    cw = CMP_STRIDE * N_KV * HEAD_DIM
    eye = jnp.eye(N_KV, dtype=F32)
    ccos, csa, csb = _rope_tables(jnp.arange(ncmp) * CMP_STRIDE + (2 * CMP_STRIDE - 1), HEAD_DIM)

    def compress(raw, which, is_key):
        w1 = c_cmp_w1[0, which].reshape(2, CMP_STRIDE, HEAD_DIM, -1)
        hid = w1.shape[-1]
        w1e = jnp.einsum("tjdu,gh->tjgdhu", w1, eye).reshape(2, cw, N_KV * hid).astype(BF16)
        w2e = jnp.einsum("ud,gh->guhd", c_cmp_w2[0, which], eye).reshape(N_KV * hid, N_KV * HEAD_DIM).astype(BF16)
        pe = c_cmp_pe[0, which].reshape(2, CMP_STRIDE, 1, HEAD_DIM)
        pe = jnp.broadcast_to(pe, (2, CMP_STRIDE, N_KV, HEAD_DIM)).reshape(2, cw)
        return pl.pallas_call(
            functools.partial(_compress_kernel, is_key=is_key),
            grid=(bsz,),
            in_specs=[pl.BlockSpec((ncmp, cw), lambda b: (b, 0)), const_spec((2, cw)),
                      const_spec((cw, N_KV * hid)), const_spec((cw, N_KV * hid)),
                      const_spec((N_KV * hid, LANES)), const_spec((1, LANES)),
                      const_spec((ncmp, LANES)), const_spec((ncmp, LANES)), const_spec((ncmp, LANES)),
                      const_spec((2 * LANES, LANES))],
            out_specs=pl.BlockSpec((ncmp, LANES), lambda b: (b, 0)),
            out_shape=sds((bsz * ncmp, LANES), BF16),
            compiler_params=_params("parallel"),
        )(raw.reshape(bsz * ncmp, cw), pe, w1e[0], w1e[1], w2e, _tile_gain(c_k_norm[0, 0])[None, :],
          ccos, csa, csb, bd64)

    kc = compress(kc_raw, 0, True)
    vc = compress(vc_raw, 1, False)

    ratio = SLC_BLOCK // CMP_STRIDE
    mm = jnp.arange(ncmp)[:, None]
    jj = jnp.arange(LANES)[None, :]
    wsel = (jnp.where((mm == ratio * jj - 1) | (mm == ratio * jj + ratio - 1), 1.0, 0.0)
            + jnp.where((mm >= ratio * jj) & (mm < ratio * jj + ratio - 1), 2.0, 0.0))
    wsel_t = jnp.where((mm < ncmp - 1) & (jj < seq // SLC_BLOCK), wsel, 0.0).astype(BF16).T

    cmp_rows = pl.BlockSpec((ncmp, LANES), lambda b, i: (b, 0))
    mix_c = pl.pallas_call(
        functools.partial(_nsa_kernel, n_sel=min(SLC_TOPN, seq // SLC_BLOCK)),
        grid=(bsz, nqa),
        in_specs=[qt_heads(8), cmp_rows, cmp_rows, seq_rows, seq_rows_v, seq_rows, seq_rows_v, qt_rows(LANES),
                  qt_rows(512), pl.BlockSpec((LANES, ncmp), lambda b, i: (0, 0))],
        out_specs=qt_rows(512),
        out_shape=sds((n, 512), BF16),
        scratch_shapes=att_scratch,
        compiler_params=_params("parallel", "arbitrary"),
    )(qc, kc, vc, ks, vs, kw, vw, gates, zc, wsel_t)

    wq = d_w_uq[0].reshape(Q_LORA, N_HEADS, NOPE_DIM + ROPE_DIM)
    wq = jnp.concatenate([wq[:, :, :NOPE_DIM].reshape(Q_LORA, -1), wq[:, :, NOPE_DIM:].reshape(Q_LORA, -1)],
                         axis=1).astype(BF16)
    wkv = d_w_ukv[0].reshape(KV_LORA, N_HEADS, NOPE_DIM + HEAD_DIM)
    wkv = jnp.concatenate([wkv[:, :, :NOPE_DIM].reshape(KV_LORA, -1), wkv[:, :, NOPE_DIM:].reshape(KV_LORA, -1)],
                          axis=1).astype(BF16)
    lat_gains = jnp.stack([d_q_lat_norm[0].astype(F32), _pad_lanes(d_kv_lat_norm[0], Q_LORA)]
                          + [jnp.zeros((Q_LORA,), F32)] * 6)
    gains_d = jnp.stack([_tile_gain(d_nope_norm[0, 0]), _tile_gain(d_rope_norm[0, 0]), _tile_gain(d_nope_norm[0, 1]),
                         _pad_lanes(d_rope_norm[0, 1])] + [jnp.zeros((LANES,), F32)] * 4)
    q_cat, k_cat, v_d = pl.pallas_call(
        _mla_prep_kernel,
        grid=(nrow,),
        in_specs=[row_spec(Q_LORA), row_spec(LANES), row_spec(LANES), const_spec((Q_LORA, 768)),
                  const_spec((KV_LORA, 1024)), const_spec((8, Q_LORA)), const_spec((8, LANES)),
                  tab_spec, tab_spec, tab_spec, const_spec((2 * LANES, LANES)), const_spec((2 * LANES, LANES))],
        out_specs=[head_spec(8), head_spec(8), pl.BlockSpec((4, TM, 2 * LANES), lambda i: (0, i, 0))],
        out_shape=[sds((8, n, LANES), BF16), sds((8, n, LANES), BF16), sds((4, n, 2 * LANES), BF16)],
        compiler_params=_params("parallel"),
    )(cq, ckv, kr, wq, wkv, lat_gains, gains_d, cos32, sa32, sb32, bd64, bd32)

    nqm = seq // TQ_MLA
    mix_d = pl.pallas_call(
        _mla_kernel,
        grid=(bsz, N_HEADS // 2, nqm),
        in_specs=[pl.BlockSpec((2, TQ_MLA, LANES), lambda b, hp, i: (hp, b * nqm + i, 0)),
                  pl.BlockSpec((2, seq, LANES), lambda b, hp, i: (hp, b, 0)),
                  pl.BlockSpec((1, seq, 2 * LANES), lambda b, hp, i: (hp, b, 0)),
                  pl.BlockSpec((TQ_MLA, LANES), lambda b, hp, i: (b * nqm + i, hp))],
        out_specs=pl.BlockSpec((TQ_MLA, LANES), lambda b, hp, i: (b * nqm + i, hp)),
        out_shape=sds((n, 512), BF16),
        scratch_shapes=[pltpu.VMEM((2 * TQ_MLA, LANES), F32), pltpu.VMEM((2 * TQ_MLA, 2 * LANES), F32)],
        compiler_params=_params("parallel", "parallel", "arbitrary"),
    )(q_cat, k_cat, v_d, zd)

    x2 = out_proj(x1, mix_c, mix_d, cd_w_out[0])
    return x2.reshape(bsz, seq, d)
```

```python
import functools

import jax
import jax.numpy as jnp
from jax import lax
from jax.experimental import pallas as pl
from jax.experimental.pallas import tpu as pltpu

F32, BF16, I32 = jnp.float32, jnp.bfloat16, jnp.int32

D_MODEL = 1024
HEAD_DIM = 64
N_HEADS = 8
N_KV = 2
GRP = N_HEADS // N_KV
IDX_HEADS = 4
DSA_TOPK = 256
SWA_WINDOW = 128
CMP_STRIDE = 16
SLC_BLOCK = 64
SLC_TOPN = 16
NSA_WINDOW = 512
Q_LORA = 256
KV_LORA = 128
NOPE_DIM = 64
ROPE_DIM = 32
ROPE_THETA = 10000.0
EPS = 1e-6

LANES = 128
TM = 1024
TQ = 128
CH = 512
TQ_ATT = 256
TQ_MLA = 512
UNROLL = 4
COUNT_ROWS = 16
ROW_BLOCK = 128
VMEM_LIMIT = 56 * 1024 * 1024

NEG_INF = float("-inf")
M_INIT = -1e30
INT_MIN = -(2 ** 31)

AB_W = 2944
LOG2E = 1.4426950408889634
Q_SCALE = HEAD_DIM ** -0.5 * LOG2E


def _nn(a, b):
    return jnp.dot(a, b, preferred_element_type=F32)


def _nt(a, b):
    return lax.dot_general(a, b, (((1,), (1,)), ((), ())), preferred_element_type=F32)


def _sigmoid(z):
    return 1.0 / (1.0 + jnp.exp(-z))


def _silu(z):
    return z * _sigmoid(z)


def _row_rms(x, g):
    return x * lax.rsqrt(jnp.mean(x * x, axis=-1, keepdims=True) + EPS) * g


def _seg_norm(s, bd, gain, group):
    s2 = s * s
    hi = s2.astype(BF16)
    lo = (s2 - hi.astype(F32)).astype(BF16)
    ss = _nn(jnp.concatenate([hi, lo], axis=1), bd)
    return s * lax.rsqrt(ss * (1.0 / group) + EPS) * gain


def _rope(s, cos, sa, sb, half):
    return s * cos + pltpu.roll(s, half, 1) * sa + pltpu.roll(s, LANES - half, 1) * sb


def _order_key(v):
    bits = lax.bitcast_convert_type(v, I32)
    return bits ^ ((bits >> 31) & 0x7FFFFFFF)


def _stack_rows(x, n):
    return jnp.concatenate([x] * n, axis=0)


def _with_ones(v):
    return jnp.concatenate([v.astype(BF16), jnp.ones(v.shape, BF16)], axis=1)


def _flash_update(s, v_ext, m_ref, acc_ref, rows):
    m_prev = m_ref[rows]
    sb = s.astype(BF16)
    m_new = jnp.maximum(m_prev, jnp.max(sb, axis=1, keepdims=True).astype(F32))
    alpha = jnp.exp2(m_prev - m_new)
    p = jnp.exp2(sb - jnp.concatenate([m_new.astype(BF16)] * (s.shape[1] // LANES), axis=1))
    acc_ref[rows] = jnp.concatenate([alpha, alpha], axis=1) * acc_ref[rows] + _nn(p, v_ext)
    m_ref[rows] = m_new


def _unrolled_loop(n, step, unroll=UNROLL):
    def body(j, carry):
        for u in range(unroll):
            step(j * unroll + u)
        return carry

    lax.fori_loop(0, n // unroll, body, 0)
    base = (n // unroll) * unroll
    p = unroll // 2
    while p >= 1:
        def tail(base=base, p=p):
            for u in range(p):
                step(base + u)

        pl.when((n & p) != 0)(tail)
        base = base + (n & p)
        p //= 2


def _flash_init(m_ref, acc_ref):
    m_ref[...] = jnp.full(m_ref.shape, M_INIT, F32)
    acc_ref[...] = jnp.zeros(acc_ref.shape, F32)


def _flash_result(acc_ref):
    acc = acc_ref[...]
    return acc[:, 0:LANES] / jnp.maximum(acc[:, LANES:2 * LANES], 1e-30)


def _place_q_heads(y, bd, gain, cos, sa, sb, out_ref, scale):
    lo_half = lax.broadcasted_iota(I32, (y.shape[0], LANES), 1) < HEAD_DIM
    for j in range(4):
        s = y[:, LANES * j:LANES * (j + 1)]
        s = _rope(_seg_norm(s, bd, gain, HEAD_DIM), cos, sa, sb, HEAD_DIM // 2) * scale
        r = pltpu.roll(s, HEAD_DIM, 1)
        if j < 2:
            out_ref[2 * j] = jnp.where(lo_half, s, 0.0).astype(BF16)
            out_ref[2 * j + 1] = jnp.where(lo_half, r, 0.0).astype(BF16)
        else:
            out_ref[2 * j] = jnp.where(lo_half, 0.0, r).astype(BF16)
            out_ref[2 * j + 1] = jnp.where(lo_half, 0.0, s).astype(BF16)


def _gated_store(o, z_ref, o_ref, tq, row0=0):
    lo_half = lax.broadcasted_iota(I32, (tq, LANES), 1) < HEAD_DIM
    for j in range(4):
        a = o[(2 * j) * tq:(2 * j + 1) * tq]
        b = o[(2 * j + 1) * tq:(2 * j + 2) * tq]
        if j < 2:
            slab = jnp.where(lo_half, a, pltpu.roll(b, HEAD_DIM, 1))
        else:
            slab = jnp.where(lo_half, pltpu.roll(a, HEAD_DIM, 1), b)
        z = z_ref[row0:row0 + tq, LANES * j:LANES * (j + 1)]
        o_ref[row0:row0 + tq, LANES * j:LANES * (j + 1)] = (slab * _silu(z)).astype(BF16)


def _proj_ab_kernel(x_ref, g_ref, w_ref, cos_ref, sa_ref, sb_ref, bd_ref, gains_ref,
                    qa_ref, ka_ref, va_ref, qi_ref, ki_ref, wi_ref, za_ref,
                    qb_ref, kb_ref, vb_ref, zb_ref):
    xn = _row_rms(x_ref[...], g_ref[...]).astype(BF16)
    cos, sa, sb, bd = cos_ref[...], sa_ref[...], sb_ref[...], bd_ref[...]
    lo_half = lax.broadcasted_iota(I32, (TM, LANES), 1) < HEAD_DIM

    def proj(c0, n):
        return _nn(xn, w_ref[:, c0:c0 + n])

    def k_slab(c0, gain):
        return _rope(_seg_norm(proj(c0, LANES), bd, gain, HEAD_DIM), cos, sa, sb, HEAD_DIM // 2)

    _place_q_heads(proj(0, 512), bd, gains_ref[0:1, :], cos, sa, sb, qa_ref, Q_SCALE)
    ka_ref[...] = k_slab(512, gains_ref[1:2, :]).astype(BF16)
    va_ref[...] = _with_ones(proj(640, LANES))
    qi = proj(768, 256)
    for j in range(2):
        s = _rope(qi[:, LANES * j:LANES * (j + 1)], cos, sa, sb, HEAD_DIM // 2)
        qi_ref[2 * j] = jnp.where(lo_half, s, 0.0).astype(BF16)
        qi_ref[2 * j + 1] = jnp.where(lo_half, pltpu.roll(s, HEAD_DIM, 1), 0.0).astype(BF16)
    kiw = proj(1024, LANES)
    wi_ref[...] = kiw
    ki_ref[...] = _rope(_seg_norm(kiw, bd, gains_ref[2:3, :], HEAD_DIM), cos, sa, sb, HEAD_DIM // 2).astype(BF16)
    za_ref[...] = proj(1152, 512)
    _place_q_heads(proj(1664, 512), bd, gains_ref[3:4, :], cos, sa, sb, qb_ref, Q_SCALE)
    kb_ref[...] = k_slab(2176, gains_ref[4:5, :]).astype(BF16)
    vb_ref[...] = _with_ones(proj(2304, LANES))
    zb_ref[...] = proj(2432, 512)


def _count_chunks(keys_ref, nch, nq, pred):
    def body(c, acc):
        return acc + jnp.sum(pred(keys_ref[c]).reshape(CH // COUNT_ROWS, COUNT_ROWS, nq), axis=0)

    acc = lax.fori_loop(0, nch, body, jnp.zeros((COUNT_ROWS, nq), F32))
    return jnp.sum(acc, axis=0, keepdims=True)


def _kth_threshold(count, nq, k):
    def bit_body(it, carry):
        tb, above = carry
        cand_b = tb | jnp.left_shift(jnp.int32(1), 31 - it)
        cand_s = cand_b ^ INT_MIN
        cnt = count(lambda kk: jnp.where(kk >= cand_s, 1.0, 0.0))
        ok = cnt >= k
        return jnp.where(ok, cand_b, tb), jnp.where(ok, above, cnt)

    tb, above = lax.fori_loop(0, 32, bit_body, (jnp.zeros((1, nq), I32), jnp.zeros((1, nq), F32)))
    return tb ^ INT_MIN, k - above


def _dsa_kernel(qi_ref, wi_ref, ki_ref, qa_ref, ka_ref, va_ref, za_ref, tril_ref, o_ref,
                keys_ref, bias_ref, ties_ref, m_ref, acc_ref, *, topk):
    TQ = TQ_ATT
    i = pl.program_id(1)
    t0 = i * TQ
    nch = (t0 + TQ + CH - 1) // CH
    key_c = lax.broadcasted_iota(I32, (CH, TQ), 0)
    qry_c = t0 + lax.broadcasted_iota(I32, (CH, TQ), 1)

    qi = qi_ref[...].reshape(IDX_HEADS * TQ, LANES)
    w_t = wi_ref[...].T * ((HEAD_DIM * IDX_HEADS) ** -0.5)
    ws = [w_t[HEAD_DIM + h:HEAD_DIM + h + 1, :] for h in range(IDX_HEADS)]

    def score_chunk(c):
        off = pl.multiple_of(c * CH, CH)
        lg = _nt(ki_ref[pl.ds(off, CH), :], qi)
        sc = ws[0] * jnp.maximum(lg[:, 0:TQ], 0.0)
        for h in range(1, IDX_HEADS):
            sc = sc + ws[h] * jnp.maximum(lg[:, h * TQ:(h + 1) * TQ], 0.0)
        sc = jnp.where(sc == 0.0, 0.0, sc)
        keys_ref[c] = jnp.where(key_c + off <= qry_c, _order_key(sc), INT_MIN)

    _unrolled_loop(nch, score_chunk, unroll=2)

    count = functools.partial(_count_chunks, keys_ref, nch, TQ)
    thr, need = _kth_threshold(count, TQ, float(topk))

    tril = tril_ref[...]

    ties_ref[...] = jnp.zeros(ties_ref.shape, F32)

    def select_chunk(c):
        kk = keys_ref[c]
        pref = _nn(tril, jnp.where(kk == thr, 1.0, 0.0).astype(BF16)) + ties_ref[0:1, :]
        ties_ref[0:1, :] = pref[CH - 1:CH, :]
        tie = jnp.where(kk == thr, jnp.where(pref <= need, 0.0, NEG_INF), NEG_INF)
        bias = jnp.where(key_c + c * CH <= qry_c, jnp.where(kk > thr, 0.0, tie), NEG_INF)
        bias_ref[c] = bias.T.astype(BF16)

    _unrolled_loop(nch, select_chunk)

    _flash_init(m_ref, acc_ref)
    q_all = qa_ref[...].reshape(N_HEADS * TQ, LANES)

    def att_chunk(c):
        off = pl.multiple_of(c * CH, CH)
        s = _nt(q_all, ka_ref[pl.ds(off, CH), :]).astype(BF16) + _stack_rows(bias_ref[c], N_HEADS)
        _flash_update(s, va_ref[pl.ds(off, CH), :], m_ref, acc_ref, slice(None))

    _unrolled_loop(nch, att_chunk)
    _gated_store(_flash_result(acc_ref), za_ref, o_ref, TQ)


def _swa_kernel(q_ref, kp_ref, kc_ref, vp_ref, vc_ref, sink_ref, z_ref, o_ref):
    i = pl.program_id(1)
    qi = lax.broadcasted_iota(I32, (TQ, 2 * TQ), 0)
    kj = lax.broadcasted_iota(I32, (TQ, 2 * TQ), 1)
    sink = sink_ref[...]
    for half in range(2):
        q = q_ref[:, half * TQ:(half + 1) * TQ, :].reshape(N_HEADS * TQ, LANES)
        if half == 0:
            k2 = jnp.concatenate([kp_ref[...], kc_ref[0:TQ, :]], axis=0)
            v2 = jnp.concatenate([vp_ref[...], vc_ref[0:TQ, :]], axis=0)
            lo = jnp.where(i > 0, 0, TQ)
        else:
            k2, v2, lo = kc_ref[...], vc_ref[...], 0
        keep = jnp.where(kj >= lo, jnp.where(kj > qi, jnp.where(kj <= qi + TQ, 0.0, NEG_INF), NEG_INF), NEG_INF)
        s = _nt(q, k2) + _stack_rows(keep, N_HEADS)
        m = jnp.maximum(jnp.max(s, axis=1, keepdims=True), sink)
        p = jnp.exp2(s - jnp.concatenate([m, m], axis=1))
        o = _nn(p.astype(BF16), v2)
        den = o[:, LANES:2 * LANES] + jnp.exp2(sink - m)
        _gated_store(o[:, 0:LANES] / den, z_ref, o_ref, TQ, half * TQ)


def _out_proj_kernel(x_ref, ma_ref, mb_ref, w_ref, o_ref):
    half = ma_ref.shape[1]
    o_ref[...] = x_ref[...] + _nn(ma_ref[...], w_ref[0:half, :]) + _nn(mb_ref[...], w_ref[half:2 * half, :])


def _proj_cd_kernel(x_ref, g_ref, w_ref, cos_ref, sa_ref, sb_ref, bd_ref, gains_ref,
                    qc_ref, kc_ref, vc_ref, ks_ref, vs_ref, kw_ref, vw_ref, gt_ref, zc_ref,
                    cq_ref, ckv_ref, kr_ref, zd_ref):
    xn = _row_rms(x_ref[...], g_ref[...]).astype(BF16)
    cos, sa, sb, bd = cos_ref[...], sa_ref[...], sb_ref[...], bd_ref[...]

    def proj(c0, n):
        return _nn(xn, w_ref[:, c0:c0 + n])

    def k_slab(c0, gain):
        return _rope(_seg_norm(proj(c0, LANES), bd, gain, HEAD_DIM), cos, sa, sb, HEAD_DIM // 2)

    _place_q_heads(proj(0, 512), bd, gains_ref[0:1, :], cos, sa, sb, qc_ref, Q_SCALE)
    kc_ref[...] = proj(512, LANES)
    vc_ref[...] = proj(640, LANES)
    ks_ref[...] = k_slab(768, gains_ref[1:2, :]).astype(BF16)
    vs_ref[...] = _with_ones(proj(896, LANES))
    kw_ref[...] = k_slab(1024, gains_ref[2:3, :]).astype(BF16)
    vw_ref[...] = _with_ones(proj(1152, LANES))
    gt_ref[...] = _sigmoid(proj(1280, LANES))
    zc_ref[...] = proj(1408, 512)
    cq_ref[...] = proj(1920, 256)
    ckv_ref[...] = proj(2176, LANES)
    kr_ref[...] = proj(2304, LANES)
    zd_ref[...] = proj(2432, 512)


def _compress_kernel(c_ref, pe_ref, w1a_ref, w1b_ref, w2_ref, gain_ref, cos_ref, sa_ref, sb_ref, bd_ref,
                     o_ref, *, is_key):
    c = c_ref[...]
    n = c.shape[0]
    a = _nn((c + pe_ref[0:1, :]).astype(BF16), w1a_ref[...])
    b = _nn((c + pe_ref[1:2, :]).astype(BF16), w1b_ref[...])
    h = _silu(a + pltpu.roll(b, n - 1, 0))
    y = _nn(h.astype(BF16), w2_ref[...])
    if is_key:
        y = _rope(_seg_norm(y, bd_ref[...], gain_ref[...], HEAD_DIM), cos_ref[...], sa_ref[...], sb_ref[...],
                  HEAD_DIM // 2)
    o_ref[...] = y.astype(BF16)


def _nsa_kernel(q_ref, kc_ref, vc_ref, ks_ref, vs_ref, kw_ref, vw_ref, gt_ref, z_ref, wsel_ref, o_ref,
                m_ref, acc_ref, *, n_sel):
    TQ = TQ_ATT
    i = pl.program_id(1)
    t0 = i * TQ
    nch = (t0 + TQ + CH - 1) // CH
    ncmp = kc_ref.shape[0]
    rows_all = N_HEADS * TQ
    q = q_ref[...].reshape(rows_all, LANES)

    cend = lax.broadcasted_iota(I32, (TQ, ncmp), 1) * CMP_STRIDE + (2 * CMP_STRIDE - 1)
    bias_c = jnp.where(cend <= t0 + lax.broadcasted_iota(I32, (TQ, ncmp), 0), 0.0, NEG_INF)
    s = _nt(q, kc_ref[...]) + _stack_rows(bias_c, N_HEADS)
    m = jnp.max(s, axis=1, keepdims=True)
    m = jnp.where(m > NEG_INF, m, 0.0)
    p = jnp.exp2(s - m)
    p = p / jnp.maximum(jnp.sum(p, axis=1, keepdims=True), 1e-30)
    o_cmp = _nn(p.astype(BF16), vc_ref[...])

    span = NSA_WINDOW + TQ
    start = pl.multiple_of(jnp.maximum(t0 - NSA_WINDOW, 0), TQ)
    diff = t0 + lax.broadcasted_iota(I32, (TQ, span), 0) - (start + lax.broadcasted_iota(I32, (TQ, span), 1))
    bias_w = jnp.where(diff >= 0, jnp.where(diff < NSA_WINDOW, 0.0, NEG_INF), NEG_INF)
    sw = _nt(q, kw_ref[pl.ds(start, span), :]).astype(BF16) + _stack_rows(bias_w.astype(BF16), N_HEADS)
    pw = jnp.exp2(sw - jnp.max(sw, axis=1, keepdims=True))
    o_win = _nn(pw, vw_ref[pl.ds(start, span), :])
    o_win = o_win[:, 0:LANES] / o_win[:, LANES:2 * LANES]

    blk = lax.broadcasted_iota(I32, (LANES, TQ), 0)
    cur = (t0 + lax.broadcasted_iota(I32, (LANES, TQ), 1)) >> 6
    wsel_t = wsel_ref[...]
    keys = []
    for g in range(N_KV):
        imp = p[(GRP * g) * TQ:(GRP * g + 1) * TQ]
        for r in range(1, GRP):
            imp = imp + p[(GRP * g + r) * TQ:(GRP * g + r + 1) * TQ]
        hi = imp.astype(BF16)
        lo = (imp - hi.astype(F32)).astype(BF16)
        imp_s = _nt(wsel_t, hi) + _nt(wsel_t, lo)
        forced = jnp.where(blk == 0, jnp.inf, jnp.where(blk >= cur - 1, jnp.inf, imp_s))
        keys.append(_order_key(jnp.where(blk <= cur, forced, NEG_INF)))
    kk = jnp.concatenate(keys, axis=1)

    def bit_body(it, tb):
        cand_b = tb | jnp.left_shift(jnp.int32(1), 31 - it)
        cnt = jnp.sum(jnp.where(kk >= (cand_b ^ INT_MIN), 1.0, 0.0), axis=0, keepdims=True)
        return jnp.where(cnt >= n_sel, cand_b, tb)

    thr = lax.fori_loop(0, 32, bit_body, jnp.zeros((1, N_KV * TQ), I32)) ^ INT_MIN
    need = n_sel - jnp.sum(jnp.where(kk > thr, 1.0, 0.0), axis=0, keepdims=True)
    tril = jnp.where(lax.broadcasted_iota(I32, (LANES, LANES), 0) >= lax.broadcasted_iota(I32, (LANES, LANES), 1),
                     1.0, 0.0).astype(BF16)
    pref = _nn(tril, jnp.where(kk == thr, 1.0, 0.0).astype(BF16))
    tie = jnp.where(kk == thr, jnp.where(pref <= need, 1.0, 0.0), 0.0)
    sel_t = jnp.where(kk > thr, 1.0, tie)
    sel = jnp.concatenate([sel_t[:, g * TQ:(g + 1) * TQ].T for g in range(N_KV)], axis=0).astype(BF16)

    _flash_init(m_ref, acc_ref)
    blk_row =lax.broadcasted_iota(I32, (LANES, CH), 0)
    blk_col = lax.broadcasted_iota(I32, (LANES, CH), 1) >> 6
    row_c = t0 + lax.broadcasted_iota(I32, (TQ, CH), 0)
    col_c = lax.broadcasted_iota(I32, (TQ, CH), 1)

    def slc_chunk(c):
        off = pl.multiple_of(c * CH, CH)
        expand = jnp.where(blk_row == blk_col + c * (CH // SLC_BLOCK), 1.0, 0.0).astype(BF16)
        tok = _nn(sel, expand)
        causal = col_c + off <= row_c
        parts = []
        for g in range(N_KV):
            bias_g = jnp.where(causal, jnp.where(tok[g * TQ:(g + 1) * TQ] > 0.5, 0.0, NEG_INF), NEG_INF)
            parts.extend([bias_g.astype(BF16)] * GRP)
        sc = _nt(q, ks_ref[pl.ds(off, CH), :]).astype(BF16) + jnp.concatenate(parts, axis=0)
        _flash_update(sc, vs_ref[pl.ds(off, CH), :], m_ref, acc_ref, slice(None))

    _unrolled_loop(nch, slc_chunk)
    o_slc = _flash_result(acc_ref)

    gates = gt_ref[...]
    outs = []
    for h in range(N_HEADS):
        rs = slice(h * TQ, (h + 1) * TQ)
        outs.append(gates[:, 3 * h:3 * h + 1] * o_cmp[rs] + gates[:, 3 * h + 1:3 * h + 2] * o_slc[rs]
                    + gates[:, 3 * h + 2:3 * h + 3] * o_win[rs])
    _gated_store(jnp.concatenate(outs, axis=0), z_ref, o_ref, TQ)


def _mla_prep_kernel(cq_ref, ckv_ref, kr_ref, wq_ref, wkv_ref, lat_ref, gains_ref, cos_ref, sa_ref, sb_ref,
                     bd64_ref, bd32_ref, q_ref, k_ref, v_ref):
    lane = lax.broadcasted_iota(I32, (TM, LANES), 1)
    lo_half = lane < NOPE_DIM
    cos, sa, sb = cos_ref[...], sa_ref[...], sb_ref[...]
    bd64, bd32 = bd64_ref[...], bd32_ref[...]
    scale = (NOPE_DIM + ROPE_DIM) ** -0.5 * LOG2E

    def rope_slab(s, gain):
        return _rope(_seg_norm(s, bd32, gain, ROPE_DIM), cos, sa, sb, ROPE_DIM // 2)

    q = _nn(_row_rms(cq_ref[...], lat_ref[0:1, :]).astype(BF16), wq_ref[...])
    kv = _nn(_row_rms(ckv_ref[...], lat_ref[1:2, 0:KV_LORA]).astype(BF16), wkv_ref[...])
    k_rope = pltpu.roll(rope_slab(kr_ref[...], gains_ref[3:4, :]), NOPE_DIM, 1)
    q_rope = [rope_slab(q[:, 512 + LANES * j:512 + LANES * (j + 1)], gains_ref[1:2, :]) * scale for j in range(2)]
    for j in range(4):
        qn = _seg_norm(q[:, LANES * j:LANES * (j + 1)], bd64, gains_ref[0:1, :], NOPE_DIM) * scale
        kn = _seg_norm(kv[:, LANES * j:LANES * (j + 1)], bd64, gains_ref[2:3, :], NOPE_DIM)
        for e in range(2):
            h = 2 * j + e
            qn_h = qn if e == 0 else pltpu.roll(qn, NOPE_DIM, 1)
            kn_h = kn if e == 0 else pltpu.roll(kn, NOPE_DIM, 1)
            shift = (NOPE_DIM - ROPE_DIM * (h % 4)) % LANES
            qr = q_rope[h // 4]
            qr_h = qr if shift == 0 else pltpu.roll(qr, shift, 1)
            q_ref[h] = jnp.where(lo_half, qn_h, jnp.where(lane < NOPE_DIM + ROPE_DIM, qr_h, 0.0)).astype(BF16)
            k_ref[h] = jnp.where(lo_half, kn_h, k_rope).astype(BF16)
    for j in range(4):
        v_ref[j] = _with_ones(kv[:, 512 + LANES * j:512 + LANES * (j + 1)])


def _mla_kernel(q_ref, k_ref, v_ref, z_ref, o_ref, m_ref, acc_ref):
    i = pl.program_id(2)
    tq = TQ_MLA
    _flash_init(m_ref, acc_ref)

    def step(c, masked):
        off = pl.multiple_of(c * CH, CH)
        v = v_ref[0, pl.ds(off, CH), :]
        for e in range(2):
            k = k_ref[e, pl.ds(off, CH), :]
            for rb in range(tq // ROW_BLOCK):
                s = _nt(q_ref[e, rb * ROW_BLOCK:(rb + 1) * ROW_BLOCK, :], k)
                if masked:
                    qpos = rb * ROW_BLOCK + lax.broadcasted_iota(I32, (ROW_BLOCK, CH), 0)
                    s = jnp.where(lax.broadcasted_iota(I32, (ROW_BLOCK, CH), 1) <= qpos, s, NEG_INF)
                _flash_update(s, v, m_ref, acc_ref, slice(e * tq + rb * ROW_BLOCK, e * tq + (rb + 1) * ROW_BLOCK))

    _unrolled_loop(i, lambda c: step(c, False))
    step(i, True)
    o = _flash_result(acc_ref)
    lo_half = lax.broadcasted_iota(I32, (tq, LANES), 1) < NOPE_DIM
    slab = jnp.where(lo_half, o[0:tq], o[tq:2 * tq])
    o_ref[...] = (slab * _silu(z_ref[...])).astype(BF16)


def _rope_tables(pos, dim):
    inv = jnp.power(jnp.float32(ROPE_THETA), -jnp.arange(0, dim, 2, dtype=F32) / dim)
    ang = pos.astype(F32)[:, None] * inv[None, :]
    cos, sin = jnp.cos(ang), jnp.sin(ang)
    reps = LANES // dim
    zero = jnp.zeros_like(sin)
    cos_t = jnp.tile(jnp.concatenate([cos, cos], axis=1), (1, reps))
    sa_t = jnp.tile(jnp.concatenate([zero, sin], axis=1), (1, reps))
    sb_t = jnp.tile(jnp.concatenate([-sin, zero], axis=1), (1, reps))
    return cos_t, sa_t, sb_t


def _block_diag(group):
    r = jnp.arange(LANES) // group
    bd = (r[:, None] == r[None, :]).astype(BF16)
    return jnp.concatenate([bd, bd], axis=0)


def _tile_gain(g, width=LANES):
    return jnp.tile(g.astype(F32), width // g.shape[0])


def _pad_lanes(g, width=LANES):
    return jnp.concatenate([g.astype(F32), jnp.zeros((width - g.shape[0],), F32)])


def _pad_cols(w, width):
    return jnp.concatenate([w, jnp.zeros((w.shape[0], width - w.shape[1]), w.dtype)], axis=1)


def _params(*sem, flags=None):
    return pltpu.CompilerParams(dimension_semantics=sem, vmem_limit_bytes=VMEM_LIMIT, flags=flags)


def kernel(x, ab_norm, ab_w_in, a_qk_norm, a_kidx_norm, b_qk_norm, b_sinks, ab_w_out, cd_norm, cd_w_in, c_q_norm, c_k_norm, c_cmp_pe, c_cmp_w1, c_cmp_w2, d_q_lat_norm, d_kv_lat_norm, d_w_uq, d_w_ukv, d_nope_norm, d_rope_norm, cd_w_out):
    bsz, seq, d = x.shape
    n = bsz * seq
    assert d == D_MODEL and seq % CH == 0 and seq >= NSA_WINDOW + TQ_ATT and seq // SLC_BLOCK <= LANES
    nq = seq // TQ
    nrow = n // TM
    srow = seq // TM
    xf = x.reshape(n, d)
    pos = jnp.arange(seq)
    cos64, sa64, sb64 = _rope_tables(pos, HEAD_DIM)
    cos32, sa32, sb32 = _rope_tables(pos, ROPE_DIM)
    bd64, bd32 = _block_diag(HEAD_DIM), _block_diag(ROPE_DIM)
    tril = (jnp.arange(CH)[:, None] >= jnp.arange(CH)[None, :]).astype(BF16)

    def row_spec(width, rows=TM):
        return pl.BlockSpec((rows, width), lambda i: (i, 0))

    def const_spec(shape):
        return pl.BlockSpec(shape, lambda i: tuple(0 for _ in shape))

    tab_spec = pl.BlockSpec((TM, LANES), lambda i: (i % srow, 0))
    head_spec = lambda nh: pl.BlockSpec((nh, TM, LANES), lambda i: (0, i, 0))

    wa = ab_w_in[0]
    w_ab = jnp.concatenate([wa[:, 0:1088], _pad_cols(wa[:, 1088:1092], 64), wa[:, 1092:]], axis=1).astype(BF16)
    gains_ab = jnp.stack([_tile_gain(a_qk_norm[0, 0]), _tile_gain(a_qk_norm[0, 1]), _pad_lanes(a_kidx_norm[0]),
                          _tile_gain(b_qk_norm[0, 0]), _tile_gain(b_qk_norm[0, 1])]
                         + [jnp.zeros((LANES,), F32)] * 3)
    sds = jax.ShapeDtypeStruct
    qa, ka, va, qi, ki, wi, za, qb, kb, vb, zb = pl.pallas_call(
        _proj_ab_kernel,
        grid=(nrow,),
        in_specs=[row_spec(d), const_spec((1, d)), const_spec((d, AB_W)), tab_spec, tab_spec, tab_spec,
                  const_spec((2 * LANES, LANES)), const_spec((8, LANES))],
        out_specs=[head_spec(8), row_spec(LANES), row_spec(2 * LANES), head_spec(4), row_spec(LANES), row_spec(LANES),
                   row_spec(512), head_spec(8), row_spec(LANES), row_spec(2 * LANES), row_spec(512)],
        out_shape=[sds((8, n, LANES), BF16), sds((n, LANES), BF16), sds((n, 2 * LANES), BF16),
                   sds((4, n, LANES), BF16), sds((n, LANES), BF16), sds((n, LANES), F32), sds((n, 512), F32),
                   sds((8, n, LANES), BF16), sds((n, LANES), BF16), sds((n, 2 * LANES), BF16), sds((n, 512), F32)],
        compiler_params=_params("parallel"),
    )(xf, ab_norm[0][None, :], w_ab, cos64, sa64, sb64, bd64, gains_ab)

    nqa = seq // TQ_ATT
    qt_heads = lambda nh: pl.BlockSpec((nh, TQ_ATT, LANES), lambda b, i: (0, b * nqa + i, 0))
    qt_rows = lambda width: pl.BlockSpec((TQ_ATT, width), lambda b, i: (b * nqa + i, 0))
    seq_rows = pl.BlockSpec((seq, LANES), lambda b, i: (b, 0))
    seq_rows_v = pl.BlockSpec((seq, 2 * LANES), lambda b, i: (b, 0))
    att_scratch = [pltpu.VMEM((N_HEADS * TQ_ATT, LANES), F32), pltpu.VMEM((N_HEADS * TQ_ATT, 2 * LANES), F32)]

    mix_a = pl.pallas_call(
        functools.partial(_dsa_kernel, topk=min(DSA_TOPK, seq // 4)),
        grid=(bsz, nqa),
        in_specs=[qt_heads(4), qt_rows(LANES), seq_rows, qt_heads(8), seq_rows, seq_rows_v, qt_rows(512),
                  pl.BlockSpec((CH, CH), lambda b, i: (0, 0))],
        out_specs=qt_rows(512),
        out_shape=sds((n, 512), BF16),
        scratch_shapes=[pltpu.VMEM((seq // CH, CH, TQ_ATT), I32), pltpu.VMEM((seq // CH, TQ_ATT, CH), BF16),
                        pltpu.VMEM((8, TQ_ATT), F32)] + att_scratch,
        compiler_params=_params("parallel", "arbitrary"),
    )(qi, wi, ki, qa, ka, va, za, tril)

    nq2 = nq // 2
    prev_rows = lambda width: pl.BlockSpec((TQ, width), lambda b, i: (b * nq + jnp.maximum(2 * i - 1, 0), 0))
    pair_rows = lambda width: pl.BlockSpec((2 * TQ, width), lambda b, i: (b * nq2 + i, 0))
    sink_rows = jnp.broadcast_to(jnp.repeat(b_sinks[0].astype(F32) * LOG2E, TQ)[:, None], (N_HEADS * TQ, LANES))
    mix_b = pl.pallas_call(
        _swa_kernel,
        grid=(bsz, nq2),
        in_specs=[pl.BlockSpec((N_HEADS, 2 * TQ, LANES), lambda b, i: (0, b * nq2 + i, 0)),
                  prev_rows(LANES), pair_rows(LANES), prev_rows(2 * LANES), pair_rows(2 * LANES),
                  pl.BlockSpec((N_HEADS * TQ, LANES), lambda b, i: (0, 0)), pair_rows(512)],
        out_specs=pair_rows(512),
        out_shape=sds((n, 512), BF16),
        compiler_params=_params("parallel", "arbitrary"),
    )(qb, kb, kb, vb, vb, sink_rows, zb)

    def out_proj(xin, m0, m1, w):
        return pl.pallas_call(
            _out_proj_kernel,
            grid=(nrow,),
            in_specs=[row_spec(d), row_spec(512), row_spec(512), const_spec((2 * 512, d))],
            out_specs=row_spec(d),
            out_shape=sds((n, d), F32),
            compiler_params=_params("parallel"),
        )(xin, m0, m1, w.astype(BF16))

    x1 = out_proj(xf, mix_a, mix_b, ab_w_out[0])

    wc = cd_w_in[0]
    w_cd = jnp.concatenate([wc[:, 0:1280], _pad_cols(wc[:, 1280:1304], LANES), wc[:, 1304:2200],
                            _pad_cols(wc[:, 2200:2232], LANES), wc[:, 2232:]], axis=1).astype(BF16)
    gains_cd = jnp.stack([_tile_gain(c_q_norm[0]), _tile_gain(c_k_norm[0, 1]), _tile_gain(c_k_norm[0, 2])]
                         + [jnp.zeros((LANES,), F32)] * 5)
    qc, kc_raw, vc_raw, ks, vs, kw, vw, gates, zc, cq, ckv, kr, zd = pl.pallas_call(
        _proj_cd_kernel,
        grid=(nrow,),
        in_specs=[row_spec(d), const_spec((1, d)), const_spec((d, AB_W)), tab_spec, tab_spec, tab_spec,
                  const_spec((2 * LANES, LANES)), const_spec((8, LANES))],
        out_specs=[head_spec(8), row_spec(LANES), row_spec(LANES), row_spec(LANES), row_spec(2 * LANES),
                   row_spec(LANES), row_spec(2 * LANES), row_spec(LANES), row_spec(512), row_spec(256),
                   row_spec(LANES), row_spec(LANES), row_spec(512)],
        out_shape=[sds((8, n, LANES), BF16), sds((n, LANES), F32), sds((n, LANES), F32), sds((n, LANES), BF16),
                   sds((n, 2 * LANES), BF16), sds((n, LANES), BF16), sds((n, 2 * LANES), BF16), sds((n, LANES), F32),
                   sds((n, 512), F32), sds((n, 256), F32), sds((n, LANES), F32), sds((n, LANES), F32),
                   sds((n, 512), F32)],
        compiler_params=_params("parallel"),
    )(x1, cd_norm[0][None, :], w_cd, cos64, sa64, sb64, bd64, gains_cd)

    ncmp = seq // CMP_STRIDE
    cw = CMP_STRIDE * N_KV * HEAD_DIM
    eye = jnp.eye(N_KV, dtype=F32)
    ccos, csa, csb = _rope_tables(jnp.arange(ncmp) * CMP_STRIDE + (2 * CMP_STRIDE - 1), HEAD_DIM)

    def compress(raw, which, is_key):
        w1 = c_cmp_w1[0, which].reshape(2, CMP_STRIDE, HEAD_DIM, -1)
        hid = w1.shape[-1]
        w1e = jnp.einsum("tjdu,gh->tjgdhu", w1, eye).reshape(2, cw, N_KV * hid).astype(BF16)
        w2e = jnp.einsum("ud,gh->guhd", c_cmp_w2[0, which], eye).reshape(N_KV * hid, N_KV * HEAD_DIM).astype(BF16)
        pe = c_cmp_pe[0, which].reshape(2, CMP_STRIDE, 1, HEAD_DIM)
        pe = jnp.broadcast_to(pe, (2, CMP_STRIDE, N_KV, HEAD_DIM)).reshape(2, cw)
        return pl.pallas_call(
            functools.partial(_compress_kernel, is_key=is_key),
            grid=(bsz,),
            in_specs=[pl.BlockSpec((ncmp, cw), lambda b: (b, 0)), const_spec((2, cw)),
                      const_spec((cw, N_KV * hid)), const_spec((cw, N_KV * hid)),
                      const_spec((N_KV * hid, LANES)), const_spec((1, LANES)),
                      const_spec((ncmp, LANES)), const_spec((ncmp, LANES)), const_spec((ncmp, LANES)),
                      const_spec((2 * LANES, LANES))],
            out_specs=pl.BlockSpec((ncmp, LANES), lambda b: (b, 0)),
            out_shape=sds((bsz * ncmp, LANES), BF16),
            compiler_params=_params("parallel"),
        )(raw.reshape(bsz * ncmp, cw), pe, w1e[0], w1e[1], w2e, _tile_gain(c_k_norm[0, 0])[None, :],
          ccos, csa, csb, bd64)

    kc = compress(kc_raw, 0, True)
    vc = compress(vc_raw, 1, False)

    ratio = SLC_BLOCK // CMP_STRIDE
    mm = jnp.arange(ncmp)[:, None]
    jj = jnp.arange(LANES)[None, :]
    wsel = (jnp.where((mm == ratio * jj - 1) | (mm == ratio * jj + ratio - 1), 1.0, 0.0)
            + jnp.where((mm >= ratio * jj) & (mm < ratio * jj + ratio - 1), 2.0, 0.0))
    wsel_t = jnp.where((mm < ncmp - 1) & (jj < seq // SLC_BLOCK), wsel, 0.0).astype(BF16).T

    cmp_rows = pl.BlockSpec((ncmp, LANES), lambda b, i: (b, 0))
    mix_c = pl.pallas_call(
        functools.partial(_nsa_kernel, n_sel=min(SLC_TOPN, seq // SLC_BLOCK)),
        grid=(bsz, nqa),
        in_specs=[qt_heads(8), cmp_rows, cmp_rows, seq_rows, seq_rows_v, seq_rows, seq_rows_v, qt_rows(LANES),
                  qt_rows(512), pl.BlockSpec((LANES, ncmp), lambda b, i: (0, 0))],
        out_specs=qt_rows(512),
        out_shape=sds((n, 512), BF16),
        scratch_shapes=att_scratch,
        compiler_params=_params("parallel", "arbitrary"),
    )(qc, kc, vc, ks, vs, kw, vw, gates, zc, wsel_t)

    wq = d_w_uq[0].reshape(Q_LORA, N_HEADS, NOPE_DIM + ROPE_DIM)
    wq = jnp.concatenate([wq[:, :, :NOPE_DIM].reshape(Q_LORA, -1), wq[:, :, NOPE_DIM:].reshape(Q_LORA, -1)],
                         axis=1).astype(BF16)
    wkv = d_w_ukv[0].reshape(KV_LORA, N_HEADS, NOPE_DIM + HEAD_DIM)
    wkv = jnp.concatenate([wkv[:, :, :NOPE_DIM].reshape(KV_LORA, -1), wkv[:, :, NOPE_DIM:].reshape(KV_LORA, -1)],
                          axis=1).astype(BF16)
    lat_gains = jnp.stack([d_q_lat_norm[0].astype(F32), _pad_lanes(d_kv_lat_norm[0], Q_LORA)]
                          + [jnp.zeros((Q_LORA,), F32)] * 6)
    gains_d = jnp.stack([_tile_gain(d_nope_norm[0, 0]), _tile_gain(d_rope_norm[0, 0]), _tile_gain(d_nope_norm[0, 1]),
                         _pad_lanes(d_rope_norm[0, 1])] + [jnp.zeros((LANES,), F32)] * 4)
    q_cat, k_cat, v_d = pl.pallas_call(
        _mla_prep_kernel,
        grid=(nrow,),
        in_specs=[row_spec(Q_LORA), row_spec(LANES), row_spec(LANES), const_spec((Q_LORA, 768)),
                  const_spec((KV_LORA, 1024)), const_spec((8, Q_LORA)), const_spec((8, LANES)),
                  tab_spec, tab_spec, tab_spec, const_spec((2 * LANES, LANES)), const_spec((2 * LANES, LANES))],
        out_specs=[head_spec(8), head_spec(8), pl.BlockSpec((4, TM, 2 * LANES), lambda i: (0, i, 0))],
        out_shape=[sds((8, n, LANES), BF16), sds((8, n, LANES), BF16), sds((4, n, 2 * LANES), BF16)],
        compiler_params=_params("parallel"),
    )(cq, ckv, kr, wq, wkv, lat_gains, gains_d, cos32, sa32, sb32, bd64, bd32)

    nqm = seq // TQ_MLA
    mix_d = pl.pallas_call(
        _mla_kernel,
        grid=(bsz, N_HEADS // 2, nqm),
        in_specs=[pl.BlockSpec((2, TQ_MLA, LANES), lambda b, hp, i: (hp, b * nqm + i, 0)),
                  pl.BlockSpec((2, seq, LANES), lambda b, hp, i: (hp, b, 0)),
                  pl.BlockSpec((1, seq, 2 * LANES), lambda b, hp, i: (hp, b, 0)),
                  pl.BlockSpec((TQ_MLA, LANES), lambda b, hp, i: (b * nqm + i, hp))],
        out_specs=pl.BlockSpec((TQ_MLA, LANES), lambda b, hp, i: (b * nqm + i, hp)),
        out_shape=sds((n, 512), BF16),
        scratch_shapes=[pltpu.VMEM((2 * TQ_MLA, LANES), F32), pltpu.VMEM((2 * TQ_MLA, 2 * LANES), F32)],
        compiler_params=_params("parallel", "parallel", "arbitrary"),
    )(q_cat, k_cat, v_d, zd)

    x2 = out_proj(x1, mix_c, mix_d, cd_w_out[0])
    return x2.reshape(bsz, seq, d)
```

```python
import functools

import jax
import jax.numpy as jnp
from jax import lax
from jax.experimental import pallas as pl
from jax.experimental.pallas import tpu as pltpu

F32, BF16, I32 = jnp.float32, jnp.bfloat16, jnp.int32

D_MODEL = 1024
HEAD_DIM = 64
N_HEADS = 8
N_KV = 2
GRP = N_HEADS // N_KV
IDX_HEADS = 4
DSA_TOPK = 256
SWA_WINDOW = 128
CMP_STRIDE = 16
SLC_BLOCK = 64
SLC_TOPN = 16
NSA_WINDOW = 512
Q_LORA = 256
KV_LORA = 128
NOPE_DIM = 64
ROPE_DIM = 32
ROPE_THETA = 10000.0
EPS = 1e-6

LANES = 128
TM = 1024
TQ = 128
CH = 512
TQ_ATT = 256
TQ_MLA = 512
MLA_HEADS = 4
UNROLL = 4
COUNT_ROWS = 16
ROW_BLOCK = 128
VMEM_LIMIT = 56 * 1024 * 1024

NEG_INF = float("-inf")
M_INIT = -1e30
INT_MIN = -(2 ** 31)

AB_W = 2944
LOG2E = 1.4426950408889634
Q_SCALE = HEAD_DIM ** -0.5 * LOG2E


def _nn(a, b):
    return jnp.dot(a, b, preferred_element_type=F32)


def _nt(a, b):
    return lax.dot_general(a, b, (((1,), (1,)), ((), ())), preferred_element_type=F32)


def _sigmoid(z):
    return 1.0 / (1.0 + jnp.exp(-z))


def _silu(z):
    return z * _sigmoid(z)


def _row_rms(x, g):
    return x * lax.rsqrt(jnp.mean(x * x, axis=-1, keepdims=True) + EPS) * g


def _seg_norm(s, bd, gain, group):
    s2 = s * s
    hi = s2.astype(BF16)
    lo = (s2 - hi.astype(F32)).astype(BF16)
    ss = _nn(jnp.concatenate([hi, lo], axis=1), bd)
    return s * lax.rsqrt(ss * (1.0 / group) + EPS) * gain


def _rope(s, cos, sa, sb, half):
    return s * cos + pltpu.roll(s, half, 1) * sa + pltpu.roll(s, LANES - half, 1) * sb


def _order_key(v):
    bits = lax.bitcast_convert_type(v, I32)
    return bits ^ ((bits >> 31) & 0x7FFFFFFF)


def _stack_rows(x, n):
    return jnp.concatenate([x] * n, axis=0)


def _with_ones(v):
    return jnp.concatenate([v.astype(BF16), jnp.ones(v.shape, BF16)], axis=1)


def _flash_update(s, v_ext, m_ref, acc_ref, rows):
    m_prev = m_ref[rows]
    sb = s.astype(BF16)
    m_new = jnp.maximum(m_prev, jnp.max(sb, axis=1, keepdims=True).astype(F32))
    alpha = jnp.exp2(m_prev - m_new)
    p = jnp.exp2(sb - jnp.concatenate([m_new.astype(BF16)] * (s.shape[1] // LANES), axis=1))
    acc_ref[rows] = jnp.concatenate([alpha, alpha], axis=1) * acc_ref[rows] + _nn(p, v_ext)
    m_ref[rows] = m_new


def _unrolled_loop(n, step, unroll=UNROLL):
    def body(j, carry):
        for u in range(unroll):
            step(j * unroll + u)
        return carry

    lax.fori_loop(0, n // unroll, body, 0)
    base = (n // unroll) * unroll
    p = unroll // 2
    while p >= 1:
        def tail(base=base, p=p):
            for u in range(p):
                step(base + u)

        pl.when((n & p) != 0)(tail)
        base = base + (n & p)
        p //= 2


def _flash_init(m_ref, acc_ref):
    m_ref[...] = jnp.full(m_ref.shape, M_INIT, F32)
    acc_ref[...] = jnp.zeros(acc_ref.shape, F32)


def _flash_result(acc_ref):
    acc = acc_ref[...]
    return acc[:, 0:LANES] / jnp.maximum(acc[:, LANES:2 * LANES], 1e-30)


def _place_q_heads(y, bd, gain, cos, sa, sb, out_ref, scale):
    lo_half = lax.broadcasted_iota(I32, (y.shape[0], LANES), 1) < HEAD_DIM
    for j in range(4):
        s = y[:, LANES * j:LANES * (j + 1)]
        s = _rope(_seg_norm(s, bd, gain, HEAD_DIM), cos, sa, sb, HEAD_DIM // 2) * scale
        r = pltpu.roll(s, HEAD_DIM, 1)
        if j < 2:
            out_ref[2 * j] = jnp.where(lo_half, s, 0.0).astype(BF16)
            out_ref[2 * j + 1] = jnp.where(lo_half, r, 0.0).astype(BF16)
        else:
            out_ref[2 * j] = jnp.where(lo_half, 0.0, r).astype(BF16)
            out_ref[2 * j + 1] = jnp.where(lo_half, 0.0, s).astype(BF16)


def _gated_store(o, z_ref, o_ref, tq, row0=0):
    lo_half = lax.broadcasted_iota(I32, (tq, LANES), 1) < HEAD_DIM
    for j in range(4):
        a = o[(2 * j) * tq:(2 * j + 1) * tq]
        b = o[(2 * j + 1) * tq:(2 * j + 2) * tq]
        if j < 2:
            slab = jnp.where(lo_half, a, pltpu.roll(b, HEAD_DIM, 1))
        else:
            slab = jnp.where(lo_half, pltpu.roll(a, HEAD_DIM, 1), b)
        z = z_ref[row0:row0 + tq, LANES * j:LANES * (j + 1)]
        o_ref[row0:row0 + tq, LANES * j:LANES * (j + 1)] = (slab * _silu(z)).astype(BF16)


def _proj_ab_kernel(x_ref, g_ref, w_ref, cos_ref, sa_ref, sb_ref, bd_ref, gains_ref,
                    qa_ref, ka_ref, va_ref, qi_ref, ki_ref, wi_ref, za_ref,
                    qb_ref, kb_ref, vb_ref, zb_ref):
    xn = _row_rms(x_ref[...], g_ref[...]).astype(BF16)
    cos, sa, sb, bd = cos_ref[...], sa_ref[...], sb_ref[...], bd_ref[...]
    lo_half = lax.broadcasted_iota(I32, (TM, LANES), 1) < HEAD_DIM

    def proj(c0, n):
        return _nn(xn, w_ref[:, c0:c0 + n])

    def k_slab(c0, gain):
        return _rope(_seg_norm(proj(c0, LANES), bd, gain, HEAD_DIM), cos, sa, sb, HEAD_DIM // 2)

    _place_q_heads(proj(0, 512), bd, gains_ref[0:1, :], cos, sa, sb, qa_ref, Q_SCALE)
    ka_ref[...] = k_slab(512, gains_ref[1:2, :]).astype(BF16)
    va_ref[...] = _with_ones(proj(640, LANES))
    qi = proj(768, 256)
    for j in range(2):
        s = _rope(qi[:, LANES * j:LANES * (j + 1)], cos, sa, sb, HEAD_DIM // 2)
        qi_ref[2 * j] = jnp.where(lo_half, s, 0.0).astype(BF16)
        qi_ref[2 * j + 1] = jnp.where(lo_half, pltpu.roll(s, HEAD_DIM, 1), 0.0).astype(BF16)
    kiw = proj(1024, LANES)
    wi_ref[...] = kiw
    ki_ref[...] = _rope(_seg_norm(kiw, bd, gains_ref[2:3, :], HEAD_DIM), cos, sa, sb, HEAD_DIM // 2).astype(BF16)
    za_ref[...] = proj(1152, 512)
    _place_q_heads(proj(1664, 512), bd, gains_ref[3:4, :], cos, sa, sb, qb_ref, Q_SCALE)
    kb_ref[...] = k_slab(2176, gains_ref[4:5, :]).astype(BF16)
    vb_ref[...] = _with_ones(proj(2304, LANES))
    zb_ref[...] = proj(2432, 512)


def _count_chunks(keys_ref, nch, nq, pred):
    def body(c, acc):
        return acc + jnp.sum(pred(keys_ref[c]).reshape(CH // COUNT_ROWS, COUNT_ROWS, nq), axis=0)

    acc = lax.fori_loop(0, nch, body, jnp.zeros((COUNT_ROWS, nq), F32))
    return jnp.sum(acc, axis=0, keepdims=True)


def _kth_threshold(count, nq, k):
    def bit_body(it, carry):
        tb, above = carry
        cand_b = tb | jnp.left_shift(jnp.int32(1), 31 - it)
        cand_s = cand_b ^ INT_MIN
        cnt = count(lambda kk: jnp.where(kk >= cand_s, 1.0, 0.0))
        ok = cnt >= k
        return jnp.where(ok, cand_b, tb), jnp.where(ok, above, cnt)

    tb, above = lax.fori_loop(0, 32, bit_body, (jnp.zeros((1, nq), I32), jnp.zeros((1, nq), F32)))
    return tb ^ INT_MIN, k - above


def _dsa_kernel(qi_ref, wi_ref, ki_ref, qa_ref, ka_ref, va_ref, za_ref, tril_ref, o_ref,
                keys_ref, bias_ref, ties_ref, m_ref, acc_ref, *, topk):
    TQ = TQ_ATT
    i = pl.program_id(1)
    t0 = i * TQ
    nch = (t0 + TQ + CH - 1) // CH
    key_c = lax.broadcasted_iota(I32, (CH, TQ), 0)
    qry_c = t0 + lax.broadcasted_iota(I32, (CH, TQ), 1)

    qi = qi_ref[...].reshape(IDX_HEADS * TQ, LANES)
    w_t = wi_ref[...].T * ((HEAD_DIM * IDX_HEADS) ** -0.5)
    ws = [w_t[HEAD_DIM + h:HEAD_DIM + h + 1, :] for h in range(IDX_HEADS)]

    def score_chunk(c):
        off = pl.multiple_of(c * CH, CH)
        lg = _nt(ki_ref[pl.ds(off, CH), :], qi)
        sc = ws[0] * jnp.maximum(lg[:, 0:TQ], 0.0)
        for h in range(1, IDX_HEADS):
            sc = sc + ws[h] * jnp.maximum(lg[:, h * TQ:(h + 1) * TQ], 0.0)
        sc = jnp.where(sc == 0.0, 0.0, sc)
        keys_ref[c] = jnp.where(key_c + off <= qry_c, _order_key(sc), INT_MIN)

    _unrolled_loop(nch, score_chunk, unroll=2)

    count = functools.partial(_count_chunks, keys_ref, nch, TQ)
    thr, need = _kth_threshold(count, TQ, float(topk))

    tril = tril_ref[...]

    ties_ref[...] = jnp.zeros(ties_ref.shape, F32)

    def select_chunk(c):
        kk = keys_ref[c]
        pref = _nn(tril, jnp.where(kk == thr, 1.0, 0.0).astype(BF16)) + ties_ref[0:1, :]
        ties_ref[0:1, :] = pref[CH - 1:CH, :]
        tie = jnp.where(kk == thr, jnp.where(pref <= need, 0.0, NEG_INF), NEG_INF)
        bias = jnp.where(key_c + c * CH <= qry_c, jnp.where(kk > thr, 0.0, tie), NEG_INF)
        bias_ref[c] = bias.T.astype(BF16)

    _unrolled_loop(nch, select_chunk)

    _flash_init(m_ref, acc_ref)
    q_all = qa_ref[...].reshape(N_HEADS * TQ, LANES)

    def att_chunk(c):
        off = pl.multiple_of(c * CH, CH)
        s = _nt(q_all, ka_ref[pl.ds(off, CH), :]).astype(BF16) + _stack_rows(bias_ref[c], N_HEADS)
        _flash_update(s, va_ref[pl.ds(off, CH), :], m_ref, acc_ref, slice(None))

    _unrolled_loop(nch, att_chunk)
    _gated_store(_flash_result(acc_ref), za_ref, o_ref, TQ)


def _swa_kernel(q_ref, kp_ref, kc_ref, vp_ref, vc_ref, sink_ref, z_ref, o_ref):
    i = pl.program_id(1)
    qi = lax.broadcasted_iota(I32, (TQ, 2 * TQ), 0)
    kj = lax.broadcasted_iota(I32, (TQ, 2 * TQ), 1)
    sink = sink_ref[...]
    for half in range(2):
        q = q_ref[:, half * TQ:(half + 1) * TQ, :].reshape(N_HEADS * TQ, LANES)
        if half == 0:
            k2 = jnp.concatenate([kp_ref[...], kc_ref[0:TQ, :]], axis=0)
            v2 = jnp.concatenate([vp_ref[...], vc_ref[0:TQ, :]], axis=0)
            lo = jnp.where(i > 0, 0, TQ)
        else:
            k2, v2, lo = kc_ref[...], vc_ref[...], 0
        keep = jnp.where(kj >= lo, jnp.where(kj > qi, jnp.where(kj <= qi + TQ, 0.0, NEG_INF), NEG_INF), NEG_INF)
        s = _nt(q, k2) + _stack_rows(keep, N_HEADS)
        m = jnp.maximum(jnp.max(s, axis=1, keepdims=True), sink)
        p = jnp.exp2(s - jnp.concatenate([m, m], axis=1))
        o = _nn(p.astype(BF16), v2)
        den = o[:, LANES:2 * LANES] + jnp.exp2(sink - m)
        _gated_store(o[:, 0:LANES] / den, z_ref, o_ref, TQ, half * TQ)


def _out_proj_kernel(x_ref, ma_ref, mb_ref, w_ref, o_ref):
    half = ma_ref.shape[1]
    o_ref[...] = x_ref[...] + _nn(ma_ref[...], w_ref[0:half, :]) + _nn(mb_ref[...], w_ref[half:2 * half, :])


def _proj_cd_kernel(x_ref, g_ref, w_ref, cos_ref, sa_ref, sb_ref, bd_ref, gains_ref,
                    qc_ref, kc_ref, vc_ref, ks_ref, vs_ref, kw_ref, vw_ref, gt_ref, zc_ref,
                    cq_ref, ckv_ref, kr_ref, zd_ref):
    xn = _row_rms(x_ref[...], g_ref[...]).astype(BF16)
    cos, sa, sb, bd = cos_ref[...], sa_ref[...], sb_ref[...], bd_ref[...]

    def proj(c0, n):
        return _nn(xn, w_ref[:, c0:c0 + n])

    def k_slab(c0, gain):
        return _rope(_seg_norm(proj(c0, LANES), bd, gain, HEAD_DIM), cos, sa, sb, HEAD_DIM // 2)

    _place_q_heads(proj(0, 512), bd, gains_ref[0:1, :], cos, sa, sb, qc_ref, Q_SCALE)
    kc_ref[...] = proj(512, LANES)
    vc_ref[...] = proj(640, LANES)
    ks_ref[...] = k_slab(768, gains_ref[1:2, :]).astype(BF16)
    vs_ref[...] = _with_ones(proj(896, LANES))
    kw_ref[...] = k_slab(1024, gains_ref[2:3, :]).astype(BF16)
    vw_ref[...] = _with_ones(proj(1152, LANES))
    gt_ref[...] = _sigmoid(proj(1280, LANES))
    zc_ref[...] = proj(1408, 512)
    cq_ref[...] = proj(1920, 256)
    ckv_ref[...] = proj(2176, LANES)
    kr_ref[...] = proj(2304, LANES)
    zd_ref[...] = proj(2432, 512)


def _compress_kernel(c_ref, pe_ref, w1a_ref, w1b_ref, w2_ref, gain_ref, cos_ref, sa_ref, sb_ref, bd_ref,
                     o_ref, *, is_key):
    c = c_ref[...]
    n = c.shape[0]
    a = _nn((c + pe_ref[0:1, :]).astype(BF16), w1a_ref[...])
    b = _nn((c + pe_ref[1:2, :]).astype(BF16), w1b_ref[...])
    h = _silu(a + pltpu.roll(b, n - 1, 0))
    y = _nn(h.astype(BF16), w2_ref[...])
    if is_key:
        y = _rope(_seg_norm(y, bd_ref[...], gain_ref[...], HEAD_DIM), cos_ref[...], sa_ref[...], sb_ref[...],
                  HEAD_DIM // 2)
    o_ref[...] = y.astype(BF16)


def _nsa_kernel(q_ref, kc_ref, vc_ref, ks_ref, vs_ref, kw_ref, vw_ref, gt_ref, z_ref, wsel_ref, o_ref,
                m_ref, acc_ref, *, n_sel):
    TQ = TQ_ATT
    i = pl.program_id(1)
    t0 = i * TQ
    nch = (t0 + TQ + CH - 1) // CH
    ncmp = kc_ref.shape[0]
    rows_all = N_HEADS * TQ
    q = q_ref[...].reshape(rows_all, LANES)

    cend = lax.broadcasted_iota(I32, (TQ, ncmp), 1) * CMP_STRIDE + (2 * CMP_STRIDE - 1)
    bias_c = jnp.where(cend <= t0 + lax.broadcasted_iota(I32, (TQ, ncmp), 0), 0.0, NEG_INF)
    s = _nt(q, kc_ref[...]) + _stack_rows(bias_c, N_HEADS)
    m = jnp.max(s, axis=1, keepdims=True)
    m = jnp.where(m > NEG_INF, m, 0.0)
    p = jnp.exp2(s - m)
    p = p / jnp.maximum(jnp.sum(p, axis=1, keepdims=True), 1e-30)
    o_cmp = _nn(p.astype(BF16), vc_ref[...])

    span = NSA_WINDOW + TQ
    start = pl.multiple_of(jnp.maximum(t0 - NSA_WINDOW, 0), TQ)
    diff = t0 + lax.broadcasted_iota(I32, (TQ, span), 0) - (start + lax.broadcasted_iota(I32, (TQ, span), 1))
    bias_w = jnp.where(diff >= 0, jnp.where(diff < NSA_WINDOW, 0.0, NEG_INF), NEG_INF)
    sw = _nt(q, kw_ref[pl.ds(start, span), :]).astype(BF16) + _stack_rows(bias_w.astype(BF16), N_HEADS)
    pw = jnp.exp2(sw - jnp.max(sw, axis=1, keepdims=True))
    o_win = _nn(pw, vw_ref[pl.ds(start, span), :])
    o_win = o_win[:, 0:LANES] / o_win[:, LANES:2 * LANES]

    blk = lax.broadcasted_iota(I32, (LANES, TQ), 0)
    cur = (t0 + lax.broadcasted_iota(I32, (LANES, TQ), 1)) >> 6
    wsel_t = wsel_ref[...]
    keys = []
    for g in range(N_KV):
        imp = p[(GRP * g) * TQ:(GRP * g + 1) * TQ]
        for r in range(1, GRP):
            imp = imp + p[(GRP * g + r) * TQ:(GRP * g + r + 1) * TQ]
        hi = imp.astype(BF16)
        lo = (imp - hi.astype(F32)).astype(BF16)
        imp_s = _nt(wsel_t, hi) + _nt(wsel_t, lo)
        forced = jnp.where(blk == 0, jnp.inf, jnp.where(blk >= cur - 1, jnp.inf, imp_s))
        keys.append(_order_key(jnp.where(blk <= cur, forced, NEG_INF)))
    kk = jnp.concatenate(keys, axis=1)

    def bit_body(it, tb):
        cand_b = tb | jnp.left_shift(jnp.int32(1), 31 - it)
        cnt = jnp.sum(jnp.where(kk >= (cand_b ^ INT_MIN), 1.0, 0.0), axis=0, keepdims=True)
        return jnp.where(cnt >= n_sel, cand_b, tb)

    thr = lax.fori_loop(0, 32, bit_body, jnp.zeros((1, N_KV * TQ), I32)) ^ INT_MIN
    need = n_sel - jnp.sum(jnp.where(kk > thr, 1.0, 0.0), axis=0, keepdims=True)
    tril = jnp.where(lax.broadcasted_iota(I32, (LANES, LANES), 0) >= lax.broadcasted_iota(I32, (LANES, LANES), 1),
                     1.0, 0.0).astype(BF16)
    pref = _nn(tril, jnp.where(kk == thr, 1.0, 0.0).astype(BF16))
    tie = jnp.where(kk == thr, jnp.where(pref <= need, 1.0, 0.0), 0.0)
    sel_t = jnp.where(kk > thr, 1.0, tie)
    sel = jnp.concatenate([sel_t[:, g * TQ:(g + 1) * TQ].T for g in range(N_KV)], axis=0).astype(BF16)

    _flash_init(m_ref, acc_ref)
    blk_row =lax.broadcasted_iota(I32, (LANES, CH), 0)
    blk_col = lax.broadcasted_iota(I32, (LANES, CH), 1) >> 6
    row_c = t0 + lax.broadcasted_iota(I32, (TQ, CH), 0)
    col_c = lax.broadcasted_iota(I32, (TQ, CH), 1)

    def slc_chunk(c):
        off = pl.multiple_of(c * CH, CH)
        expand = jnp.where(blk_row == blk_col + c * (CH // SLC_BLOCK), 1.0, 0.0).astype(BF16)
        tok = _nn(sel, expand)
        causal = col_c + off <= row_c
        parts = []
        for g in range(N_KV):
            bias_g = jnp.where(causal, jnp.where(tok[g * TQ:(g + 1) * TQ] > 0.5, 0.0, NEG_INF), NEG_INF)
            parts.extend([bias_g.astype(BF16)] * GRP)
        sc = _nt(q, ks_ref[pl.ds(off, CH), :]).astype(BF16) + jnp.concatenate(parts, axis=0)
        _flash_update(sc, vs_ref[pl.ds(off, CH), :], m_ref, acc_ref, slice(None))

    _unrolled_loop(nch, slc_chunk)
    o_slc = _flash_result(acc_ref)

    gates = gt_ref[...]
    outs = []
    for h in range(N_HEADS):
        rs = slice(h * TQ, (h + 1) * TQ)
        outs.append(gates[:, 3 * h:3 * h + 1] * o_cmp[rs] + gates[:, 3 * h + 1:3 * h + 2] * o_slc[rs]
                    + gates[:, 3 * h + 2:3 * h + 3] * o_win[rs])
    _gated_store(jnp.concatenate(outs, axis=0), z_ref, o_ref, TQ)


def _mla_prep_kernel(cq_ref, ckv_ref, kr_ref, wq_ref, wkv_ref, lat_ref, gains_ref, cos_ref, sa_ref, sb_ref,
                     bd64_ref, bd32_ref, q_ref, k_ref, v_ref):
    lane = lax.broadcasted_iota(I32, (TM, LANES), 1)
    lo_half = lane < NOPE_DIM
    cos, sa, sb = cos_ref[...], sa_ref[...], sb_ref[...]
    bd64, bd32 = bd64_ref[...], bd32_ref[...]
    scale = (NOPE_DIM + ROPE_DIM) ** -0.5 * LOG2E

    def rope_slab(s, gain):
        return _rope(_seg_norm(s, bd32, gain, ROPE_DIM), cos, sa, sb, ROPE_DIM // 2)

    q = _nn(_row_rms(cq_ref[...], lat_ref[0:1, :]).astype(BF16), wq_ref[...])
    kv = _nn(_row_rms(ckv_ref[...], lat_ref[1:2, 0:KV_LORA]).astype(BF16), wkv_ref[...])
    k_rope = pltpu.roll(rope_slab(kr_ref[...], gains_ref[3:4, :]), NOPE_DIM, 1)
    q_rope = [rope_slab(q[:, 512 + LANES * j:512 + LANES * (j + 1)], gains_ref[1:2, :]) * scale for j in range(2)]
    for j in range(4):
        qn = _seg_norm(q[:, LANES * j:LANES * (j + 1)], bd64, gains_ref[0:1, :], NOPE_DIM) * scale
        kn = _seg_norm(kv[:, LANES * j:LANES * (j + 1)], bd64, gains_ref[2:3, :], NOPE_DIM)
        for e in range(2):
            h = 2 * j + e
            qn_h = qn if e == 0 else pltpu.roll(qn, NOPE_DIM, 1)
            kn_h = kn if e == 0 else pltpu.roll(kn, NOPE_DIM, 1)
            shift = (NOPE_DIM - ROPE_DIM * (h % 4)) % LANES
            qr = q_rope[h // 4]
            qr_h = qr if shift == 0 else pltpu.roll(qr, shift, 1)
            q_ref[h] = jnp.where(lo_half, qn_h, jnp.where(lane < NOPE_DIM + ROPE_DIM, qr_h, 0.0)).astype(BF16)
            k_ref[h] = jnp.where(lo_half, kn_h, k_rope).astype(BF16)
    for j in range(4):
        v_ref[j] = _with_ones(kv[:, 512 + LANES * j:512 + LANES * (j + 1)])


def _mla_kernel(q_ref, k_ref, v_ref, z_ref, o_ref, m_ref, acc_ref):
    i = pl.program_id(2)
    tq = TQ_MLA
    _flash_init(m_ref, acc_ref)

    def step(c, masked):
        off = pl.multiple_of(c * CH, CH)
        for e in range(MLA_HEADS):
            k = k_ref[e, pl.ds(off, CH), :]
            v = v_ref[e // 2, pl.ds(off, CH), :]
            for rb in range(tq // ROW_BLOCK):
                s = _nt(q_ref[e, rb * ROW_BLOCK:(rb + 1) * ROW_BLOCK, :], k)
                if masked:
                    qpos = rb * ROW_BLOCK + lax.broadcasted_iota(I32, (ROW_BLOCK, CH), 0)
                    s = jnp.where(lax.broadcasted_iota(I32, (ROW_BLOCK, CH), 1) <= qpos, s, NEG_INF)
                _flash_update(s, v, m_ref, acc_ref, slice(e * tq + rb * ROW_BLOCK, e * tq + (rb + 1) * ROW_BLOCK))

    _unrolled_loop(i, lambda c: step(c, False))
    step(i, True)
    o = _flash_result(acc_ref)
    lo_half = lax.broadcasted_iota(I32, (tq, LANES), 1) < NOPE_DIM
    for pr in range(MLA_HEADS // 2):
        slab = jnp.where(lo_half, o[(2 * pr) * tq:(2 * pr + 1) * tq], o[(2 * pr + 1) * tq:(2 * pr + 2) * tq])
        lanes = slice(pr * LANES, (pr + 1) * LANES)
        o_ref[:, lanes] = (slab * _silu(z_ref[:, lanes])).astype(BF16)


def _rope_tables(pos, dim):
    inv = jnp.power(jnp.float32(ROPE_THETA), -jnp.arange(0, dim, 2, dtype=F32) / dim)
    ang = pos.astype(F32)[:, None] * inv[None, :]
    cos, sin = jnp.cos(ang), jnp.sin(ang)
    reps = LANES // dim
    zero = jnp.zeros_like(sin)
    cos_t = jnp.tile(jnp.concatenate([cos, cos], axis=1), (1, reps))
    sa_t = jnp.tile(jnp.concatenate([zero, sin], axis=1), (1, reps))
    sb_t = jnp.tile(jnp.concatenate([-sin, zero], axis=1), (1, reps))
    return cos_t, sa_t, sb_t


def _block_diag(group):
    r = jnp.arange(LANES) // group
    bd = (r[:, None] == r[None, :]).astype(BF16)
    return jnp.concatenate([bd, bd], axis=0)


def _tile_gain(g, width=LANES):
    return jnp.tile(g.astype(F32), width // g.shape[0])


def _pad_lanes(g, width=LANES):
    return jnp.concatenate([g.astype(F32), jnp.zeros((width - g.shape[0],), F32)])


def _pad_cols(w, width):
    return jnp.concatenate([w, jnp.zeros((w.shape[0], width - w.shape[1]), w.dtype)], axis=1)


def _params(*sem, flags=None):
    return pltpu.CompilerParams(dimension_semantics=sem, vmem_limit_bytes=VMEM_LIMIT, flags=flags)


def kernel(x, ab_norm, ab_w_in, a_qk_norm, a_kidx_norm, b_qk_norm, b_sinks, ab_w_out, cd_norm, cd_w_in, c_q_norm, c_k_norm, c_cmp_pe, c_cmp_w1, c_cmp_w2, d_q_lat_norm, d_kv_lat_norm, d_w_uq, d_w_ukv, d_nope_norm, d_rope_norm, cd_w_out):
    bsz, seq, d = x.shape
    n = bsz * seq
    assert d == D_MODEL and seq % CH == 0 and seq >= NSA_WINDOW + TQ_ATT and seq // SLC_BLOCK <= LANES
    nq = seq // TQ
    nrow = n // TM
    srow = seq // TM
    xf = x.reshape(n, d)
    pos = jnp.arange(seq)
    cos64, sa64, sb64 = _rope_tables(pos, HEAD_DIM)
    cos32, sa32, sb32 = _rope_tables(pos, ROPE_DIM)
    bd64, bd32 = _block_diag(HEAD_DIM), _block_diag(ROPE_DIM)
    tril = (jnp.arange(CH)[:, None] >= jnp.arange(CH)[None, :]).astype(BF16)

    def row_spec(width, rows=TM):
        return pl.BlockSpec((rows, width), lambda i: (i, 0))

    def const_spec(shape):
        return pl.BlockSpec(shape, lambda i: tuple(0 for _ in shape))

    tab_spec = pl.BlockSpec((TM, LANES), lambda i: (i % srow, 0))
    head_spec = lambda nh: pl.BlockSpec((nh, TM, LANES), lambda i: (0, i, 0))

    wa = ab_w_in[0]
    w_ab = jnp.concatenate([wa[:, 0:1088], _pad_cols(wa[:, 1088:1092], 64), wa[:, 1092:]], axis=1).astype(BF16)
    gains_ab = jnp.stack([_tile_gain(a_qk_norm[0, 0]), _tile_gain(a_qk_norm[0, 1]), _pad_lanes(a_kidx_norm[0]),
                          _tile_gain(b_qk_norm[0, 0]), _tile_gain(b_qk_norm[0, 1])]
                         + [jnp.zeros((LANES,), F32)] * 3)
    sds = jax.ShapeDtypeStruct
    qa, ka, va, qi, ki, wi, za, qb, kb, vb, zb = pl.pallas_call(
        _proj_ab_kernel,
        grid=(nrow,),
        in_specs=[row_spec(d), const_spec((1, d)), const_spec((d, AB_W)), tab_spec, tab_spec, tab_spec,
                  const_spec((2 * LANES, LANES)), const_spec((8, LANES))],
        out_specs=[head_spec(8), row_spec(LANES), row_spec(2 * LANES), head_spec(4), row_spec(LANES), row_spec(LANES),
                   row_spec(512), head_spec(8), row_spec(LANES), row_spec(2 * LANES), row_spec(512)],
        out_shape=[sds((8, n, LANES), BF16), sds((n, LANES), BF16), sds((n, 2 * LANES), BF16),
                   sds((4, n, LANES), BF16), sds((n, LANES), BF16), sds((n, LANES), F32), sds((n, 512), F32),
                   sds((8, n, LANES), BF16), sds((n, LANES), BF16), sds((n, 2 * LANES), BF16), sds((n, 512), F32)],
        compiler_params=_params("parallel"),
    )(xf, ab_norm[0][None, :], w_ab, cos64, sa64, sb64, bd64, gains_ab)

    nqa = seq // TQ_ATT
    qt_heads = lambda nh: pl.BlockSpec((nh, TQ_ATT, LANES), lambda b, i: (0, b * nqa + i, 0))
    qt_rows = lambda width: pl.BlockSpec((TQ_ATT, width), lambda b, i: (b * nqa + i, 0))
    seq_rows = pl.BlockSpec((seq, LANES), lambda b, i: (b, 0))
    seq_rows_v = pl.BlockSpec((seq, 2 * LANES), lambda b, i: (b, 0))
    att_scratch = [pltpu.VMEM((N_HEADS * TQ_ATT, LANES), F32), pltpu.VMEM((N_HEADS * TQ_ATT, 2 * LANES), F32)]

    mix_a = pl.pallas_call(
        functools.partial(_dsa_kernel, topk=min(DSA_TOPK, seq // 4)),
        grid=(bsz, nqa),
        in_specs=[qt_heads(4), qt_rows(LANES), seq_rows, qt_heads(8), seq_rows, seq_rows_v, qt_rows(512),
                  pl.BlockSpec((CH, CH), lambda b, i: (0, 0))],
        out_specs=qt_rows(512),
        out_shape=sds((n, 512), BF16),
        scratch_shapes=[pltpu.VMEM((seq // CH, CH, TQ_ATT), I32), pltpu.VMEM((seq // CH, TQ_ATT, CH), BF16),
                        pltpu.VMEM((8, TQ_ATT), F32)] + att_scratch,
        compiler_params=_params("parallel", "arbitrary"),
    )(qi, wi, ki, qa, ka, va, za, tril)

    nq2 = nq // 2
    prev_rows = lambda width: pl.BlockSpec((TQ, width), lambda b, i: (b * nq + jnp.maximum(2 * i - 1, 0), 0))
    pair_rows = lambda width: pl.BlockSpec((2 * TQ, width), lambda b, i: (b * nq2 + i, 0))
    sink_rows = jnp.broadcast_to(jnp.repeat(b_sinks[0].astype(F32) * LOG2E, TQ)[:, None], (N_HEADS * TQ, LANES))
    mix_b = pl.pallas_call(
        _swa_kernel,
        grid=(bsz, nq2),
        in_specs=[pl.BlockSpec((N_HEADS, 2 * TQ, LANES), lambda b, i: (0, b * nq2 + i, 0)),
                  prev_rows(LANES), pair_rows(LANES), prev_rows(2 * LANES), pair_rows(2 * LANES),
                  pl.BlockSpec((N_HEADS * TQ, LANES), lambda b, i: (0, 0)), pair_rows(512)],
        out_specs=pair_rows(512),
        out_shape=sds((n, 512), BF16),
        compiler_params=_params("parallel", "arbitrary"),
    )(qb, kb, kb, vb, vb, sink_rows, zb)

    def out_proj(xin, m0, m1, w):
        return pl.pallas_call(
            _out_proj_kernel,
            grid=(nrow,),
            in_specs=[row_spec(d), row_spec(512), row_spec(512), const_spec((2 * 512, d))],
            out_specs=row_spec(d),
            out_shape=sds((n, d), F32),
            compiler_params=_params("parallel"),
        )(xin, m0, m1, w.astype(BF16))

    x1 = out_proj(xf, mix_a, mix_b, ab_w_out[0])

    wc = cd_w_in[0]
    w_cd = jnp.concatenate([wc[:, 0:1280], _pad_cols(wc[:, 1280:1304], LANES), wc[:, 1304:2200],
                            _pad_cols(wc[:, 2200:2232], LANES), wc[:, 2232:]], axis=1).astype(BF16)
    gains_cd = jnp.stack([_tile_gain(c_q_norm[0]), _tile_gain(c_k_norm[0, 1]), _tile_gain(c_k_norm[0, 2])]
                         + [jnp.zeros((LANES,), F32)] * 5)
    qc, kc_raw, vc_raw, ks, vs, kw, vw, gates, zc, cq, ckv, kr, zd = pl.pallas_call(
        _proj_cd_kernel,
        grid=(nrow,),
        in_specs=[row_spec(d), const_spec((1, d)), const_spec((d, AB_W)), tab_spec, tab_spec, tab_spec,
                  const_spec((2 * LANES, LANES)), const_spec((8, LANES))],
        out_specs=[head_spec(8), row_spec(LANES), row_spec(LANES), row_spec(LANES), row_spec(2 * LANES),
                   row_spec(LANES), row_spec(2 * LANES), row_spec(LANES), row_spec(512), row_spec(256),
                   row_spec(LANES), row_spec(LANES), row_spec(512)],
        out_shape=[sds((8, n, LANES), BF16), sds((n, LANES), F32), sds((n, LANES), F32), sds((n, LANES), BF16),
                   sds((n, 2 * LANES), BF16), sds((n, LANES), BF16), sds((n, 2 * LANES), BF16), sds((n, LANES), F32),
                   sds((n, 512), F32), sds((n, 256), F32), sds((n, LANES), F32), sds((n, LANES), F32),
                   sds((n, 512), F32)],
        compiler_params=_params("parallel"),
    )(x1, cd_norm[0][None, :], w_cd, cos64, sa64, sb64, bd64, gains_cd)

    ncmp = seq // CMP_STRIDE
    cw = CMP_STRIDE * N_KV * HEAD_DIM
    eye = jnp.eye(N_KV, dtype=F32)
    ccos, csa, csb = _rope_tables(jnp.arange(ncmp) * CMP_STRIDE + (2 * CMP_STRIDE - 1), HEAD_DIM)

    def compress(raw, which, is_key):
        w1 = c_cmp_w1[0, which].reshape(2, CMP_STRIDE, HEAD_DIM, -1)
        hid = w1.shape[-1]
        w1e = jnp.einsum("tjdu,gh->tjgdhu", w1, eye).reshape(2, cw, N_KV * hid).astype(BF16)
        w2e = jnp.einsum("ud,gh->guhd", c_cmp_w2[0, which], eye).reshape(N_KV * hid, N_KV * HEAD_DIM).astype(BF16)
        pe = c_cmp_pe[0, which].reshape(2, CMP_STRIDE, 1, HEAD_DIM)
        pe = jnp.broadcast_to(pe, (2, CMP_STRIDE, N_KV, HEAD_DIM)).reshape(2, cw)
        return pl.pallas_call(
            functools.partial(_compress_kernel, is_key=is_key),
            grid=(bsz,),
            in_specs=[pl.BlockSpec((ncmp, cw), lambda b: (b, 0)), const_spec((2, cw)),
                      const_spec((cw, N_KV * hid)), const_spec((cw, N_KV * hid)),
                      const_spec((N_KV * hid, LANES)), const_spec((1, LANES)),
                      const_spec((ncmp, LANES)), const_spec((ncmp, LANES)), const_spec((ncmp, LANES)),
                      const_spec((2 * LANES, LANES))],
            out_specs=pl.BlockSpec((ncmp, LANES), lambda b: (b, 0)),
            out_shape=sds((bsz * ncmp, LANES), BF16),
            compiler_params=_params("parallel"),
        )(raw.reshape(bsz * ncmp, cw), pe, w1e[0], w1e[1], w2e, _tile_gain(c_k_norm[0, 0])[None, :],
          ccos, csa, csb, bd64)

    kc = compress(kc_raw, 0, True)
    vc = compress(vc_raw, 1, False)

    ratio = SLC_BLOCK // CMP_STRIDE
    mm = jnp.arange(ncmp)[:, None]
    jj = jnp.arange(LANES)[None, :]
    wsel = (jnp.where((mm == ratio * jj - 1) | (mm == ratio * jj + ratio - 1), 1.0, 0.0)
            + jnp.where((mm >= ratio * jj) & (mm < ratio * jj + ratio - 1), 2.0, 0.0))
    wsel_t = jnp.where((mm < ncmp - 1) & (jj < seq // SLC_BLOCK), wsel, 0.0).astype(BF16).T

    cmp_rows = pl.BlockSpec((ncmp, LANES), lambda b, i: (b, 0))
    mix_c = pl.pallas_call(
        functools.partial(_nsa_kernel, n_sel=min(SLC_TOPN, seq // SLC_BLOCK)),
        grid=(bsz, nqa),
        in_specs=[qt_heads(8), cmp_rows, cmp_rows, seq_rows, seq_rows_v, seq_rows, seq_rows_v, qt_rows(LANES),
                  qt_rows(512), pl.BlockSpec((LANES, ncmp), lambda b, i: (0, 0))],
        out_specs=qt_rows(512),
        out_shape=sds((n, 512), BF16),
        scratch_shapes=att_scratch,
        compiler_params=_params("parallel", "arbitrary"),
    )(qc, kc, vc, ks, vs, kw, vw, gates, zc, wsel_t)

    wq = d_w_uq[0].reshape(Q_LORA, N_HEADS, NOPE_DIM + ROPE_DIM)
    wq = jnp.concatenate([wq[:, :, :NOPE_DIM].reshape(Q_LORA, -1), wq[:, :, NOPE_DIM:].reshape(Q_LORA, -1)],
                         axis=1).astype(BF16)
    wkv = d_w_ukv[0].reshape(KV_LORA, N_HEADS, NOPE_DIM + HEAD_DIM)
    wkv = jnp.concatenate([wkv[:, :, :NOPE_DIM].reshape(KV_LORA, -1), wkv[:, :, NOPE_DIM:].reshape(KV_LORA, -1)],
                          axis=1).astype(BF16)
    lat_gains = jnp.stack([d_q_lat_norm[0].astype(F32), _pad_lanes(d_kv_lat_norm[0], Q_LORA)]
                          + [jnp.zeros((Q_LORA,), F32)] * 6)
    gains_d = jnp.stack([_tile_gain(d_nope_norm[0, 0]), _tile_gain(d_rope_norm[0, 0]), _tile_gain(d_nope_norm[0, 1]),
                         _pad_lanes(d_rope_norm[0, 1])] + [jnp.zeros((LANES,), F32)] * 4)
    q_cat, k_cat, v_d = pl.pallas_call(
        _mla_prep_kernel,
        grid=(nrow,),
        in_specs=[row_spec(Q_LORA), row_spec(LANES), row_spec(LANES), const_spec((Q_LORA, 768)),
                  const_spec((KV_LORA, 1024)), const_spec((8, Q_LORA)), const_spec((8, LANES)),
                  tab_spec, tab_spec, tab_spec, const_spec((2 * LANES, LANES)), const_spec((2 * LANES, LANES))],
        out_specs=[head_spec(8), head_spec(8), pl.BlockSpec((4, TM, 2 * LANES), lambda i: (0, i, 0))],
        out_shape=[sds((8, n, LANES), BF16), sds((8, n, LANES), BF16), sds((4, n, 2 * LANES), BF16)],
        compiler_params=_params("parallel"),
    )(cq, ckv, kr, wq, wkv, lat_gains, gains_d, cos32, sa32, sb32, bd64, bd32)

    nqm = seq // TQ_MLA
    mix_d = pl.pallas_call(
        _mla_kernel,
        grid=(bsz, N_HEADS // MLA_HEADS, nqm),
        in_specs=[pl.BlockSpec((MLA_HEADS, TQ_MLA, LANES), lambda b, hg, i: (hg, b * nqm + i, 0)),
                  pl.BlockSpec((MLA_HEADS, seq, LANES), lambda b, hg, i: (hg, b, 0)),
                  pl.BlockSpec((MLA_HEADS // 2, seq, 2 * LANES), lambda b, hg, i: (hg, b, 0)),
                  pl.BlockSpec((TQ_MLA, MLA_HEADS * HEAD_DIM), lambda b, hg, i: (b * nqm + i, hg))],
        out_specs=pl.BlockSpec((TQ_MLA, MLA_HEADS * HEAD_DIM), lambda b, hg, i: (b * nqm + i, hg)),
        out_shape=sds((n, 512), BF16),
        scratch_shapes=[pltpu.VMEM((MLA_HEADS * TQ_MLA, LANES), F32),
                        pltpu.VMEM((MLA_HEADS * TQ_MLA, 2 * LANES), F32)],
        compiler_params=_params("parallel", "parallel", "arbitrary"),
    )(q_cat, k_cat, v_d, zd)

    x2 = out_proj(x1, mix_c, mix_d, cd_w_out[0])
    return x2.reshape(bsz, seq, d)
```

```python
import functools

import jax
import jax.numpy as jnp
from jax import lax
from jax.experimental import pallas as pl
from jax.experimental.pallas import tpu as pltpu

F32, BF16, I32 = jnp.float32, jnp.bfloat16, jnp.int32

D_MODEL = 1024
HEAD_DIM = 64
N_HEADS = 8
N_KV = 2
GRP = N_HEADS // N_KV
IDX_HEADS = 4
DSA_TOPK = 256
SWA_WINDOW = 128
CMP_STRIDE = 16
SLC_BLOCK = 64
SLC_TOPN = 16
NSA_WINDOW = 512
Q_LORA = 256
KV_LORA = 128
NOPE_DIM = 64
ROPE_DIM = 32
ROPE_THETA = 10000.0
EPS = 1e-6

LANES = 128
TM = 1024
TQ = 128
CH = 512
TQ_ATT = 256
TQ_MLA = 512
MLA_HEADS = 8
UNROLL = 4
COUNT_ROWS = 16
ROW_BLOCK = 128
VMEM_LIMIT = 56 * 1024 * 1024

NEG_INF = float("-inf")
M_INIT = -1e30
INT_MIN = -(2 ** 31)

AB_W = 2944
LOG2E = 1.4426950408889634
Q_SCALE = HEAD_DIM ** -0.5 * LOG2E


def _nn(a, b):
    return jnp.dot(a, b, preferred_element_type=F32)


def _nt(a, b):
    return lax.dot_general(a, b, (((1,), (1,)), ((), ())), preferred_element_type=F32)


def _sigmoid(z):
    return 1.0 / (1.0 + jnp.exp(-z))


def _silu(z):
    return z * _sigmoid(z)


def _row_rms(x, g):
    return x * lax.rsqrt(jnp.mean(x * x, axis=-1, keepdims=True) + EPS) * g


def _seg_norm(s, bd, gain, group):
    s2 = s * s
    hi = s2.astype(BF16)
    lo = (s2 - hi.astype(F32)).astype(BF16)
    ss = _nn(jnp.concatenate([hi, lo], axis=1), bd)
    return s * lax.rsqrt(ss * (1.0 / group) + EPS) * gain


def _rope(s, cos, sa, sb, half):
    return s * cos + pltpu.roll(s, half, 1) * sa + pltpu.roll(s, LANES - half, 1) * sb


def _order_key(v):
    bits = lax.bitcast_convert_type(v, I32)
    return bits ^ ((bits >> 31) & 0x7FFFFFFF)


def _stack_rows(x, n):
    return jnp.concatenate([x] * n, axis=0)


def _with_ones(v):
    return jnp.concatenate([v.astype(BF16), jnp.ones(v.shape, BF16)], axis=1)


def _flash_update(s, v_ext, m_ref, acc_ref, rows):
    m_prev = m_ref[rows]
    sb = s.astype(BF16)
    m_new = jnp.maximum(m_prev, jnp.max(sb, axis=1, keepdims=True).astype(F32))
    alpha = jnp.exp2(m_prev - m_new)
    p = jnp.exp2(sb - jnp.concatenate([m_new.astype(BF16)] * (s.shape[1] // LANES), axis=1))
    acc_ref[rows] = jnp.concatenate([alpha, alpha], axis=1) * acc_ref[rows] + _nn(p, v_ext)
    m_ref[rows] = m_new


def _unrolled_loop(n, step, unroll=UNROLL):
    def body(j, carry):
        for u in range(unroll):
            step(j * unroll + u)
        return carry

    lax.fori_loop(0, n // unroll, body, 0)
    base = (n // unroll) * unroll
    p = unroll // 2
    while p >= 1:
        def tail(base=base, p=p):
            for u in range(p):
                step(base + u)

        pl.when((n & p) != 0)(tail)
        base = base + (n & p)
        p //= 2


def _flash_init(m_ref, acc_ref):
    m_ref[...] = jnp.full(m_ref.shape, M_INIT, F32)
    acc_ref[...] = jnp.zeros(acc_ref.shape, F32)


def _flash_result(acc_ref):
    acc = acc_ref[...]
    return acc[:, 0:LANES] / jnp.maximum(acc[:, LANES:2 * LANES], 1e-30)


def _place_q_heads(y, bd, gain, cos, sa, sb, out_ref, scale):
    lo_half = lax.broadcasted_iota(I32, (y.shape[0], LANES), 1) < HEAD_DIM
    for j in range(4):
        s = y[:, LANES * j:LANES * (j + 1)]
        s = _rope(_seg_norm(s, bd, gain, HEAD_DIM), cos, sa, sb, HEAD_DIM // 2) * scale
        r = pltpu.roll(s, HEAD_DIM, 1)
        if j < 2:
            out_ref[2 * j] = jnp.where(lo_half, s, 0.0).astype(BF16)
            out_ref[2 * j + 1] = jnp.where(lo_half, r, 0.0).astype(BF16)
        else:
            out_ref[2 * j] = jnp.where(lo_half, 0.0, r).astype(BF16)
            out_ref[2 * j + 1] = jnp.where(lo_half, 0.0, s).astype(BF16)


def _gated_store(o, z_ref, o_ref, tq, row0=0):
    lo_half = lax.broadcasted_iota(I32, (tq, LANES), 1) < HEAD_DIM
    for j in range(4):
        a = o[(2 * j) * tq:(2 * j + 1) * tq]
        b = o[(2 * j + 1) * tq:(2 * j + 2) * tq]
        if j < 2:
            slab = jnp.where(lo_half, a, pltpu.roll(b, HEAD_DIM, 1))
        else:
            slab = jnp.where(lo_half, pltpu.roll(a, HEAD_DIM, 1), b)
        z = z_ref[row0:row0 + tq, LANES * j:LANES * (j + 1)]
        o_ref[row0:row0 + tq, LANES * j:LANES * (j + 1)] = (slab * _silu(z)).astype(BF16)


def _proj_ab_kernel(x_ref, g_ref, w_ref, cos_ref, sa_ref, sb_ref, bd_ref, gains_ref,
                    qa_ref, ka_ref, va_ref, qi_ref, ki_ref, wi_ref, za_ref,
                    qb_ref, kb_ref, vb_ref, zb_ref):
    xn = _row_rms(x_ref[...], g_ref[...]).astype(BF16)
    cos, sa, sb, bd = cos_ref[...], sa_ref[...], sb_ref[...], bd_ref[...]
    lo_half = lax.broadcasted_iota(I32, (TM, LANES), 1) < HEAD_DIM

    def proj(c0, n):
        return _nn(xn, w_ref[:, c0:c0 + n])

    def k_slab(c0, gain):
        return _rope(_seg_norm(proj(c0, LANES), bd, gain, HEAD_DIM), cos, sa, sb, HEAD_DIM // 2)

    _place_q_heads(proj(0, 512), bd, gains_ref[0:1, :], cos, sa, sb, qa_ref, Q_SCALE)
    ka_ref[...] = k_slab(512, gains_ref[1:2, :]).astype(BF16)
    va_ref[...] = _with_ones(proj(640, LANES))
    qi = proj(768, 256)
    for j in range(2):
        s = _rope(qi[:, LANES * j:LANES * (j + 1)], cos, sa, sb, HEAD_DIM // 2)
        qi_ref[2 * j] = jnp.where(lo_half, s, 0.0).astype(BF16)
        qi_ref[2 * j + 1] = jnp.where(lo_half, pltpu.roll(s, HEAD_DIM, 1), 0.0).astype(BF16)
    kiw = proj(1024, LANES)
    wi_ref[...] = kiw
    ki_ref[...] = _rope(_seg_norm(kiw, bd, gains_ref[2:3, :], HEAD_DIM), cos, sa, sb, HEAD_DIM // 2).astype(BF16)
    za_ref[...] = proj(1152, 512)
    _place_q_heads(proj(1664, 512), bd, gains_ref[3:4, :], cos, sa, sb, qb_ref, Q_SCALE)
    kb_ref[...] = k_slab(2176, gains_ref[4:5, :]).astype(BF16)
    vb_ref[...] = _with_ones(proj(2304, LANES))
    zb_ref[...] = proj(2432, 512)


def _count_chunks(keys_ref, nch, nq, pred):
    def body(c, acc):
        return acc + jnp.sum(pred(keys_ref[c]).reshape(CH // COUNT_ROWS, COUNT_ROWS, nq), axis=0)

    acc = lax.fori_loop(0, nch, body, jnp.zeros((COUNT_ROWS, nq), F32))
    return jnp.sum(acc, axis=0, keepdims=True)


def _kth_threshold(count, nq, k):
    def bit_body(it, carry):
        tb, above = carry
        cand_b = tb | jnp.left_shift(jnp.int32(1), 31 - it)
        cand_s = cand_b ^ INT_MIN
        cnt = count(lambda kk: jnp.where(kk >= cand_s, 1.0, 0.0))
        ok = cnt >= k
        return jnp.where(ok, cand_b, tb), jnp.where(ok, above, cnt)

    tb, above = lax.fori_loop(0, 32, bit_body, (jnp.zeros((1, nq), I32), jnp.zeros((1, nq), F32)))
    return tb ^ INT_MIN, k - above


def _dsa_kernel(qi_ref, wi_ref, ki_ref, qa_ref, ka_ref, va_ref, za_ref, tril_ref, o_ref,
                keys_ref, bias_ref, ties_ref, m_ref, acc_ref, *, topk):
    TQ = TQ_ATT
    i = pl.program_id(1)
    t0 = i * TQ
    nch = (t0 + TQ + CH - 1) // CH
    key_c = lax.broadcasted_iota(I32, (CH, TQ), 0)
    qry_c = t0 + lax.broadcasted_iota(I32, (CH, TQ), 1)

    qi = qi_ref[...].reshape(IDX_HEADS * TQ, LANES)
    w_t = wi_ref[...].T * ((HEAD_DIM * IDX_HEADS) ** -0.5)
    ws = [w_t[HEAD_DIM + h:HEAD_DIM + h + 1, :] for h in range(IDX_HEADS)]

    def score_chunk(c):
        off = pl.multiple_of(c * CH, CH)
        lg = _nt(ki_ref[pl.ds(off, CH), :], qi)
        sc = ws[0] * jnp.maximum(lg[:, 0:TQ], 0.0)
        for h in range(1, IDX_HEADS):
            sc = sc + ws[h] * jnp.maximum(lg[:, h * TQ:(h + 1) * TQ], 0.0)
        sc = jnp.where(sc == 0.0, 0.0, sc)
        keys_ref[c] = jnp.where(key_c + off <= qry_c, _order_key(sc), INT_MIN)

    _unrolled_loop(nch, score_chunk, unroll=2)

    count = functools.partial(_count_chunks, keys_ref, nch, TQ)
    thr, need = _kth_threshold(count, TQ, float(topk))

    tril = tril_ref[...]

    ties_ref[...] = jnp.zeros(ties_ref.shape, F32)

    def select_chunk(c):
        kk = keys_ref[c]
        pref = _nn(tril, jnp.where(kk == thr, 1.0, 0.0).astype(BF16)) + ties_ref[0:1, :]
        ties_ref[0:1, :] = pref[CH - 1:CH, :]
        tie = jnp.where(kk == thr, jnp.where(pref <= need, 0.0, NEG_INF), NEG_INF)
        bias = jnp.where(key_c + c * CH <= qry_c, jnp.where(kk > thr, 0.0, tie), NEG_INF)
        bias_ref[c] = bias.T.astype(BF16)

    _unrolled_loop(nch, select_chunk)

    _flash_init(m_ref, acc_ref)
    q_all = qa_ref[...].reshape(N_HEADS * TQ, LANES)

    def att_chunk(c):
        off = pl.multiple_of(c * CH, CH)
        s = _nt(q_all, ka_ref[pl.ds(off, CH), :]).astype(BF16) + _stack_rows(bias_ref[c], N_HEADS)
        _flash_update(s, va_ref[pl.ds(off, CH), :], m_ref, acc_ref, slice(None))

    _unrolled_loop(nch, att_chunk)
    _gated_store(_flash_result(acc_ref), za_ref, o_ref, TQ)


def _swa_kernel(q_ref, kp_ref, kc_ref, vp_ref, vc_ref, sink_ref, z_ref, o_ref):
    i = pl.program_id(1)
    qi = lax.broadcasted_iota(I32, (TQ, 2 * TQ), 0)
    kj = lax.broadcasted_iota(I32, (TQ, 2 * TQ), 1)
    sink = sink_ref[...]
    for half in range(2):
        q = q_ref[:, half * TQ:(half + 1) * TQ, :].reshape(N_HEADS * TQ, LANES)
        if half == 0:
            k2 = jnp.concatenate([kp_ref[...], kc_ref[0:TQ, :]], axis=0)
            v2 = jnp.concatenate([vp_ref[...], vc_ref[0:TQ, :]], axis=0)
            lo = jnp.where(i > 0, 0, TQ)
        else:
            k2, v2, lo = kc_ref[...], vc_ref[...], 0
        keep = jnp.where(kj >= lo, jnp.where(kj > qi, jnp.where(kj <= qi + TQ, 0.0, NEG_INF), NEG_INF), NEG_INF)
        s = _nt(q, k2) + _stack_rows(keep, N_HEADS)
        m = jnp.maximum(jnp.max(s, axis=1, keepdims=True), sink)
        p = jnp.exp2(s - jnp.concatenate([m, m], axis=1))
        o = _nn(p.astype(BF16), v2)
        den = o[:, LANES:2 * LANES] + jnp.exp2(sink - m)
        _gated_store(o[:, 0:LANES] / den, z_ref, o_ref, TQ, half * TQ)


def _out_proj_kernel(x_ref, ma_ref, mb_ref, w_ref, o_ref):
    half = ma_ref.shape[1]
    o_ref[...] = x_ref[...] + _nn(ma_ref[...], w_ref[0:half, :]) + _nn(mb_ref[...], w_ref[half:2 * half, :])


def _proj_cd_kernel(x_ref, g_ref, w_ref, cos_ref, sa_ref, sb_ref, bd_ref, gains_ref,
                    qc_ref, kc_ref, vc_ref, ks_ref, vs_ref, kw_ref, vw_ref, gt_ref, zc_ref,
                    cq_ref, ckv_ref, kr_ref, zd_ref):
    xn = _row_rms(x_ref[...], g_ref[...]).astype(BF16)
    cos, sa, sb, bd = cos_ref[...], sa_ref[...], sb_ref[...], bd_ref[...]

    def proj(c0, n):
        return _nn(xn, w_ref[:, c0:c0 + n])

    def k_slab(c0, gain):
        return _rope(_seg_norm(proj(c0, LANES), bd, gain, HEAD_DIM), cos, sa, sb, HEAD_DIM // 2)

    _place_q_heads(proj(0, 512), bd, gains_ref[0:1, :], cos, sa, sb, qc_ref, Q_SCALE)
    kc_ref[...] = proj(512, LANES)
    vc_ref[...] = proj(640, LANES)
    ks_ref[...] = k_slab(768, gains_ref[1:2, :]).astype(BF16)
    vs_ref[...] = _with_ones(proj(896, LANES))
    kw_ref[...] = k_slab(1024, gains_ref[2:3, :]).astype(BF16)
    vw_ref[...] = _with_ones(proj(1152, LANES))
    gt_ref[...] = _sigmoid(proj(1280, LANES))
    zc_ref[...] = proj(1408, 512)
    cq_ref[...] = proj(1920, 256)
    ckv_ref[...] = proj(2176, LANES)
    kr_ref[...] = proj(2304, LANES)
    zd_ref[...] = proj(2432, 512)


def _compress_kernel(c_ref, pe_ref, w1a_ref, w1b_ref, w2_ref, gain_ref, cos_ref, sa_ref, sb_ref, bd_ref,
                     o_ref, *, is_key):
    c = c_ref[...]
    n = c.shape[0]
    a = _nn((c + pe_ref[0:1, :]).astype(BF16), w1a_ref[...])
    b = _nn((c + pe_ref[1:2, :]).astype(BF16), w1b_ref[...])
    h = _silu(a + pltpu.roll(b, n - 1, 0))
    y = _nn(h.astype(BF16), w2_ref[...])
    if is_key:
        y = _rope(_seg_norm(y, bd_ref[...], gain_ref[...], HEAD_DIM), cos_ref[...], sa_ref[...], sb_ref[...],
                  HEAD_DIM // 2)
    o_ref[...] = y.astype(BF16)


def _nsa_kernel(q_ref, kc_ref, vc_ref, ks_ref, vs_ref, kw_ref, vw_ref, gt_ref, z_ref, wsel_ref, o_ref,
                m_ref, acc_ref, *, n_sel):
    TQ = TQ_ATT
    i = pl.program_id(1)
    t0 = i * TQ
    nch = (t0 + TQ + CH - 1) // CH
    ncmp = kc_ref.shape[0]
    rows_all = N_HEADS * TQ
    q = q_ref[...].reshape(rows_all, LANES)

    cend = lax.broadcasted_iota(I32, (TQ, ncmp), 1) * CMP_STRIDE + (2 * CMP_STRIDE - 1)
    bias_c = jnp.where(cend <= t0 + lax.broadcasted_iota(I32, (TQ, ncmp), 0), 0.0, NEG_INF)
    s = _nt(q, kc_ref[...]) + _stack_rows(bias_c, N_HEADS)
    m = jnp.max(s, axis=1, keepdims=True)
    m = jnp.where(m > NEG_INF, m, 0.0)
    p = jnp.exp2(s - m)
    p = p / jnp.maximum(jnp.sum(p, axis=1, keepdims=True), 1e-30)
    o_cmp = _nn(p.astype(BF16), vc_ref[...])

    span = NSA_WINDOW + TQ
    start = pl.multiple_of(jnp.maximum(t0 - NSA_WINDOW, 0), TQ)
    diff = t0 + lax.broadcasted_iota(I32, (TQ, span), 0) - (start + lax.broadcasted_iota(I32, (TQ, span), 1))
    bias_w = jnp.where(diff >= 0, jnp.where(diff < NSA_WINDOW, 0.0, NEG_INF), NEG_INF)
    sw = _nt(q, kw_ref[pl.ds(start, span), :]).astype(BF16) + _stack_rows(bias_w.astype(BF16), N_HEADS)
    pw = jnp.exp2(sw - jnp.max(sw, axis=1, keepdims=True))
    o_win = _nn(pw, vw_ref[pl.ds(start, span), :])
    o_win = o_win[:, 0:LANES] / o_win[:, LANES:2 * LANES]

    blk = lax.broadcasted_iota(I32, (LANES, TQ), 0)
    cur = (t0 + lax.broadcasted_iota(I32, (LANES, TQ), 1)) >> 6
    wsel_t = wsel_ref[...]
    keys = []
    for g in range(N_KV):
        imp = p[(GRP * g) * TQ:(GRP * g + 1) * TQ]
        for r in range(1, GRP):
            imp = imp + p[(GRP * g + r) * TQ:(GRP * g + r + 1) * TQ]
        hi = imp.astype(BF16)
        lo = (imp - hi.astype(F32)).astype(BF16)
        imp_s = _nt(wsel_t, hi) + _nt(wsel_t, lo)
        forced = jnp.where(blk == 0, jnp.inf, jnp.where(blk >= cur - 1, jnp.inf, imp_s))
        keys.append(_order_key(jnp.where(blk <= cur, forced, NEG_INF)))
    kk = jnp.concatenate(keys, axis=1)

    def bit_body(it, tb):
        cand_b = tb | jnp.left_shift(jnp.int32(1), 31 - it)
        cnt = jnp.sum(jnp.where(kk >= (cand_b ^ INT_MIN), 1.0, 0.0), axis=0, keepdims=True)
        return jnp.where(cnt >= n_sel, cand_b, tb)

    thr = lax.fori_loop(0, 32, bit_body, jnp.zeros((1, N_KV * TQ), I32)) ^ INT_MIN
    need = n_sel - jnp.sum(jnp.where(kk > thr, 1.0, 0.0), axis=0, keepdims=True)
    tril = jnp.where(lax.broadcasted_iota(I32, (LANES, LANES), 0) >= lax.broadcasted_iota(I32, (LANES, LANES), 1),
                     1.0, 0.0).astype(BF16)
    pref = _nn(tril, jnp.where(kk == thr, 1.0, 0.0).astype(BF16))
    tie = jnp.where(kk == thr, jnp.where(pref <= need, 1.0, 0.0), 0.0)
    sel_t = jnp.where(kk > thr, 1.0, tie)
    sel = jnp.concatenate([sel_t[:, g * TQ:(g + 1) * TQ].T for g in range(N_KV)], axis=0).astype(BF16)

    _flash_init(m_ref, acc_ref)
    blk_row =lax.broadcasted_iota(I32, (LANES, CH), 0)
    blk_col = lax.broadcasted_iota(I32, (LANES, CH), 1) >> 6
    row_c = t0 + lax.broadcasted_iota(I32, (TQ, CH), 0)
    col_c = lax.broadcasted_iota(I32, (TQ, CH), 1)

    def slc_chunk(c):
        off = pl.multiple_of(c * CH, CH)
        expand = jnp.where(blk_row == blk_col + c * (CH // SLC_BLOCK), 1.0, 0.0).astype(BF16)
        tok = _nn(sel, expand)
        causal = col_c + off <= row_c
        parts = []
        for g in range(N_KV):
            bias_g = jnp.where(causal, jnp.where(tok[g * TQ:(g + 1) * TQ] > 0.5, 0.0, NEG_INF), NEG_INF)
            parts.extend([bias_g.astype(BF16)] * GRP)
        sc = _nt(q, ks_ref[pl.ds(off, CH), :]).astype(BF16) + jnp.concatenate(parts, axis=0)
        _flash_update(sc, vs_ref[pl.ds(off, CH), :], m_ref, acc_ref, slice(None))

    _unrolled_loop(nch, slc_chunk)
    o_slc = _flash_result(acc_ref)

    gates = gt_ref[...]
    outs = []
    for h in range(N_HEADS):
        rs = slice(h * TQ, (h + 1) * TQ)
        outs.append(gates[:, 3 * h:3 * h + 1] * o_cmp[rs] + gates[:, 3 * h + 1:3 * h + 2] * o_slc[rs]
                    + gates[:, 3 * h + 2:3 * h + 3] * o_win[rs])
    _gated_store(jnp.concatenate(outs, axis=0), z_ref, o_ref, TQ)


def _mla_prep_kernel(cq_ref, ckv_ref, kr_ref, wq_ref, wkv_ref, lat_ref, gains_ref, cos_ref, sa_ref, sb_ref,
                     bd64_ref, bd32_ref, q_ref, k_ref, v_ref):
    lane = lax.broadcasted_iota(I32, (TM, LANES), 1)
    lo_half = lane < NOPE_DIM
    cos, sa, sb = cos_ref[...], sa_ref[...], sb_ref[...]
    bd64, bd32 = bd64_ref[...], bd32_ref[...]
    scale = (NOPE_DIM + ROPE_DIM) ** -0.5 * LOG2E

    def rope_slab(s, gain):
        return _rope(_seg_norm(s, bd32, gain, ROPE_DIM), cos, sa, sb, ROPE_DIM // 2)

    q = _nn(_row_rms(cq_ref[...], lat_ref[0:1, :]).astype(BF16), wq_ref[...])
    kv = _nn(_row_rms(ckv_ref[...], lat_ref[1:2, 0:KV_LORA]).astype(BF16), wkv_ref[...])
    k_rope = pltpu.roll(rope_slab(kr_ref[...], gains_ref[3:4, :]), NOPE_DIM, 1)
    q_rope = [rope_slab(q[:, 512 + LANES * j:512 + LANES * (j + 1)], gains_ref[1:2, :]) * scale for j in range(2)]
    for j in range(4):
        qn = _seg_norm(q[:, LANES * j:LANES * (j + 1)], bd64, gains_ref[0:1, :], NOPE_DIM) * scale
        kn = _seg_norm(kv[:, LANES * j:LANES * (j + 1)], bd64, gains_ref[2:3, :], NOPE_DIM)
        for e in range(2):
            h = 2 * j + e
            qn_h = qn if e == 0 else pltpu.roll(qn, NOPE_DIM, 1)
            kn_h = kn if e == 0 else pltpu.roll(kn, NOPE_DIM, 1)
            shift = (NOPE_DIM - ROPE_DIM * (h % 4)) % LANES
            qr = q_rope[h // 4]
            qr_h = qr if shift == 0 else pltpu.roll(qr, shift, 1)
            q_ref[h] = jnp.where(lo_half, qn_h, jnp.where(lane < NOPE_DIM + ROPE_DIM, qr_h, 0.0)).astype(BF16)
            k_ref[h] = jnp.where(lo_half, kn_h, k_rope).astype(BF16)
    for j in range(4):
        v_ref[j] = _with_ones(kv[:, 512 + LANES * j:512 + LANES * (j + 1)])


def _mla_kernel(q_ref, k_ref, v_ref, z_ref, o_ref, m_ref, acc_ref):
    i = pl.program_id(2)
    tq = TQ_MLA
    _flash_init(m_ref, acc_ref)

    def step(c, masked):
        off = pl.multiple_of(c * CH, CH)
        for e in range(MLA_HEADS):
            k = k_ref[e, pl.ds(off, CH), :]
            v = v_ref[e // 2, pl.ds(off, CH), :]
            for rb in range(tq // ROW_BLOCK):
                s = _nt(q_ref[e, rb * ROW_BLOCK:(rb + 1) * ROW_BLOCK, :], k)
                if masked:
                    qpos = rb * ROW_BLOCK + lax.broadcasted_iota(I32, (ROW_BLOCK, CH), 0)
                    s = jnp.where(lax.broadcasted_iota(I32, (ROW_BLOCK, CH), 1) <= qpos, s, NEG_INF)
                _flash_update(s, v, m_ref, acc_ref, slice(e * tq + rb * ROW_BLOCK, e * tq + (rb + 1) * ROW_BLOCK))

    _unrolled_loop(i, lambda c: step(c, False))
    step(i, True)
    o = _flash_result(acc_ref)
    lo_half = lax.broadcasted_iota(I32, (tq, LANES), 1) < NOPE_DIM
    for pr in range(MLA_HEADS // 2):
        slab = jnp.where(lo_half, o[(2 * pr) * tq:(2 * pr + 1) * tq], o[(2 * pr + 1) * tq:(2 * pr + 2) * tq])
        lanes = slice(pr * LANES, (pr + 1) * LANES)
        o_ref[:, lanes] = (slab * _silu(z_ref[:, lanes])).astype(BF16)


def _rope_tables(pos, dim):
    inv = jnp.power(jnp.float32(ROPE_THETA), -jnp.arange(0, dim, 2, dtype=F32) / dim)
    ang = pos.astype(F32)[:, None] * inv[None, :]
    cos, sin = jnp.cos(ang), jnp.sin(ang)
    reps = LANES // dim
    zero = jnp.zeros_like(sin)
    cos_t = jnp.tile(jnp.concatenate([cos, cos], axis=1), (1, reps))
    sa_t = jnp.tile(jnp.concatenate([zero, sin], axis=1), (1, reps))
    sb_t = jnp.tile(jnp.concatenate([-sin, zero], axis=1), (1, reps))
    return cos_t, sa_t, sb_t


def _block_diag(group):
    r = jnp.arange(LANES) // group
    bd = (r[:, None] == r[None, :]).astype(BF16)
    return jnp.concatenate([bd, bd], axis=0)


def _tile_gain(g, width=LANES):
    return jnp.tile(g.astype(F32), width // g.shape[0])


def _pad_lanes(g, width=LANES):
    return jnp.concatenate([g.astype(F32), jnp.zeros((width - g.shape[0],), F32)])


def _pad_cols(w, width):
    return jnp.concatenate([w, jnp.zeros((w.shape[0], width - w.shape[1]), w.dtype)], axis=1)


def _params(*sem, flags=None):
    return pltpu.CompilerParams(dimension_semantics=sem, vmem_limit_bytes=VMEM_LIMIT, flags=flags)


def kernel(x, ab_norm, ab_w_in, a_qk_norm, a_kidx_norm, b_qk_norm, b_sinks, ab_w_out, cd_norm, cd_w_in, c_q_norm, c_k_norm, c_cmp_pe, c_cmp_w1, c_cmp_w2, d_q_lat_norm, d_kv_lat_norm, d_w_uq, d_w_ukv, d_nope_norm, d_rope_norm, cd_w_out):
    bsz, seq, d = x.shape
    n = bsz * seq
    assert d == D_MODEL and seq % CH == 0 and seq >= NSA_WINDOW + TQ_ATT and seq // SLC_BLOCK <= LANES
    nq = seq // TQ
    nrow = n // TM
    srow = seq // TM
    xf = x.reshape(n, d)
    pos = jnp.arange(seq)
    cos64, sa64, sb64 = _rope_tables(pos, HEAD_DIM)
    cos32, sa32, sb32 = _rope_tables(pos, ROPE_DIM)
    bd64, bd32 = _block_diag(HEAD_DIM), _block_diag(ROPE_DIM)
    tril = (jnp.arange(CH)[:, None] >= jnp.arange(CH)[None, :]).astype(BF16)

    def row_spec(width, rows=TM):
        return pl.BlockSpec((rows, width), lambda i: (i, 0))

    def const_spec(shape):
        return pl.BlockSpec(shape, lambda i: tuple(0 for _ in shape))

    tab_spec = pl.BlockSpec((TM, LANES), lambda i: (i % srow, 0))
    head_spec = lambda nh: pl.BlockSpec((nh, TM, LANES), lambda i: (0, i, 0))

    wa = ab_w_in[0]
    w_ab = jnp.concatenate([wa[:, 0:1088], _pad_cols(wa[:, 1088:1092], 64), wa[:, 1092:]], axis=1).astype(BF16)
    gains_ab = jnp.stack([_tile_gain(a_qk_norm[0, 0]), _tile_gain(a_qk_norm[0, 1]), _pad_lanes(a_kidx_norm[0]),
                          _tile_gain(b_qk_norm[0, 0]), _tile_gain(b_qk_norm[0, 1])]
                         + [jnp.zeros((LANES,), F32)] * 3)
    sds = jax.ShapeDtypeStruct
    qa, ka, va, qi, ki, wi, za, qb, kb, vb, zb = pl.pallas_call(
        _proj_ab_kernel,
        grid=(nrow,),
        in_specs=[row_spec(d), const_spec((1, d)), const_spec((d, AB_W)), tab_spec, tab_spec, tab_spec,
                  const_spec((2 * LANES, LANES)), const_spec((8, LANES))],
        out_specs=[head_spec(8), row_spec(LANES), row_spec(2 * LANES), head_spec(4), row_spec(LANES), row_spec(LANES),
                   row_spec(512), head_spec(8), row_spec(LANES), row_spec(2 * LANES), row_spec(512)],
        out_shape=[sds((8, n, LANES), BF16), sds((n, LANES), BF16), sds((n, 2 * LANES), BF16),
                   sds((4, n, LANES), BF16), sds((n, LANES), BF16), sds((n, LANES), F32), sds((n, 512), F32),
                   sds((8, n, LANES), BF16), sds((n, LANES), BF16), sds((n, 2 * LANES), BF16), sds((n, 512), F32)],
        compiler_params=_params("parallel"),
    )(xf, ab_norm[0][None, :], w_ab, cos64, sa64, sb64, bd64, gains_ab)

    nqa = seq // TQ_ATT
    qt_heads = lambda nh: pl.BlockSpec((nh, TQ_ATT, LANES), lambda b, i: (0, b * nqa + i, 0))
    qt_rows = lambda width: pl.BlockSpec((TQ_ATT, width), lambda b, i: (b * nqa + i, 0))
    seq_rows = pl.BlockSpec((seq, LANES), lambda b, i: (b, 0))
    seq_rows_v = pl.BlockSpec((seq, 2 * LANES), lambda b, i: (b, 0))
    att_scratch = [pltpu.VMEM((N_HEADS * TQ_ATT, LANES), F32), pltpu.VMEM((N_HEADS * TQ_ATT, 2 * LANES), F32)]

    mix_a = pl.pallas_call(
        functools.partial(_dsa_kernel, topk=min(DSA_TOPK, seq // 4)),
        grid=(bsz, nqa),
        in_specs=[qt_heads(4), qt_rows(LANES), seq_rows, qt_heads(8), seq_rows, seq_rows_v, qt_rows(512),
                  pl.BlockSpec((CH, CH), lambda b, i: (0, 0))],
        out_specs=qt_rows(512),
        out_shape=sds((n, 512), BF16),
        scratch_shapes=[pltpu.VMEM((seq // CH, CH, TQ_ATT), I32), pltpu.VMEM((seq // CH, TQ_ATT, CH), BF16),
                        pltpu.VMEM((8, TQ_ATT), F32)] + att_scratch,
        compiler_params=_params("parallel", "arbitrary"),
    )(qi, wi, ki, qa, ka, va, za, tril)

    nq2 = nq // 2
    prev_rows = lambda width: pl.BlockSpec((TQ, width), lambda b, i: (b * nq + jnp.maximum(2 * i - 1, 0), 0))
    pair_rows = lambda width: pl.BlockSpec((2 * TQ, width), lambda b, i: (b * nq2 + i, 0))
    sink_rows = jnp.broadcast_to(jnp.repeat(b_sinks[0].astype(F32) * LOG2E, TQ)[:, None], (N_HEADS * TQ, LANES))
    mix_b = pl.pallas_call(
        _swa_kernel,
        grid=(bsz, nq2),
        in_specs=[pl.BlockSpec((N_HEADS, 2 * TQ, LANES), lambda b, i: (0, b * nq2 + i, 0)),
                  prev_rows(LANES), pair_rows(LANES), prev_rows(2 * LANES), pair_rows(2 * LANES),
                  pl.BlockSpec((N_HEADS * TQ, LANES), lambda b, i: (0, 0)), pair_rows(512)],
        out_specs=pair_rows(512),
        out_shape=sds((n, 512), BF16),
        compiler_params=_params("parallel", "arbitrary"),
    )(qb, kb, kb, vb, vb, sink_rows, zb)

    def out_proj(xin, m0, m1, w):
        return pl.pallas_call(
            _out_proj_kernel,
            grid=(nrow,),
            in_specs=[row_spec(d), row_spec(512), row_spec(512), const_spec((2 * 512, d))],
            out_specs=row_spec(d),
            out_shape=sds((n, d), F32),
            compiler_params=_params("parallel"),
        )(xin, m0, m1, w.astype(BF16))

    x1 = out_proj(xf, mix_a, mix_b, ab_w_out[0])

    wc = cd_w_in[0]
    w_cd = jnp.concatenate([wc[:, 0:1280], _pad_cols(wc[:, 1280:1304], LANES), wc[:, 1304:2200],
                            _pad_cols(wc[:, 2200:2232], LANES), wc[:, 2232:]], axis=1).astype(BF16)
    gains_cd = jnp.stack([_tile_gain(c_q_norm[0]), _tile_gain(c_k_norm[0, 1]), _tile_gain(c_k_norm[0, 2])]
                         + [jnp.zeros((LANES,), F32)] * 5)
    qc, kc_raw, vc_raw, ks, vs, kw, vw, gates, zc, cq, ckv, kr, zd = pl.pallas_call(
        _proj_cd_kernel,
        grid=(nrow,),
        in_specs=[row_spec(d), const_spec((1, d)), const_spec((d, AB_W)), tab_spec, tab_spec, tab_spec,
                  const_spec((2 * LANES, LANES)), const_spec((8, LANES))],
        out_specs=[head_spec(8), row_spec(LANES), row_spec(LANES), row_spec(LANES), row_spec(2 * LANES),
                   row_spec(LANES), row_spec(2 * LANES), row_spec(LANES), row_spec(512), row_spec(256),
                   row_spec(LANES), row_spec(LANES), row_spec(512)],
        out_shape=[sds((8, n, LANES), BF16), sds((n, LANES), F32), sds((n, LANES), F32), sds((n, LANES), BF16),
                   sds((n, 2 * LANES), BF16), sds((n, LANES), BF16), sds((n, 2 * LANES), BF16), sds((n, LANES), F32),
                   sds((n, 512), F32), sds((n, 256), F32), sds((n, LANES), F32), sds((n, LANES), F32),
                   sds((n, 512), F32)],
        compiler_params=_params("parallel"),
    )(x1, cd_norm[0][None, :], w_cd, cos64, sa64, sb64, bd64, gains_cd)

    ncmp = seq // CMP_STRIDE
    cw = CMP_STRIDE * N_KV * HEAD_DIM
    eye = jnp.eye(N_KV, dtype=F32)
    ccos, csa, csb = _rope_tables(jnp.arange(ncmp) * CMP_STRIDE + (2 * CMP_STRIDE - 1), HEAD_DIM)

    def compress(raw, which, is_key):
        w1 = c_cmp_w1[0, which].reshape(2, CMP_STRIDE, HEAD_DIM, -1)
        hid = w1.shape[-1]
        w1e = jnp.einsum("tjdu,gh->tjgdhu", w1, eye).reshape(2, cw, N_KV * hid).astype(BF16)
        w2e = jnp.einsum("ud,gh->guhd", c_cmp_w2[0, which], eye).reshape(N_KV * hid, N_KV * HEAD_DIM).astype(BF16)
        pe = c_cmp_pe[0, which].reshape(2, CMP_STRIDE, 1, HEAD_DIM)
        pe = jnp.broadcast_to(pe, (2, CMP_STRIDE, N_KV, HEAD_DIM)).reshape(2, cw)
        return pl.pallas_call(
            functools.partial(_compress_kernel, is_key=is_key),
            grid=(bsz,),
            in_specs=[pl.BlockSpec((ncmp, cw), lambda b: (b, 0)), const_spec((2, cw)),
                      const_spec((cw, N_KV * hid)), const_spec((cw, N_KV * hid)),
                      const_spec((N_KV * hid, LANES)), const_spec((1, LANES)),
                      const_spec((ncmp, LANES)), const_spec((ncmp, LANES)), const_spec((ncmp, LANES)),
                      const_spec((2 * LANES, LANES))],
            out_specs=pl.BlockSpec((ncmp, LANES), lambda b: (b, 0)),
            out_shape=sds((bsz * ncmp, LANES), BF16),
            compiler_params=_params("parallel"),
        )(raw.reshape(bsz * ncmp, cw), pe, w1e[0], w1e[1], w2e, _tile_gain(c_k_norm[0, 0])[None, :],
          ccos, csa, csb, bd64)

    kc = compress(kc_raw, 0, True)
    vc = compress(vc_raw, 1, False)

    ratio = SLC_BLOCK // CMP_STRIDE
    mm = jnp.arange(ncmp)[:, None]
    jj = jnp.arange(LANES)[None, :]
    wsel = (jnp.where((mm == ratio * jj - 1) | (mm == ratio * jj + ratio - 1), 1.0, 0.0)
            + jnp.where((mm >= ratio * jj) & (mm < ratio * jj + ratio - 1), 2.0, 0.0))
    wsel_t = jnp.where((mm < ncmp - 1) & (jj < seq // SLC_BLOCK), wsel, 0.0).astype(BF16).T

    cmp_rows = pl.BlockSpec((ncmp, LANES), lambda b, i: (b, 0))
    mix_c = pl.pallas_call(
        functools.partial(_nsa_kernel, n_sel=min(SLC_TOPN, seq // SLC_BLOCK)),
        grid=(bsz, nqa),
        in_specs=[qt_heads(8), cmp_rows, cmp_rows, seq_rows, seq_rows_v, seq_rows, seq_rows_v, qt_rows(LANES),
                  qt_rows(512), pl.BlockSpec((LANES, ncmp), lambda b, i: (0, 0))],
        out_specs=qt_rows(512),
        out_shape=sds((n, 512), BF16),
        scratch_shapes=att_scratch,
        compiler_params=_params("parallel", "arbitrary"),
    )(qc, kc, vc, ks, vs, kw, vw, gates, zc, wsel_t)

    wq = d_w_uq[0].reshape(Q_LORA, N_HEADS, NOPE_DIM + ROPE_DIM)
    wq = jnp.concatenate([wq[:, :, :NOPE_DIM].reshape(Q_LORA, -1), wq[:, :, NOPE_DIM:].reshape(Q_LORA, -1)],
                         axis=1).astype(BF16)
    wkv = d_w_ukv[0].reshape(KV_LORA, N_HEADS, NOPE_DIM + HEAD_DIM)
    wkv = jnp.concatenate([wkv[:, :, :NOPE_DIM].reshape(KV_LORA, -1), wkv[:, :, NOPE_DIM:].reshape(KV_LORA, -1)],
                          axis=1).astype(BF16)
    lat_gains = jnp.stack([d_q_lat_norm[0].astype(F32), _pad_lanes(d_kv_lat_norm[0], Q_LORA)]
                          + [jnp.zeros((Q_LORA,), F32)] * 6)
    gains_d = jnp.stack([_tile_gain(d_nope_norm[0, 0]), _tile_gain(d_rope_norm[0, 0]), _tile_gain(d_nope_norm[0, 1]),
                         _pad_lanes(d_rope_norm[0, 1])] + [jnp.zeros((LANES,), F32)] * 4)
    q_cat, k_cat, v_d = pl.pallas_call(
        _mla_prep_kernel,
        grid=(nrow,),
        in_specs=[row_spec(Q_LORA), row_spec(LANES), row_spec(LANES), const_spec((Q_LORA, 768)),
                  const_spec((KV_LORA, 1024)), const_spec((8, Q_LORA)), const_spec((8, LANES)),
                  tab_spec, tab_spec, tab_spec, const_spec((2 * LANES, LANES)), const_spec((2 * LANES, LANES))],
        out_specs=[head_spec(8), head_spec(8), pl.BlockSpec((4, TM, 2 * LANES), lambda i: (0, i, 0))],
        out_shape=[sds((8, n, LANES), BF16), sds((8, n, LANES), BF16), sds((4, n, 2 * LANES), BF16)],
        compiler_params=_params("parallel"),
    )(cq, ckv, kr, wq, wkv, lat_gains, gains_d, cos32, sa32, sb32, bd64, bd32)

    nqm = seq // TQ_MLA
    mix_d = pl.pallas_call(
        _mla_kernel,
        grid=(bsz, N_HEADS // MLA_HEADS, nqm),
        in_specs=[pl.BlockSpec((MLA_HEADS, TQ_MLA, LANES), lambda b, hg, i: (hg, b * nqm + i, 0)),
                  pl.BlockSpec((MLA_HEADS, seq, LANES), lambda b, hg, i: (hg, b, 0), pipeline_mode=pl.Buffered(1)),
                  pl.BlockSpec((MLA_HEADS // 2, seq, 2 * LANES), lambda b, hg, i: (hg, b, 0),
                               pipeline_mode=pl.Buffered(1)),
                  pl.BlockSpec((TQ_MLA, MLA_HEADS * HEAD_DIM), lambda b, hg, i: (b * nqm + i, hg))],
        out_specs=pl.BlockSpec((TQ_MLA, MLA_HEADS * HEAD_DIM), lambda b, hg, i: (b * nqm + i, hg)),
        out_shape=sds((n, 512), BF16),
        scratch_shapes=[pltpu.VMEM((MLA_HEADS * TQ_MLA, LANES), F32),
                        pltpu.VMEM((MLA_HEADS * TQ_MLA, 2 * LANES), F32)],
        compiler_params=_params("parallel", "parallel", "arbitrary"),
    )(q_cat, k_cat, v_d, zd)

    x2 = out_proj(x1, mix_c, mix_d, cd_w_out[0])
    return x2.reshape(bsz, seq, d)
```

```python
import functools

import jax
import jax.numpy as jnp
from jax import lax
from jax.experimental import pallas as pl
from jax.experimental.pallas import tpu as pltpu

F32, BF16, I32 = jnp.float32, jnp.bfloat16, jnp.int32

D_MODEL = 1024
HEAD_DIM = 64
N_HEADS = 8
N_KV = 2
GRP = N_HEADS // N_KV
IDX_HEADS = 4
DSA_TOPK = 256
SWA_WINDOW = 128
CMP_STRIDE = 16
SLC_BLOCK = 64
SLC_TOPN = 16
NSA_WINDOW = 512
Q_LORA = 256
KV_LORA = 128
NOPE_DIM = 64
ROPE_DIM = 32
ROPE_THETA = 10000.0
EPS = 1e-6

LANES = 128
TM = 1024
TQ = 128
CH = 512
TQ_ATT = 256
TQ_MLA = 512
MLA_HEADS = 8
UNROLL = 4
COUNT_ROWS = 16
ROW_BLOCK = 128
VMEM_LIMIT = 56 * 1024 * 1024

NEG_INF = float("-inf")
M_INIT = -1e30
INT_MIN = -(2 ** 31)

AB_W = 2944
LOG2E = 1.4426950408889634
Q_SCALE = HEAD_DIM ** -0.5 * LOG2E


def _nn(a, b):
    return jnp.dot(a, b, preferred_element_type=F32)


def _nt(a, b):
    return lax.dot_general(a, b, (((1,), (1,)), ((), ())), preferred_element_type=F32)


def _sigmoid(z):
    return 1.0 / (1.0 + jnp.exp(-z))


def _silu(z):
    return z * _sigmoid(z)


def _row_rms(x, g):
    return x * lax.rsqrt(jnp.mean(x * x, axis=-1, keepdims=True) + EPS) * g


def _seg_norm(s, bd, gain, group):
    s2 = s * s
    hi = s2.astype(BF16)
    lo = (s2 - hi.astype(F32)).astype(BF16)
    ss = _nn(jnp.concatenate([hi, lo], axis=1), bd)
    return s * lax.rsqrt(ss * (1.0 / group) + EPS) * gain


def _rope(s, cos, sa, sb, half):
    return s * cos + pltpu.roll(s, half, 1) * sa + pltpu.roll(s, LANES - half, 1) * sb


def _order_key(v):
    bits = lax.bitcast_convert_type(v, I32)
    return bits ^ ((bits >> 31) & 0x7FFFFFFF)


def _stack_rows(x, n):
    return jnp.concatenate([x] * n, axis=0)


def _with_ones(v):
    return jnp.concatenate([v.astype(BF16), jnp.ones(v.shape, BF16)], axis=1)


def _flash_update(s, v_ext, m_ref, acc_ref, rows):
    m_prev = m_ref[rows]
    sb = s.astype(BF16)
    m_new = jnp.maximum(m_prev, jnp.max(sb, axis=1, keepdims=True).astype(F32))
    alpha = jnp.exp2(m_prev - m_new)
    p = jnp.exp2(sb - jnp.concatenate([m_new.astype(BF16)] * (s.shape[1] // LANES), axis=1))
    acc_ref[rows] = jnp.concatenate([alpha, alpha], axis=1) * acc_ref[rows] + _nn(p, v_ext)
    m_ref[rows] = m_new


def _unrolled_loop(n, step, unroll=UNROLL):
    def body(j, carry):
        for u in range(unroll):
            step(j * unroll + u)
        return carry

    lax.fori_loop(0, n // unroll, body, 0)
    base = (n // unroll) * unroll
    p = unroll // 2
    while p >= 1:
        def tail(base=base, p=p):
            for u in range(p):
                step(base + u)

        pl.when((n & p) != 0)(tail)
        base = base + (n & p)
        p //= 2


def _flash_init(m_ref, acc_ref):
    m_ref[...] = jnp.full(m_ref.shape, M_INIT, F32)
    acc_ref[...] = jnp.zeros(acc_ref.shape, F32)


def _flash_result(acc_ref):
    acc = acc_ref[...]
    return acc[:, 0:LANES] / jnp.maximum(acc[:, LANES:2 * LANES], 1e-30)


def _place_q_heads(y, bd, gain, cos, sa, sb, out_ref, scale):
    lo_half = lax.broadcasted_iota(I32, (y.shape[0], LANES), 1) < HEAD_DIM
    for j in range(4):
        s = y[:, LANES * j:LANES * (j + 1)]
        s = _rope(_seg_norm(s, bd, gain, HEAD_DIM), cos, sa, sb, HEAD_DIM // 2) * scale
        r = pltpu.roll(s, HEAD_DIM, 1)
        if j < 2:
            out_ref[2 * j] = jnp.where(lo_half, s, 0.0).astype(BF16)
            out_ref[2 * j + 1] = jnp.where(lo_half, r, 0.0).astype(BF16)
        else:
            out_ref[2 * j] = jnp.where(lo_half, 0.0, r).astype(BF16)
            out_ref[2 * j + 1] = jnp.where(lo_half, 0.0, s).astype(BF16)


def _gated_store(o, z_ref, o_ref, tq, row0=0):
    lo_half = lax.broadcasted_iota(I32, (tq, LANES), 1) < HEAD_DIM
    for j in range(4):
        a = o[(2 * j) * tq:(2 * j + 1) * tq]
        b = o[(2 * j + 1) * tq:(2 * j + 2) * tq]
        if j < 2:
            slab = jnp.where(lo_half, a, pltpu.roll(b, HEAD_DIM, 1))
        else:
            slab = jnp.where(lo_half, pltpu.roll(a, HEAD_DIM, 1), b)
        z = z_ref[row0:row0 + tq, LANES * j:LANES * (j + 1)]
        o_ref[row0:row0 + tq, LANES * j:LANES * (j + 1)] = (slab * _silu(z)).astype(BF16)


def _proj_ab_kernel(x_ref, g_ref, w_ref, cos_ref, sa_ref, sb_ref, bd_ref, gains_ref,
                    qa_ref, ka_ref, va_ref, qi_ref, ki_ref, wi_ref, za_ref,
                    qb_ref, kb_ref, vb_ref, zb_ref):
    xn = _row_rms(x_ref[...], g_ref[...]).astype(BF16)
    cos, sa, sb, bd = cos_ref[...], sa_ref[...], sb_ref[...], bd_ref[...]
    lo_half = lax.broadcasted_iota(I32, (TM, LANES), 1) < HEAD_DIM

    def proj(c0, n):
        return _nn(xn, w_ref[:, c0:c0 + n])

    def k_slab(c0, gain):
        return _rope(_seg_norm(proj(c0, LANES), bd, gain, HEAD_DIM), cos, sa, sb, HEAD_DIM // 2)

    _place_q_heads(proj(0, 512), bd, gains_ref[0:1, :], cos, sa, sb, qa_ref, Q_SCALE)
    ka_ref[...] = k_slab(512, gains_ref[1:2, :]).astype(BF16)
    va_ref[...] = _with_ones(proj(640, LANES))
    qi = proj(768, 256)
    for j in range(2):
        s = _rope(qi[:, LANES * j:LANES * (j + 1)], cos, sa, sb, HEAD_DIM // 2)
        qi_ref[2 * j] = jnp.where(lo_half, s, 0.0).astype(BF16)
        qi_ref[2 * j + 1] = jnp.where(lo_half, pltpu.roll(s, HEAD_DIM, 1), 0.0).astype(BF16)
    kiw = proj(1024, LANES)
    wi_ref[...] = kiw
    ki_ref[...] = _rope(_seg_norm(kiw, bd, gains_ref[2:3, :], HEAD_DIM), cos, sa, sb, HEAD_DIM // 2).astype(BF16)
    za_ref[...] = proj(1152, 512)
    _place_q_heads(proj(1664, 512), bd, gains_ref[3:4, :], cos, sa, sb, qb_ref, Q_SCALE)
    kb_ref[...] = k_slab(2176, gains_ref[4:5, :]).astype(BF16)
    vb_ref[...] = _with_ones(proj(2304, LANES))
    zb_ref[...] = proj(2432, 512)


def _count_chunks(keys_ref, nch, nq, pred):
    def body(c, acc):
        return acc + jnp.sum(pred(keys_ref[c]).reshape(CH // COUNT_ROWS, COUNT_ROWS, nq), axis=0)

    acc = lax.fori_loop(0, nch, body, jnp.zeros((COUNT_ROWS, nq), F32))
    return jnp.sum(acc, axis=0, keepdims=True)


def _kth_threshold(count, nq, k):
    def bit_body(it, carry):
        tb, above = carry
        cand_b = tb | jnp.left_shift(jnp.int32(1), 31 - it)
        cand_s = cand_b ^ INT_MIN
        cnt = count(lambda kk: jnp.where(kk >= cand_s, 1.0, 0.0))
        ok = cnt >= k
        return jnp.where(ok, cand_b, tb), jnp.where(ok, above, cnt)

    tb, above = lax.fori_loop(0, 32, bit_body, (jnp.zeros((1, nq), I32), jnp.zeros((1, nq), F32)))
    return tb ^ INT_MIN, k - above


def _dsa_kernel(qi_ref, wi_ref, ki_ref, qa_ref, ka_ref, va_ref, za_ref, tril_ref, o_ref,
                keys_ref, bias_ref, ties_ref, m_ref, acc_ref, *, topk):
    TQ = TQ_ATT
    i = pl.program_id(1)
    t0 = i * TQ
    nch = (t0 + TQ + CH - 1) // CH
    key_c = lax.broadcasted_iota(I32, (CH, TQ), 0)
    qry_c = t0 + lax.broadcasted_iota(I32, (CH, TQ), 1)

    qi = qi_ref[...].reshape(IDX_HEADS * TQ, LANES)
    w_t = wi_ref[...].T * ((HEAD_DIM * IDX_HEADS) ** -0.5)
    ws = [w_t[HEAD_DIM + h:HEAD_DIM + h + 1, :] for h in range(IDX_HEADS)]

    def score_chunk(c):
        off = pl.multiple_of(c * CH, CH)
        lg = _nt(ki_ref[pl.ds(off, CH), :], qi)
        sc = ws[0] * jnp.maximum(lg[:, 0:TQ], 0.0)
        for h in range(1, IDX_HEADS):
            sc = sc + ws[h] * jnp.maximum(lg[:, h * TQ:(h + 1) * TQ], 0.0)
        sc = jnp.where(sc == 0.0, 0.0, sc)
        keys_ref[c] = jnp.where(key_c + off <= qry_c, _order_key(sc), INT_MIN)

    _unrolled_loop(nch, score_chunk)

    count = functools.partial(_count_chunks, keys_ref, nch, TQ)
    thr, need = _kth_threshold(count, TQ, float(topk))

    tril = tril_ref[...]

    ties_ref[...] = jnp.zeros(ties_ref.shape, F32)

    def select_chunk(c):
        kk = keys_ref[c]
        pref = _nn(tril, jnp.where(kk == thr, 1.0, 0.0).astype(BF16)) + ties_ref[0:1, :]
        ties_ref[0:1, :] = pref[CH - 1:CH, :]
        tie = jnp.where(kk == thr, jnp.where(pref <= need, 0.0, NEG_INF), NEG_INF)
        bias = jnp.where(key_c + c * CH <= qry_c, jnp.where(kk > thr, 0.0, tie), NEG_INF)
        bias_ref[c] = bias.T.astype(BF16)

    _unrolled_loop(nch, select_chunk)

    _flash_init(m_ref, acc_ref)
    q_all = qa_ref[...].reshape(N_HEADS * TQ, LANES)

    def att_chunk(c):
        off = pl.multiple_of(c * CH, CH)
        s = _nt(q_all, ka_ref[pl.ds(off, CH), :]).astype(BF16) + _stack_rows(bias_ref[c], N_HEADS)
        _flash_update(s, va_ref[pl.ds(off, CH), :], m_ref, acc_ref, slice(None))

    _unrolled_loop(nch, att_chunk)
    _gated_store(_flash_result(acc_ref), za_ref, o_ref, TQ)


def _swa_kernel(q_ref, kp_ref, kc_ref, vp_ref, vc_ref, sink_ref, z_ref, o_ref):
    i = pl.program_id(1)
    qi = lax.broadcasted_iota(I32, (TQ, 2 * TQ), 0)
    kj = lax.broadcasted_iota(I32, (TQ, 2 * TQ), 1)
    sink = sink_ref[...]
    for half in range(2):
        q = q_ref[:, half * TQ:(half + 1) * TQ, :].reshape(N_HEADS * TQ, LANES)
        if half == 0:
            k2 = jnp.concatenate([kp_ref[...], kc_ref[0:TQ, :]], axis=0)
            v2 = jnp.concatenate([vp_ref[...], vc_ref[0:TQ, :]], axis=0)
            lo = jnp.where(i > 0, 0, TQ)
        else:
            k2, v2, lo = kc_ref[...], vc_ref[...], 0
        keep = jnp.where(kj >= lo, jnp.where(kj > qi, jnp.where(kj <= qi + TQ, 0.0, NEG_INF), NEG_INF), NEG_INF)
        s = _nt(q, k2) + _stack_rows(keep, N_HEADS)
        m = jnp.maximum(jnp.max(s, axis=1, keepdims=True), sink)
        p = jnp.exp2(s - jnp.concatenate([m, m], axis=1))
        o = _nn(p.astype(BF16), v2)
        den = o[:, LANES:2 * LANES] + jnp.exp2(sink - m)
        _gated_store(o[:, 0:LANES] / den, z_ref, o_ref, TQ, half * TQ)


def _out_proj_kernel(x_ref, ma_ref, mb_ref, w_ref, o_ref):
    half = ma_ref.shape[1]
    o_ref[...] = x_ref[...] + _nn(ma_ref[...], w_ref[0:half, :]) + _nn(mb_ref[...], w_ref[half:2 * half, :])


def _proj_cd_kernel(x_ref, g_ref, w_ref, cos_ref, sa_ref, sb_ref, bd_ref, gains_ref,
                    qc_ref, kc_ref, vc_ref, ks_ref, vs_ref, kw_ref, vw_ref, gt_ref, zc_ref,
                    cq_ref, ckv_ref, kr_ref, zd_ref):
    xn = _row_rms(x_ref[...], g_ref[...]).astype(BF16)
    cos, sa, sb, bd = cos_ref[...], sa_ref[...], sb_ref[...], bd_ref[...]

    def proj(c0, n):
        return _nn(xn, w_ref[:, c0:c0 + n])

    def k_slab(c0, gain):
        return _rope(_seg_norm(proj(c0, LANES), bd, gain, HEAD_DIM), cos, sa, sb, HEAD_DIM // 2)

    _place_q_heads(proj(0, 512), bd, gains_ref[0:1, :], cos, sa, sb, qc_ref, Q_SCALE)
    kc_ref[...] = proj(512, LANES)
    vc_ref[...] = proj(640, LANES)
    ks_ref[...] = k_slab(768, gains_ref[1:2, :]).astype(BF16)
    vs_ref[...] = _with_ones(proj(896, LANES))
    kw_ref[...] = k_slab(1024, gains_ref[2:3, :]).astype(BF16)
    vw_ref[...] = _with_ones(proj(1152, LANES))
    gt_ref[...] = _sigmoid(proj(1280, LANES))
    zc_ref[...] = proj(1408, 512)
    cq_ref[...] = proj(1920, 256)
    ckv_ref[...] = proj(2176, LANES)
    kr_ref[...] = proj(2304, LANES)
    zd_ref[...] = proj(2432, 512)


def _compress_kernel(c_ref, pe_ref, w1a_ref, w1b_ref, w2_ref, gain_ref, cos_ref, sa_ref, sb_ref, bd_ref,
                     o_ref, *, is_key):
    c = c_ref[...]
    n = c.shape[0]
    a = _nn((c + pe_ref[0:1, :]).astype(BF16), w1a_ref[...])
    b = _nn((c + pe_ref[1:2, :]).astype(BF16), w1b_ref[...])
    h = _silu(a + pltpu.roll(b, n - 1, 0))
    y = _nn(h.astype(BF16), w2_ref[...])
    if is_key:
        y = _rope(_seg_norm(y, bd_ref[...], gain_ref[...], HEAD_DIM), cos_ref[...], sa_ref[...], sb_ref[...],
                  HEAD_DIM // 2)
    o_ref[...] = y.astype(BF16)


def _nsa_kernel(q_ref, kc_ref, vc_ref, ks_ref, vs_ref, kw_ref, vw_ref, gt_ref, z_ref, wsel_ref, o_ref,
                m_ref, acc_ref, *, n_sel):
    TQ = TQ_ATT
    i = pl.program_id(1)
    t0 = i * TQ
    nch = (t0 + TQ + CH - 1) // CH
    ncmp = kc_ref.shape[0]
    rows_all = N_HEADS * TQ
    q = q_ref[...].reshape(rows_all, LANES)

    cend = lax.broadcasted_iota(I32, (TQ, ncmp), 1) * CMP_STRIDE + (2 * CMP_STRIDE - 1)
    bias_c = jnp.where(cend <= t0 + lax.broadcasted_iota(I32, (TQ, ncmp), 0), 0.0, NEG_INF)
    s = _nt(q, kc_ref[...]) + _stack_rows(bias_c, N_HEADS)
    m = jnp.max(s, axis=1, keepdims=True)
    m = jnp.where(m > NEG_INF, m, 0.0)
    p = jnp.exp2(s - m)
    p = p / jnp.maximum(jnp.sum(p, axis=1, keepdims=True), 1e-30)
    o_cmp = _nn(p.astype(BF16), vc_ref[...])

    span = NSA_WINDOW + TQ
    start = pl.multiple_of(jnp.maximum(t0 - NSA_WINDOW, 0), TQ)
    diff = t0 + lax.broadcasted_iota(I32, (TQ, span), 0) - (start + lax.broadcasted_iota(I32, (TQ, span), 1))
    bias_w = jnp.where(diff >= 0, jnp.where(diff < NSA_WINDOW, 0.0, NEG_INF), NEG_INF)
    sw = _nt(q, kw_ref[pl.ds(start, span), :]).astype(BF16) + _stack_rows(bias_w.astype(BF16), N_HEADS)
    pw = jnp.exp2(sw - jnp.max(sw, axis=1, keepdims=True))
    o_win = _nn(pw, vw_ref[pl.ds(start, span), :])
    o_win = o_win[:, 0:LANES] / o_win[:, LANES:2 * LANES]

    blk = lax.broadcasted_iota(I32, (LANES, TQ), 0)
    cur = (t0 + lax.broadcasted_iota(I32, (LANES, TQ), 1)) >> 6
    wsel_t = wsel_ref[...]
    keys = []
    for g in range(N_KV):
        imp = p[(GRP * g) * TQ:(GRP * g + 1) * TQ]
        for r in range(1, GRP):
            imp = imp + p[(GRP * g + r) * TQ:(GRP * g + r + 1) * TQ]
        hi = imp.astype(BF16)
        lo = (imp - hi.astype(F32)).astype(BF16)
        imp_s = _nt(wsel_t, hi) + _nt(wsel_t, lo)
        forced = jnp.where(blk == 0, jnp.inf, jnp.where(blk >= cur - 1, jnp.inf, imp_s))
        keys.append(_order_key(jnp.where(blk <= cur, forced, NEG_INF)))
    kk = jnp.concatenate(keys, axis=1)

    def bit_body(it, tb):
        cand_b = tb | jnp.left_shift(jnp.int32(1), 31 - it)
        cnt = jnp.sum(jnp.where(kk >= (cand_b ^ INT_MIN), 1.0, 0.0), axis=0, keepdims=True)
        return jnp.where(cnt >= n_sel, cand_b, tb)

    thr = lax.fori_loop(0, 32, bit_body, jnp.zeros((1, N_KV * TQ), I32)) ^ INT_MIN
    need = n_sel - jnp.sum(jnp.where(kk > thr, 1.0, 0.0), axis=0, keepdims=True)
    tril = jnp.where(lax.broadcasted_iota(I32, (LANES, LANES), 0) >= lax.broadcasted_iota(I32, (LANES, LANES), 1),
                     1.0, 0.0).astype(BF16)
    pref = _nn(tril, jnp.where(kk == thr, 1.0, 0.0).astype(BF16))
    tie = jnp.where(kk == thr, jnp.where(pref <= need, 1.0, 0.0), 0.0)
    sel_t = jnp.where(kk > thr, 1.0, tie)
    sel = jnp.concatenate([sel_t[:, g * TQ:(g + 1) * TQ].T for g in range(N_KV)], axis=0).astype(BF16)

    _flash_init(m_ref, acc_ref)
    blk_row =lax.broadcasted_iota(I32, (LANES, CH), 0)
    blk_col = lax.broadcasted_iota(I32, (LANES, CH), 1) >> 6
    row_c = t0 + lax.broadcasted_iota(I32, (TQ, CH), 0)
    col_c = lax.broadcasted_iota(I32, (TQ, CH), 1)

    def slc_chunk(c):
        off = pl.multiple_of(c * CH, CH)
        expand = jnp.where(blk_row == blk_col + c * (CH // SLC_BLOCK), 1.0, 0.0).astype(BF16)
        tok = _nn(sel, expand)
        causal = col_c + off <= row_c
        parts = []
        for g in range(N_KV):
            bias_g = jnp.where(causal, jnp.where(tok[g * TQ:(g + 1) * TQ] > 0.5, 0.0, NEG_INF), NEG_INF)
            parts.extend([bias_g.astype(BF16)] * GRP)
        sc = _nt(q, ks_ref[pl.ds(off, CH), :]).astype(BF16) + jnp.concatenate(parts, axis=0)
        _flash_update(sc, vs_ref[pl.ds(off, CH), :], m_ref, acc_ref, slice(None))

    _unrolled_loop(nch, slc_chunk)
    o_slc = _flash_result(acc_ref)

    gates = gt_ref[...]
    outs = []
    for h in range(N_HEADS):
        rs = slice(h * TQ, (h + 1) * TQ)
        outs.append(gates[:, 3 * h:3 * h + 1] * o_cmp[rs] + gates[:, 3 * h + 1:3 * h + 2] * o_slc[rs]
                    + gates[:, 3 * h + 2:3 * h + 3] * o_win[rs])
    _gated_store(jnp.concatenate(outs, axis=0), z_ref, o_ref, TQ)


def _mla_prep_kernel(cq_ref, ckv_ref, kr_ref, wq_ref, wkv_ref, lat_ref, gains_ref, cos_ref, sa_ref, sb_ref,
                     bd64_ref, bd32_ref, q_ref, k_ref, v_ref):
    lane = lax.broadcasted_iota(I32, (TM, LANES), 1)
    lo_half = lane < NOPE_DIM
    cos, sa, sb = cos_ref[...], sa_ref[...], sb_ref[...]
    bd64, bd32 = bd64_ref[...], bd32_ref[...]
    scale = (NOPE_DIM + ROPE_DIM) ** -0.5 * LOG2E

    def rope_slab(s, gain):
        return _rope(_seg_norm(s, bd32, gain, ROPE_DIM), cos, sa, sb, ROPE_DIM // 2)

    q = _nn(_row_rms(cq_ref[...], lat_ref[0:1, :]).astype(BF16), wq_ref[...])
    kv = _nn(_row_rms(ckv_ref[...], lat_ref[1:2, 0:KV_LORA]).astype(BF16), wkv_ref[...])
    k_rope = pltpu.roll(rope_slab(kr_ref[...], gains_ref[3:4, :]), NOPE_DIM, 1)
    q_rope = [rope_slab(q[:, 512 + LANES * j:512 + LANES * (j + 1)], gains_ref[1:2, :]) * scale for j in range(2)]
    for j in range(4):
        qn = _seg_norm(q[:, LANES * j:LANES * (j + 1)], bd64, gains_ref[0:1, :], NOPE_DIM) * scale
        kn = _seg_norm(kv[:, LANES * j:LANES * (j + 1)], bd64, gains_ref[2:3, :], NOPE_DIM)
        for e in range(2):
            h = 2 * j + e
            qn_h = qn if e == 0 else pltpu.roll(qn, NOPE_DIM, 1)
            kn_h = kn if e == 0 else pltpu.roll(kn, NOPE_DIM, 1)
            shift = (NOPE_DIM - ROPE_DIM * (h % 4)) % LANES
            qr = q_rope[h // 4]
            qr_h = qr if shift == 0 else pltpu.roll(qr, shift, 1)
            q_ref[h] = jnp.where(lo_half, qn_h, jnp.where(lane < NOPE_DIM + ROPE_DIM, qr_h, 0.0)).astype(BF16)
            k_ref[h] = jnp.where(lo_half, kn_h, k_rope).astype(BF16)
    for j in range(4):
        v_ref[j] = _with_ones(kv[:, 512 + LANES * j:512 + LANES * (j + 1)])


def _mla_kernel(q_ref, k_ref, v_ref, z_ref, o_ref, m_ref, acc_ref):
    i = pl.program_id(2)
    tq = TQ_MLA
    _flash_init(m_ref, acc_ref)

    def step(c, masked):
        off = pl.multiple_of(c * CH, CH)
        for e in range(MLA_HEADS):
            k = k_ref[e, pl.ds(off, CH), :]
            v = v_ref[e // 2, pl.ds(off, CH), :]
            for rb in range(tq // ROW_BLOCK):
                s = _nt(q_ref[e, rb * ROW_BLOCK:(rb + 1) * ROW_BLOCK, :], k)
                if masked:
                    qpos = rb * ROW_BLOCK + lax.broadcasted_iota(I32, (ROW_BLOCK, CH), 0)
                    s = jnp.where(lax.broadcasted_iota(I32, (ROW_BLOCK, CH), 1) <= qpos, s, NEG_INF)
                _flash_update(s, v, m_ref, acc_ref, slice(e * tq + rb * ROW_BLOCK, e * tq + (rb + 1) * ROW_BLOCK))

    _unrolled_loop(i, lambda c: step(c, False))
    step(i, True)
    o = _flash_result(acc_ref)
    lo_half = lax.broadcasted_iota(I32, (tq, LANES), 1) < NOPE_DIM
    for pr in range(MLA_HEADS // 2):
        slab = jnp.where(lo_half, o[(2 * pr) * tq:(2 * pr + 1) * tq], o[(2 * pr + 1) * tq:(2 * pr + 2) * tq])
        lanes = slice(pr * LANES, (pr + 1) * LANES)
        o_ref[:, lanes] = (slab * _silu(z_ref[:, lanes])).astype(BF16)


def _rope_tables(pos, dim):
    inv = jnp.power(jnp.float32(ROPE_THETA), -jnp.arange(0, dim, 2, dtype=F32) / dim)
    ang = pos.astype(F32)[:, None] * inv[None, :]
    cos, sin = jnp.cos(ang), jnp.sin(ang)
    reps = LANES // dim
    zero = jnp.zeros_like(sin)
    cos_t = jnp.tile(jnp.concatenate([cos, cos], axis=1), (1, reps))
    sa_t = jnp.tile(jnp.concatenate([zero, sin], axis=1), (1, reps))
    sb_t = jnp.tile(jnp.concatenate([-sin, zero], axis=1), (1, reps))
    return cos_t, sa_t, sb_t


def _block_diag(group):
    r = jnp.arange(LANES) // group
    bd = (r[:, None] == r[None, :]).astype(BF16)
    return jnp.concatenate([bd, bd], axis=0)


def _tile_gain(g, width=LANES):
    return jnp.tile(g.astype(F32), width // g.shape[0])


def _pad_lanes(g, width=LANES):
    return jnp.concatenate([g.astype(F32), jnp.zeros((width - g.shape[0],), F32)])


def _pad_cols(w, width):
    return jnp.concatenate([w, jnp.zeros((w.shape[0], width - w.shape[1]), w.dtype)], axis=1)


def _params(*sem, flags=None):
    return pltpu.CompilerParams(dimension_semantics=sem, vmem_limit_bytes=VMEM_LIMIT, flags=flags)


def kernel(x, ab_norm, ab_w_in, a_qk_norm, a_kidx_norm, b_qk_norm, b_sinks, ab_w_out, cd_norm, cd_w_in, c_q_norm, c_k_norm, c_cmp_pe, c_cmp_w1, c_cmp_w2, d_q_lat_norm, d_kv_lat_norm, d_w_uq, d_w_ukv, d_nope_norm, d_rope_norm, cd_w_out):
    bsz, seq, d = x.shape
    n = bsz * seq
    assert d == D_MODEL and seq % CH == 0 and seq >= NSA_WINDOW + TQ_ATT and seq // SLC_BLOCK <= LANES
    nq = seq // TQ
    nrow = n // TM
    srow = seq // TM
    xf = x.reshape(n, d)
    pos = jnp.arange(seq)
    cos64, sa64, sb64 = _rope_tables(pos, HEAD_DIM)
    cos32, sa32, sb32 = _rope_tables(pos, ROPE_DIM)
    bd64, bd32 = _block_diag(HEAD_DIM), _block_diag(ROPE_DIM)
    tril = (jnp.arange(CH)[:, None] >= jnp.arange(CH)[None, :]).astype(BF16)

    def row_spec(width, rows=TM):
        return pl.BlockSpec((rows, width), lambda i: (i, 0))

    def const_spec(shape):
        return pl.BlockSpec(shape, lambda i: tuple(0 for _ in shape))

    tab_spec = pl.BlockSpec((TM, LANES), lambda i: (i % srow, 0))
    head_spec = lambda nh: pl.BlockSpec((nh, TM, LANES), lambda i: (0, i, 0))

    wa = ab_w_in[0]
    w_ab = jnp.concatenate([wa[:, 0:1088], _pad_cols(wa[:, 1088:1092], 64), wa[:, 1092:]], axis=1).astype(BF16)
    gains_ab = jnp.stack([_tile_gain(a_qk_norm[0, 0]), _tile_gain(a_qk_norm[0, 1]), _pad_lanes(a_kidx_norm[0]),
                          _tile_gain(b_qk_norm[0, 0]), _tile_gain(b_qk_norm[0, 1])]
                         + [jnp.zeros((LANES,), F32)] * 3)
    sds = jax.ShapeDtypeStruct
    qa, ka, va, qi, ki, wi, za, qb, kb, vb, zb = pl.pallas_call(
        _proj_ab_kernel,
        grid=(nrow,),
        in_specs=[row_spec(d), const_spec((1, d)), const_spec((d, AB_W)), tab_spec, tab_spec, tab_spec,
                  const_spec((2 * LANES, LANES)), const_spec((8, LANES))],
        out_specs=[head_spec(8), row_spec(LANES), row_spec(2 * LANES), head_spec(4), row_spec(LANES), row_spec(LANES),
                   row_spec(512), head_spec(8), row_spec(LANES), row_spec(2 * LANES), row_spec(512)],
        out_shape=[sds((8, n, LANES), BF16), sds((n, LANES), BF16), sds((n, 2 * LANES), BF16),
                   sds((4, n, LANES), BF16), sds((n, LANES), BF16), sds((n, LANES), F32), sds((n, 512), F32),
                   sds((8, n, LANES), BF16), sds((n, LANES), BF16), sds((n, 2 * LANES), BF16), sds((n, 512), F32)],
        compiler_params=_params("parallel"),
    )(xf, ab_norm[0][None, :], w_ab, cos64, sa64, sb64, bd64, gains_ab)

    nqa = seq // TQ_ATT
    qt_heads = lambda nh: pl.BlockSpec((nh, TQ_ATT, LANES), lambda b, i: (0, b * nqa + i, 0))
    qt_rows = lambda width: pl.BlockSpec((TQ_ATT, width), lambda b, i: (b * nqa + i, 0))
    seq_rows = pl.BlockSpec((seq, LANES), lambda b, i: (b, 0))
    seq_rows_v = pl.BlockSpec((seq, 2 * LANES), lambda b, i: (b, 0))
    att_scratch = [pltpu.VMEM((N_HEADS * TQ_ATT, LANES), F32), pltpu.VMEM((N_HEADS * TQ_ATT, 2 * LANES), F32)]

    mix_a = pl.pallas_call(
        functools.partial(_dsa_kernel, topk=min(DSA_TOPK, seq // 4)),
        grid=(bsz, nqa),
        in_specs=[qt_heads(4), qt_rows(LANES), seq_rows, qt_heads(8), seq_rows, seq_rows_v, qt_rows(512),
                  pl.BlockSpec((CH, CH), lambda b, i: (0, 0))],
        out_specs=qt_rows(512),
        out_shape=sds((n, 512), BF16),
        scratch_shapes=[pltpu.VMEM((seq // CH, CH, TQ_ATT), I32), pltpu.VMEM((seq // CH, TQ_ATT, CH), BF16),
                        pltpu.VMEM((8, TQ_ATT), F32)] + att_scratch,
        compiler_params=_params("parallel", "arbitrary"),
    )(qi, wi, ki, qa, ka, va, za, tril)

    nq2 = nq // 2
    prev_rows = lambda width: pl.BlockSpec((TQ, width), lambda b, i: (b * nq + jnp.maximum(2 * i - 1, 0), 0))
    pair_rows = lambda width: pl.BlockSpec((2 * TQ, width), lambda b, i: (b * nq2 + i, 0))
    sink_rows = jnp.broadcast_to(jnp.repeat(b_sinks[0].astype(F32) * LOG2E, TQ)[:, None], (N_HEADS * TQ, LANES))
    mix_b = pl.pallas_call(
        _swa_kernel,
        grid=(bsz, nq2),
        in_specs=[pl.BlockSpec((N_HEADS, 2 * TQ, LANES), lambda b, i: (0, b * nq2 + i, 0)),
                  prev_rows(LANES), pair_rows(LANES), prev_rows(2 * LANES), pair_rows(2 * LANES),
                  pl.BlockSpec((N_HEADS * TQ, LANES), lambda b, i: (0, 0)), pair_rows(512)],
        out_specs=pair_rows(512),
        out_shape=sds((n, 512), BF16),
        compiler_params=_params("parallel", "arbitrary"),
    )(qb, kb, kb, vb, vb, sink_rows, zb)

    def out_proj(xin, m0, m1, w):
        return pl.pallas_call(
            _out_proj_kernel,
            grid=(nrow,),
            in_specs=[row_spec(d), row_spec(512), row_spec(512), const_spec((2 * 512, d))],
            out_specs=row_spec(d),
            out_shape=sds((n, d), F32),
            compiler_params=_params("parallel"),
        )(xin, m0, m1, w.astype(BF16))

    x1 = out_proj(xf, mix_a, mix_b, ab_w_out[0])

    wc = cd_w_in[0]
    w_cd = jnp.concatenate([wc[:, 0:1280], _pad_cols(wc[:, 1280:1304], LANES), wc[:, 1304:2200],
                            _pad_cols(wc[:, 2200:2232], LANES), wc[:, 2232:]], axis=1).astype(BF16)
    gains_cd = jnp.stack([_tile_gain(c_q_norm[0]), _tile_gain(c_k_norm[0, 1]), _tile_gain(c_k_norm[0, 2])]
                         + [jnp.zeros((LANES,), F32)] * 5)
    qc, kc_raw, vc_raw, ks, vs, kw, vw, gates, zc, cq, ckv, kr, zd = pl.pallas_call(
        _proj_cd_kernel,
        grid=(nrow,),
        in_specs=[row_spec(d), const_spec((1, d)), const_spec((d, AB_W)), tab_spec, tab_spec, tab_spec,
                  const_spec((2 * LANES, LANES)), const_spec((8, LANES))],
        out_specs=[head_spec(8), row_spec(LANES), row_spec(LANES), row_spec(LANES), row_spec(2 * LANES),
                   row_spec(LANES), row_spec(2 * LANES), row_spec(LANES), row_spec(512), row_spec(256),
                   row_spec(LANES), row_spec(LANES), row_spec(512)],
        out_shape=[sds((8, n, LANES), BF16), sds((n, LANES), F32), sds((n, LANES), F32), sds((n, LANES), BF16),
                   sds((n, 2 * LANES), BF16), sds((n, LANES), BF16), sds((n, 2 * LANES), BF16), sds((n, LANES), F32),
                   sds((n, 512), F32), sds((n, 256), F32), sds((n, LANES), F32), sds((n, LANES), F32),
                   sds((n, 512), F32)],
        compiler_params=_params("parallel"),
    )(x1, cd_norm[0][None, :], w_cd, cos64, sa64, sb64, bd64, gains_cd)

    ncmp = seq // CMP_STRIDE
    cw = CMP_STRIDE * N_KV * HEAD_DIM
    eye = jnp.eye(N_KV, dtype=F32)
    ccos, csa, csb = _rope_tables(jnp.arange(ncmp) * CMP_STRIDE + (2 * CMP_STRIDE - 1), HEAD_DIM)

    def compress(raw, which, is_key):
        w1 = c_cmp_w1[0, which].reshape(2, CMP_STRIDE, HEAD_DIM, -1)
        hid = w1.shape[-1]
        w1e = jnp.einsum("tjdu,gh->tjgdhu", w1, eye).reshape(2, cw, N_KV * hid).astype(BF16)
        w2e = jnp.einsum("ud,gh->guhd", c_cmp_w2[0, which], eye).reshape(N_KV * hid, N_KV * HEAD_DIM).astype(BF16)
        pe = c_cmp_pe[0, which].reshape(2, CMP_STRIDE, 1, HEAD_DIM)
        pe = jnp.broadcast_to(pe, (2, CMP_STRIDE, N_KV, HEAD_DIM)).reshape(2, cw)
        return pl.pallas_call(
            functools.partial(_compress_kernel, is_key=is_key),
            grid=(bsz,),
            in_specs=[pl.BlockSpec((ncmp, cw), lambda b: (b, 0)), const_spec((2, cw)),
                      const_spec((cw, N_KV * hid)), const_spec((cw, N_KV * hid)),
                      const_spec((N_KV * hid, LANES)), const_spec((1, LANES)),
                      const_spec((ncmp, LANES)), const_spec((ncmp, LANES)), const_spec((ncmp, LANES)),
                      const_spec((2 * LANES, LANES))],
            out_specs=pl.BlockSpec((ncmp, LANES), lambda b: (b, 0)),
            out_shape=sds((bsz * ncmp, LANES), BF16),
            compiler_params=_params("parallel"),
        )(raw.reshape(bsz * ncmp, cw), pe, w1e[0], w1e[1], w2e, _tile_gain(c_k_norm[0, 0])[None, :],
          ccos, csa, csb, bd64)

    kc = compress(kc_raw, 0, True)
    vc = compress(vc_raw, 1, False)

    ratio = SLC_BLOCK // CMP_STRIDE
    mm = jnp.arange(ncmp)[:, None]
    jj = jnp.arange(LANES)[None, :]
    wsel = (jnp.where((mm == ratio * jj - 1) | (mm == ratio * jj + ratio - 1), 1.0, 0.0)
            + jnp.where((mm >= ratio * jj) & (mm < ratio * jj + ratio - 1), 2.0, 0.0))
    wsel_t = jnp.where((mm < ncmp - 1) & (jj < seq // SLC_BLOCK), wsel, 0.0).astype(BF16).T

    cmp_rows = pl.BlockSpec((ncmp, LANES), lambda b, i: (b, 0))
    mix_c = pl.pallas_call(
        functools.partial(_nsa_kernel, n_sel=min(SLC_TOPN, seq // SLC_BLOCK)),
        grid=(bsz, nqa),
        in_specs=[qt_heads(8), cmp_rows, cmp_rows, seq_rows, seq_rows_v, seq_rows, seq_rows_v, qt_rows(LANES),
                  qt_rows(512), pl.BlockSpec((LANES, ncmp), lambda b, i: (0, 0))],
        out_specs=qt_rows(512),
        out_shape=sds((n, 512), BF16),
        scratch_shapes=att_scratch,
        compiler_params=_params("parallel", "arbitrary"),
    )(qc, kc, vc, ks, vs, kw, vw, gates, zc, wsel_t)

    wq = d_w_uq[0].reshape(Q_LORA, N_HEADS, NOPE_DIM + ROPE_DIM)
    wq = jnp.concatenate([wq[:, :, :NOPE_DIM].reshape(Q_LORA, -1), wq[:, :, NOPE_DIM:].reshape(Q_LORA, -1)],
                         axis=1).astype(BF16)
    wkv = d_w_ukv[0].reshape(KV_LORA, N_HEADS, NOPE_DIM + HEAD_DIM)
    wkv = jnp.concatenate([wkv[:, :, :NOPE_DIM].reshape(KV_LORA, -1), wkv[:, :, NOPE_DIM:].reshape(KV_LORA, -1)],
                          axis=1).astype(BF16)
    lat_gains = jnp.stack([d_q_lat_norm[0].astype(F32), _pad_lanes(d_kv_lat_norm[0], Q_LORA)]
                          + [jnp.zeros((Q_LORA,), F32)] * 6)
    gains_d = jnp.stack([_tile_gain(d_nope_norm[0, 0]), _tile_gain(d_rope_norm[0, 0]), _tile_gain(d_nope_norm[0, 1]),
                         _pad_lanes(d_rope_norm[0, 1])] + [jnp.zeros((LANES,), F32)] * 4)
    q_cat, k_cat, v_d = pl.pallas_call(
        _mla_prep_kernel,
        grid=(nrow,),
        in_specs=[row_spec(Q_LORA), row_spec(LANES), row_spec(LANES), const_spec((Q_LORA, 768)),
                  const_spec((KV_LORA, 1024)), const_spec((8, Q_LORA)), const_spec((8, LANES)),
                  tab_spec, tab_spec, tab_spec, const_spec((2 * LANES, LANES)), const_spec((2 * LANES, LANES))],
        out_specs=[head_spec(8), head_spec(8), pl.BlockSpec((4, TM, 2 * LANES), lambda i: (0, i, 0))],
        out_shape=[sds((8, n, LANES), BF16), sds((8, n, LANES), BF16), sds((4, n, 2 * LANES), BF16)],
        compiler_params=_params("parallel"),
    )(cq, ckv, kr, wq, wkv, lat_gains, gains_d, cos32, sa32, sb32, bd64, bd32)

    nqm = seq // TQ_MLA
    mix_d = pl.pallas_call(
        _mla_kernel,
        grid=(bsz, N_HEADS // MLA_HEADS, nqm),
        in_specs=[pl.BlockSpec((MLA_HEADS, TQ_MLA, LANES), lambda b, hg, i: (hg, b * nqm + i, 0)),
                  pl.BlockSpec((MLA_HEADS, seq, LANES), lambda b, hg, i: (hg, b, 0), pipeline_mode=pl.Buffered(1)),
                  pl.BlockSpec((MLA_HEADS // 2, seq, 2 * LANES), lambda b, hg, i: (hg, b, 0),
                               pipeline_mode=pl.Buffered(1)),
                  pl.BlockSpec((TQ_MLA, MLA_HEADS * HEAD_DIM), lambda b, hg, i: (b * nqm + i, hg))],
        out_specs=pl.BlockSpec((TQ_MLA, MLA_HEADS * HEAD_DIM), lambda b, hg, i: (b * nqm + i, hg)),
        out_shape=sds((n, 512), BF16),
        scratch_shapes=[pltpu.VMEM((MLA_HEADS * TQ_MLA, LANES), F32),
                        pltpu.VMEM((MLA_HEADS * TQ_MLA, 2 * LANES), F32)],
        compiler_params=_params("parallel", "parallel", "arbitrary"),
    )(q_cat, k_cat, v_d, zd)

    x2 = out_proj(x1, mix_c, mix_d, cd_w_out[0])
    return x2.reshape(bsz, seq, d)
```

```python
import functools

import jax
import jax.numpy as jnp
from jax import lax
from jax.experimental import pallas as pl
from jax.experimental.pallas import tpu as pltpu

F32, BF16, I32 = jnp.float32, jnp.bfloat16, jnp.int32

D_MODEL = 1024
HEAD_DIM = 64
N_HEADS = 8
N_KV = 2
GRP = N_HEADS // N_KV
IDX_HEADS = 4
DSA_TOPK = 256
SWA_WINDOW = 128
CMP_STRIDE = 16
SLC_BLOCK = 64
SLC_TOPN = 16
NSA_WINDOW = 512
Q_LORA = 256
KV_LORA = 128
NOPE_DIM = 64
ROPE_DIM = 32
ROPE_THETA = 10000.0
EPS = 1e-6

LANES = 128
TM = 1024
TQ = 128
CH = 512
TQ_ATT = 256
TQ_MLA = 512
MLA_HEADS = 8
UNROLL = 4
COUNT_ROWS = 16
ROW_BLOCK = 128
VMEM_LIMIT = 56 * 1024 * 1024

NEG_INF = float("-inf")
M_INIT = -1e30
INT_MIN = -(2 ** 31)

AB_W = 2944
LOG2E = 1.4426950408889634
Q_SCALE = HEAD_DIM ** -0.5 * LOG2E


def _nn(a, b):
    return jnp.dot(a, b, preferred_element_type=F32)


def _nt(a, b):
    return lax.dot_general(a, b, (((1,), (1,)), ((), ())), preferred_element_type=F32)


def _sigmoid(z):
    return 1.0 / (1.0 + jnp.exp(-z))


def _silu(z):
    return z * _sigmoid(z)


def _row_rms(x, g):
    return x * lax.rsqrt(jnp.mean(x * x, axis=-1, keepdims=True) + EPS) * g


def _seg_norm(s, bd, gain, group):
    s2 = s * s
    hi = s2.astype(BF16)
    lo = (s2 - hi.astype(F32)).astype(BF16)
    ss = _nn(jnp.concatenate([hi, lo], axis=1), bd)
    return s * lax.rsqrt(ss * (1.0 / group) + EPS) * gain


def _rope(s, cos, sa, sb, half):
    return s * cos + pltpu.roll(s, half, 1) * sa + pltpu.roll(s, LANES - half, 1) * sb


def _order_key(v):
    bits = lax.bitcast_convert_type(v, I32)
    return bits ^ ((bits >> 31) & 0x7FFFFFFF)


def _stack_rows(x, n):
    return jnp.concatenate([x] * n, axis=0)


def _with_ones(v):
    return jnp.concatenate([v.astype(BF16), jnp.ones(v.shape, BF16)], axis=1)


def _flash_update(s, v_ext, m_ref, acc_ref, rows):
    m_prev = m_ref[rows]
    sb = s.astype(BF16)
    m_new = jnp.maximum(m_prev, jnp.max(sb, axis=1, keepdims=True).astype(F32))
    alpha = jnp.exp2(m_prev - m_new)
    p = jnp.exp2(sb - jnp.concatenate([m_new.astype(BF16)] * (s.shape[1] // LANES), axis=1))
    acc_ref[rows] = jnp.concatenate([alpha, alpha], axis=1) * acc_ref[rows] + _nn(p, v_ext)
    m_ref[rows] = m_new


def _unrolled_loop(n, step, unroll=UNROLL):
    def body(j, carry):
        for u in range(unroll):
            step(j * unroll + u)
        return carry

    lax.fori_loop(0, n // unroll, body, 0)
    base = (n // unroll) * unroll
    p = unroll // 2
    while p >= 1:
        def tail(base=base, p=p):
            for u in range(p):
                step(base + u)

        pl.when((n & p) != 0)(tail)
        base = base + (n & p)
        p //= 2


def _flash_init(m_ref, acc_ref):
    m_ref[...] = jnp.full(m_ref.shape, M_INIT, F32)
    acc_ref[...] = jnp.zeros(acc_ref.shape, F32)


def _flash_result(acc_ref):
    acc = acc_ref[...]
    return acc[:, 0:LANES] / jnp.maximum(acc[:, LANES:2 * LANES], 1e-30)


def _place_q_heads(y, bd, gain, cos, sa, sb, out_ref, scale):
    lo_half = lax.broadcasted_iota(I32, (y.shape[0], LANES), 1) < HEAD_DIM
    for j in range(4):
        s = y[:, LANES * j:LANES * (j + 1)]
        s = _rope(_seg_norm(s, bd, gain, HEAD_DIM), cos, sa, sb, HEAD_DIM // 2) * scale
        r = pltpu.roll(s, HEAD_DIM, 1)
        if j < 2:
            out_ref[2 * j] = jnp.where(lo_half, s, 0.0).astype(BF16)
            out_ref[2 * j + 1] = jnp.where(lo_half, r, 0.0).astype(BF16)
        else:
            out_ref[2 * j] = jnp.where(lo_half, 0.0, r).astype(BF16)
            out_ref[2 * j + 1] = jnp.where(lo_half, 0.0, s).astype(BF16)


def _gated_store(o, z_ref, o_ref, tq, row0=0):
    lo_half = lax.broadcasted_iota(I32, (tq, LANES), 1) < HEAD_DIM
    for j in range(4):
        a = o[(2 * j) * tq:(2 * j + 1) * tq]
        b = o[(2 * j + 1) * tq:(2 * j + 2) * tq]
        if j < 2:
            slab = jnp.where(lo_half, a, pltpu.roll(b, HEAD_DIM, 1))
        else:
            slab = jnp.where(lo_half, pltpu.roll(a, HEAD_DIM, 1), b)
        z = z_ref[row0:row0 + tq, LANES * j:LANES * (j + 1)]
        o_ref[row0:row0 + tq, LANES * j:LANES * (j + 1)] = (slab * _silu(z)).astype(BF16)


def _proj_ab_kernel(x_ref, g_ref, w_ref, cos_ref, sa_ref, sb_ref, bd_ref, gains_ref,
                    qa_ref, ka_ref, va_ref, qi_ref, ki_ref, wi_ref, za_ref,
                    qb_ref, kb_ref, vb_ref, zb_ref):
    xn = _row_rms(x_ref[...], g_ref[...]).astype(BF16)
    cos, sa, sb, bd = cos_ref[...], sa_ref[...], sb_ref[...], bd_ref[...]
    lo_half = lax.broadcasted_iota(I32, (TM, LANES), 1) < HEAD_DIM

    def proj(c0, n):
        return _nn(xn, w_ref[:, c0:c0 + n])

    def k_slab(c0, gain):
        return _rope(_seg_norm(proj(c0, LANES), bd, gain, HEAD_DIM), cos, sa, sb, HEAD_DIM // 2)

    _place_q_heads(proj(0, 512), bd, gains_ref[0:1, :], cos, sa, sb, qa_ref, Q_SCALE)
    ka_ref[...] = k_slab(512, gains_ref[1:2, :]).astype(BF16)
    va_ref[...] = _with_ones(proj(640, LANES))
    qi = proj(768, 256)
    for j in range(2):
        s = _rope(qi[:, LANES * j:LANES * (j + 1)], cos, sa, sb, HEAD_DIM // 2)
        qi_ref[2 * j] = jnp.where(lo_half, s, 0.0).astype(BF16)
        qi_ref[2 * j + 1] = jnp.where(lo_half, pltpu.roll(s, HEAD_DIM, 1), 0.0).astype(BF16)
    kiw = proj(1024, LANES)
    wi_ref[...] = kiw
    ki_ref[...] = _rope(_seg_norm(kiw, bd, gains_ref[2:3, :], HEAD_DIM), cos, sa, sb, HEAD_DIM // 2).astype(BF16)
    za_ref[...] = proj(1152, 512)
    _place_q_heads(proj(1664, 512), bd, gains_ref[3:4, :], cos, sa, sb, qb_ref, Q_SCALE)
    kb_ref[...] = k_slab(2176, gains_ref[4:5, :]).astype(BF16)
    vb_ref[...] = _with_ones(proj(2304, LANES))
    zb_ref[...] = proj(2432, 512)


def _count_chunks(keys_ref, nch, nq, pred):
    def body(c, acc):
        return acc + jnp.sum(pred(keys_ref[c]).reshape(CH // COUNT_ROWS, COUNT_ROWS, nq), axis=0)

    acc = lax.fori_loop(0, nch, body, jnp.zeros((COUNT_ROWS, nq), F32))
    return jnp.sum(acc, axis=0, keepdims=True)


def _kth_threshold(count, nq, k):
    def bit_body(it, carry):
        tb, above = carry
        cand_b = tb | jnp.left_shift(jnp.int32(1), 31 - it)
        cand_s = cand_b ^ INT_MIN
        cnt = count(lambda kk: jnp.where(kk >= cand_s, 1.0, 0.0))
        ok = cnt >= k
        return jnp.where(ok, cand_b, tb), jnp.where(ok, above, cnt)

    tb, above = lax.fori_loop(0, 32, bit_body, (jnp.zeros((1, nq), I32), jnp.zeros((1, nq), F32)))
    return tb ^ INT_MIN, k - above


def _dsa_kernel(qi_ref, wi_ref, ki_ref, qa_ref, ka_ref, va_ref, za_ref, tril_ref, o_ref,
                keys_ref, bias_ref, ties_ref, m_ref, acc_ref, *, topk):
    TQ = TQ_ATT
    i = pl.program_id(1)
    t0 = i * TQ
    nch = (t0 + TQ + CH - 1) // CH
    key_c = lax.broadcasted_iota(I32, (CH, TQ), 0)
    qry_c = t0 + lax.broadcasted_iota(I32, (CH, TQ), 1)

    qi = qi_ref[...].reshape(IDX_HEADS * TQ, LANES)
    w_t = wi_ref[...].T * ((HEAD_DIM * IDX_HEADS) ** -0.5)
    ws = [w_t[HEAD_DIM + h:HEAD_DIM + h + 1, :] for h in range(IDX_HEADS)]

    def score_chunk(c):
        off = pl.multiple_of(c * CH, CH)
        lg = _nt(ki_ref[pl.ds(off, CH), :], qi)
        sc = ws[0] * jnp.maximum(lg[:, 0:TQ], 0.0)
        for h in range(1, IDX_HEADS):
            sc = sc + ws[h] * jnp.maximum(lg[:, h * TQ:(h + 1) * TQ], 0.0)
        sc = jnp.where(sc == 0.0, 0.0, sc)
        keys_ref[c] = jnp.where(key_c + off <= qry_c, _order_key(sc), INT_MIN)

    _unrolled_loop(nch, score_chunk)

    count = functools.partial(_count_chunks, keys_ref, nch, TQ)
    thr, need = _kth_threshold(count, TQ, float(topk))

    tril = tril_ref[...]

    ties_ref[...] = jnp.zeros(ties_ref.shape, F32)

    def select_chunk(c):
        kk = keys_ref[c]
        pref = _nn(tril, jnp.where(kk == thr, 1.0, 0.0).astype(BF16)) + ties_ref[0:1, :]
        ties_ref[0:1, :] = pref[CH - 1:CH, :]
        tie = jnp.where(kk == thr, jnp.where(pref <= need, 0.0, NEG_INF), NEG_INF)
        bias = jnp.where(key_c + c * CH <= qry_c, jnp.where(kk > thr, 0.0, tie), NEG_INF)
        bias_ref[c] = bias.T.astype(BF16)

    _unrolled_loop(nch, select_chunk)

    _flash_init(m_ref, acc_ref)
    q_all = qa_ref[...].reshape(N_HEADS * TQ, LANES)

    def att_chunk(c):
        off = pl.multiple_of(c * CH, CH)
        s = _nt(q_all, ka_ref[pl.ds(off, CH), :]).astype(BF16) + _stack_rows(bias_ref[c], N_HEADS)
        _flash_update(s, va_ref[pl.ds(off, CH), :], m_ref, acc_ref, slice(None))

    _unrolled_loop(nch, att_chunk, unroll=2 * UNROLL)
    _gated_store(_flash_result(acc_ref), za_ref, o_ref, TQ)


def _swa_kernel(q_ref, kp_ref, kc_ref, vp_ref, vc_ref, sink_ref, z_ref, o_ref):
    i = pl.program_id(1)
    qi = lax.broadcasted_iota(I32, (TQ, 2 * TQ), 0)
    kj = lax.broadcasted_iota(I32, (TQ, 2 * TQ), 1)
    sink = sink_ref[...]
    for half in range(2):
        q = q_ref[:, half * TQ:(half + 1) * TQ, :].reshape(N_HEADS * TQ, LANES)
        if half == 0:
            k2 = jnp.concatenate([kp_ref[...], kc_ref[0:TQ, :]], axis=0)
            v2 = jnp.concatenate([vp_ref[...], vc_ref[0:TQ, :]], axis=0)
            lo = jnp.where(i > 0, 0, TQ)
        else:
            k2, v2, lo = kc_ref[...], vc_ref[...], 0
        keep = jnp.where(kj >= lo, jnp.where(kj > qi, jnp.where(kj <= qi + TQ, 0.0, NEG_INF), NEG_INF), NEG_INF)
        s = _nt(q, k2) + _stack_rows(keep, N_HEADS)
        m = jnp.maximum(jnp.max(s, axis=1, keepdims=True), sink)
        p = jnp.exp2(s - jnp.concatenate([m, m], axis=1))
        o = _nn(p.astype(BF16), v2)
        den = o[:, LANES:2 * LANES] + jnp.exp2(sink - m)
        _gated_store(o[:, 0:LANES] / den, z_ref, o_ref, TQ, half * TQ)


def _out_proj_kernel(x_ref, ma_ref, mb_ref, w_ref, o_ref):
    half = ma_ref.shape[1]
    o_ref[...] = x_ref[...] + _nn(ma_ref[...], w_ref[0:half, :]) + _nn(mb_ref[...], w_ref[half:2 * half, :])


def _proj_cd_kernel(x_ref, g_ref, w_ref, cos_ref, sa_ref, sb_ref, bd_ref, gains_ref,
                    qc_ref, kc_ref, vc_ref, ks_ref, vs_ref, kw_ref, vw_ref, gt_ref, zc_ref,
                    cq_ref, ckv_ref, kr_ref, zd_ref):
    xn = _row_rms(x_ref[...], g_ref[...]).astype(BF16)
    cos, sa, sb, bd = cos_ref[...], sa_ref[...], sb_ref[...], bd_ref[...]

    def proj(c0, n):
        return _nn(xn, w_ref[:, c0:c0 + n])

    def k_slab(c0, gain):
        return _rope(_seg_norm(proj(c0, LANES), bd, gain, HEAD_DIM), cos, sa, sb, HEAD_DIM // 2)

    _place_q_heads(proj(0, 512), bd, gains_ref[0:1, :], cos, sa, sb, qc_ref, Q_SCALE)
    kc_ref[...] = proj(512, LANES)
    vc_ref[...] = proj(640, LANES)
    ks_ref[...] = k_slab(768, gains_ref[1:2, :]).astype(BF16)
    vs_ref[...] = _with_ones(proj(896, LANES))
    kw_ref[...] = k_slab(1024, gains_ref[2:3, :]).astype(BF16)
    vw_ref[...] = _with_ones(proj(1152, LANES))
    gt_ref[...] = _sigmoid(proj(1280, LANES))
    zc_ref[...] = proj(1408, 512)
    cq_ref[...] = proj(1920, 256)
    ckv_ref[...] = proj(2176, LANES)
    kr_ref[...] = proj(2304, LANES)
    zd_ref[...] = proj(2432, 512)


def _compress_kernel(c_ref, pe_ref, w1a_ref, w1b_ref, w2_ref, gain_ref, cos_ref, sa_ref, sb_ref, bd_ref,
                     o_ref, *, is_key):
    c = c_ref[...]
    n = c.shape[0]
    a = _nn((c + pe_ref[0:1, :]).astype(BF16), w1a_ref[...])
    b = _nn((c + pe_ref[1:2, :]).astype(BF16), w1b_ref[...])
    h = _silu(a + pltpu.roll(b, n - 1, 0))
    y = _nn(h.astype(BF16), w2_ref[...])
    if is_key:
        y = _rope(_seg_norm(y, bd_ref[...], gain_ref[...], HEAD_DIM), cos_ref[...], sa_ref[...], sb_ref[...],
                  HEAD_DIM // 2)
    o_ref[...] = y.astype(BF16)


def _nsa_kernel(q_ref, kc_ref, vc_ref, ks_ref, vs_ref, kw_ref, vw_ref, gt_ref, z_ref, wsel_ref, o_ref,
                m_ref, acc_ref, *, n_sel):
    TQ = TQ_ATT
    i = pl.program_id(1)
    t0 = i * TQ
    nch = (t0 + TQ + CH - 1) // CH
    ncmp = kc_ref.shape[0]
    rows_all = N_HEADS * TQ
    q = q_ref[...].reshape(rows_all, LANES)

    cend = lax.broadcasted_iota(I32, (TQ, ncmp), 1) * CMP_STRIDE + (2 * CMP_STRIDE - 1)
    bias_c = jnp.where(cend <= t0 + lax.broadcasted_iota(I32, (TQ, ncmp), 0), 0.0, NEG_INF)
    s = _nt(q, kc_ref[...]) + _stack_rows(bias_c, N_HEADS)
    m = jnp.max(s, axis=1, keepdims=True)
    m = jnp.where(m > NEG_INF, m, 0.0)
    p = jnp.exp2(s - m)
    p = p / jnp.maximum(jnp.sum(p, axis=1, keepdims=True), 1e-30)
    o_cmp = _nn(p.astype(BF16), vc_ref[...])

    span = NSA_WINDOW + TQ
    start = pl.multiple_of(jnp.maximum(t0 - NSA_WINDOW, 0), TQ)
    diff = t0 + lax.broadcasted_iota(I32, (TQ, span), 0) - (start + lax.broadcasted_iota(I32, (TQ, span), 1))
    bias_w = jnp.where(diff >= 0, jnp.where(diff < NSA_WINDOW, 0.0, NEG_INF), NEG_INF)
    sw = _nt(q, kw_ref[pl.ds(start, span), :]).astype(BF16) + _stack_rows(bias_w.astype(BF16), N_HEADS)
    pw = jnp.exp2(sw - jnp.max(sw, axis=1, keepdims=True))
    o_win = _nn(pw, vw_ref[pl.ds(start, span), :])
    o_win = o_win[:, 0:LANES] / o_win[:, LANES:2 * LANES]

    blk = lax.broadcasted_iota(I32, (LANES, TQ), 0)
    cur = (t0 + lax.broadcasted_iota(I32, (LANES, TQ), 1)) >> 6
    wsel_t = wsel_ref[...]
    keys = []
    for g in range(N_KV):
        imp = p[(GRP * g) * TQ:(GRP * g + 1) * TQ]
        for r in range(1, GRP):
            imp = imp + p[(GRP * g + r) * TQ:(GRP * g + r + 1) * TQ]
        hi = imp.astype(BF16)
        lo = (imp - hi.astype(F32)).astype(BF16)
        imp_s = _nt(wsel_t, hi) + _nt(wsel_t, lo)
        forced = jnp.where(blk == 0, jnp.inf, jnp.where(blk >= cur - 1, jnp.inf, imp_s))
        keys.append(_order_key(jnp.where(blk <= cur, forced, NEG_INF)))
    kk = jnp.concatenate(keys, axis=1)

    def bit_body(it, tb):
        cand_b = tb | jnp.left_shift(jnp.int32(1), 31 - it)
        cnt = jnp.sum(jnp.where(kk >= (cand_b ^ INT_MIN), 1.0, 0.0), axis=0, keepdims=True)
        return jnp.where(cnt >= n_sel, cand_b, tb)

    thr = lax.fori_loop(0, 32, bit_body, jnp.zeros((1, N_KV * TQ), I32)) ^ INT_MIN
    need = n_sel - jnp.sum(jnp.where(kk > thr, 1.0, 0.0), axis=0, keepdims=True)
    tril = jnp.where(lax.broadcasted_iota(I32, (LANES, LANES), 0) >= lax.broadcasted_iota(I32, (LANES, LANES), 1),
                     1.0, 0.0).astype(BF16)
    pref = _nn(tril, jnp.where(kk == thr, 1.0, 0.0).astype(BF16))
    tie = jnp.where(kk == thr, jnp.where(pref <= need, 1.0, 0.0), 0.0)
    sel_t = jnp.where(kk > thr, 1.0, tie)
    sel = jnp.concatenate([sel_t[:, g * TQ:(g + 1) * TQ].T for g in range(N_KV)], axis=0).astype(BF16)

    _flash_init(m_ref, acc_ref)
    blk_row =lax.broadcasted_iota(I32, (LANES, CH), 0)
    blk_col = lax.broadcasted_iota(I32, (LANES, CH), 1) >> 6
    row_c = t0 + lax.broadcasted_iota(I32, (TQ, CH), 0)
    col_c = lax.broadcasted_iota(I32, (TQ, CH), 1)

    def slc_chunk(c):
        off = pl.multiple_of(c * CH, CH)
        expand = jnp.where(blk_row == blk_col + c * (CH // SLC_BLOCK), 1.0, 0.0).astype(BF16)
        tok = _nn(sel, expand)
        causal = col_c + off <= row_c
        parts = []
        for g in range(N_KV):
            bias_g = jnp.where(causal, jnp.where(tok[g * TQ:(g + 1) * TQ] > 0.5, 0.0, NEG_INF), NEG_INF)
            parts.extend([bias_g.astype(BF16)] * GRP)
        sc = _nt(q, ks_ref[pl.ds(off, CH), :]).astype(BF16) + jnp.concatenate(parts, axis=0)
        _flash_update(sc, vs_ref[pl.ds(off, CH), :], m_ref, acc_ref, slice(None))

    _unrolled_loop(nch, slc_chunk, unroll=2 * UNROLL)
    o_slc = _flash_result(acc_ref)

    gates = gt_ref[...]
    outs = []
    for h in range(N_HEADS):
        rs = slice(h * TQ, (h + 1) * TQ)
        outs.append(gates[:, 3 * h:3 * h + 1] * o_cmp[rs] + gates[:, 3 * h + 1:3 * h + 2] * o_slc[rs]
                    + gates[:, 3 * h + 2:3 * h + 3] * o_win[rs])
    _gated_store(jnp.concatenate(outs, axis=0), z_ref, o_ref, TQ)


def _mla_prep_kernel(cq_ref, ckv_ref, kr_ref, wq_ref, wkv_ref, lat_ref, gains_ref, cos_ref, sa_ref, sb_ref,
                     bd64_ref, bd32_ref, q_ref, k_ref, v_ref):
    lane = lax.broadcasted_iota(I32, (TM, LANES), 1)
    lo_half = lane < NOPE_DIM
    cos, sa, sb = cos_ref[...], sa_ref[...], sb_ref[...]
    bd64, bd32 = bd64_ref[...], bd32_ref[...]
    scale = (NOPE_DIM + ROPE_DIM) ** -0.5 * LOG2E

    def rope_slab(s, gain):
        return _rope(_seg_norm(s, bd32, gain, ROPE_DIM), cos, sa, sb, ROPE_DIM // 2)

    q = _nn(_row_rms(cq_ref[...], lat_ref[0:1, :]).astype(BF16), wq_ref[...])
    kv = _nn(_row_rms(ckv_ref[...], lat_ref[1:2, 0:KV_LORA]).astype(BF16), wkv_ref[...])
    k_rope = pltpu.roll(rope_slab(kr_ref[...], gains_ref[3:4, :]), NOPE_DIM, 1)
    q_rope = [rope_slab(q[:, 512 + LANES * j:512 + LANES * (j + 1)], gains_ref[1:2, :]) * scale for j in range(2)]
    for j in range(4):
        qn = _seg_norm(q[:, LANES * j:LANES * (j + 1)], bd64, gains_ref[0:1, :], NOPE_DIM) * scale
        kn = _seg_norm(kv[:, LANES * j:LANES * (j + 1)], bd64, gains_ref[2:3, :], NOPE_DIM)
        for e in range(2):
            h = 2 * j + e
            qn_h = qn if e == 0 else pltpu.roll(qn, NOPE_DIM, 1)
            kn_h = kn if e == 0 else pltpu.roll(kn, NOPE_DIM, 1)
            shift = (NOPE_DIM - ROPE_DIM * (h % 4)) % LANES
            qr = q_rope[h // 4]
            qr_h = qr if shift == 0 else pltpu.roll(qr, shift, 1)
            q_ref[h] = jnp.where(lo_half, qn_h, jnp.where(lane < NOPE_DIM + ROPE_DIM, qr_h, 0.0)).astype(BF16)
            k_ref[h] = jnp.where(lo_half, kn_h, k_rope).astype(BF16)
    for j in range(4):
        v_ref[j] = _with_ones(kv[:, 512 + LANES * j:512 + LANES * (j + 1)])


def _mla_kernel(q_ref, k_ref, v_ref, z_ref, o_ref, m_ref, acc_ref):
    i = pl.program_id(2)
    tq = TQ_MLA
    _flash_init(m_ref, acc_ref)

    def step(c, masked):
        off = pl.multiple_of(c * CH, CH)
        for e in range(MLA_HEADS):
            k = k_ref[e, pl.ds(off, CH), :]
            v = v_ref[e // 2, pl.ds(off, CH), :]
            for rb in range(tq // ROW_BLOCK):
                s = _nt(q_ref[e, rb * ROW_BLOCK:(rb + 1) * ROW_BLOCK, :], k)
                if masked:
                    qpos = rb * ROW_BLOCK + lax.broadcasted_iota(I32, (ROW_BLOCK, CH), 0)
                    s = jnp.where(lax.broadcasted_iota(I32, (ROW_BLOCK, CH), 1) <= qpos, s, NEG_INF)
                _flash_update(s, v, m_ref, acc_ref, slice(e * tq + rb * ROW_BLOCK, e * tq + (rb + 1) * ROW_BLOCK))

    _unrolled_loop(i, lambda c: step(c, False))
    step(i, True)
    o = _flash_result(acc_ref)
    lo_half = lax.broadcasted_iota(I32, (tq, LANES), 1) < NOPE_DIM
    for pr in range(MLA_HEADS // 2):
        slab = jnp.where(lo_half, o[(2 * pr) * tq:(2 * pr + 1) * tq], o[(2 * pr + 1) * tq:(2 * pr + 2) * tq])
        lanes = slice(pr * LANES, (pr + 1) * LANES)
        o_ref[:, lanes] = (slab * _silu(z_ref[:, lanes])).astype(BF16)


def _rope_tables(pos, dim):
    inv = jnp.power(jnp.float32(ROPE_THETA), -jnp.arange(0, dim, 2, dtype=F32) / dim)
    ang = pos.astype(F32)[:, None] * inv[None, :]
    cos, sin = jnp.cos(ang), jnp.sin(ang)
    reps = LANES // dim
    zero = jnp.zeros_like(sin)
    cos_t = jnp.tile(jnp.concatenate([cos, cos], axis=1), (1, reps))
    sa_t = jnp.tile(jnp.concatenate([zero, sin], axis=1), (1, reps))
    sb_t = jnp.tile(jnp.concatenate([-sin, zero], axis=1), (1, reps))
    return cos_t, sa_t, sb_t


def _block_diag(group):
    r = jnp.arange(LANES) // group
    bd = (r[:, None] == r[None, :]).astype(BF16)
    return jnp.concatenate([bd, bd], axis=0)


def _tile_gain(g, width=LANES):
    return jnp.tile(g.astype(F32), width // g.shape[0])


def _pad_lanes(g, width=LANES):
    return jnp.concatenate([g.astype(F32), jnp.zeros((width - g.shape[0],), F32)])


def _pad_cols(w, width):
    return jnp.concatenate([w, jnp.zeros((w.shape[0], width - w.shape[1]), w.dtype)], axis=1)


def _params(*sem, flags=None):
    return pltpu.CompilerParams(dimension_semantics=sem, vmem_limit_bytes=VMEM_LIMIT, flags=flags)


def kernel(x, ab_norm, ab_w_in, a_qk_norm, a_kidx_norm, b_qk_norm, b_sinks, ab_w_out, cd_norm, cd_w_in, c_q_norm, c_k_norm, c_cmp_pe, c_cmp_w1, c_cmp_w2, d_q_lat_norm, d_kv_lat_norm, d_w_uq, d_w_ukv, d_nope_norm, d_rope_norm, cd_w_out):
    bsz, seq, d = x.shape
    n = bsz * seq
    assert d == D_MODEL and seq % CH == 0 and seq >= NSA_WINDOW + TQ_ATT and seq // SLC_BLOCK <= LANES
    nq = seq // TQ
    nrow = n // TM
    srow = seq // TM
    xf = x.reshape(n, d)
    pos = jnp.arange(seq)
    cos64, sa64, sb64 = _rope_tables(pos, HEAD_DIM)
    cos32, sa32, sb32 = _rope_tables(pos, ROPE_DIM)
    bd64, bd32 = _block_diag(HEAD_DIM), _block_diag(ROPE_DIM)
    tril = (jnp.arange(CH)[:, None] >= jnp.arange(CH)[None, :]).astype(BF16)

    def row_spec(width, rows=TM):
        return pl.BlockSpec((rows, width), lambda i: (i, 0))

    def const_spec(shape):
        return pl.BlockSpec(shape, lambda i: tuple(0 for _ in shape))

    tab_spec = pl.BlockSpec((TM, LANES), lambda i: (i % srow, 0))
    head_spec = lambda nh: pl.BlockSpec((nh, TM, LANES), lambda i: (0, i, 0))

    wa = ab_w_in[0]
    w_ab = jnp.concatenate([wa[:, 0:1088], _pad_cols(wa[:, 1088:1092], 64), wa[:, 1092:]], axis=1).astype(BF16)
    gains_ab = jnp.stack([_tile_gain(a_qk_norm[0, 0]), _tile_gain(a_qk_norm[0, 1]), _pad_lanes(a_kidx_norm[0]),
                          _tile_gain(b_qk_norm[0, 0]), _tile_gain(b_qk_norm[0, 1])]
                         + [jnp.zeros((LANES,), F32)] * 3)
    sds = jax.ShapeDtypeStruct
    qa, ka, va, qi, ki, wi, za, qb, kb, vb, zb = pl.pallas_call(
        _proj_ab_kernel,
        grid=(nrow,),
        in_specs=[row_spec(d), const_spec((1, d)), const_spec((d, AB_W)), tab_spec, tab_spec, tab_spec,
                  const_spec((2 * LANES, LANES)), const_spec((8, LANES))],
        out_specs=[head_spec(8), row_spec(LANES), row_spec(2 * LANES), head_spec(4), row_spec(LANES), row_spec(LANES),
                   row_spec(512), head_spec(8), row_spec(LANES), row_spec(2 * LANES), row_spec(512)],
        out_shape=[sds((8, n, LANES), BF16), sds((n, LANES), BF16), sds((n, 2 * LANES), BF16),
                   sds((4, n, LANES), BF16), sds((n, LANES), BF16), sds((n, LANES), F32), sds((n, 512), F32),
                   sds((8, n, LANES), BF16), sds((n, LANES), BF16), sds((n, 2 * LANES), BF16), sds((n, 512), F32)],
        compiler_params=_params("parallel"),
    )(xf, ab_norm[0][None, :], w_ab, cos64, sa64, sb64, bd64, gains_ab)

    nqa = seq // TQ_ATT
    qt_heads = lambda nh: pl.BlockSpec((nh, TQ_ATT, LANES), lambda b, i: (0, b * nqa + i, 0))
    qt_rows = lambda width: pl.BlockSpec((TQ_ATT, width), lambda b, i: (b * nqa + i, 0))
    seq_rows = pl.BlockSpec((seq, LANES), lambda b, i: (b, 0))
    seq_rows_v = pl.BlockSpec((seq, 2 * LANES), lambda b, i: (b, 0))
    att_scratch = [pltpu.VMEM((N_HEADS * TQ_ATT, LANES), F32), pltpu.VMEM((N_HEADS * TQ_ATT, 2 * LANES), F32)]

    mix_a = pl.pallas_call(
        functools.partial(_dsa_kernel, topk=min(DSA_TOPK, seq // 4)),
        grid=(bsz, nqa),
        in_specs=[qt_heads(4), qt_rows(LANES), seq_rows, qt_heads(8), seq_rows, seq_rows_v, qt_rows(512),
                  pl.BlockSpec((CH, CH), lambda b, i: (0, 0))],
        out_specs=qt_rows(512),
        out_shape=sds((n, 512), BF16),
        scratch_shapes=[pltpu.VMEM((seq // CH, CH, TQ_ATT), I32), pltpu.VMEM((seq // CH, TQ_ATT, CH), BF16),
                        pltpu.VMEM((8, TQ_ATT), F32)] + att_scratch,
        compiler_params=_params("parallel", "arbitrary"),
    )(qi, wi, ki, qa, ka, va, za, tril)

    nq2 = nq // 2
    prev_rows = lambda width: pl.BlockSpec((TQ, width), lambda b, i: (b * nq + jnp.maximum(2 * i - 1, 0), 0))
    pair_rows = lambda width: pl.BlockSpec((2 * TQ, width), lambda b, i: (b * nq2 + i, 0))
    sink_rows = jnp.broadcast_to(jnp.repeat(b_sinks[0].astype(F32) * LOG2E, TQ)[:, None], (N_HEADS * TQ, LANES))
    mix_b = pl.pallas_call(
        _swa_kernel,
        grid=(bsz, nq2),
        in_specs=[pl.BlockSpec((N_HEADS, 2 * TQ, LANES), lambda b, i: (0, b * nq2 + i, 0)),
                  prev_rows(LANES), pair_rows(LANES), prev_rows(2 * LANES), pair_rows(2 * LANES),
                  pl.BlockSpec((N_HEADS * TQ, LANES), lambda b, i: (0, 0)), pair_rows(512)],
        out_specs=pair_rows(512),
        out_shape=sds((n, 512), BF16),
        compiler_params=_params("parallel", "arbitrary"),
    )(qb, kb, kb, vb, vb, sink_rows, zb)

    def out_proj(xin, m0, m1, w):
        return pl.pallas_call(
            _out_proj_kernel,
            grid=(nrow,),
            in_specs=[row_spec(d), row_spec(512), row_spec(512), const_spec((2 * 512, d))],
            out_specs=row_spec(d),
            out_shape=sds((n, d), F32),
            compiler_params=_params("parallel"),
        )(xin, m0, m1, w.astype(BF16))

    x1 = out_proj(xf, mix_a, mix_b, ab_w_out[0])

    wc = cd_w_in[0]
    w_cd = jnp.concatenate([wc[:, 0:1280], _pad_cols(wc[:, 1280:1304], LANES), wc[:, 1304:2200],
                            _pad_cols(wc[:, 2200:2232], LANES), wc[:, 2232:]], axis=1).astype(BF16)
    gains_cd = jnp.stack([_tile_gain(c_q_norm[0]), _tile_gain(c_k_norm[0, 1]), _tile_gain(c_k_norm[0, 2])]
                         + [jnp.zeros((LANES,), F32)] * 5)
    qc, kc_raw, vc_raw, ks, vs, kw, vw, gates, zc, cq, ckv, kr, zd = pl.pallas_call(
        _proj_cd_kernel,
        grid=(nrow,),
        in_specs=[row_spec(d), const_spec((1, d)), const_spec((d, AB_W)), tab_spec, tab_spec, tab_spec,
                  const_spec((2 * LANES, LANES)), const_spec((8, LANES))],
        out_specs=[head_spec(8), row_spec(LANES), row_spec(LANES), row_spec(LANES), row_spec(2 * LANES),
                   row_spec(LANES), row_spec(2 * LANES), row_spec(LANES), row_spec(512), row_spec(256),
                   row_spec(LANES), row_spec(LANES), row_spec(512)],
        out_shape=[sds((8, n, LANES), BF16), sds((n, LANES), F32), sds((n, LANES), F32), sds((n, LANES), BF16),
                   sds((n, 2 * LANES), BF16), sds((n, LANES), BF16), sds((n, 2 * LANES), BF16), sds((n, LANES), F32),
                   sds((n, 512), F32), sds((n, 256), F32), sds((n, LANES), F32), sds((n, LANES), F32),
                   sds((n, 512), F32)],
        compiler_params=_params("parallel"),
    )(x1, cd_norm[0][None, :], w_cd, cos64, sa64, sb64, bd64, gains_cd)

    ncmp = seq // CMP_STRIDE
    cw = CMP_STRIDE * N_KV * HEAD_DIM
    eye = jnp.eye(N_KV, dtype=F32)
    ccos, csa, csb = _rope_tables(jnp.arange(ncmp) * CMP_STRIDE + (2 * CMP_STRIDE - 1), HEAD_DIM)

    def compress(raw, which, is_key):
        w1 = c_cmp_w1[0, which].reshape(2, CMP_STRIDE, HEAD_DIM, -1)
        hid = w1.shape[-1]
        w1e = jnp.einsum("tjdu,gh->tjgdhu", w1, eye).reshape(2, cw, N_KV * hid).astype(BF16)
        w2e = jnp.einsum("ud,gh->guhd", c_cmp_w2[0, which], eye).reshape(N_KV * hid, N_KV * HEAD_DIM).astype(BF16)
        pe = c_cmp_pe[0, which].reshape(2, CMP_STRIDE, 1, HEAD_DIM)
        pe = jnp.broadcast_to(pe, (2, CMP_STRIDE, N_KV, HEAD_DIM)).reshape(2, cw)
        return pl.pallas_call(
            functools.partial(_compress_kernel, is_key=is_key),
            grid=(bsz,),
            in_specs=[pl.BlockSpec((ncmp, cw), lambda b: (b, 0)), const_spec((2, cw)),
                      const_spec((cw, N_KV * hid)), const_spec((cw, N_KV * hid)),
                      const_spec((N_KV * hid, LANES)), const_spec((1, LANES)),
                      const_spec((ncmp, LANES)), const_spec((ncmp, LANES)), const_spec((ncmp, LANES)),
                      const_spec((2 * LANES, LANES))],
            out_specs=pl.BlockSpec((ncmp, LANES), lambda b: (b, 0)),
            out_shape=sds((bsz * ncmp, LANES), BF16),
            compiler_params=_params("parallel"),
        )(raw.reshape(bsz * ncmp, cw), pe, w1e[0], w1e[1], w2e, _tile_gain(c_k_norm[0, 0])[None, :],
          ccos, csa, csb, bd64)

    kc = compress(kc_raw, 0, True)
    vc = compress(vc_raw, 1, False)

    ratio = SLC_BLOCK // CMP_STRIDE
    mm = jnp.arange(ncmp)[:, None]
    jj = jnp.arange(LANES)[None, :]
    wsel = (jnp.where((mm == ratio * jj - 1) | (mm == ratio * jj + ratio - 1), 1.0, 0.0)
            + jnp.where((mm >= ratio * jj) & (mm < ratio * jj + ratio - 1), 2.0, 0.0))
    wsel_t = jnp.where((mm < ncmp - 1) & (jj < seq // SLC_BLOCK), wsel, 0.0).astype(BF16).T

    cmp_rows = pl.BlockSpec((ncmp, LANES), lambda b, i: (b, 0))
    mix_c = pl.pallas_call(
        functools.partial(_nsa_kernel, n_sel=min(SLC_TOPN, seq // SLC_BLOCK)),
        grid=(bsz, nqa),
        in_specs=[qt_heads(8), cmp_rows, cmp_rows, seq_rows, seq_rows_v, seq_rows, seq_rows_v, qt_rows(LANES),
                  qt_rows(512), pl.BlockSpec((LANES, ncmp), lambda b, i: (0, 0))],
        out_specs=qt_rows(512),
        out_shape=sds((n, 512), BF16),
        scratch_shapes=att_scratch,
        compiler_params=_params("parallel", "arbitrary"),
    )(qc, kc, vc, ks, vs, kw, vw, gates, zc, wsel_t)

    wq = d_w_uq[0].reshape(Q_LORA, N_HEADS, NOPE_DIM + ROPE_DIM)
    wq = jnp.concatenate([wq[:, :, :NOPE_DIM].reshape(Q_LORA, -1), wq[:, :, NOPE_DIM:].reshape(Q_LORA, -1)],
                         axis=1).astype(BF16)
    wkv = d_w_ukv[0].reshape(KV_LORA, N_HEADS, NOPE_DIM + HEAD_DIM)
    wkv = jnp.concatenate([wkv[:, :, :NOPE_DIM].reshape(KV_LORA, -1), wkv[:, :, NOPE_DIM:].reshape(KV_LORA, -1)],
                          axis=1).astype(BF16)
    lat_gains = jnp.stack([d_q_lat_norm[0].astype(F32), _pad_lanes(d_kv_lat_norm[0], Q_LORA)]
                          + [jnp.zeros((Q_LORA,), F32)] * 6)
    gains_d = jnp.stack([_tile_gain(d_nope_norm[0, 0]), _tile_gain(d_rope_norm[0, 0]), _tile_gain(d_nope_norm[0, 1]),
                         _pad_lanes(d_rope_norm[0, 1])] + [jnp.zeros((LANES,), F32)] * 4)
    q_cat, k_cat, v_d = pl.pallas_call(
        _mla_prep_kernel,
        grid=(nrow,),
        in_specs=[row_spec(Q_LORA), row_spec(LANES), row_spec(LANES), const_spec((Q_LORA, 768)),
                  const_spec((KV_LORA, 1024)), const_spec((8, Q_LORA)), const_spec((8, LANES)),
                  tab_spec, tab_spec, tab_spec, const_spec((2 * LANES, LANES)), const_spec((2 * LANES, LANES))],
        out_specs=[head_spec(8), head_spec(8), pl.BlockSpec((4, TM, 2 * LANES), lambda i: (0, i, 0))],
        out_shape=[sds((8, n, LANES), BF16), sds((8, n, LANES), BF16), sds((4, n, 2 * LANES), BF16)],
        compiler_params=_params("parallel"),
    )(cq, ckv, kr, wq, wkv, lat_gains, gains_d, cos32, sa32, sb32, bd64, bd32)

    nqm = seq // TQ_MLA
    mix_d = pl.pallas_call(
        _mla_kernel,
        grid=(bsz, N_HEADS // MLA_HEADS, nqm),
        in_specs=[pl.BlockSpec((MLA_HEADS, TQ_MLA, LANES), lambda b, hg, i: (hg, b * nqm + i, 0)),
                  pl.BlockSpec((MLA_HEADS, seq, LANES), lambda b, hg, i: (hg, b, 0), pipeline_mode=pl.Buffered(1)),
                  pl.BlockSpec((MLA_HEADS // 2, seq, 2 * LANES), lambda b, hg, i: (hg, b, 0),
                               pipeline_mode=pl.Buffered(1)),
                  pl.BlockSpec((TQ_MLA, MLA_HEADS * HEAD_DIM), lambda b, hg, i: (b * nqm + i, hg))],
        out_specs=pl.BlockSpec((TQ_MLA, MLA_HEADS * HEAD_DIM), lambda b, hg, i: (b * nqm + i, hg)),
        out_shape=sds((n, 512), BF16),
        scratch_shapes=[pltpu.VMEM((MLA_HEADS * TQ_MLA, LANES), F32),
                        pltpu.VMEM((MLA_HEADS * TQ_MLA, 2 * LANES), F32)],
        compiler_params=_params("parallel", "parallel", "arbitrary"),
    )(q_cat, k_cat, v_d, zd)

    x2 = out_proj(x1, mix_c, mix_d, cd_w_out[0])
    return x2.reshape(bsz, seq, d)
```

```python
import functools

import jax
import jax.numpy as jnp
from jax import lax
from jax.experimental import pallas as pl
from jax.experimental.pallas import tpu as pltpu

F32, BF16, I32 = jnp.float32, jnp.bfloat16, jnp.int32

D_MODEL = 1024
HEAD_DIM = 64
N_HEADS = 8
N_KV = 2
GRP = N_HEADS // N_KV
IDX_HEADS = 4
DSA_TOPK = 256
SWA_WINDOW = 128
CMP_STRIDE = 16
SLC_BLOCK = 64
SLC_TOPN = 16
NSA_WINDOW = 512
Q_LORA = 256
KV_LORA = 128
NOPE_DIM = 64
ROPE_DIM = 32
ROPE_THETA = 10000.0
EPS = 1e-6

LANES = 128
TM = 1024
TQ = 128
CH = 512
TQ_ATT = 256
TQ_MLA = 512
MLA_HEADS = 8
UNROLL = 4
COUNT_ROWS = 16
ROW_BLOCK = 128
VMEM_LIMIT = 56 * 1024 * 1024

NEG_INF = float("-inf")
M_INIT = -1e30
INT_MIN = -(2 ** 31)

AB_W = 2944
LOG2E = 1.4426950408889634
Q_SCALE = HEAD_DIM ** -0.5 * LOG2E


def _nn(a, b):
    return jnp.dot(a, b, preferred_element_type=F32)


def _nt(a, b):
    return lax.dot_general(a, b, (((1,), (1,)), ((), ())), preferred_element_type=F32)


def _sigmoid(z):
    return 1.0 / (1.0 + jnp.exp(-z))


def _silu(z):
    return z * _sigmoid(z)


def _row_rms(x, g):
    return x * lax.rsqrt(jnp.mean(x * x, axis=-1, keepdims=True) + EPS) * g


def _seg_norm(s, bd, gain, group):
    s2 = s * s
    hi = s2.astype(BF16)
    lo = (s2 - hi.astype(F32)).astype(BF16)
    ss = _nn(jnp.concatenate([hi, lo], axis=1), bd)
    return s * lax.rsqrt(ss * (1.0 / group) + EPS) * gain


def _rope(s, cos, sa, sb, half):
    return s * cos + pltpu.roll(s, half, 1) * sa + pltpu.roll(s, LANES - half, 1) * sb


def _order_key(v):
    bits = lax.bitcast_convert_type(v, I32)
    return bits ^ ((bits >> 31) & 0x7FFFFFFF)


def _stack_rows(x, n):
    return jnp.concatenate([x] * n, axis=0)


def _with_ones(v):
    return jnp.concatenate([v.astype(BF16), jnp.ones(v.shape, BF16)], axis=1)


def _flash_update(s, v_ext, m_ref, acc_ref, rows):
    m_prev = m_ref[rows]
    sb = s.astype(BF16)
    m_new = jnp.maximum(m_prev, jnp.max(sb, axis=1, keepdims=True).astype(F32))
    alpha = jnp.exp2(m_prev - m_new)
    p = jnp.exp2(sb - jnp.concatenate([m_new.astype(BF16)] * (s.shape[1] // LANES), axis=1))
    acc_ref[rows] = jnp.concatenate([alpha, alpha], axis=1) * acc_ref[rows] + _nn(p, v_ext)
    m_ref[rows] = m_new


def _unrolled_loop(n, step, unroll=UNROLL):
    def body(j, carry):
        for u in range(unroll):
            step(j * unroll + u)
        return carry

    lax.fori_loop(0, n // unroll, body, 0)
    base = (n // unroll) * unroll
    p = unroll // 2
    while p >= 1:
        def tail(base=base, p=p):
            for u in range(p):
                step(base + u)

        pl.when((n & p) != 0)(tail)
        base = base + (n & p)
        p //= 2


def _flash_init(m_ref, acc_ref):
    m_ref[...] = jnp.full(m_ref.shape, M_INIT, F32)
    acc_ref[...] = jnp.zeros(acc_ref.shape, F32)


def _flash_result(acc_ref):
    acc = acc_ref[...]
    return acc[:, 0:LANES] / jnp.maximum(acc[:, LANES:2 * LANES], 1e-30)


def _place_q_heads(y, bd, gain, cos, sa, sb, out_ref, scale):
    lo_half = lax.broadcasted_iota(I32, (y.shape[0], LANES), 1) < HEAD_DIM
    for j in range(4):
        s = y[:, LANES * j:LANES * (j + 1)]
        s = _rope(_seg_norm(s, bd, gain, HEAD_DIM), cos, sa, sb, HEAD_DIM // 2) * scale
        r = pltpu.roll(s, HEAD_DIM, 1)
        if j < 2:
            out_ref[2 * j] = jnp.where(lo_half, s, 0.0).astype(BF16)
            out_ref[2 * j + 1] = jnp.where(lo_half, r, 0.0).astype(BF16)
        else:
            out_ref[2 * j] = jnp.where(lo_half, 0.0, r).astype(BF16)
            out_ref[2 * j + 1] = jnp.where(lo_half, 0.0, s).astype(BF16)


def _gated_store(o, z_ref, o_ref, tq, row0=0):
    lo_half = lax.broadcasted_iota(I32, (tq, LANES), 1) < HEAD_DIM
    for j in range(4):
        a = o[(2 * j) * tq:(2 * j + 1) * tq]
        b = o[(2 * j + 1) * tq:(2 * j + 2) * tq]
        if j < 2:
            slab = jnp.where(lo_half, a, pltpu.roll(b, HEAD_DIM, 1))
        else:
            slab = jnp.where(lo_half, pltpu.roll(a, HEAD_DIM, 1), b)
        z = z_ref[row0:row0 + tq, LANES * j:LANES * (j + 1)]
        o_ref[row0:row0 + tq, LANES * j:LANES * (j + 1)] = (slab * _silu(z)).astype(BF16)


def _proj_ab_kernel(x_ref, g_ref, w_ref, cos_ref, sa_ref, sb_ref, bd_ref, gains_ref,
                    qa_ref, ka_ref, va_ref, qi_ref, ki_ref, wi_ref, za_ref,
                    qb_ref, kb_ref, vb_ref, zb_ref):
    xn = _row_rms(x_ref[...], g_ref[...]).astype(BF16)
    cos, sa, sb, bd = cos_ref[...], sa_ref[...], sb_ref[...], bd_ref[...]
    lo_half = lax.broadcasted_iota(I32, (TM, LANES), 1) < HEAD_DIM

    def proj(c0, n):
        return _nn(xn, w_ref[:, c0:c0 + n])

    def k_slab(c0, gain):
        return _rope(_seg_norm(proj(c0, LANES), bd, gain, HEAD_DIM), cos, sa, sb, HEAD_DIM // 2)

    _place_q_heads(proj(0, 512), bd, gains_ref[0:1, :], cos, sa, sb, qa_ref, Q_SCALE)
    ka_ref[...] = k_slab(512, gains_ref[1:2, :]).astype(BF16)
    va_ref[...] = _with_ones(proj(640, LANES))
    qi = proj(768, 256)
    for j in range(2):
        s = _rope(qi[:, LANES * j:LANES * (j + 1)], cos, sa, sb, HEAD_DIM // 2)
        qi_ref[2 * j] = jnp.where(lo_half, s, 0.0).astype(BF16)
        qi_ref[2 * j + 1] = jnp.where(lo_half, pltpu.roll(s, HEAD_DIM, 1), 0.0).astype(BF16)
    kiw = proj(1024, LANES)
    wi_ref[...] = kiw
    ki_ref[...] = _rope(_seg_norm(kiw, bd, gains_ref[2:3, :], HEAD_DIM), cos, sa, sb, HEAD_DIM // 2).astype(BF16)
    za_ref[...] = proj(1152, 512)
    _place_q_heads(proj(1664, 512), bd, gains_ref[3:4, :], cos, sa, sb, qb_ref, Q_SCALE)
    kb_ref[...] = k_slab(2176, gains_ref[4:5, :]).astype(BF16)
    vb_ref[...] = _with_ones(proj(2304, LANES))
    zb_ref[...] = proj(2432, 512)


def _count_chunks(keys_ref, nch, nq, pred):
    def body(c, acc):
        return acc + jnp.sum(pred(keys_ref[c]).reshape(CH // COUNT_ROWS, COUNT_ROWS, nq), axis=0)

    acc = lax.fori_loop(0, nch, body, jnp.zeros((COUNT_ROWS, nq), F32))
    return jnp.sum(acc, axis=0, keepdims=True)


def _kth_threshold(count, nq, k):
    def bit_body(it, carry):
        tb, above = carry
        cand_b = tb | jnp.left_shift(jnp.int32(1), 31 - it)
        cand_s = cand_b ^ INT_MIN
        cnt = count(lambda kk: jnp.where(kk >= cand_s, 1.0, 0.0))
        ok = cnt >= k
        return jnp.where(ok, cand_b, tb), jnp.where(ok, above, cnt)

    tb, above = lax.fori_loop(0, 32, bit_body, (jnp.zeros((1, nq), I32), jnp.zeros((1, nq), F32)))
    return tb ^ INT_MIN, k - above


def _dsa_kernel(qi_ref, wi_ref, ki_ref, qa_ref, ka_ref, va_ref, za_ref, tril_ref, o_ref,
                keys_ref, bias_ref, ties_ref, m_ref, acc_ref, *, topk):
    TQ = TQ_ATT
    i = pl.program_id(1)
    t0 = i * TQ
    nch = (t0 + TQ + CH - 1) // CH
    key_c = lax.broadcasted_iota(I32, (CH, TQ), 0)
    qry_c = t0 + lax.broadcasted_iota(I32, (CH, TQ), 1)

    qi = qi_ref[...].reshape(IDX_HEADS * TQ, LANES)
    w_t = wi_ref[...].T * ((HEAD_DIM * IDX_HEADS) ** -0.5)
    ws = [w_t[HEAD_DIM + h:HEAD_DIM + h + 1, :] for h in range(IDX_HEADS)]

    def score_chunk(c):
        off = pl.multiple_of(c * CH, CH)
        lg = _nt(ki_ref[pl.ds(off, CH), :], qi)
        sc = ws[0] * jnp.maximum(lg[:, 0:TQ], 0.0)
        for h in range(1, IDX_HEADS):
            sc = sc + ws[h] * jnp.maximum(lg[:, h * TQ:(h + 1) * TQ], 0.0)
        sc = jnp.where(sc == 0.0, 0.0, sc)
        keys_ref[c] = jnp.where(key_c + off <= qry_c, _order_key(sc), INT_MIN)

    _unrolled_loop(nch, score_chunk)

    count = functools.partial(_count_chunks, keys_ref, nch, TQ)
    thr, need = _kth_threshold(count, TQ, float(topk))

    tril = tril_ref[...]

    ties_ref[...] = jnp.zeros(ties_ref.shape, F32)

    def select_chunk(c):
        kk = keys_ref[c]
        pref = _nn(tril, jnp.where(kk == thr, 1.0, 0.0).astype(BF16)) + ties_ref[0:1, :]
        ties_ref[0:1, :] = pref[CH - 1:CH, :]
        tie = jnp.where(kk == thr, jnp.where(pref <= need, 0.0, NEG_INF), NEG_INF)
        bias = jnp.where(key_c + c * CH <= qry_c, jnp.where(kk > thr, 0.0, tie), NEG_INF)
        bias_ref[c] = bias.T.astype(BF16)

    _unrolled_loop(nch, select_chunk, unroll=2 * UNROLL)

    _flash_init(m_ref, acc_ref)
    q_all = qa_ref[...].reshape(N_HEADS * TQ, LANES)

    def att_chunk(c):
        off = pl.multiple_of(c * CH, CH)
        s = _nt(q_all, ka_ref[pl.ds(off, CH), :]).astype(BF16) + _stack_rows(bias_ref[c], N_HEADS)
        _flash_update(s, va_ref[pl.ds(off, CH), :], m_ref, acc_ref, slice(None))

    _unrolled_loop(nch, att_chunk, unroll=2 * UNROLL)
    _gated_store(_flash_result(acc_ref), za_ref, o_ref, TQ)


def _swa_kernel(q_ref, kp_ref, kc_ref, vp_ref, vc_ref, sink_ref, z_ref, o_ref):
    i = pl.program_id(1)
    qi = lax.broadcasted_iota(I32, (TQ, 2 * TQ), 0)
    kj = lax.broadcasted_iota(I32, (TQ, 2 * TQ), 1)
    sink = sink_ref[...]
    for half in range(2):
        q = q_ref[:, half * TQ:(half + 1) * TQ, :].reshape(N_HEADS * TQ, LANES)
        if half == 0:
            k2 = jnp.concatenate([kp_ref[...], kc_ref[0:TQ, :]], axis=0)
            v2 = jnp.concatenate([vp_ref[...], vc_ref[0:TQ, :]], axis=0)
            lo = jnp.where(i > 0, 0, TQ)
        else:
            k2, v2, lo = kc_ref[...], vc_ref[...], 0
        keep = jnp.where(kj >= lo, jnp.where(kj > qi, jnp.where(kj <= qi + TQ, 0.0, NEG_INF), NEG_INF), NEG_INF)
        s = _nt(q, k2) + _stack_rows(keep, N_HEADS)
        m = jnp.maximum(jnp.max(s, axis=1, keepdims=True), sink)
        p = jnp.exp2(s - jnp.concatenate([m, m], axis=1))
        o = _nn(p.astype(BF16), v2)
        den = o[:, LANES:2 * LANES] + jnp.exp2(sink - m)
        _gated_store(o[:, 0:LANES] / den, z_ref, o_ref, TQ, half * TQ)


def _out_proj_kernel(x_ref, ma_ref, mb_ref, w_ref, o_ref):
    half = ma_ref.shape[1]
    o_ref[...] = x_ref[...] + _nn(ma_ref[...], w_ref[0:half, :]) + _nn(mb_ref[...], w_ref[half:2 * half, :])


def _proj_cd_kernel(x_ref, g_ref, w_ref, cos_ref, sa_ref, sb_ref, bd_ref, gains_ref,
                    qc_ref, kc_ref, vc_ref, ks_ref, vs_ref, kw_ref, vw_ref, gt_ref, zc_ref,
                    cq_ref, ckv_ref, kr_ref, zd_ref):
    xn = _row_rms(x_ref[...], g_ref[...]).astype(BF16)
    cos, sa, sb, bd = cos_ref[...], sa_ref[...], sb_ref[...], bd_ref[...]

    def proj(c0, n):
        return _nn(xn, w_ref[:, c0:c0 + n])

    def k_slab(c0, gain):
        return _rope(_seg_norm(proj(c0, LANES), bd, gain, HEAD_DIM), cos, sa, sb, HEAD_DIM // 2)

    _place_q_heads(proj(0, 512), bd, gains_ref[0:1, :], cos, sa, sb, qc_ref, Q_SCALE)
    kc_ref[...] = proj(512, LANES)
    vc_ref[...] = proj(640, LANES)
    ks_ref[...] = k_slab(768, gains_ref[1:2, :]).astype(BF16)
    vs_ref[...] = _with_ones(proj(896, LANES))
    kw_ref[...] = k_slab(1024, gains_ref[2:3, :]).astype(BF16)
    vw_ref[...] = _with_ones(proj(1152, LANES))
    gt_ref[...] = _sigmoid(proj(1280, LANES))
    zc_ref[...] = proj(1408, 512)
    cq_ref[...] = proj(1920, 256)
    ckv_ref[...] = proj(2176, LANES)
    kr_ref[...] = proj(2304, LANES)
    zd_ref[...] = proj(2432, 512)


def _compress_kernel(c_ref, pe_ref, w1a_ref, w1b_ref, w2_ref, gain_ref, cos_ref, sa_ref, sb_ref, bd_ref,
                     o_ref, *, is_key):
    c = c_ref[...]
    n = c.shape[0]
    a = _nn((c + pe_ref[0:1, :]).astype(BF16), w1a_ref[...])
    b = _nn((c + pe_ref[1:2, :]).astype(BF16), w1b_ref[...])
    h = _silu(a + pltpu.roll(b, n - 1, 0))
    y = _nn(h.astype(BF16), w2_ref[...])
    if is_key:
        y = _rope(_seg_norm(y, bd_ref[...], gain_ref[...], HEAD_DIM), cos_ref[...], sa_ref[...], sb_ref[...],
                  HEAD_DIM // 2)
    o_ref[...] = y.astype(BF16)


def _nsa_kernel(q_ref, kc_ref, vc_ref, ks_ref, vs_ref, kw_ref, vw_ref, gt_ref, z_ref, wsel_ref, o_ref,
                m_ref, acc_ref, *, n_sel):
    TQ = TQ_ATT
    i = pl.program_id(1)
    t0 = i * TQ
    nch = (t0 + TQ + CH - 1) // CH
    ncmp = kc_ref.shape[0]
    rows_all = N_HEADS * TQ
    q = q_ref[...].reshape(rows_all, LANES)

    cend = lax.broadcasted_iota(I32, (TQ, ncmp), 1) * CMP_STRIDE + (2 * CMP_STRIDE - 1)
    bias_c = jnp.where(cend <= t0 + lax.broadcasted_iota(I32, (TQ, ncmp), 0), 0.0, NEG_INF)
    s = _nt(q, kc_ref[...]) + _stack_rows(bias_c, N_HEADS)
    m = jnp.max(s, axis=1, keepdims=True)
    m = jnp.where(m > NEG_INF, m, 0.0)
    p = jnp.exp2(s - m)
    p = p / jnp.maximum(jnp.sum(p, axis=1, keepdims=True), 1e-30)
    o_cmp = _nn(p.astype(BF16), vc_ref[...])

    span = NSA_WINDOW + TQ
    start = pl.multiple_of(jnp.maximum(t0 - NSA_WINDOW, 0), TQ)
    diff = t0 + lax.broadcasted_iota(I32, (TQ, span), 0) - (start + lax.broadcasted_iota(I32, (TQ, span), 1))
    bias_w = jnp.where(diff >= 0, jnp.where(diff < NSA_WINDOW, 0.0, NEG_INF), NEG_INF)
    sw = _nt(q, kw_ref[pl.ds(start, span), :]).astype(BF16) + _stack_rows(bias_w.astype(BF16), N_HEADS)
    pw = jnp.exp2(sw - jnp.max(sw, axis=1, keepdims=True))
    o_win = _nn(pw, vw_ref[pl.ds(start, span), :])
    o_win = o_win[:, 0:LANES] / o_win[:, LANES:2 * LANES]

    blk = lax.broadcasted_iota(I32, (LANES, TQ), 0)
    cur = (t0 + lax.broadcasted_iota(I32, (LANES, TQ), 1)) >> 6
    wsel_t = wsel_ref[...]
    keys = []
    for g in range(N_KV):
        imp = p[(GRP * g) * TQ:(GRP * g + 1) * TQ]
        for r in range(1, GRP):
            imp = imp + p[(GRP * g + r) * TQ:(GRP * g + r + 1) * TQ]
        hi = imp.astype(BF16)
        lo = (imp - hi.astype(F32)).astype(BF16)
        imp_s = _nt(wsel_t, hi) + _nt(wsel_t, lo)
        forced = jnp.where(blk == 0, jnp.inf, jnp.where(blk >= cur - 1, jnp.inf, imp_s))
        keys.append(_order_key(jnp.where(blk <= cur, forced, NEG_INF)))
    kk = jnp.concatenate(keys, axis=1)

    def bit_body(it, tb):
        cand_b = tb | jnp.left_shift(jnp.int32(1), 31 - it)
        cnt = jnp.sum(jnp.where(kk >= (cand_b ^ INT_MIN), 1.0, 0.0), axis=0, keepdims=True)
        return jnp.where(cnt >= n_sel, cand_b, tb)

    thr = lax.fori_loop(0, 32, bit_body, jnp.zeros((1, N_KV * TQ), I32)) ^ INT_MIN
    need = n_sel - jnp.sum(jnp.where(kk > thr, 1.0, 0.0), axis=0, keepdims=True)
    tril = jnp.where(lax.broadcasted_iota(I32, (LANES, LANES), 0) >= lax.broadcasted_iota(I32, (LANES, LANES), 1),
                     1.0, 0.0).astype(BF16)
    pref = _nn(tril, jnp.where(kk == thr, 1.0, 0.0).astype(BF16))
    tie = jnp.where(kk == thr, jnp.where(pref <= need, 1.0, 0.0), 0.0)
    sel_t = jnp.where(kk > thr, 1.0, tie)
    sel = jnp.concatenate([sel_t[:, g * TQ:(g + 1) * TQ].T for g in range(N_KV)], axis=0).astype(BF16)

    _flash_init(m_ref, acc_ref)
    blk_row =lax.broadcasted_iota(I32, (LANES, CH), 0)
    blk_col = lax.broadcasted_iota(I32, (LANES, CH), 1) >> 6
    row_c = t0 + lax.broadcasted_iota(I32, (TQ, CH), 0)
    col_c = lax.broadcasted_iota(I32, (TQ, CH), 1)

    def slc_chunk(c):
        off = pl.multiple_of(c * CH, CH)
        expand = jnp.where(blk_row == blk_col + c * (CH // SLC_BLOCK), 1.0, 0.0).astype(BF16)
        tok = _nn(sel, expand)
        causal = col_c + off <= row_c
        parts = []
        for g in range(N_KV):
            bias_g = jnp.where(causal, jnp.where(tok[g * TQ:(g + 1) * TQ] > 0.5, 0.0, NEG_INF), NEG_INF)
            parts.extend([bias_g.astype(BF16)] * GRP)
        sc = _nt(q, ks_ref[pl.ds(off, CH), :]).astype(BF16) + jnp.concatenate(parts, axis=0)
        _flash_update(sc, vs_ref[pl.ds(off, CH), :], m_ref, acc_ref, slice(None))

    _unrolled_loop(nch, slc_chunk, unroll=2 * UNROLL)
    o_slc = _flash_result(acc_ref)

    gates = gt_ref[...]
    outs = []
    for h in range(N_HEADS):
        rs = slice(h * TQ, (h + 1) * TQ)
        outs.append(gates[:, 3 * h:3 * h + 1] * o_cmp[rs] + gates[:, 3 * h + 1:3 * h + 2] * o_slc[rs]
                    + gates[:, 3 * h + 2:3 * h + 3] * o_win[rs])
    _gated_store(jnp.concatenate(outs, axis=0), z_ref, o_ref, TQ)


def _mla_prep_kernel(cq_ref, ckv_ref, kr_ref, wq_ref, wkv_ref, lat_ref, gains_ref, cos_ref, sa_ref, sb_ref,
                     bd64_ref, bd32_ref, q_ref, k_ref, v_ref):
    lane = lax.broadcasted_iota(I32, (TM, LANES), 1)
    lo_half = lane < NOPE_DIM
    cos, sa, sb = cos_ref[...], sa_ref[...], sb_ref[...]
    bd64, bd32 = bd64_ref[...], bd32_ref[...]
    scale = (NOPE_DIM + ROPE_DIM) ** -0.5 * LOG2E

    def rope_slab(s, gain):
        return _rope(_seg_norm(s, bd32, gain, ROPE_DIM), cos, sa, sb, ROPE_DIM // 2)

    q = _nn(_row_rms(cq_ref[...], lat_ref[0:1, :]).astype(BF16), wq_ref[...])
    kv = _nn(_row_rms(ckv_ref[...], lat_ref[1:2, 0:KV_LORA]).astype(BF16), wkv_ref[...])
    k_rope = pltpu.roll(rope_slab(kr_ref[...], gains_ref[3:4, :]), NOPE_DIM, 1)
    q_rope = [rope_slab(q[:, 512 + LANES * j:512 + LANES * (j + 1)], gains_ref[1:2, :]) * scale for j in range(2)]
    for j in range(4):
        qn = _seg_norm(q[:, LANES * j:LANES * (j + 1)], bd64, gains_ref[0:1, :], NOPE_DIM) * scale
        kn = _seg_norm(kv[:, LANES * j:LANES * (j + 1)], bd64, gains_ref[2:3, :], NOPE_DIM)
        for e in range(2):
            h = 2 * j + e
            qn_h = qn if e == 0 else pltpu.roll(qn, NOPE_DIM, 1)
            kn_h = kn if e == 0 else pltpu.roll(kn, NOPE_DIM, 1)
            shift = (NOPE_DIM - ROPE_DIM * (h % 4)) % LANES
            qr = q_rope[h // 4]
            qr_h = qr if shift == 0 else pltpu.roll(qr, shift, 1)
            q_ref[h] = jnp.where(lo_half, qn_h, jnp.where(lane < NOPE_DIM + ROPE_DIM, qr_h, 0.0)).astype(BF16)
            k_ref[h] = jnp.where(lo_half, kn_h, k_rope).astype(BF16)
    for j in range(4):
        v_ref[j] = _with_ones(kv[:, 512 + LANES * j:512 + LANES * (j + 1)])


def _mla_kernel(q_ref, k_ref, v_ref, z_ref, o_ref, m_ref, acc_ref):
    i = pl.program_id(2)
    tq = TQ_MLA
    _flash_init(m_ref, acc_ref)

    def step(c, masked):
        off = pl.multiple_of(c * CH, CH)
        for e in range(MLA_HEADS):
            k = k_ref[e, pl.ds(off, CH), :]
            v = v_ref[e // 2, pl.ds(off, CH), :]
            for rb in range(tq // ROW_BLOCK):
                s = _nt(q_ref[e, rb * ROW_BLOCK:(rb + 1) * ROW_BLOCK, :], k)
                if masked:
                    qpos = rb * ROW_BLOCK + lax.broadcasted_iota(I32, (ROW_BLOCK, CH), 0)
                    s = jnp.where(lax.broadcasted_iota(I32, (ROW_BLOCK, CH), 1) <= qpos, s, NEG_INF)
                _flash_update(s, v, m_ref, acc_ref, slice(e * tq + rb * ROW_BLOCK, e * tq + (rb + 1) * ROW_BLOCK))

    _unrolled_loop(i, lambda c: step(c, False))
    step(i, True)
    o = _flash_result(acc_ref)
    lo_half = lax.broadcasted_iota(I32, (tq, LANES), 1) < NOPE_DIM
    for pr in range(MLA_HEADS // 2):
        slab = jnp.where(lo_half, o[(2 * pr) * tq:(2 * pr + 1) * tq], o[(2 * pr + 1) * tq:(2 * pr + 2) * tq])
        lanes = slice(pr * LANES, (pr + 1) * LANES)
        o_ref[:, lanes] = (slab * _silu(z_ref[:, lanes])).astype(BF16)


def _rope_tables(pos, dim):
    inv = jnp.power(jnp.float32(ROPE_THETA), -jnp.arange(0, dim, 2, dtype=F32) / dim)
    ang = pos.astype(F32)[:, None] * inv[None, :]
    cos, sin = jnp.cos(ang), jnp.sin(ang)
    reps = LANES // dim
    zero = jnp.zeros_like(sin)
    cos_t = jnp.tile(jnp.concatenate([cos, cos], axis=1), (1, reps))
    sa_t = jnp.tile(jnp.concatenate([zero, sin], axis=1), (1, reps))
    sb_t = jnp.tile(jnp.concatenate([-sin, zero], axis=1), (1, reps))
    return cos_t, sa_t, sb_t


def _block_diag(group):
    r = jnp.arange(LANES) // group
    bd = (r[:, None] == r[None, :]).astype(BF16)
    return jnp.concatenate([bd, bd], axis=0)


def _tile_gain(g, width=LANES):
    return jnp.tile(g.astype(F32), width // g.shape[0])


def _pad_lanes(g, width=LANES):
    return jnp.concatenate([g.astype(F32), jnp.zeros((width - g.shape[0],), F32)])


def _pad_cols(w, width):
    return jnp.concatenate([w, jnp.zeros((w.shape[0], width - w.shape[1]), w.dtype)], axis=1)


def _params(*sem, flags=None):
    return pltpu.CompilerParams(dimension_semantics=sem, vmem_limit_bytes=VMEM_LIMIT, flags=flags)


def kernel(x, ab_norm, ab_w_in, a_qk_norm, a_kidx_norm, b_qk_norm, b_sinks, ab_w_out, cd_norm, cd_w_in, c_q_norm, c_k_norm, c_cmp_pe, c_cmp_w1, c_cmp_w2, d_q_lat_norm, d_kv_lat_norm, d_w_uq, d_w_ukv, d_nope_norm, d_rope_norm, cd_w_out):
    bsz, seq, d = x.shape
    n = bsz * seq
    assert d == D_MODEL and seq % CH == 0 and seq >= NSA_WINDOW + TQ_ATT and seq // SLC_BLOCK <= LANES
    assert TQ == SWA_WINDOW and TQ_MLA == CH and seq % TM == 0
    nq = seq // TQ
    nrow = n // TM
    srow = seq // TM
    xf = x.reshape(n, d)
    pos = jnp.arange(seq)
    cos64, sa64, sb64 = _rope_tables(pos, HEAD_DIM)
    cos32, sa32, sb32 = _rope_tables(pos, ROPE_DIM)
    bd64, bd32 = _block_diag(HEAD_DIM), _block_diag(ROPE_DIM)
    tril = (jnp.arange(CH)[:, None] >= jnp.arange(CH)[None, :]).astype(BF16)

    def row_spec(width, rows=TM):
        return pl.BlockSpec((rows, width), lambda i: (i, 0))

    def const_spec(shape):
        return pl.BlockSpec(shape, lambda i: tuple(0 for _ in shape))

    tab_spec = pl.BlockSpec((TM, LANES), lambda i: (i % srow, 0))
    head_spec = lambda nh: pl.BlockSpec((nh, TM, LANES), lambda i: (0, i, 0))

    wa = ab_w_in[0]
    w_ab = jnp.concatenate([wa[:, 0:1088], _pad_cols(wa[:, 1088:1092], 64), wa[:, 1092:]], axis=1).astype(BF16)
    gains_ab = jnp.stack([_tile_gain(a_qk_norm[0, 0]), _tile_gain(a_qk_norm[0, 1]), _pad_lanes(a_kidx_norm[0]),
                          _tile_gain(b_qk_norm[0, 0]), _tile_gain(b_qk_norm[0, 1])]
                         + [jnp.zeros((LANES,), F32)] * 3)
    sds = jax.ShapeDtypeStruct
    qa, ka, va, qi, ki, wi, za, qb, kb, vb, zb = pl.pallas_call(
        _proj_ab_kernel,
        grid=(nrow,),
        in_specs=[row_spec(d), const_spec((1, d)), const_spec((d, AB_W)), tab_spec, tab_spec, tab_spec,
                  const_spec((2 * LANES, LANES)), const_spec((8, LANES))],
        out_specs=[head_spec(8), row_spec(LANES), row_spec(2 * LANES), head_spec(4), row_spec(LANES), row_spec(LANES),
                   row_spec(512), head_spec(8), row_spec(LANES), row_spec(2 * LANES), row_spec(512)],
        out_shape=[sds((8, n, LANES), BF16), sds((n, LANES), BF16), sds((n, 2 * LANES), BF16),
                   sds((4, n, LANES), BF16), sds((n, LANES), BF16), sds((n, LANES), F32), sds((n, 512), F32),
                   sds((8, n, LANES), BF16), sds((n, LANES), BF16), sds((n, 2 * LANES), BF16), sds((n, 512), F32)],
        compiler_params=_params("parallel"),
    )(xf, ab_norm[0][None, :], w_ab, cos64, sa64, sb64, bd64, gains_ab)

    nqa = seq // TQ_ATT
    qt_heads = lambda nh: pl.BlockSpec((nh, TQ_ATT, LANES), lambda b, i: (0, b * nqa + i, 0))
    qt_rows = lambda width: pl.BlockSpec((TQ_ATT, width), lambda b, i: (b * nqa + i, 0))
    seq_rows = pl.BlockSpec((seq, LANES), lambda b, i: (b, 0))
    seq_rows_v = pl.BlockSpec((seq, 2 * LANES), lambda b, i: (b, 0))
    att_scratch = [pltpu.VMEM((N_HEADS * TQ_ATT, LANES), F32), pltpu.VMEM((N_HEADS * TQ_ATT, 2 * LANES), F32)]

    mix_a = pl.pallas_call(
        functools.partial(_dsa_kernel, topk=min(DSA_TOPK, seq // 4)),
        grid=(bsz, nqa),
        in_specs=[qt_heads(4), qt_rows(LANES), seq_rows, qt_heads(8), seq_rows, seq_rows_v, qt_rows(512),
                  pl.BlockSpec((CH, CH), lambda b, i: (0, 0))],
        out_specs=qt_rows(512),
        out_shape=sds((n, 512), BF16),
        scratch_shapes=[pltpu.VMEM((seq // CH, CH, TQ_ATT), I32), pltpu.VMEM((seq // CH, TQ_ATT, CH), BF16),
                        pltpu.VMEM((8, TQ_ATT), F32)] + att_scratch,
        compiler_params=_params("parallel", "arbitrary"),
    )(qi, wi, ki, qa, ka, va, za, tril)

    nq2 = nq // 2
    prev_rows = lambda width: pl.BlockSpec((TQ, width), lambda b, i: (b * nq + jnp.maximum(2 * i - 1, 0), 0))
    pair_rows = lambda width: pl.BlockSpec((2 * TQ, width), lambda b, i: (b * nq2 + i, 0))
    sink_rows = jnp.broadcast_to(jnp.repeat(b_sinks[0].astype(F32) * LOG2E, TQ)[:, None], (N_HEADS * TQ, LANES))
    mix_b = pl.pallas_call(
        _swa_kernel,
        grid=(bsz, nq2),
        in_specs=[pl.BlockSpec((N_HEADS, 2 * TQ, LANES), lambda b, i: (0, b * nq2 + i, 0)),
                  prev_rows(LANES), pair_rows(LANES), prev_rows(2 * LANES), pair_rows(2 * LANES),
                  pl.BlockSpec((N_HEADS * TQ, LANES), lambda b, i: (0, 0)), pair_rows(512)],
        out_specs=pair_rows(512),
        out_shape=sds((n, 512), BF16),
        compiler_params=_params("parallel", "arbitrary"),
    )(qb, kb, kb, vb, vb, sink_rows, zb)

    def out_proj(xin, m0, m1, w):
        return pl.pallas_call(
            _out_proj_kernel,
            grid=(nrow,),
            in_specs=[row_spec(d), row_spec(512), row_spec(512), const_spec((2 * 512, d))],
            out_specs=row_spec(d),
            out_shape=sds((n, d), F32),
            compiler_params=_params("parallel"),
        )(xin, m0, m1, w.astype(BF16))

    x1 = out_proj(xf, mix_a, mix_b, ab_w_out[0])

    wc = cd_w_in[0]
    w_cd = jnp.concatenate([wc[:, 0:1280], _pad_cols(wc[:, 1280:1304], LANES), wc[:, 1304:2200],
                            _pad_cols(wc[:, 2200:2232], LANES), wc[:, 2232:]], axis=1).astype(BF16)
    gains_cd = jnp.stack([_tile_gain(c_q_norm[0]), _tile_gain(c_k_norm[0, 1]), _tile_gain(c_k_norm[0, 2])]
                         + [jnp.zeros((LANES,), F32)] * 5)
    qc, kc_raw, vc_raw, ks, vs, kw, vw, gates, zc, cq, ckv, kr, zd = pl.pallas_call(
        _proj_cd_kernel,
        grid=(nrow,),
        in_specs=[row_spec(d), const_spec((1, d)), const_spec((d, AB_W)), tab_spec, tab_spec, tab_spec,
                  const_spec((2 * LANES, LANES)), const_spec((8, LANES))],
        out_specs=[head_spec(8), row_spec(LANES), row_spec(LANES), row_spec(LANES), row_spec(2 * LANES),
                   row_spec(LANES), row_spec(2 * LANES), row_spec(LANES), row_spec(512), row_spec(256),
                   row_spec(LANES), row_spec(LANES), row_spec(512)],
        out_shape=[sds((8, n, LANES), BF16), sds((n, LANES), F32), sds((n, LANES), F32), sds((n, LANES), BF16),
                   sds((n, 2 * LANES), BF16), sds((n, LANES), BF16), sds((n, 2 * LANES), BF16), sds((n, LANES), F32),
                   sds((n, 512), F32), sds((n, 256), F32), sds((n, LANES), F32), sds((n, LANES), F32),
                   sds((n, 512), F32)],
        compiler_params=_params("parallel"),
    )(x1, cd_norm[0][None, :], w_cd, cos64, sa64, sb64, bd64, gains_cd)

    ncmp = seq // CMP_STRIDE
    cw = CMP_STRIDE * N_KV * HEAD_DIM
    eye = jnp.eye(N_KV, dtype=F32)
    ccos, csa, csb = _rope_tables(jnp.arange(ncmp) * CMP_STRIDE + (2 * CMP_STRIDE - 1), HEAD_DIM)

    def compress(raw, which, is_key):
        w1 = c_cmp_w1[0, which].reshape(2, CMP_STRIDE, HEAD_DIM, -1)
        hid = w1.shape[-1]
        w1e = jnp.einsum("tjdu,gh->tjgdhu", w1, eye).reshape(2, cw, N_KV * hid).astype(BF16)
        w2e = jnp.einsum("ud,gh->guhd", c_cmp_w2[0, which], eye).reshape(N_KV * hid, N_KV * HEAD_DIM).astype(BF16)
        pe = c_cmp_pe[0, which].reshape(2, CMP_STRIDE, 1, HEAD_DIM)
        pe = jnp.broadcast_to(pe, (2, CMP_STRIDE, N_KV, HEAD_DIM)).reshape(2, cw)
        return pl.pallas_call(
            functools.partial(_compress_kernel, is_key=is_key),
            grid=(bsz,),
            in_specs=[pl.BlockSpec((ncmp, cw), lambda b: (b, 0)), const_spec((2, cw)),
                      const_spec((cw, N_KV * hid)), const_spec((cw, N_KV * hid)),
                      const_spec((N_KV * hid, LANES)), const_spec((1, LANES)),
                      const_spec((ncmp, LANES)), const_spec((ncmp, LANES)), const_spec((ncmp, LANES)),
                      const_spec((2 * LANES, LANES))],
            out_specs=pl.BlockSpec((ncmp, LANES), lambda b: (b, 0)),
            out_shape=sds((bsz * ncmp, LANES), BF16),
            compiler_params=_params("parallel"),
        )(raw.reshape(bsz * ncmp, cw), pe, w1e[0], w1e[1], w2e, _tile_gain(c_k_norm[0, 0])[None, :],
          ccos, csa, csb, bd64)

    kc = compress(kc_raw, 0, True)
    vc = compress(vc_raw, 1, False)

    ratio = SLC_BLOCK // CMP_STRIDE
    mm = jnp.arange(ncmp)[:, None]
    jj = jnp.arange(LANES)[None, :]
    wsel = (jnp.where((mm == ratio * jj - 1) | (mm == ratio * jj + ratio - 1), 1.0, 0.0)
            + jnp.where((mm >= ratio * jj) & (mm < ratio * jj + ratio - 1), 2.0, 0.0))
    wsel_t = jnp.where((mm < ncmp - 1) & (jj < seq // SLC_BLOCK), wsel, 0.0).astype(BF16).T

    cmp_rows = pl.BlockSpec((ncmp, LANES), lambda b, i: (b, 0))
    mix_c = pl.pallas_call(
        functools.partial(_nsa_kernel, n_sel=min(SLC_TOPN, seq // SLC_BLOCK)),
        grid=(bsz, nqa),
        in_specs=[qt_heads(8), cmp_rows, cmp_rows, seq_rows, seq_rows_v, seq_rows, seq_rows_v, qt_rows(LANES),
                  qt_rows(512), pl.BlockSpec((LANES, ncmp), lambda b, i: (0, 0))],
        out_specs=qt_rows(512),
        out_shape=sds((n, 512), BF16),
        scratch_shapes=att_scratch,
        compiler_params=_params("parallel", "arbitrary"),
    )(qc, kc, vc, ks, vs, kw, vw, gates, zc, wsel_t)

    wq = d_w_uq[0].reshape(Q_LORA, N_HEADS, NOPE_DIM + ROPE_DIM)
    wq = jnp.concatenate([wq[:, :, :NOPE_DIM].reshape(Q_LORA, -1), wq[:, :, NOPE_DIM:].reshape(Q_LORA, -1)],
                         axis=1).astype(BF16)
    wkv = d_w_ukv[0].reshape(KV_LORA, N_HEADS, NOPE_DIM + HEAD_DIM)
    wkv = jnp.concatenate([wkv[:, :, :NOPE_DIM].reshape(KV_LORA, -1), wkv[:, :, NOPE_DIM:].reshape(KV_LORA, -1)],
                          axis=1).astype(BF16)
    lat_gains = jnp.stack([d_q_lat_norm[0].astype(F32), _pad_lanes(d_kv_lat_norm[0], Q_LORA)]
                          + [jnp.zeros((Q_LORA,), F32)] * 6)
    gains_d = jnp.stack([_tile_gain(d_nope_norm[0, 0]), _tile_gain(d_rope_norm[0, 0]), _tile_gain(d_nope_norm[0, 1]),
                         _pad_lanes(d_rope_norm[0, 1])] + [jnp.zeros((LANES,), F32)] * 4)
    q_cat, k_cat, v_d = pl.pallas_call(
        _mla_prep_kernel,
        grid=(nrow,),
        in_specs=[row_spec(Q_LORA), row_spec(LANES), row_spec(LANES), const_spec((Q_LORA, 768)),
                  const_spec((KV_LORA, 1024)), const_spec((8, Q_LORA)), const_spec((8, LANES)),
                  tab_spec, tab_spec, tab_spec, const_spec((2 * LANES, LANES)), const_spec((2 * LANES, LANES))],
        out_specs=[head_spec(8), head_spec(8), pl.BlockSpec((4, TM, 2 * LANES), lambda i: (0, i, 0))],
        out_shape=[sds((8, n, LANES), BF16), sds((8, n, LANES), BF16), sds((4, n, 2 * LANES), BF16)],
        compiler_params=_params("parallel"),
    )(cq, ckv, kr, wq, wkv, lat_gains, gains_d, cos32, sa32, sb32, bd64, bd32)

    nqm = seq // TQ_MLA
    mix_d = pl.pallas_call(
        _mla_kernel,
        grid=(bsz, N_HEADS // MLA_HEADS, nqm),
        in_specs=[pl.BlockSpec((MLA_HEADS, TQ_MLA, LANES), lambda b, hg, i: (hg, b * nqm + i, 0)),
                  pl.BlockSpec((MLA_HEADS, seq, LANES), lambda b, hg, i: (hg, b, 0), pipeline_mode=pl.Buffered(1)),
                  pl.BlockSpec((MLA_HEADS // 2, seq, 2 * LANES), lambda b, hg, i: (hg, b, 0),
                               pipeline_mode=pl.Buffered(1)),
                  pl.BlockSpec((TQ_MLA, MLA_HEADS * HEAD_DIM), lambda b, hg, i: (b * nqm + i, hg))],
        out_specs=pl.BlockSpec((TQ_MLA, MLA_HEADS * HEAD_DIM), lambda b, hg, i: (b * nqm + i, hg)),
        out_shape=sds((n, 512), BF16),
        scratch_shapes=[pltpu.VMEM((MLA_HEADS * TQ_MLA, LANES), F32),
                        pltpu.VMEM((MLA_HEADS * TQ_MLA, 2 * LANES), F32)],
        compiler_params=_params("parallel", "parallel", "arbitrary"),
    )(q_cat, k_cat, v_d, zd)

    x2 = out_proj(x1, mix_c, mix_d, cd_w_out[0])
    return x2.reshape(bsz, seq, d)
```

```python
import functools

import jax
import jax.numpy as jnp
from jax import lax
from jax.experimental import pallas as pl
from jax.experimental.pallas import tpu as pltpu

F32, BF16, I32 = jnp.float32, jnp.bfloat16, jnp.int32

D_MODEL = 1024
HEAD_DIM = 64
N_HEADS = 8
N_KV = 2
GRP = N_HEADS // N_KV
IDX_HEADS = 4
DSA_TOPK = 256
SWA_WINDOW = 128
CMP_STRIDE = 16
SLC_BLOCK = 64
SLC_TOPN = 16
NSA_WINDOW = 512
Q_LORA = 256
KV_LORA = 128
NOPE_DIM = 64
ROPE_DIM = 32
ROPE_THETA = 10000.0
EPS = 1e-6

LANES = 128
TM = 1024
TM_FUSED = 512
TQ = 128
CH = 512
TQ_ATT = 256
TQ_MLA = 512
MLA_HEADS = 8
UNROLL = 4
COUNT_ROWS = 16
ROW_BLOCK = 128
VMEM_LIMIT = 56 * 1024 * 1024

NEG_INF = float("-inf")
M_INIT = -1e30
INT_MIN = -(2 ** 31)

AB_W = 2944
LOG2E = 1.4426950408889634
Q_SCALE = HEAD_DIM ** -0.5 * LOG2E


def _nn(a, b):
    return jnp.dot(a, b, preferred_element_type=F32)


def _nt(a, b):
    return lax.dot_general(a, b, (((1,), (1,)), ((), ())), preferred_element_type=F32)


def _sigmoid(z):
    return 1.0 / (1.0 + jnp.exp(-z))


def _silu(z):
    return z * _sigmoid(z)


def _row_rms(x, g):
    return x * lax.rsqrt(jnp.mean(x * x, axis=-1, keepdims=True) + EPS) * g


def _seg_norm(s, bd, gain, group):
    s2 = s * s
    hi = s2.astype(BF16)
    lo = (s2 - hi.astype(F32)).astype(BF16)
    ss = _nn(jnp.concatenate([hi, lo], axis=1), bd)
    return s * lax.rsqrt(ss * (1.0 / group) + EPS) * gain


def _rope(s, cos, sa, sb, half):
    return s * cos + pltpu.roll(s, half, 1) * sa + pltpu.roll(s, LANES - half, 1) * sb


def _order_key(v):
    bits = lax.bitcast_convert_type(v, I32)
    return bits ^ ((bits >> 31) & 0x7FFFFFFF)


def _stack_rows(x, n):
    return jnp.concatenate([x] * n, axis=0)


def _with_ones(v):
    return jnp.concatenate([v.astype(BF16), jnp.ones(v.shape, BF16)], axis=1)


def _flash_update(s, v_ext, m_ref, acc_ref, rows):
    m_prev = m_ref[rows]
    sb = s.astype(BF16)
    m_new = jnp.maximum(m_prev, jnp.max(sb, axis=1, keepdims=True).astype(F32))
    alpha = jnp.exp2(m_prev - m_new)
    p = jnp.exp2(sb - jnp.concatenate([m_new.astype(BF16)] * (s.shape[1] // LANES), axis=1))
    acc_ref[rows] = jnp.concatenate([alpha, alpha], axis=1) * acc_ref[rows] + _nn(p, v_ext)
    m_ref[rows] = m_new


def _unrolled_loop(n, step, unroll=UNROLL):
    def body(j, carry):
        for u in range(unroll):
            step(j * unroll + u)
        return carry

    lax.fori_loop(0, n // unroll, body, 0)
    base = (n // unroll) * unroll
    p = unroll // 2
    while p >= 1:
        def tail(base=base, p=p):
            for u in range(p):
                step(base + u)

        pl.when((n & p) != 0)(tail)
        base = base + (n & p)
        p //= 2


def _flash_init(m_ref, acc_ref):
    m_ref[...] = jnp.full(m_ref.shape, M_INIT, F32)
    acc_ref[...] = jnp.zeros(acc_ref.shape, F32)


def _flash_result(acc_ref):
    acc = acc_ref[...]
    return acc[:, 0:LANES] / jnp.maximum(acc[:, LANES:2 * LANES], 1e-30)


def _place_q_heads(y, bd, gain, cos, sa, sb, out_ref, scale):
    lo_half = lax.broadcasted_iota(I32, (y.shape[0], LANES), 1) < HEAD_DIM
    for j in range(4):
        s = y[:, LANES * j:LANES * (j + 1)]
        s = _rope(_seg_norm(s, bd, gain, HEAD_DIM), cos, sa, sb, HEAD_DIM // 2) * scale
        r = pltpu.roll(s, HEAD_DIM, 1)
        if j < 2:
            out_ref[2 * j] = jnp.where(lo_half, s, 0.0).astype(BF16)
            out_ref[2 * j + 1] = jnp.where(lo_half, r, 0.0).astype(BF16)
        else:
            out_ref[2 * j] = jnp.where(lo_half, 0.0, r).astype(BF16)
            out_ref[2 * j + 1] = jnp.where(lo_half, 0.0, s).astype(BF16)


def _gated_store(o, z_ref, o_ref, tq, row0=0):
    lo_half = lax.broadcasted_iota(I32, (tq, LANES), 1) < HEAD_DIM
    for j in range(4):
        a = o[(2 * j) * tq:(2 * j + 1) * tq]
        b = o[(2 * j + 1) * tq:(2 * j + 2) * tq]
        if j < 2:
            slab = jnp.where(lo_half, a, pltpu.roll(b, HEAD_DIM, 1))
        else:
            slab = jnp.where(lo_half, pltpu.roll(a, HEAD_DIM, 1), b)
        z = z_ref[row0:row0 + tq, LANES * j:LANES * (j + 1)]
        o_ref[row0:row0 + tq, LANES * j:LANES * (j + 1)] = (slab * _silu(z)).astype(BF16)


def _proj_ab_kernel(x_ref, g_ref, w_ref, cos_ref, sa_ref, sb_ref, bd_ref, gains_ref,
                    qa_ref, ka_ref, va_ref, qi_ref, ki_ref, wi_ref, za_ref,
                    qb_ref, kb_ref, vb_ref, zb_ref):
    xn = _row_rms(x_ref[...], g_ref[...]).astype(BF16)
    cos, sa, sb, bd = cos_ref[...], sa_ref[...], sb_ref[...], bd_ref[...]
    lo_half = lax.broadcasted_iota(I32, (TM, LANES), 1) < HEAD_DIM

    def proj(c0, n):
        return _nn(xn, w_ref[:, c0:c0 + n])

    def k_slab(c0, gain):
        return _rope(_seg_norm(proj(c0, LANES), bd, gain, HEAD_DIM), cos, sa, sb, HEAD_DIM // 2)

    _place_q_heads(proj(0, 512), bd, gains_ref[0:1, :], cos, sa, sb, qa_ref, Q_SCALE)
    ka_ref[...] = k_slab(512, gains_ref[1:2, :]).astype(BF16)
    va_ref[...] = _with_ones(proj(640, LANES))
    qi = proj(768, 256)
    for j in range(2):
        s = _rope(qi[:, LANES * j:LANES * (j + 1)], cos, sa, sb, HEAD_DIM // 2)
        qi_ref[2 * j] = jnp.where(lo_half, s, 0.0).astype(BF16)
        qi_ref[2 * j + 1] = jnp.where(lo_half, pltpu.roll(s, HEAD_DIM, 1), 0.0).astype(BF16)
    kiw = proj(1024, LANES)
    wi_ref[...] = kiw
    ki_ref[...] = _rope(_seg_norm(kiw, bd, gains_ref[2:3, :], HEAD_DIM), cos, sa, sb, HEAD_DIM // 2).astype(BF16)
    za_ref[...] = proj(1152, 512)
    _place_q_heads(proj(1664, 512), bd, gains_ref[3:4, :], cos, sa, sb, qb_ref, Q_SCALE)
    kb_ref[...] = k_slab(2176, gains_ref[4:5, :]).astype(BF16)
    vb_ref[...] = _with_ones(proj(2304, LANES))
    zb_ref[...] = proj(2432, 512)


def _count_chunks(keys_ref, nch, nq, pred):
    def body(c, acc):
        return acc + jnp.sum(pred(keys_ref[c]).reshape(CH // COUNT_ROWS, COUNT_ROWS, nq), axis=0)

    acc = lax.fori_loop(0, nch, body, jnp.zeros((COUNT_ROWS, nq), F32))
    return jnp.sum(acc, axis=0, keepdims=True)


def _kth_threshold(count, nq, k):
    def bit_body(it, carry):
        tb, above = carry
        cand_b = tb | jnp.left_shift(jnp.int32(1), 31 - it)
        cand_s = cand_b ^ INT_MIN
        cnt = count(lambda kk: jnp.where(kk >= cand_s, 1.0, 0.0))
        ok = cnt >= k
        return jnp.where(ok, cand_b, tb), jnp.where(ok, above, cnt)

    tb, above = lax.fori_loop(0, 32, bit_body, (jnp.zeros((1, nq), I32), jnp.zeros((1, nq), F32)))
    return tb ^ INT_MIN, k - above


def _dsa_kernel(qi_ref, wi_ref, ki_ref, qa_ref, ka_ref, va_ref, za_ref, tril_ref, o_ref,
                keys_ref, bias_ref, ties_ref, m_ref, acc_ref, *, topk):
    TQ = TQ_ATT
    i = pl.program_id(1)
    t0 = i * TQ
    nch = (t0 + TQ + CH - 1) // CH
    key_c = lax.broadcasted_iota(I32, (CH, TQ), 0)
    qry_c = t0 + lax.broadcasted_iota(I32, (CH, TQ), 1)

    qi = qi_ref[...].reshape(IDX_HEADS * TQ, LANES)
    w_t = wi_ref[...].T * ((HEAD_DIM * IDX_HEADS) ** -0.5)
    ws = [w_t[HEAD_DIM + h:HEAD_DIM + h + 1, :] for h in range(IDX_HEADS)]

    def score_chunk(c):
        off = pl.multiple_of(c * CH, CH)
        lg = _nt(ki_ref[pl.ds(off, CH), :], qi)
        sc = ws[0] * jnp.maximum(lg[:, 0:TQ], 0.0)
        for h in range(1, IDX_HEADS):
            sc = sc + ws[h] * jnp.maximum(lg[:, h * TQ:(h + 1) * TQ], 0.0)
        sc = jnp.where(sc == 0.0, 0.0, sc)
        keys_ref[c] = jnp.where(key_c + off <= qry_c, _order_key(sc), INT_MIN)

    _unrolled_loop(nch, score_chunk)

    count = functools.partial(_count_chunks, keys_ref, nch, TQ)
    thr, need = _kth_threshold(count, TQ, float(topk))

    tril = tril_ref[...]

    ties_ref[...] = jnp.zeros(ties_ref.shape, F32)

    def select_chunk(c):
        kk = keys_ref[c]
        pref = _nn(tril, jnp.where(kk == thr, 1.0, 0.0).astype(BF16)) + ties_ref[0:1, :]
        ties_ref[0:1, :] = pref[CH - 1:CH, :]
        tie = jnp.where(kk == thr, jnp.where(pref <= need, 0.0, NEG_INF), NEG_INF)
        bias = jnp.where(key_c + c * CH <= qry_c, jnp.where(kk > thr, 0.0, tie), NEG_INF)
        bias_ref[c] = bias.T.astype(BF16)

    _unrolled_loop(nch, select_chunk, unroll=2 * UNROLL)

    _flash_init(m_ref, acc_ref)
    q_all = qa_ref[...].reshape(N_HEADS * TQ, LANES)

    def att_chunk(c):
        off = pl.multiple_of(c * CH, CH)
        s = _nt(q_all, ka_ref[pl.ds(off, CH), :]).astype(BF16) + _stack_rows(bias_ref[c], N_HEADS)
        _flash_update(s, va_ref[pl.ds(off, CH), :], m_ref, acc_ref, slice(None))

    _unrolled_loop(nch, att_chunk, unroll=2 * UNROLL)
    _gated_store(_flash_result(acc_ref), za_ref, o_ref, TQ)


def _swa_kernel(q_ref, kp_ref, kc_ref, vp_ref, vc_ref, sink_ref, z_ref, o_ref):
    i = pl.program_id(1)
    qi = lax.broadcasted_iota(I32, (TQ, 2 * TQ), 0)
    kj = lax.broadcasted_iota(I32, (TQ, 2 * TQ), 1)
    sink = sink_ref[...]
    for half in range(2):
        q = q_ref[:, half * TQ:(half + 1) * TQ, :].reshape(N_HEADS * TQ, LANES)
        if half == 0:
            k2 = jnp.concatenate([kp_ref[...], kc_ref[0:TQ, :]], axis=0)
            v2 = jnp.concatenate([vp_ref[...], vc_ref[0:TQ, :]], axis=0)
            lo = jnp.where(i > 0, 0, TQ)
        else:
            k2, v2, lo = kc_ref[...], vc_ref[...], 0
        keep = jnp.where(kj >= lo, jnp.where(kj > qi, jnp.where(kj <= qi + TQ, 0.0, NEG_INF), NEG_INF), NEG_INF)
        s = _nt(q, k2) + _stack_rows(keep, N_HEADS)
        m = jnp.maximum(jnp.max(s, axis=1, keepdims=True), sink)
        p = jnp.exp2(s - jnp.concatenate([m, m], axis=1))
        o = _nn(p.astype(BF16), v2)
        den = o[:, LANES:2 * LANES] + jnp.exp2(sink - m)
        _gated_store(o[:, 0:LANES] / den, z_ref, o_ref, TQ, half * TQ)


def _out_proj_kernel(x_ref, ma_ref, mb_ref, w_ref, o_ref):
    half = ma_ref.shape[1]
    o_ref[...] = x_ref[...] + _nn(ma_ref[...], w_ref[0:half, :]) + _nn(mb_ref[...], w_ref[half:2 * half, :])


def _proj_cd_kernel(x_ref, ma_ref, mb_ref, wo_ref, g_ref, w_ref, cos_ref, sa_ref, sb_ref, bd_ref, gains_ref,
                    x1_ref, qc_ref, kc_ref, vc_ref, ks_ref, vs_ref, kw_ref, vw_ref, gt_ref, zc_ref,
                    cq_ref, ckv_ref, kr_ref, zd_ref):
    half = ma_ref.shape[1]
    x1 = x_ref[...] + _nn(ma_ref[...], wo_ref[0:half, :]) + _nn(mb_ref[...], wo_ref[half:2 * half, :])
    x1_ref[...] = x1
    xn = _row_rms(x1, g_ref[...]).astype(BF16)
    cos, sa, sb, bd = cos_ref[...], sa_ref[...], sb_ref[...], bd_ref[...]

    def proj(c0, n):
        return _nn(xn, w_ref[:, c0:c0 + n])

    def k_slab(c0, gain):
        return _rope(_seg_norm(proj(c0, LANES), bd, gain, HEAD_DIM), cos, sa, sb, HEAD_DIM // 2)

    _place_q_heads(proj(0, 512), bd, gains_ref[0:1, :], cos, sa, sb, qc_ref, Q_SCALE)
    kc_ref[...] = proj(512, LANES)
    vc_ref[...] = proj(640, LANES)
    ks_ref[...] = k_slab(768, gains_ref[1:2, :]).astype(BF16)
    vs_ref[...] = _with_ones(proj(896, LANES))
    kw_ref[...] = k_slab(1024, gains_ref[2:3, :]).astype(BF16)
    vw_ref[...] = _with_ones(proj(1152, LANES))
    gt_ref[...] = _sigmoid(proj(1280, LANES))
    zc_ref[...] = proj(1408, 512)
    cq_ref[...] = proj(1920, 256)
    ckv_ref[...] = proj(2176, LANES)
    kr_ref[...] = proj(2304, LANES)
    zd_ref[...] = proj(2432, 512)


def _compress_kernel(c_ref, pe_ref, w1a_ref, w1b_ref, w2_ref, gain_ref, cos_ref, sa_ref, sb_ref, bd_ref,
                     o_ref, *, is_key):
    c = c_ref[...]
    n = c.shape[0]
    a = _nn((c + pe_ref[0:1, :]).astype(BF16), w1a_ref[...])
    b = _nn((c + pe_ref[1:2, :]).astype(BF16), w1b_ref[...])
    h = _silu(a + pltpu.roll(b, n - 1, 0))
    y = _nn(h.astype(BF16), w2_ref[...])
    if is_key:
        y = _rope(_seg_norm(y, bd_ref[...], gain_ref[...], HEAD_DIM), cos_ref[...], sa_ref[...], sb_ref[...],
                  HEAD_DIM // 2)
    o_ref[...] = y.astype(BF16)


def _nsa_kernel(q_ref, kc_ref, vc_ref, ks_ref, vs_ref, kw_ref, vw_ref, gt_ref, z_ref, wsel_ref, o_ref,
                m_ref, acc_ref, *, n_sel):
    TQ = TQ_ATT
    i = pl.program_id(1)
    t0 = i * TQ
    nch = (t0 + TQ + CH - 1) // CH
    ncmp = kc_ref.shape[0]
    rows_all = N_HEADS * TQ
    q = q_ref[...].reshape(rows_all, LANES)

    cend = lax.broadcasted_iota(I32, (TQ, ncmp), 1) * CMP_STRIDE + (2 * CMP_STRIDE - 1)
    bias_c = jnp.where(cend <= t0 + lax.broadcasted_iota(I32, (TQ, ncmp), 0), 0.0, NEG_INF)
    s = _nt(q, kc_ref[...]) + _stack_rows(bias_c, N_HEADS)
    m = jnp.max(s, axis=1, keepdims=True)
    m = jnp.where(m > NEG_INF, m, 0.0)
    p = jnp.exp2(s - m)
    p = p / jnp.maximum(jnp.sum(p, axis=1, keepdims=True), 1e-30)
    o_cmp = _nn(p.astype(BF16), vc_ref[...])

    span = NSA_WINDOW + TQ
    start = pl.multiple_of(jnp.maximum(t0 - NSA_WINDOW, 0), TQ)
    diff = t0 + lax.broadcasted_iota(I32, (TQ, span), 0) - (start + lax.broadcasted_iota(I32, (TQ, span), 1))
    bias_w = jnp.where(diff >= 0, jnp.where(diff < NSA_WINDOW, 0.0, NEG_INF), NEG_INF)
    sw = _nt(q, kw_ref[pl.ds(start, span), :]).astype(BF16) + _stack_rows(bias_w.astype(BF16), N_HEADS)
    pw = jnp.exp2(sw - jnp.max(sw, axis=1, keepdims=True))
    o_win = _nn(pw, vw_ref[pl.ds(start, span), :])
    o_win = o_win[:, 0:LANES] / o_win[:, LANES:2 * LANES]

    blk = lax.broadcasted_iota(I32, (LANES, TQ), 0)
    cur = (t0 + lax.broadcasted_iota(I32, (LANES, TQ), 1)) >> 6
    wsel_t = wsel_ref[...]
    keys = []
    for g in range(N_KV):
        imp = p[(GRP * g) * TQ:(GRP * g + 1) * TQ]
        for r in range(1, GRP):
            imp = imp + p[(GRP * g + r) * TQ:(GRP * g + r + 1) * TQ]
        hi = imp.astype(BF16)
        lo = (imp - hi.astype(F32)).astype(BF16)
        imp_s = _nt(wsel_t, hi) + _nt(wsel_t, lo)
        forced = jnp.where(blk == 0, jnp.inf, jnp.where(blk >= cur - 1, jnp.inf, imp_s))
        keys.append(_order_key(jnp.where(blk <= cur, forced, NEG_INF)))
    kk = jnp.concatenate(keys, axis=1)

    def bit_body(it, tb):
        cand_b = tb | jnp.left_shift(jnp.int32(1), 31 - it)
        cnt = jnp.sum(jnp.where(kk >= (cand_b ^ INT_MIN), 1.0, 0.0), axis=0, keepdims=True)
        return jnp.where(cnt >= n_sel, cand_b, tb)

    thr = lax.fori_loop(0, 32, bit_body, jnp.zeros((1, N_KV * TQ), I32)) ^ INT_MIN
    need = n_sel - jnp.sum(jnp.where(kk > thr, 1.0, 0.0), axis=0, keepdims=True)
    tril = jnp.where(lax.broadcasted_iota(I32, (LANES, LANES), 0) >= lax.broadcasted_iota(I32, (LANES, LANES), 1),
                     1.0, 0.0).astype(BF16)
    pref = _nn(tril, jnp.where(kk == thr, 1.0, 0.0).astype(BF16))
    tie = jnp.where(kk == thr, jnp.where(pref <= need, 1.0, 0.0), 0.0)
    sel_t = jnp.where(kk > thr, 1.0, tie)
    sel = jnp.concatenate([sel_t[:, g * TQ:(g + 1) * TQ].T for g in range(N_KV)], axis=0).astype(BF16)

    _flash_init(m_ref, acc_ref)
    blk_row =lax.broadcasted_iota(I32, (LANES, CH), 0)
    blk_col = lax.broadcasted_iota(I32, (LANES, CH), 1) >> 6
    row_c = t0 + lax.broadcasted_iota(I32, (TQ, CH), 0)
    col_c = lax.broadcasted_iota(I32, (TQ, CH), 1)

    def slc_chunk(c):
        off = pl.multiple_of(c * CH, CH)
        expand = jnp.where(blk_row == blk_col + c * (CH // SLC_BLOCK), 1.0, 0.0).astype(BF16)
        tok = _nn(sel, expand)
        causal = col_c + off <= row_c
        parts = []
        for g in range(N_KV):
            bias_g = jnp.where(causal, jnp.where(tok[g * TQ:(g + 1) * TQ] > 0.5, 0.0, NEG_INF), NEG_INF)
            parts.extend([bias_g.astype(BF16)] * GRP)
        sc = _nt(q, ks_ref[pl.ds(off, CH), :]).astype(BF16) + jnp.concatenate(parts, axis=0)
        _flash_update(sc, vs_ref[pl.ds(off, CH), :], m_ref, acc_ref, slice(None))

    _unrolled_loop(nch, slc_chunk, unroll=2 * UNROLL)
    o_slc = _flash_result(acc_ref)

    gates = gt_ref[...]
    outs = []
    for h in range(N_HEADS):
        rs = slice(h * TQ, (h + 1) * TQ)
        outs.append(gates[:, 3 * h:3 * h + 1] * o_cmp[rs] + gates[:, 3 * h + 1:3 * h + 2] * o_slc[rs]
                    + gates[:, 3 * h + 2:3 * h + 3] * o_win[rs])
    _gated_store(jnp.concatenate(outs, axis=0), z_ref, o_ref, TQ)


def _mla_prep_kernel(cq_ref, ckv_ref, kr_ref, wq_ref, wkv_ref, lat_ref, gains_ref, cos_ref, sa_ref, sb_ref,
                     bd64_ref, bd32_ref, q_ref, k_ref, v_ref):
    lane = lax.broadcasted_iota(I32, (TM, LANES), 1)
    lo_half = lane < NOPE_DIM
    cos, sa, sb = cos_ref[...], sa_ref[...], sb_ref[...]
    bd64, bd32 = bd64_ref[...], bd32_ref[...]
    scale = (NOPE_DIM + ROPE_DIM) ** -0.5 * LOG2E

    def rope_slab(s, gain):
        return _rope(_seg_norm(s, bd32, gain, ROPE_DIM), cos, sa, sb, ROPE_DIM // 2)

    q = _nn(_row_rms(cq_ref[...], lat_ref[0:1, :]).astype(BF16), wq_ref[...])
    kv = _nn(_row_rms(ckv_ref[...], lat_ref[1:2, 0:KV_LORA]).astype(BF16), wkv_ref[...])
    k_rope = pltpu.roll(rope_slab(kr_ref[...], gains_ref[3:4, :]), NOPE_DIM, 1)
    q_rope = [rope_slab(q[:, 512 + LANES * j:512 + LANES * (j + 1)], gains_ref[1:2, :]) * scale for j in range(2)]
    for j in range(4):
        qn = _seg_norm(q[:, LANES * j:LANES * (j + 1)], bd64, gains_ref[0:1, :], NOPE_DIM) * scale
        kn = _seg_norm(kv[:, LANES * j:LANES * (j + 1)], bd64, gains_ref[2:3, :], NOPE_DIM)
        for e in range(2):
            h = 2 * j + e
            qn_h = qn if e == 0 else pltpu.roll(qn, NOPE_DIM, 1)
            kn_h = kn if e == 0 else pltpu.roll(kn, NOPE_DIM, 1)
            shift = (NOPE_DIM - ROPE_DIM * (h % 4)) % LANES
            qr = q_rope[h // 4]
            qr_h = qr if shift == 0 else pltpu.roll(qr, shift, 1)
            q_ref[h] = jnp.where(lo_half, qn_h, jnp.where(lane < NOPE_DIM + ROPE_DIM, qr_h, 0.0)).astype(BF16)
            k_ref[h] = jnp.where(lo_half, kn_h, k_rope).astype(BF16)
    for j in range(4):
        v_ref[j] = _with_ones(kv[:, 512 + LANES * j:512 + LANES * (j + 1)])


def _mla_kernel(q_ref, k_ref, v_ref, z_ref, o_ref, m_ref, acc_ref):
    i = pl.program_id(2)
    tq = TQ_MLA
    _flash_init(m_ref, acc_ref)

    def step(c, masked):
        off = pl.multiple_of(c * CH, CH)
        for e in range(MLA_HEADS):
            k = k_ref[e, pl.ds(off, CH), :]
            v = v_ref[e // 2, pl.ds(off, CH), :]
            for rb in range(tq // ROW_BLOCK):
                s = _nt(q_ref[e, rb * ROW_BLOCK:(rb + 1) * ROW_BLOCK, :], k)
                if masked:
                    qpos = rb * ROW_BLOCK + lax.broadcasted_iota(I32, (ROW_BLOCK, CH), 0)
                    s = jnp.where(lax.broadcasted_iota(I32, (ROW_BLOCK, CH), 1) <= qpos, s, NEG_INF)
                _flash_update(s, v, m_ref, acc_ref, slice(e * tq + rb * ROW_BLOCK, e * tq + (rb + 1) * ROW_BLOCK))

    _unrolled_loop(i, lambda c: step(c, False))
    step(i, True)
    o = _flash_result(acc_ref)
    lo_half = lax.broadcasted_iota(I32, (tq, LANES), 1) < NOPE_DIM
    for pr in range(MLA_HEADS // 2):
        slab = jnp.where(lo_half, o[(2 * pr) * tq:(2 * pr + 1) * tq], o[(2 * pr + 1) * tq:(2 * pr + 2) * tq])
        lanes = slice(pr * LANES, (pr + 1) * LANES)
        o_ref[:, lanes] = (slab * _silu(z_ref[:, lanes])).astype(BF16)


def _rope_tables(pos, dim):
    inv = jnp.power(jnp.float32(ROPE_THETA), -jnp.arange(0, dim, 2, dtype=F32) / dim)
    ang = pos.astype(F32)[:, None] * inv[None, :]
    cos, sin = jnp.cos(ang), jnp.sin(ang)
    reps = LANES // dim
    zero = jnp.zeros_like(sin)
    cos_t = jnp.tile(jnp.concatenate([cos, cos], axis=1), (1, reps))
    sa_t = jnp.tile(jnp.concatenate([zero, sin], axis=1), (1, reps))
    sb_t = jnp.tile(jnp.concatenate([-sin, zero], axis=1), (1, reps))
    return cos_t, sa_t, sb_t


def _block_diag(group):
    r = jnp.arange(LANES) // group
    bd = (r[:, None] == r[None, :]).astype(BF16)
    return jnp.concatenate([bd, bd], axis=0)


def _tile_gain(g, width=LANES):
    return jnp.tile(g.astype(F32), width // g.shape[0])


def _pad_lanes(g, width=LANES):
    return jnp.concatenate([g.astype(F32), jnp.zeros((width - g.shape[0],), F32)])


def _pad_cols(w, width):
    return jnp.concatenate([w, jnp.zeros((w.shape[0], width - w.shape[1]), w.dtype)], axis=1)


def _params(*sem, flags=None):
    return pltpu.CompilerParams(dimension_semantics=sem, vmem_limit_bytes=VMEM_LIMIT, flags=flags)


def kernel(x, ab_norm, ab_w_in, a_qk_norm, a_kidx_norm, b_qk_norm, b_sinks, ab_w_out, cd_norm, cd_w_in, c_q_norm, c_k_norm, c_cmp_pe, c_cmp_w1, c_cmp_w2, d_q_lat_norm, d_kv_lat_norm, d_w_uq, d_w_ukv, d_nope_norm, d_rope_norm, cd_w_out):
    bsz, seq, d = x.shape
    n = bsz * seq
    assert d == D_MODEL and seq % CH == 0 and seq >= NSA_WINDOW + TQ_ATT and seq // SLC_BLOCK <= LANES
    assert TQ == SWA_WINDOW and TQ_MLA == CH and seq % TM == 0
    nq = seq // TQ
    nrow = n // TM
    srow = seq // TM
    xf = x.reshape(n, d)
    pos = jnp.arange(seq)
    cos64, sa64, sb64 = _rope_tables(pos, HEAD_DIM)
    cos32, sa32, sb32 = _rope_tables(pos, ROPE_DIM)
    bd64, bd32 = _block_diag(HEAD_DIM), _block_diag(ROPE_DIM)
    tril = (jnp.arange(CH)[:, None] >= jnp.arange(CH)[None, :]).astype(BF16)

    def row_spec(width, rows=TM):
        return pl.BlockSpec((rows, width), lambda i: (i, 0))

    def const_spec(shape):
        return pl.BlockSpec(shape, lambda i: tuple(0 for _ in shape))

    tab_spec = pl.BlockSpec((TM, LANES), lambda i: (i % srow, 0))
    head_spec = lambda nh: pl.BlockSpec((nh, TM, LANES), lambda i: (0, i, 0))

    wa = ab_w_in[0]
    w_ab = jnp.concatenate([wa[:, 0:1088], _pad_cols(wa[:, 1088:1092], 64), wa[:, 1092:]], axis=1).astype(BF16)
    gains_ab = jnp.stack([_tile_gain(a_qk_norm[0, 0]), _tile_gain(a_qk_norm[0, 1]), _pad_lanes(a_kidx_norm[0]),
                          _tile_gain(b_qk_norm[0, 0]), _tile_gain(b_qk_norm[0, 1])]
                         + [jnp.zeros((LANES,), F32)] * 3)
    sds = jax.ShapeDtypeStruct
    qa, ka, va, qi, ki, wi, za, qb, kb, vb, zb = pl.pallas_call(
        _proj_ab_kernel,
        grid=(nrow,),
        in_specs=[row_spec(d), const_spec((1, d)), const_spec((d, AB_W)), tab_spec, tab_spec, tab_spec,
                  const_spec((2 * LANES, LANES)), const_spec((8, LANES))],
        out_specs=[head_spec(8), row_spec(LANES), row_spec(2 * LANES), head_spec(4), row_spec(LANES), row_spec(LANES),
                   row_spec(512), head_spec(8), row_spec(LANES), row_spec(2 * LANES), row_spec(512)],
        out_shape=[sds((8, n, LANES), BF16), sds((n, LANES), BF16), sds((n, 2 * LANES), BF16),
                   sds((4, n, LANES), BF16), sds((n, LANES), BF16), sds((n, LANES), F32), sds((n, 512), F32),
                   sds((8, n, LANES), BF16), sds((n, LANES), BF16), sds((n, 2 * LANES), BF16), sds((n, 512), F32)],
        compiler_params=_params("parallel"),
    )(xf, ab_norm[0][None, :], w_ab, cos64, sa64, sb64, bd64, gains_ab)

    nqa = seq // TQ_ATT
    qt_heads = lambda nh: pl.BlockSpec((nh, TQ_ATT, LANES), lambda b, i: (0, b * nqa + i, 0))
    qt_rows = lambda width: pl.BlockSpec((TQ_ATT, width), lambda b, i: (b * nqa + i, 0))
    seq_rows = pl.BlockSpec((seq, LANES), lambda b, i: (b, 0))
    seq_rows_v = pl.BlockSpec((seq, 2 * LANES), lambda b, i: (b, 0))
    att_scratch = [pltpu.VMEM((N_HEADS * TQ_ATT, LANES), F32), pltpu.VMEM((N_HEADS * TQ_ATT, 2 * LANES), F32)]

    mix_a = pl.pallas_call(
        functools.partial(_dsa_kernel, topk=min(DSA_TOPK, seq // 4)),
        grid=(bsz, nqa),
        in_specs=[qt_heads(4), qt_rows(LANES), seq_rows, qt_heads(8), seq_rows, seq_rows_v, qt_rows(512),
                  pl.BlockSpec((CH, CH), lambda b, i: (0, 0))],
        out_specs=qt_rows(512),
        out_shape=sds((n, 512), BF16),
        scratch_shapes=[pltpu.VMEM((seq // CH, CH, TQ_ATT), I32), pltpu.VMEM((seq // CH, TQ_ATT, CH), BF16),
                        pltpu.VMEM((8, TQ_ATT), F32)] + att_scratch,
        compiler_params=_params("parallel", "arbitrary"),
    )(qi, wi, ki, qa, ka, va, za, tril)

    nq2 = nq // 2
    prev_rows = lambda width: pl.BlockSpec((TQ, width), lambda b, i: (b * nq + jnp.maximum(2 * i - 1, 0), 0))
    pair_rows = lambda width: pl.BlockSpec((2 * TQ, width), lambda b, i: (b * nq2 + i, 0))
    sink_rows = jnp.broadcast_to(jnp.repeat(b_sinks[0].astype(F32) * LOG2E, TQ)[:, None], (N_HEADS * TQ, LANES))
    mix_b = pl.pallas_call(
        _swa_kernel,
        grid=(bsz, nq2),
        in_specs=[pl.BlockSpec((N_HEADS, 2 * TQ, LANES), lambda b, i: (0, b * nq2 + i, 0)),
                  prev_rows(LANES), pair_rows(LANES), prev_rows(2 * LANES), pair_rows(2 * LANES),
                  pl.BlockSpec((N_HEADS * TQ, LANES), lambda b, i: (0, 0)), pair_rows(512)],
        out_specs=pair_rows(512),
        out_shape=sds((n, 512), BF16),
        compiler_params=_params("parallel", "arbitrary"),
    )(qb, kb, kb, vb, vb, sink_rows, zb)

    def out_proj(xin, m0, m1, w):
        return pl.pallas_call(
            _out_proj_kernel,
            grid=(nrow,),
            in_specs=[row_spec(d), row_spec(512), row_spec(512), const_spec((2 * 512, d))],
            out_specs=row_spec(d),
            out_shape=sds((n, d), F32),
            compiler_params=_params("parallel"),
        )(xin, m0, m1, w.astype(BF16))


    wc = cd_w_in[0]
    w_cd = jnp.concatenate([wc[:, 0:1280], _pad_cols(wc[:, 1280:1304], LANES), wc[:, 1304:2200],
                            _pad_cols(wc[:, 2200:2232], LANES), wc[:, 2232:]], axis=1).astype(BF16)
    gains_cd = jnp.stack([_tile_gain(c_q_norm[0]), _tile_gain(c_k_norm[0, 1]), _tile_gain(c_k_norm[0, 2])]
                         + [jnp.zeros((LANES,), F32)] * 5)
    once = lambda shape: pl.BlockSpec(shape, lambda i: (0, 0), pipeline_mode=pl.Buffered(1))
    rs = lambda width: row_spec(width, TM_FUSED)
    tab_f = pl.BlockSpec((TM_FUSED, LANES), lambda i: (i % (seq // TM_FUSED), 0))
    x1, qc, kc_raw, vc_raw, ks, vs, kw, vw, gates, zc, cq, ckv, kr, zd = pl.pallas_call(
        _proj_cd_kernel,
        grid=(n // TM_FUSED,),
        in_specs=[rs(d), rs(512), rs(512), once((2 * 512, d)), const_spec((1, d)),
                  once((d, AB_W)), tab_f, tab_f, tab_f,
                  const_spec((2 * LANES, LANES)), const_spec((8, LANES))],
        out_specs=[rs(d), pl.BlockSpec((8, TM_FUSED, LANES), lambda i: (0, i, 0)), rs(LANES), rs(LANES), rs(LANES),
                   rs(2 * LANES), rs(LANES), rs(2 * LANES), rs(LANES), rs(512), rs(256),
                   rs(LANES), rs(LANES), rs(512)],
        out_shape=[sds((n, d), F32),
                   sds((8, n, LANES), BF16), sds((n, LANES), F32), sds((n, LANES), F32), sds((n, LANES), BF16),
                   sds((n, 2 * LANES), BF16), sds((n, LANES), BF16), sds((n, 2 * LANES), BF16), sds((n, LANES), F32),
                   sds((n, 512), F32), sds((n, 256), F32), sds((n, LANES), F32), sds((n, LANES), F32),
                   sds((n, 512), F32)],
        compiler_params=_params("parallel"),
    )(xf, mix_a, mix_b, ab_w_out[0].astype(BF16), cd_norm[0][None, :], w_cd, cos64, sa64, sb64, bd64, gains_cd)

    ncmp = seq // CMP_STRIDE
    cw = CMP_STRIDE * N_KV * HEAD_DIM
    eye = jnp.eye(N_KV, dtype=F32)
    ccos, csa, csb = _rope_tables(jnp.arange(ncmp) * CMP_STRIDE + (2 * CMP_STRIDE - 1), HEAD_DIM)

    def compress(raw, which, is_key):
        w1 = c_cmp_w1[0, which].reshape(2, CMP_STRIDE, HEAD_DIM, -1)
        hid = w1.shape[-1]
        w1e = jnp.einsum("tjdu,gh->tjgdhu", w1, eye).reshape(2, cw, N_KV * hid).astype(BF16)
        w2e = jnp.einsum("ud,gh->guhd", c_cmp_w2[0, which], eye).reshape(N_KV * hid, N_KV * HEAD_DIM).astype(BF16)
        pe = c_cmp_pe[0, which].reshape(2, CMP_STRIDE, 1, HEAD_DIM)
        pe = jnp.broadcast_to(pe, (2, CMP_STRIDE, N_KV, HEAD_DIM)).reshape(2, cw)
        return pl.pallas_call(
            functools.partial(_compress_kernel, is_key=is_key),
            grid=(bsz,),
            in_specs=[pl.BlockSpec((ncmp, cw), lambda b: (b, 0)), const_spec((2, cw)),
                      const_spec((cw, N_KV * hid)), const_spec((cw, N_KV * hid)),
                      const_spec((N_KV * hid, LANES)), const_spec((1, LANES)),
                      const_spec((ncmp, LANES)), const_spec((ncmp, LANES)), const_spec((ncmp, LANES)),
                      const_spec((2 * LANES, LANES))],
            out_specs=pl.BlockSpec((ncmp, LANES), lambda b: (b, 0)),
            out_shape=sds((bsz * ncmp, LANES), BF16),
            compiler_params=_params("parallel"),
        )(raw.reshape(bsz * ncmp, cw), pe, w1e[0], w1e[1], w2e, _tile_gain(c_k_norm[0, 0])[None, :],
          ccos, csa, csb, bd64)

    kc = compress(kc_raw, 0, True)
    vc = compress(vc_raw, 1, False)

    ratio = SLC_BLOCK // CMP_STRIDE
    mm = jnp.arange(ncmp)[:, None]
    jj = jnp.arange(LANES)[None, :]
    wsel = (jnp.where((mm == ratio * jj - 1) | (mm == ratio * jj + ratio - 1), 1.0, 0.0)
            + jnp.where((mm >= ratio * jj) & (mm < ratio * jj + ratio - 1), 2.0, 0.0))
    wsel_t = jnp.where((mm < ncmp - 1) & (jj < seq // SLC_BLOCK), wsel, 0.0).astype(BF16).T

    cmp_rows = pl.BlockSpec((ncmp, LANES), lambda b, i: (b, 0))
    mix_c = pl.pallas_call(
        functools.partial(_nsa_kernel, n_sel=min(SLC_TOPN, seq // SLC_BLOCK)),
        grid=(bsz, nqa),
        in_specs=[qt_heads(8), cmp_rows, cmp_rows, seq_rows, seq_rows_v, seq_rows, seq_rows_v, qt_rows(LANES),
                  qt_rows(512), pl.BlockSpec((LANES, ncmp), lambda b, i: (0, 0))],
        out_specs=qt_rows(512),
        out_shape=sds((n, 512), BF16),
        scratch_shapes=att_scratch,
        compiler_params=_params("parallel", "arbitrary"),
    )(qc, kc, vc, ks, vs, kw, vw, gates, zc, wsel_t)

    wq = d_w_uq[0].reshape(Q_LORA, N_HEADS, NOPE_DIM + ROPE_DIM)
    wq = jnp.concatenate([wq[:, :, :NOPE_DIM].reshape(Q_LORA, -1), wq[:, :, NOPE_DIM:].reshape(Q_LORA, -1)],
                         axis=1).astype(BF16)
    wkv = d_w_ukv[0].reshape(KV_LORA, N_HEADS, NOPE_DIM + HEAD_DIM)
    wkv = jnp.concatenate([wkv[:, :, :NOPE_DIM].reshape(KV_LORA, -1), wkv[:, :, NOPE_DIM:].reshape(KV_LORA, -1)],
                          axis=1).astype(BF16)
    lat_gains = jnp.stack([d_q_lat_norm[0].astype(F32), _pad_lanes(d_kv_lat_norm[0], Q_LORA)]
                          + [jnp.zeros((Q_LORA,), F32)] * 6)
    gains_d = jnp.stack([_tile_gain(d_nope_norm[0, 0]), _tile_gain(d_rope_norm[0, 0]), _tile_gain(d_nope_norm[0, 1]),
                         _pad_lanes(d_rope_norm[0, 1])] + [jnp.zeros((LANES,), F32)] * 4)
    q_cat, k_cat, v_d = pl.pallas_call(
        _mla_prep_kernel,
        grid=(nrow,),
        in_specs=[row_spec(Q_LORA), row_spec(LANES), row_spec(LANES), const_spec((Q_LORA, 768)),
                  const_spec((KV_LORA, 1024)), const_spec((8, Q_LORA)), const_spec((8, LANES)),
                  tab_spec, tab_spec, tab_spec, const_spec((2 * LANES, LANES)), const_spec((2 * LANES, LANES))],
        out_specs=[head_spec(8), head_spec(8), pl.BlockSpec((4, TM, 2 * LANES), lambda i: (0, i, 0))],
        out_shape=[sds((8, n, LANES), BF16), sds((8, n, LANES), BF16), sds((4, n, 2 * LANES), BF16)],
        compiler_params=_params("parallel"),
    )(cq, ckv, kr, wq, wkv, lat_gains, gains_d, cos32, sa32, sb32, bd64, bd32)

    nqm = seq // TQ_MLA
    mix_d = pl.pallas_call(
        _mla_kernel,
        grid=(bsz, N_HEADS // MLA_HEADS, nqm),
        in_specs=[pl.BlockSpec((MLA_HEADS, TQ_MLA, LANES), lambda b, hg, i: (hg, b * nqm + i, 0)),
                  pl.BlockSpec((MLA_HEADS, seq, LANES), lambda b, hg, i: (hg, b, 0), pipeline_mode=pl.Buffered(1)),
                  pl.BlockSpec((MLA_HEADS // 2, seq, 2 * LANES), lambda b, hg, i: (hg, b, 0),
                               pipeline_mode=pl.Buffered(1)),
                  pl.BlockSpec((TQ_MLA, MLA_HEADS * HEAD_DIM), lambda b, hg, i: (b * nqm + i, hg))],
        out_specs=pl.BlockSpec((TQ_MLA, MLA_HEADS * HEAD_DIM), lambda b, hg, i: (b * nqm + i, hg)),
        out_shape=sds((n, 512), BF16),
        scratch_shapes=[pltpu.VMEM((MLA_HEADS * TQ_MLA, LANES), F32),
                        pltpu.VMEM((MLA_HEADS * TQ_MLA, 2 * LANES), F32)],
        compiler_params=_params("parallel", "parallel", "arbitrary"),
    )(q_cat, k_cat, v_d, zd)

    x2 = out_proj(x1, mix_c, mix_d, cd_w_out[0])
    return x2.reshape(bsz, seq, d)
```
